```python
import jax, jax.numpy as jnp
from jax import lax
import numpy as np

D_MODEL = 1024
BATCH = 8
SEQ = 4096
DEPTH = 2

DIL_GROUPS = ((128, 1), (512, 4), (2048, 16))
N_GROUPS = 3
A_HEADS = 8
A_HEAD_DIM = 64
A_WIDTH = N_GROUPS * A_HEADS * A_HEAD_DIM
A_OUT = A_HEADS * A_HEAD_DIM
ROT_DIM = A_HEAD_DIM // 4
ROPE_THETA = 500000.0
ATT_BLOCK = 128
NEG_INF = -1e30
R_HEADS = 4
R_KEY_DIM = 128
R_VAL_DIM = 256
R_QK = R_HEADS * R_KEY_DIM
R_V = R_HEADS * R_VAL_DIM
R_CHUNK = 128
R_ROT_BASE = 10000.0
D_FF = 4 * D_MODEL
NORM_EPS = 1e-6

_SIZES = (A_WIDTH, A_WIDTH, A_WIDTH, R_QK, R_QK, R_V, R_V, D_MODEL, D_MODEL)
D_IN = sum(_SIZES)
SPLIT_POINTS = tuple(sum(_SIZES[:i + 1]) for i in range(len(_SIZES) - 1))

kernel_name = "dilated_attn_retention_gated_hybrid"


def rmsnorm(x, g):
    xf = x.astype(jnp.float32)
    y = xf * lax.rsqrt(jnp.mean(xf * xf, axis=-1, keepdims=True) + NORM_EPS)
    return (y * g.astype(jnp.float32)).astype(x.dtype)


def rotary(x, pos, rot_dim, base):
    half = rot_dim // 2
    inv = 1.0 / (base ** (jnp.arange(half, dtype=jnp.float32) / half))
    ang = pos.astype(jnp.float32)[:, None] * inv[None, :]
    cos = jnp.cos(ang)[None, :, None, :]
    sin = jnp.sin(ang)[None, :, None, :]
    xr = x[..., :rot_dim].astype(jnp.float32)
    x1, x2 = xr[..., :half], xr[..., half:]
    rot = jnp.concatenate([x1 * cos - x2 * sin, x1 * sin + x2 * cos], axis=-1).astype(x.dtype)
    return jnp.concatenate([rot, x[..., rot_dim:]], axis=-1)


def dilated_group_attention(q, k, v, window, dilation):
    b, s, h, dh = q.shape
    span = dilation * ATT_BLOCK
    L = -(-s // span) * span
    n_sub = L // dilation
    nb = n_sub // ATT_BLOCK
    reach = window // dilation

    def to_blocks(t):
        t = jnp.pad(t, ((0, 0), (0, L - s), (0, 0), (0, 0)))
        t = t.reshape(b, n_sub, dilation, h, dh).transpose(0, 2, 3, 1, 4)
        return t.reshape(b, dilation, h, nb, ATT_BLOCK, dh)

    def with_prev(t):
        prev = jnp.pad(t, ((0, 0), (0, 0), (0, 0), (1, 0), (0, 0), (0, 0)))[:, :, :, :-1]
        return jnp.concatenate([prev, t], axis=-2)

    qb, kb, vb = to_blocks(q), to_blocks(k), to_blocks(v)
    kk, vv = with_prev(kb), with_prev(vb)
    scores = jnp.einsum('brhnqd,brhnkd->brhnqk', qb, kk).astype(jnp.float32) * (dh ** -0.5)
    qi = jnp.arange(ATT_BLOCK)[:, None]
    kj = jnp.arange(2 * ATT_BLOCK)[None, :]
    dist = qi + ATT_BLOCK - kj
    band = (dist >= 0) & (dist <= reach)
    blk = jnp.arange(nb)[:, None, None]
    mask = band[None] & ((blk > 0) | (kj >= ATT_BLOCK)[None])
    scores = jnp.where(mask, scores, NEG_INF)
    m = jnp.max(scores, axis=-1, keepdims=True)
    p = jnp.exp(scores - m)
    den = jnp.sum(p, axis=-1, keepdims=True)
    out = jnp.einsum('brhnqk,brhnkd->brhnqd', (p / den).astype(v.dtype), vv)
    lse = (m + jnp.log(den))[..., 0]

    def from_blocks(t):
        rest = t.shape[5:]
        t = t.reshape((b, dilation, h, n_sub) + rest)
        t = jnp.moveaxis(t, 3, 1)
        return t.reshape((b, L, h) + rest)[:, :s]

    return from_blocks(out), from_blocks(lse)


def retention(q, k, v):
    b, s, h, dk = q.shape
    dv = v.shape[-1]
    nc = s // R_CHUNK
    C = R_CHUNK
    lg = jnp.log1p(-(2.0 ** (-5.0 - jnp.arange(h, dtype=jnp.float32))))

    def chunks(t, d):
        return t.astype(jnp.float32).reshape(b, nc, C, h, d).transpose(0, 3, 1, 2, 4)

    qc = chunks(q, dk)
    kc = chunks(k, dk) * (dk ** -0.5)
    vc = chunks(v, dv)
    idx = jnp.arange(C, dtype=jnp.float32)
    diff = idx[:, None] - idx[None, :]
    decay = jnp.where(diff >= 0, jnp.exp(diff[None] * lg[:, None, None]), 0.0)
    zeta = jnp.exp((C - 1 - idx)[None, :] * lg[:, None])
    xi = jnp.exp((idx + 1)[None, :] * lg[:, None])
    chunk_decay = jnp.exp(C * lg)[None, :, None, None]

    inner_s = jnp.einsum('bhncd,bhnmd->bhncm', qc, kc) * decay[None, :, None]
    inner = jnp.einsum('bhncm,bhnme->bhnce', inner_s, vc)
    kv = jnp.einsum('bhncd,bhnce->bhnde', kc * zeta[None, :, None, :, None], vc)

    def step(state, kv_i):
        return state * chunk_decay + kv_i, state

    _, prev_states = lax.scan(step, jnp.zeros((b, h, dk, dv), jnp.float32), jnp.moveaxis(kv, 2, 0))
    prev_states = jnp.moveaxis(prev_states, 0, 2)
    cross = jnp.einsum('bhncd,bhnde->bhnce', qc * xi[None, :, None, :, None], prev_states)
    out = inner + cross
    return out.transpose(0, 2, 3, 1, 4).reshape(b, s, h, dv)


def hybrid_layer(x, g_mix, w_in, w_a, w_b, w_o, g_ffn, w_up, w_down):
    b, s, _ = x.shape
    pos = jnp.arange(s)
    hn = rmsnorm(x, g_mix)
    proj = hn @ w_in
    qa, ka, va, qr, kr, vr, gr, ga, gb = jnp.split(proj, SPLIT_POINTS, axis=-1)

    def heads_a(t):
        return t.reshape(b, s, N_GROUPS * A_HEADS, A_HEAD_DIM)
    qa = rotary(heads_a(qa), pos, ROT_DIM, ROPE_THETA).reshape(b, s, N_GROUPS, A_HEADS, A_HEAD_DIM)
    ka = rotary(heads_a(ka), pos, ROT_DIM, ROPE_THETA).reshape(b, s, N_GROUPS, A_HEADS, A_HEAD_DIM)
    va = va.reshape(b, s, N_GROUPS, A_HEADS, A_HEAD_DIM)
    outs, lses = [], []
    for gi, (window, dilation) in enumerate(DIL_GROUPS):
        o, l = dilated_group_attention(qa[:, :, gi], ka[:, :, gi], va[:, :, gi], window, dilation)
        outs.append(o.astype(jnp.float32))
        lses.append(l)
    wts = jax.nn.softmax(jnp.stack(lses, axis=0), axis=0)
    y_a = jnp.sum(wts[..., None] * jnp.stack(outs, axis=0), axis=0).astype(x.dtype)
    y_a = y_a.reshape(b, s, A_OUT) @ w_a

    qr = rotary(qr.reshape(b, s, R_HEADS, R_KEY_DIM), pos, R_KEY_DIM, R_ROT_BASE)
    kr = rotary(kr.reshape(b, s, R_HEADS, R_KEY_DIM), pos, R_KEY_DIM, R_ROT_BASE)
    ret = retention(qr, kr, vr.reshape(b, s, R_HEADS, R_VAL_DIM))
    ret = ret * lax.rsqrt(jnp.mean(ret * ret, axis=-1, keepdims=True) + NORM_EPS)
    y_b = (jax.nn.silu(gr.astype(jnp.float32)) * ret.reshape(b, s, R_V)).astype(x.dtype) @ w_b

    merged = jax.nn.sigmoid(ga) * y_a + jax.nn.sigmoid(gb) * y_b
    x = x + merged @ w_o

    h2 = rmsnorm(x, g_ffn)
    x = x + jnp.square(jax.nn.relu(h2 @ w_up)) @ w_down
    return x


def setup_inputs(seed: int = 0) -> dict:
    key = jax.random.key(seed)
    ks = jax.random.split(key, 10)
    f32 = jnp.float32

    def w(k, shape, fan_in):
        return jax.random.normal(k, shape, f32) * (fan_in ** -0.5)

    def gain(k, shape):
        return 1.0 + 0.05 * jax.random.normal(k, shape, f32)

    return {
        "x": jax.random.normal(ks[0], (BATCH, SEQ, D_MODEL), f32),
        "mix_norm": gain(ks[1], (DEPTH, D_MODEL)),
        "w_in": w(ks[2], (DEPTH, D_MODEL, D_IN), D_MODEL),
        "w_a": w(ks[3], (DEPTH, A_OUT, D_MODEL), A_OUT),
        "w_b": w(ks[4], (DEPTH, R_V, D_MODEL), R_V),
        "w_o": w(ks[5], (DEPTH, D_MODEL, D_MODEL), D_MODEL),
        "ffn_norm": gain(ks[6], (DEPTH, D_MODEL)),
        "w_up": w(ks[7], (DEPTH, D_MODEL, D_FF), D_MODEL),
        "w_down": w(ks[8], (DEPTH, D_FF, D_MODEL), D_FF),
        "final_norm": gain(ks[9], (D_MODEL,)),
    }


def reference(x, mix_norm, w_in, w_a, w_b, w_o, ffn_norm, w_up, w_down, final_norm):
    for layer in range(DEPTH):
        x = hybrid_layer(x, mix_norm[layer], w_in[layer], w_a[layer], w_b[layer], w_o[layer],
                         ffn_norm[layer], w_up[layer], w_down[layer])
    return rmsnorm(x, final_norm)
```

```python
import functools

import numpy as np
import jax
import jax.numpy as jnp
from jax import lax
from jax.experimental import pallas as pl
from jax.experimental.pallas import tpu as pltpu

F32 = jnp.float32
BF16 = jnp.bfloat16

D_MODEL = 1024
N_GROUPS = 3
DILATIONS = (1, 4, 16)
A_HEADS = 8
A_HEAD_DIM = 64
A_GROUP_W = A_HEADS * A_HEAD_DIM
A_WIDTH = N_GROUPS * A_GROUP_W
A_OUT = A_GROUP_W
ROT_DIM = A_HEAD_DIM // 4
ROPE_THETA = 500000.0
ATT_BLOCK = 128
NEG_INF = -1e30
R_HEADS = 4
R_KEY_DIM = 128
R_VAL_DIM = 256
R_QK = R_HEADS * R_KEY_DIM
R_V = R_HEADS * R_VAL_DIM
R_CHUNK = 128
R_ROT_BASE = 10000.0
D_FF = 4 * D_MODEL
NORM_EPS = 1e-6
D_IN = 3 * A_WIDTH + 2 * R_QK + 2 * R_V + 2 * D_MODEL

OFF_QA = 0
OFF_KA = A_WIDTH
OFF_VA = 2 * A_WIDTH
OFF_QR = 3 * A_WIDTH
OFF_KR = OFF_QR + R_QK
OFF_VR = OFF_KR + R_QK
OFF_GR = OFF_VR + R_V
OFF_GA = OFF_GR + R_V
OFF_GB = OFF_GA + D_MODEL

LANES = 128
VMEM_LIMIT = 56 * 1024 * 1024

IN_TM = 1024
IN_TN = 512
_N_IN_TILES = D_IN // IN_TN
_ROT_A_TILES = 2 * A_WIDTH // IN_TN
_ROT_R_LO = OFF_QR // IN_TN
_ROT_R_HI = OFF_VR // IN_TN


def _in_proj_body(x_ref, g_ref, w_ref, ca_ref, sa_ref, sb_ref, cr_ref, sr_ref, o_ref, h_ref):
    j = pl.program_id(1)

    @pl.when(j == 0)
    def _norm():
        x = x_ref[...]
        ms = jnp.mean(x * x, axis=-1, keepdims=True)
        h_ref[...] = (x * lax.rsqrt(ms + NORM_EPS) * g_ref[...]).astype(BF16)

    acc = jnp.dot(h_ref[...], w_ref[...], preferred_element_type=F32)
    is_a = j < _ROT_A_TILES
    is_r = (j >= _ROT_R_LO) & (j < _ROT_R_HI)

    @pl.when(is_a)
    def _rot_a():
        for c in range(IN_TN // LANES):
            t = acc[:, c * LANES:(c + 1) * LANES]
            o_ref[:, c * LANES:(c + 1) * LANES] = (
                t * ca_ref[...]
                + pltpu.roll(t, LANES - ROT_DIM // 2, 1) * sa_ref[...]
                + pltpu.roll(t, ROT_DIM // 2, 1) * sb_ref[...])

    @pl.when(is_r)
    def _rot_r():
        for c in range(IN_TN // LANES):
            t = acc[:, c * LANES:(c + 1) * LANES]
            o_ref[:, c * LANES:(c + 1) * LANES] = (
                t * cr_ref[...] + pltpu.roll(t, R_KEY_DIM // 2, 1) * sr_ref[...])

    @pl.when(jnp.logical_not(is_a | is_r))
    def _plain():
        o_ref[...] = acc


def _rotary_tables(seq):
    pos = jnp.arange(seq, dtype=F32)[:, None]
    half = ROT_DIM // 2
    inv = 1.0 / (ROPE_THETA ** (jnp.arange(half, dtype=F32) / half))
    ang = pos * inv[None, :]
    cos, sin = jnp.cos(ang), jnp.sin(ang)
    zeros = jnp.zeros_like(cos)
    rest = A_HEAD_DIM - ROT_DIM
    ca = jnp.concatenate([cos, cos, jnp.ones((seq, rest), F32)], axis=1)
    sa = jnp.concatenate([-sin, zeros, jnp.zeros((seq, rest), F32)], axis=1)
    sb = jnp.concatenate([zeros, sin, jnp.zeros((seq, rest), F32)], axis=1)
    reps = LANES // A_HEAD_DIM
    ca, sa, sb = (jnp.tile(t, (1, reps)) for t in (ca, sa, sb))
    half_r = R_KEY_DIM // 2
    inv_r = 1.0 / (R_ROT_BASE ** (jnp.arange(half_r, dtype=F32) / half_r))
    ang_r = pos * inv_r[None, :]
    cr = jnp.concatenate([jnp.cos(ang_r), jnp.cos(ang_r)], axis=1)
    sr = jnp.concatenate([-jnp.sin(ang_r), jnp.sin(ang_r)], axis=1)
    return ca, sa, sb, cr, sr


def _in_proj(x2d, gain, w_bf16, tables, seq):
    m = x2d.shape[0]
    tiles_per_seq = seq // IN_TM
    tab_spec = pl.BlockSpec((IN_TM, LANES), lambda i, j: (i % tiles_per_seq, 0))
    return pl.pallas_call(
        _in_proj_body,
        grid=(m // IN_TM, _N_IN_TILES),
        in_specs=[
            pl.BlockSpec((IN_TM, D_MODEL), lambda i, j: (i, 0)),
            pl.BlockSpec((1, D_MODEL), lambda i, j: (0, 0)),
            pl.BlockSpec((D_MODEL, IN_TN), lambda i, j: (0, j)),
            tab_spec, tab_spec, tab_spec, tab_spec, tab_spec,
        ],
        out_specs=pl.BlockSpec((IN_TM, IN_TN), lambda i, j: (i, j)),
        out_shape=jax.ShapeDtypeStruct((m, D_IN), F32),
        scratch_shapes=[pltpu.VMEM((IN_TM, D_MODEL), BF16)],
        compiler_params=pltpu.CompilerParams(
            dimension_semantics=("parallel", "arbitrary"), vmem_limit_bytes=VMEM_LIMIT),
    )(x2d, gain, w_bf16, *tables)


A_SPAN = ATT_BLOCK * max(DILATIONS)
A_UNITS = A_SPAN // ATT_BLOCK
A_HEADS_PER_STEP = LANES // A_HEAD_DIM


def _attn_body(*refs):
    q_in = refs[0:3]
    kc_in = refs[3:6]
    kp_in = refs[6:9]
    vc_in = refs[9:12]
    vp_in = refs[12:15]
    o_ref = refs[15]
    qs, ks1, ks2, ks3, vs1, vs2, vs3, ro, rl, bias = refs[16:]
    ks = (ks1, ks2, ks3)
    vs = (vs1, vs2, vs3)
    jt = pl.program_id(1)

    qi = lax.broadcasted_iota(jnp.int32, (ATT_BLOCK, 2 * ATT_BLOCK), 0)
    kj = lax.broadcasted_iota(jnp.int32, (ATT_BLOCK, 2 * ATT_BLOCK), 1)
    dist = qi + ATT_BLOCK - kj
    band = (dist >= 0) & (dist <= ATT_BLOCK)
    bias[0] = jnp.where(band & (kj >= ATT_BLOCK), 0.0, NEG_INF).astype(F32)
    bias[1] = jnp.where(band, 0.0, NEG_INF).astype(F32)

    scale = A_HEAD_DIM ** -0.5
    for g, d in enumerate(DILATIONS):
        span = ATT_BLOCK * d
        if d == 1:
            qs[g] = (q_in[g][...] * scale).astype(BF16)
            ks[g][0:span] = kp_in[g][...].astype(BF16)
            vs[g][0:span] = vp_in[g][...].astype(BF16)
            ks[g][span:] = kc_in[g][...].astype(BF16)
            vs[g][span:] = vc_in[g][...].astype(BF16)
            continue
        for r in range(d):
            rows = pl.ds(r, ATT_BLOCK, stride=d)
            ks[g][r * ATT_BLOCK:(r + 1) * ATT_BLOCK] = kp_in[g][rows, :].astype(BF16)
            vs[g][r * ATT_BLOCK:(r + 1) * ATT_BLOCK] = vp_in[g][rows, :].astype(BF16)
        for ss in range(A_SPAN // span):
            for r in range(d):
                u = ss * d + r
                rows = pl.ds(ss * span + r, ATT_BLOCK, stride=d)
                dst = slice(u * ATT_BLOCK, (u + 1) * ATT_BLOCK)
                dst_c = slice(span + u * ATT_BLOCK, span + (u + 1) * ATT_BLOCK)
                qs[g, dst] = (q_in[g][rows, :] * scale).astype(BF16)
                ks[g][dst_c] = kc_in[g][rows, :].astype(BF16)
                vs[g][dst_c] = vc_in[g][rows, :].astype(BF16)

    lane = lax.broadcasted_iota(jnp.int32, (ATT_BLOCK, LANES), 1)
    ones = jnp.ones((2 * ATT_BLOCK, LANES), BF16)

    for g, d in enumerate(DILATIONS):
        span = ATT_BLOCK * d

        def unit(u, carry, g=g, d=d, span=span):
            base = pl.multiple_of(u * ATT_BLOCK, ATT_BLOCK)
            q = qs[g, pl.ds(base, ATT_BLOCK), :]
            k2 = jnp.concatenate([ks[g][pl.ds(base, ATT_BLOCK), :],
                                  ks[g][pl.ds(base + span, ATT_BLOCK), :]], axis=0)
            v2 = jnp.concatenate([vs[g][pl.ds(base, ATT_BLOCK), :],
                                  vs[g][pl.ds(base + span, ATT_BLOCK), :]], axis=0)
            v2e = jnp.concatenate([v2, ones], axis=1)
            has_prev = jnp.logical_or(jt > 0, u >= d).astype(jnp.int32)
            b = bias[has_prev]
            outs, lses = [], []
            for h in range(A_HEADS_PER_STEP):
                in_head = (lane >= h * A_HEAD_DIM) & (lane < (h + 1) * A_HEAD_DIM)
                qm = jnp.where(in_head, q, jnp.zeros_like(q))
                s = lax.dot_general(qm, k2, (((1,), (1,)), ((), ())),
                                    preferred_element_type=F32) + b
                m = jnp.max(s, axis=1, keepdims=True)
                p = jnp.exp(s - m)
                pv = jnp.dot(p.astype(BF16), v2e, preferred_element_type=F32)
                den = pv[:, LANES:]
                outs.append(pv[:, :LANES] / den)
                lses.append(m + jnp.log(den))
            first = lane < A_HEAD_DIM
            ro[g, pl.ds(base, ATT_BLOCK), :] = jnp.where(first, outs[0], outs[1])
            rl[g, pl.ds(base, ATT_BLOCK), :] = jnp.where(first, lses[0], lses[1])
            return carry

        lax.fori_loop(0, A_UNITS, unit, 0)

    d2, d3 = DILATIONS[1], DILATIONS[2]
    span2 = ATT_BLOCK * d2
    per_ss2 = ATT_BLOCK * d2 // d3
    for r3 in range(d3):
        blk = slice(r3 * ATT_BLOCK, (r3 + 1) * ATT_BLOCK)
        tok = pl.ds(r3, ATT_BLOCK, stride=d3)
        r2, off = r3 % d2, r3 // d2

        def from_g2(buf):
            return jnp.concatenate(
                [buf[1, pl.ds(ss2 * span2 + r2 * ATT_BLOCK + off, per_ss2, stride=d3 // d2), :]
                 for ss2 in range(A_SPAN // span2)], axis=0)

        o1, l1 = ro[0, tok, :], rl[0, tok, :]
        o2, l2 = from_g2(ro), from_g2(rl)
        o3, l3 = ro[2, blk, :], rl[2, blk, :]
        mx = jnp.maximum(jnp.maximum(l1, l2), l3)
        e1, e2, e3 = jnp.exp(l1 - mx), jnp.exp(l2 - mx), jnp.exp(l3 - mx)
        o_ref[tok, :] = (e1 * o1 + e2 * o2 + e3 * o3) / (e1 + e2 + e3)


def _attention_a(proj, batch, seq):
    proj3 = proj.reshape(batch, seq, D_IN)
    cols_per_group = A_GROUP_W // LANES

    def col(off, g):
        return off // LANES + g * cols_per_group

    def cur_spec(off, g):
        return pl.BlockSpec((None, A_SPAN, LANES), lambda b, j, hp: (b, j, col(off, g) + hp))

    def prev_spec(off, g):
        span = ATT_BLOCK * DILATIONS[g]
        per = A_SPAN // span
        return pl.BlockSpec((None, span, LANES),
                            lambda b, j, hp: (b, jnp.maximum(j * per - 1, 0), col(off, g) + hp))

    groups = range(N_GROUPS)
    in_specs = ([cur_spec(OFF_QA, g) for g in groups] + [cur_spec(OFF_KA, g) for g in groups]
                + [prev_spec(OFF_KA, g) for g in groups] + [cur_spec(OFF_VA, g) for g in groups]
                + [prev_spec(OFF_VA, g) for g in groups])
    kv_scratch = [pltpu.VMEM((ATT_BLOCK * d + A_SPAN, LANES), BF16) for d in DILATIONS]
    return pl.pallas_call(
        _attn_body,
        grid=(batch, seq // A_SPAN, A_GROUP_W // LANES),
        in_specs=in_specs,
        out_specs=pl.BlockSpec((None, A_SPAN, LANES), lambda b, j, hp: (b, j, hp)),
        out_shape=jax.ShapeDtypeStruct((batch, seq, A_OUT), F32),
        scratch_shapes=([pltpu.VMEM((N_GROUPS, A_SPAN, LANES), BF16)] + kv_scratch + kv_scratch
                        + [pltpu.VMEM((N_GROUPS, A_SPAN, LANES), F32),
                           pltpu.VMEM((N_GROUPS, A_SPAN, LANES), F32),
                           pltpu.VMEM((2, ATT_BLOCK, 2 * ATT_BLOCK), F32)]),
        compiler_params=pltpu.CompilerParams(
            dimension_semantics=("parallel", "parallel", "parallel"), vmem_limit_bytes=VMEM_LIMIT),
    )(*([proj3] * 15))


RET_TB = 512
_LOG_GAMMA = [float(v) for v in np.log1p(-(2.0 ** (-5.0 - np.arange(R_HEADS)))).astype(np.float32)]


def _ret_body(q_ref, k_ref, v_ref, g_ref, o_ref, st_ref):
    h = pl.program_id(1)
    t = pl.program_id(2)

    @pl.when(t == 0)
    def _reset():
        st_ref[...] = jnp.zeros_like(st_ref)

    lg = jnp.float32(_LOG_GAMMA[R_HEADS - 1])
    for hh in range(R_HEADS - 2, -1, -1):
        lg = jnp.where(h == hh, jnp.float32(_LOG_GAMMA[hh]), lg)
    row = lax.broadcasted_iota(jnp.int32, (R_CHUNK, R_CHUNK), 0).astype(F32)
    colv = lax.broadcasted_iota(jnp.int32, (R_CHUNK, R_CHUNK), 1).astype(F32)
    diff = row - colv
    decay = jnp.where(diff >= 0, jnp.exp(diff * lg), 0.0)
    zeta = jnp.exp((R_CHUNK - 1 - row) * lg)
    xi = jnp.exp((row + 1.0) * lg)
    chunk_decay = jnp.exp(jnp.full((R_KEY_DIM, R_VAL_DIM), float(R_CHUNK), F32) * lg)
    kscale = R_KEY_DIM ** -0.5

    for c in range(RET_TB // R_CHUNK):
        sl = slice(c * R_CHUNK, (c + 1) * R_CHUNK)
        q = q_ref[sl, :]
        k = k_ref[sl, :] * kscale
        v = v_ref[sl, :].astype(BF16)
        s = lax.dot_general(q.astype(BF16), k.astype(BF16), (((1,), (1,)), ((), ())),
                            preferred_element_type=F32) * decay
        inner = jnp.dot(s.astype(BF16), v, preferred_element_type=F32)
        state = st_ref[...]
        cross = jnp.dot((q * xi).astype(BF16), state.astype(BF16), preferred_element_type=F32)
        kz_t = (k * zeta).T.astype(BF16)
        st_ref[...] = state * chunk_decay + jnp.dot(kz_t, v, preferred_element_type=F32)
        out = inner + cross
        ms = jnp.mean(out * out, axis=-1, keepdims=True)
        gate = g_ref[sl, :]
        o_ref[sl, :] = (gate * jax.nn.sigmoid(gate) * (out * lax.rsqrt(ms + NORM_EPS))).astype(BF16)


def _retention(proj, batch, seq):
    proj3 = proj.reshape(batch, seq, D_IN)
    return pl.pallas_call(
        _ret_body,
        grid=(batch, R_HEADS, seq // RET_TB),
        in_specs=[
            pl.BlockSpec((None, RET_TB, R_KEY_DIM), lambda b, h, t: (b, t, OFF_QR // R_KEY_DIM + h)),
            pl.BlockSpec((None, RET_TB, R_KEY_DIM), lambda b, h, t: (b, t, OFF_KR // R_KEY_DIM + h)),
            pl.BlockSpec((None, RET_TB, R_VAL_DIM), lambda b, h, t: (b, t, OFF_VR // R_VAL_DIM + h)),
            pl.BlockSpec((None, RET_TB, R_VAL_DIM), lambda b, h, t: (b, t, OFF_GR // R_VAL_DIM + h)),
        ],
        out_specs=pl.BlockSpec((None, RET_TB, R_VAL_DIM), lambda b, h, t: (b, t, h)),
        out_shape=jax.ShapeDtypeStruct((batch, seq, R_V), BF16),
        scratch_shapes=[pltpu.VMEM((R_KEY_DIM, R_VAL_DIM), F32)],
        compiler_params=pltpu.CompilerParams(
            dimension_semantics=("parallel", "parallel", "arbitrary"), vmem_limit_bytes=VMEM_LIMIT),
    )(proj3, proj3, proj3, proj3)


MERGE_TM = 512
GATE_TN = 512


def _merge_body(x_ref, ya_ref, rb_ref, ga0, ga1, gb0, gb1, wa_ref, wb_ref, wo_ref, o_ref):
    ya = jnp.dot(ya_ref[...].astype(BF16), wa_ref[...], preferred_element_type=F32)
    yb = jnp.dot(rb_ref[...], wb_ref[...], preferred_element_type=F32)
    ga = jnp.concatenate([ga0[...], ga1[...]], axis=1)
    gb = jnp.concatenate([gb0[...], gb1[...]], axis=1)
    merged = jax.nn.sigmoid(ga) * ya + jax.nn.sigmoid(gb) * yb
    o_ref[...] = x_ref[...] + jnp.dot(merged.astype(BF16), wo_ref[...], preferred_element_type=F32)


def _merge(x2d, ya2d, rb2d, proj, wa, wb, wo):
    m = x2d.shape[0]

    def gate_spec(off, part):
        return pl.BlockSpec((MERGE_TM, GATE_TN), lambda i: (i, off // GATE_TN + part))

    def full(shape):
        return pl.BlockSpec(shape, lambda i: (0, 0))

    return pl.pallas_call(
        _merge_body,
        grid=(m // MERGE_TM,),
        in_specs=[
            pl.BlockSpec((MERGE_TM, D_MODEL), lambda i: (i, 0)),
            pl.BlockSpec((MERGE_TM, A_OUT), lambda i: (i, 0)),
            pl.BlockSpec((MERGE_TM, R_V), lambda i: (i, 0)),
            gate_spec(OFF_GA, 0), gate_spec(OFF_GA, 1), gate_spec(OFF_GB, 0), gate_spec(OFF_GB, 1),
            full((A_OUT, D_MODEL)), full((R_V, D_MODEL)), full((D_MODEL, D_MODEL)),
        ],
        out_specs=pl.BlockSpec((MERGE_TM, D_MODEL), lambda i: (i, 0)),
        out_shape=jax.ShapeDtypeStruct((m, D_MODEL), F32),
        compiler_params=pltpu.CompilerParams(
            dimension_semantics=("parallel",), vmem_limit_bytes=VMEM_LIMIT),
    )(x2d, ya2d, rb2d, proj, proj, proj, proj, wa, wb, wo)


FFN_TM = 1024
FFN_TF = 1024


def _rms(x, gain):
    ms = jnp.mean(x * x, axis=-1, keepdims=True)
    return x * lax.rsqrt(ms + NORM_EPS) * gain


def _ffn_body(x_ref, g_ref, wu_ref, wd_ref, fg_ref, o_ref, h_ref, *, final):
    c = pl.program_id(1)

    @pl.when(c == 0)
    def _start():
        x = x_ref[...]
        h_ref[...] = _rms(x, g_ref[...]).astype(BF16)
        o_ref[...] = x

    up = jnp.dot(h_ref[...], wu_ref[...], preferred_element_type=F32)
    up = jnp.square(jnp.maximum(up, 0.0)).astype(BF16)
    o_ref[...] += jnp.dot(up, wd_ref[...], preferred_element_type=F32)

    if final:
        @pl.when(c == pl.num_programs(1) - 1)
        def _final_norm():
            o_ref[...] = _rms(o_ref[...], fg_ref[...])


def _ffn(x2d, gain, wu, wd, final_gain, final):
    m = x2d.shape[0]
    return pl.pallas_call(
        functools.partial(_ffn_body, final=final),
        grid=(m // FFN_TM, D_FF // FFN_TF),
        in_specs=[
            pl.BlockSpec((FFN_TM, D_MODEL), lambda i, c: (i, 0)),
            pl.BlockSpec((1, D_MODEL), lambda i, c: (0, 0)),
            pl.BlockSpec((D_MODEL, FFN_TF), lambda i, c: (0, c)),
            pl.BlockSpec((FFN_TF, D_MODEL), lambda i, c: (c, 0)),
            pl.BlockSpec((1, D_MODEL), lambda i, c: (0, 0)),
        ],
        out_specs=pl.BlockSpec((FFN_TM, D_MODEL), lambda i, c: (i, 0)),
        out_shape=jax.ShapeDtypeStruct((m, D_MODEL), F32),
        scratch_shapes=[pltpu.VMEM((FFN_TM, D_MODEL), BF16)],
        compiler_params=pltpu.CompilerParams(
            dimension_semantics=("parallel", "arbitrary"), vmem_limit_bytes=VMEM_LIMIT),
    )(x2d, gain, wu, wd, final_gain)


def kernel(x, mix_norm, w_in, w_a, w_b, w_o, ffn_norm, w_up, w_down, final_norm):
    batch, seq, _ = x.shape
    depth = w_in.shape[0]
    assert seq % A_SPAN == 0 and seq % IN_TM == 0 and seq % RET_TB == 0
    tables = _rotary_tables(seq)
    final_gain = final_norm.reshape(1, D_MODEL)
    x2d = x.reshape(batch * seq, D_MODEL)
    for layer in range(depth):
        proj = _in_proj(x2d, mix_norm[layer].reshape(1, D_MODEL), w_in[layer].astype(BF16), tables, seq)
        ya = _attention_a(proj, batch, seq).reshape(batch * seq, A_OUT)
        rb = _retention(proj, batch, seq).reshape(batch * seq, R_V)
        x2d = _merge(x2d, ya, rb, proj, w_a[layer].astype(BF16), w_b[layer].astype(BF16),
                     w_o[layer].astype(BF16))
        x2d = _ffn(x2d, ffn_norm[layer].reshape(1, D_MODEL), w_up[layer].astype(BF16),
                   w_down[layer].astype(BF16), final_gain, final=(layer == depth - 1))
    return x2d.reshape(batch, seq, D_MODEL)
```

```python
import functools

import numpy as np
import jax
import jax.numpy as jnp
from jax import lax
from jax.experimental import pallas as pl
from jax.experimental.pallas import tpu as pltpu

F32 = jnp.float32
BF16 = jnp.bfloat16

D_MODEL = 1024
N_GROUPS = 3
DILATIONS = (1, 4, 16)
A_HEADS = 8
A_HEAD_DIM = 64
A_GROUP_W = A_HEADS * A_HEAD_DIM
A_WIDTH = N_GROUPS * A_GROUP_W
A_OUT = A_GROUP_W
ROT_DIM = A_HEAD_DIM // 4
ROPE_THETA = 500000.0
ATT_BLOCK = 128
NEG_INF = -1e30
R_HEADS = 4
R_KEY_DIM = 128
R_VAL_DIM = 256
R_QK = R_HEADS * R_KEY_DIM
R_V = R_HEADS * R_VAL_DIM
R_CHUNK = 128
R_ROT_BASE = 10000.0
D_FF = 4 * D_MODEL
NORM_EPS = 1e-6

OFF_QA = 0
OFF_KA = A_WIDTH
OFF_VA = 2 * A_WIDTH
OFF_QR = 3 * A_WIDTH
OFF_KR = OFF_QR + R_QK
OFF_VR = OFF_KR + R_QK
OFF_GR = OFF_VR + R_V
OFF_GA = OFF_GR + R_V
OFF_GB = OFF_GA + D_MODEL

LANES = 128
VMEM_LIMIT = 56 * 1024 * 1024

A_SPAN = ATT_BLOCK * max(DILATIONS)
A_UNITS = A_SPAN // ATT_BLOCK

IN_TM = A_SPAN
IN_TN = 512
IN_MC = 512
X_CHUNKS = D_MODEL // LANES

P_GA = 0
P_GB = P_GA + D_MODEL
P_VR = P_GB + D_MODEL
P_GR = P_VR + R_V
P_QR = P_GR + R_V
P_KR = P_QR + R_QK
P_QKV = tuple(P_KR + R_QK + g * 3 * A_GROUP_W for g in range(N_GROUPS))
P_WIDTH = P_QKV[-1] + 3 * A_GROUP_W

_TAB_R = N_GROUPS
_TAB_NONE = N_GROUPS + 1
TILE_TABLE = ((_TAB_NONE,) * (P_QR // IN_TN) + (_TAB_R,) * (2 * R_QK // IN_TN)
              + tuple(t for g in range(N_GROUPS) for t in (g, g, _TAB_NONE)))
TILE_ORDER = (0,) * (P_QKV[0] // IN_TN) + tuple(g for g in range(N_GROUPS) for _ in range(3))


def _rms(x, gain):
    ms = jnp.mean(x * x, axis=-1, keepdims=True)
    return x * lax.rsqrt(ms + NORM_EPS) * gain


def _step_lookup(j, values):
    out = values[-1]
    for t in range(len(values) - 2, -1, -1):
        if values[t] != values[t + 1]:
            out = jnp.where(j <= t, values[t], out)
    return out


def _in_proj_body(*refs):
    x_refs = refs[:X_CHUNKS]
    g_ref, w_ref, cos_ref, sin_ref, o_ref, h_ref, inv_ref = refs[X_CHUNKS:]
    j = pl.program_id(1)

    @pl.when(j == 0)
    def _norm():
        gains = [g_ref[:, c * LANES:(c + 1) * LANES] for c in range(X_CHUNKS)]
        for u in range(A_UNITS):
            rows = pl.ds(u * ATT_BLOCK, ATT_BLOCK)
            xc = [x_refs[c][rows, :] for c in range(X_CHUNKS)]
            sq = xc[0] * xc[0]
            for c in range(1, X_CHUNKS):
                sq = sq + xc[c] * xc[c]
            inv = lax.rsqrt(jnp.sum(sq, axis=-1, keepdims=True) * (1.0 / D_MODEL) + NORM_EPS)
            inv = jnp.broadcast_to(inv, (ATT_BLOCK, LANES))
            inv_ref[rows, :] = inv
            for c in range(X_CHUNKS):
                h_ref[0, rows, c * LANES:(c + 1) * LANES] = (xc[c] * inv * gains[c]).astype(BF16)
        for g in range(1, N_GROUPS):
            dil = DILATIONS[g]
            span = ATT_BLOCK * dil
            for ss in range(IN_TM // span):
                for r in range(dil):
                    u = ss * dil + r
                    rows = pl.ds(ss * span + r, ATT_BLOCK, stride=dil)
                    dst = pl.ds(u * ATT_BLOCK, ATT_BLOCK)
                    inv = inv_ref[rows, :]
                    for c in range(X_CHUNKS):
                        h_ref[g, dst, c * LANES:(c + 1) * LANES] = (
                            x_refs[c][rows, :] * inv * gains[c]).astype(BF16)

    order = _step_lookup(j, TILE_ORDER)
    for mc in range(IN_TM // IN_MC):
        rows = pl.ds(mc * IN_MC, IN_MC)
        acc = jnp.dot(h_ref[order, rows, :], w_ref[...], preferred_element_type=F32)
        for c in range(IN_TN // LANES):
            t = acc[:, c * LANES:(c + 1) * LANES]
            o_ref[rows, c * LANES:(c + 1) * LANES] = (
                t * cos_ref[rows, :] + pltpu.roll(t, LANES // 2, 1) * sin_ref[rows, :]).astype(BF16)


def _pair_layout(t):
    half = ROT_DIM // 2
    a, b = t[..., 0, :], t[..., 1, :]
    return jnp.concatenate([a[..., :half], b[..., :half], a[..., ROT_DIM:],
                            a[..., half:ROT_DIM], b[..., half:ROT_DIM], b[..., ROT_DIM:]], axis=-1)


def _arrange_w_in(w):
    rows = w.shape[0]

    def pairs(cols):
        return _pair_layout(cols.reshape(rows, N_GROUPS, A_GROUP_W // LANES, 2, A_HEAD_DIM)).reshape(
            rows, N_GROUPS, A_GROUP_W)

    qa = pairs(w[:, OFF_QA:OFF_KA])
    ka = pairs(w[:, OFF_KA:OFF_VA])
    va = w[:, OFF_VA:OFF_QR].reshape(rows, N_GROUPS, A_GROUP_W)
    parts = [w[:, OFF_GA:OFF_GB], w[:, OFF_GB:], w[:, OFF_VR:OFF_GR], w[:, OFF_GR:OFF_GA],
             w[:, OFF_QR:OFF_KR], w[:, OFF_KR:OFF_VR]]
    for g in range(N_GROUPS):
        parts += [qa[:, g], ka[:, g], va[:, g]]
    return jnp.concatenate(parts, axis=1).astype(BF16)


def _rotary_tables(seq):
    half = ROT_DIM // 2
    inv = 1.0 / (ROPE_THETA ** (jnp.arange(half, dtype=F32) / half))
    rest = A_HEAD_DIM - ROT_DIM
    ones, zeros = jnp.ones((seq, rest), F32), jnp.zeros((seq, rest), F32)
    cos_tabs, sin_tabs = [], []
    for dil in DILATIONS:
        span = ATT_BLOCK * dil
        pos = jnp.arange(seq, dtype=F32).reshape(seq // span, ATT_BLOCK, dil).transpose(0, 2, 1).reshape(seq, 1)
        ang = pos * inv[None, :]
        cos, sin = jnp.cos(ang), jnp.sin(ang)
        cos_tabs.append(jnp.concatenate([cos, cos, ones, cos, cos, ones], axis=1))
        sin_tabs.append(jnp.concatenate([-sin, -sin, zeros, sin, sin, zeros], axis=1))
    half_r = R_KEY_DIM // 2
    inv_r = 1.0 / (R_ROT_BASE ** (jnp.arange(half_r, dtype=F32) / half_r))
    ang_r = jnp.arange(seq, dtype=F32)[:, None] * inv_r[None, :]
    cos_tabs += [jnp.concatenate([jnp.cos(ang_r), jnp.cos(ang_r)], axis=1), jnp.ones((seq, LANES), F32)]
    sin_tabs += [jnp.concatenate([-jnp.sin(ang_r), jnp.sin(ang_r)], axis=1), jnp.zeros((seq, LANES), F32)]
    return jnp.stack(cos_tabs), jnp.stack(sin_tabs)


def _in_proj(x2d, gain, w_bf16, tables, seq):
    m = x2d.shape[0]
    tiles_per_seq = seq // IN_TM
    tab_spec = pl.BlockSpec((None, IN_TM, LANES),
                            lambda i, j: (_step_lookup(j, TILE_TABLE), i % tiles_per_seq, 0))
    x_specs = [pl.BlockSpec((IN_TM, LANES), lambda i, j, c=c: (i, c)) for c in range(X_CHUNKS)]
    return pl.pallas_call(
        _in_proj_body,
        grid=(m // IN_TM, P_WIDTH // IN_TN),
        in_specs=x_specs + [
            pl.BlockSpec((1, D_MODEL), lambda i, j: (0, 0)),
            pl.BlockSpec((D_MODEL, IN_TN), lambda i, j: (0, j)),
            tab_spec, tab_spec,
        ],
        out_specs=pl.BlockSpec((IN_TM, IN_TN), lambda i, j: (i, j)),
        out_shape=jax.ShapeDtypeStruct((m, P_WIDTH), BF16),
        scratch_shapes=[pltpu.VMEM((N_GROUPS, IN_TM, D_MODEL), BF16),
                        pltpu.VMEM((IN_TM, LANES), F32)],
        compiler_params=pltpu.CompilerParams(
            dimension_semantics=("parallel", "arbitrary"), vmem_limit_bytes=VMEM_LIMIT),
    )(*([x2d] * X_CHUNKS), gain, w_bf16, *tables)


A_HEADS_PER_STEP = LANES // A_HEAD_DIM
A_UNROLL = 4


def _attn_body(*refs):
    q_in = refs[0:3]
    kc_in = refs[3:6]
    kp_in = refs[6:9]
    vc_in = refs[9:12]
    vp_in = refs[12:15]
    o_ref = refs[15]
    ro, rl, bias = refs[16:]
    jt = pl.program_id(1)

    qi = lax.broadcasted_iota(jnp.int32, (ATT_BLOCK, 2 * ATT_BLOCK), 0)
    kj = lax.broadcasted_iota(jnp.int32, (ATT_BLOCK, 2 * ATT_BLOCK), 1)
    dist = qi + ATT_BLOCK - kj
    band = (dist >= 0) & (dist <= ATT_BLOCK)
    bias[0] = jnp.where(band & (kj >= ATT_BLOCK), 0.0, NEG_INF).astype(F32)
    bias[1] = jnp.where(band, 0.0, NEG_INF).astype(F32)

    lane = lax.broadcasted_iota(jnp.int32, (ATT_BLOCK, LANES), 1)
    ones = jnp.ones((2 * ATT_BLOCK, LANES), BF16)
    half = ROT_DIM // 2
    first_head = (lane < half) | ((lane >= ROT_DIM) & (lane < A_HEAD_DIM + half))
    head_lanes = (first_head, jnp.logical_not(first_head))
    scale = A_HEAD_DIM ** -0.5

    for g, d in enumerate(DILATIONS):

        def unit(u, carry, g=g, d=d, span_start=False):
            base = pl.multiple_of(u * ATT_BLOCK, ATT_BLOCK)
            if span_start:
                k_prev, v_prev, pbase = kp_in[g], vp_in[g], base
                b = bias[jnp.minimum(jt, 1)]
            else:
                k_prev, v_prev = kc_in[g], vc_in[g]
                pbase = pl.multiple_of((u - d) * ATT_BLOCK, ATT_BLOCK)
                b = bias[1]
            q = q_in[g][pl.ds(base, ATT_BLOCK), :] * scale
            k2 = jnp.concatenate([k_prev[pl.ds(pbase, ATT_BLOCK), :],
                                  kc_in[g][pl.ds(base, ATT_BLOCK), :]], axis=0)
            v2 = jnp.concatenate([v_prev[pl.ds(pbase, ATT_BLOCK), :],
                                  vc_in[g][pl.ds(base, ATT_BLOCK), :]], axis=0)
            v2e = jnp.concatenate([v2, ones], axis=1)
            outs, lses = [], []
            for h in range(A_HEADS_PER_STEP):
                qm = jnp.where(head_lanes[h], q, jnp.zeros_like(q))
                s = lax.dot_general(qm, k2, (((1,), (1,)), ((), ())),
                                    preferred_element_type=F32) + b
                m = jnp.max(s, axis=1, keepdims=True)
                p = jnp.exp(s - m)
                pv = jnp.dot(p.astype(BF16), v2e, preferred_element_type=F32)
                den = pv[:, LANES:]
                outs.append(pv[:, :LANES] / den)
                lses.append(m + jnp.log(den))
            first = lane < A_HEAD_DIM
            ro[g, pl.ds(base, ATT_BLOCK), :] = jnp.where(first, outs[0], outs[1])
            rl[g, pl.ds(base, ATT_BLOCK), :] = jnp.where(first, lses[0], lses[1])
            return carry

        lax.fori_loop(0, d, functools.partial(unit, span_start=True), 0, unroll=min(d, A_UNROLL))
        if d < A_UNITS:
            lax.fori_loop(d, A_UNITS, unit, 0, unroll=A_UNROLL)

    d2, d3 = DILATIONS[1], DILATIONS[2]
    span2 = ATT_BLOCK * d2
    per_ss2 = ATT_BLOCK * d2 // d3
    for r3 in range(d3):
        blk = slice(r3 * ATT_BLOCK, (r3 + 1) * ATT_BLOCK)
        tok = pl.ds(r3, ATT_BLOCK, stride=d3)
        r2, off = r3 % d2, r3 // d2

        def from_g2(buf):
            return jnp.concatenate(
                [buf[1, pl.ds(ss2 * span2 + r2 * ATT_BLOCK + off, per_ss2, stride=d3 // d2), :]
                 for ss2 in range(A_SPAN // span2)], axis=0)

        o1, l1 = ro[0, tok, :], rl[0, tok, :]
        o2, l2 = from_g2(ro), from_g2(rl)
        o3, l3 = ro[2, blk, :], rl[2, blk, :]
        mx = jnp.maximum(jnp.maximum(l1, l2), l3)
        e1, e2, e3 = jnp.exp(l1 - mx), jnp.exp(l2 - mx), jnp.exp(l3 - mx)
        o_ref[tok, :] = (e1 * o1 + e2 * o2 + e3 * o3) / (e1 + e2 + e3)


def _attention_a(proj, batch, seq):
    p = proj.reshape(batch, seq, P_WIDTH)

    def cur_spec(g, part):
        col = (P_QKV[g] + part * A_GROUP_W) // LANES
        return pl.BlockSpec((None, A_SPAN, LANES), lambda b, j, hp: (b, j, col + hp))

    def prev_spec(g, part):
        col = (P_QKV[g] + part * A_GROUP_W) // LANES
        span = ATT_BLOCK * DILATIONS[g]
        per = A_SPAN // span
        return pl.BlockSpec((None, span, LANES),
                            lambda b, j, hp: (b, jnp.maximum(j * per - 1, 0), col + hp))

    groups = range(N_GROUPS)
    in_specs = ([cur_spec(g, 0) for g in groups] + [cur_spec(g, 1) for g in groups]
                + [prev_spec(g, 1) for g in groups] + [cur_spec(g, 2) for g in groups]
                + [prev_spec(g, 2) for g in groups])
    return pl.pallas_call(
        _attn_body,
        grid=(batch, seq // A_SPAN, A_GROUP_W // LANES),
        in_specs=in_specs,
        out_specs=pl.BlockSpec((None, A_SPAN, LANES), lambda b, j, hp: (b, j, hp)),
        out_shape=jax.ShapeDtypeStruct((batch, seq, A_OUT), F32),
        scratch_shapes=[pltpu.VMEM((N_GROUPS, A_SPAN, LANES), F32),
                        pltpu.VMEM((N_GROUPS, A_SPAN, LANES), F32),
                        pltpu.VMEM((2, ATT_BLOCK, 2 * ATT_BLOCK), F32)],
        compiler_params=pltpu.CompilerParams(
            dimension_semantics=("parallel", "parallel", "parallel"), vmem_limit_bytes=VMEM_LIMIT),
    )(*([p] * 15))


RET_TB = 512
_LOG_GAMMA = [float(v) for v in np.log1p(-(2.0 ** (-5.0 - np.arange(R_HEADS)))).astype(np.float32)]


def _ret_body(q_ref, k_ref, v_ref, g_ref, o_ref, st_ref):
    h = pl.program_id(1)
    t = pl.program_id(2)

    @pl.when(t == 0)
    def _reset():
        st_ref[...] = jnp.zeros_like(st_ref)

    lg = jnp.float32(_LOG_GAMMA[R_HEADS - 1])
    for hh in range(R_HEADS - 2, -1, -1):
        lg = jnp.where(h == hh, jnp.float32(_LOG_GAMMA[hh]), lg)
    row = lax.broadcasted_iota(jnp.int32, (R_CHUNK, R_CHUNK), 0).astype(F32)
    colv = lax.broadcasted_iota(jnp.int32, (R_CHUNK, R_CHUNK), 1).astype(F32)
    diff = row - colv
    decay = jnp.where(diff >= 0, jnp.exp(diff * lg), 0.0)
    zeta = jnp.exp((R_CHUNK - 1 - row) * lg)
    xi = jnp.exp((row + 1.0) * lg)
    chunk_decay = jnp.exp(jnp.full((R_KEY_DIM, R_VAL_DIM), float(R_CHUNK), F32) * lg)
    kscale = R_KEY_DIM ** -0.5

    for c in range(RET_TB // R_CHUNK):
        sl = slice(c * R_CHUNK, (c + 1) * R_CHUNK)
        q = q_ref[sl, :].astype(F32)
        k = k_ref[sl, :].astype(F32) * kscale
        v = v_ref[sl, :]
        s = lax.dot_general(q.astype(BF16), k.astype(BF16), (((1,), (1,)), ((), ())),
                            preferred_element_type=F32) * decay
        inner = jnp.dot(s.astype(BF16), v, preferred_element_type=F32)
        state = st_ref[...]
        cross = jnp.dot((q * xi).astype(BF16), state.astype(BF16), preferred_element_type=F32)
        kz_t = (k * zeta).T.astype(BF16)
        st_ref[...] = state * chunk_decay + jnp.dot(kz_t, v, preferred_element_type=F32)
        out = inner + cross
        ms = jnp.mean(out * out, axis=-1, keepdims=True)
        gate = g_ref[sl, :].astype(F32)
        o_ref[sl, :] = (gate * jax.nn.sigmoid(gate) * (out * lax.rsqrt(ms + NORM_EPS))).astype(BF16)


def _retention(proj, batch, seq):
    p = proj.reshape(batch, seq, P_WIDTH)
    return pl.pallas_call(
        _ret_body,
        grid=(batch, R_HEADS, seq // RET_TB),
        in_specs=[
            pl.BlockSpec((None, RET_TB, R_KEY_DIM), lambda b, h, t: (b, t, P_QR // R_KEY_DIM + h)),
            pl.BlockSpec((None, RET_TB, R_KEY_DIM), lambda b, h, t: (b, t, P_KR // R_KEY_DIM + h)),
            pl.BlockSpec((None, RET_TB, R_VAL_DIM), lambda b, h, t: (b, t, P_VR // R_VAL_DIM + h)),
            pl.BlockSpec((None, RET_TB, R_VAL_DIM), lambda b, h, t: (b, t, P_GR // R_VAL_DIM + h)),
        ],
        out_specs=pl.BlockSpec((None, RET_TB, R_VAL_DIM), lambda b, h, t: (b, t, h)),
        out_shape=jax.ShapeDtypeStruct((batch, seq, R_V), BF16),
        scratch_shapes=[pltpu.VMEM((R_KEY_DIM, R_VAL_DIM), F32)],
        compiler_params=pltpu.CompilerParams(
            dimension_semantics=("parallel", "parallel", "arbitrary"), vmem_limit_bytes=VMEM_LIMIT),
    )(p, p, p, p)


MERGE_TM = 512


def _merge_body(x_ref, ya_ref, rb_ref, ga_ref, gb_ref, wa_ref, wb_ref, wo_ref, o_ref):
    ya = jnp.dot(ya_ref[...].astype(BF16), wa_ref[...], preferred_element_type=F32)
    yb = jnp.dot(rb_ref[...], wb_ref[...], preferred_element_type=F32)
    ga = ga_ref[...].astype(F32)
    gb = gb_ref[...].astype(F32)
    merged = jax.nn.sigmoid(ga) * ya + jax.nn.sigmoid(gb) * yb
    o_ref[...] = x_ref[...] + jnp.dot(merged.astype(BF16), wo_ref[...], preferred_element_type=F32)


def _merge(x2d, ya2d, rb2d, proj, wa, wb, wo):
    m = x2d.shape[0]

    def full(shape):
        return pl.BlockSpec(shape, lambda i: (0, 0))

    return pl.pallas_call(
        _merge_body,
        grid=(m // MERGE_TM,),
        in_specs=[
            pl.BlockSpec((MERGE_TM, D_MODEL), lambda i: (i, 0)),
            pl.BlockSpec((MERGE_TM, A_OUT), lambda i: (i, 0)),
            pl.BlockSpec((MERGE_TM, R_V), lambda i: (i, 0)),
            pl.BlockSpec((MERGE_TM, D_MODEL), lambda i: (i, P_GA // D_MODEL)),
            pl.BlockSpec((MERGE_TM, D_MODEL), lambda i: (i, P_GB // D_MODEL)),
            full((A_OUT, D_MODEL)), full((R_V, D_MODEL)), full((D_MODEL, D_MODEL)),
        ],
        out_specs=pl.BlockSpec((MERGE_TM, D_MODEL), lambda i: (i, 0)),
        out_shape=jax.ShapeDtypeStruct((m, D_MODEL), F32),
        compiler_params=pltpu.CompilerParams(
            dimension_semantics=("parallel",), vmem_limit_bytes=VMEM_LIMIT),
    )(x2d, ya2d, rb2d, proj, proj, wa, wb, wo)


FFN_TM = 1024
FFN_TF = 1024


def _ffn_body(x_ref, g_ref, wu_ref, wd_ref, fg_ref, o_ref, h_ref, *, final):
    c = pl.program_id(1)

    @pl.when(c == 0)
    def _start():
        x = x_ref[...]
        h_ref[...] = _rms(x, g_ref[...]).astype(BF16)
        o_ref[...] = x

    up = jnp.dot(h_ref[...], wu_ref[...], preferred_element_type=F32)
    up = jnp.square(jnp.maximum(up, 0.0)).astype(BF16)
    o_ref[...] += jnp.dot(up, wd_ref[...], preferred_element_type=F32)

    if final:
        @pl.when(c == pl.num_programs(1) - 1)
        def _final_norm():
            o_ref[...] = _rms(o_ref[...], fg_ref[...])


def _ffn(x2d, gain, wu, wd, final_gain, final):
    m = x2d.shape[0]
    return pl.pallas_call(
        functools.partial(_ffn_body, final=final),
        grid=(m // FFN_TM, D_FF // FFN_TF),
        in_specs=[
            pl.BlockSpec((FFN_TM, D_MODEL), lambda i, c: (i, 0)),
            pl.BlockSpec((1, D_MODEL), lambda i, c: (0, 0)),
            pl.BlockSpec((D_MODEL, FFN_TF), lambda i, c: (0, c)),
            pl.BlockSpec((FFN_TF, D_MODEL), lambda i, c: (c, 0)),
            pl.BlockSpec((1, D_MODEL), lambda i, c: (0, 0)),
        ],
        out_specs=pl.BlockSpec((FFN_TM, D_MODEL), lambda i, c: (i, 0)),
        out_shape=jax.ShapeDtypeStruct((m, D_MODEL), F32),
        scratch_shapes=[pltpu.VMEM((FFN_TM, D_MODEL), BF16)],
        compiler_params=pltpu.CompilerParams(
            dimension_semantics=("parallel", "arbitrary"), vmem_limit_bytes=VMEM_LIMIT),
    )(x2d, gain, wu, wd, final_gain)


def kernel(x, mix_norm, w_in, w_a, w_b, w_o, ffn_norm, w_up, w_down, final_norm):
    batch, seq, _ = x.shape
    depth = w_in.shape[0]
    assert seq % A_SPAN == 0 and seq % RET_TB == 0
    tables = _rotary_tables(seq)
    final_gain = final_norm.reshape(1, D_MODEL)
    x2d = x.reshape(batch * seq, D_MODEL)
    for layer in range(depth):
        proj = _in_proj(x2d, mix_norm[layer].reshape(1, D_MODEL), _arrange_w_in(w_in[layer]), tables, seq)
        ya = _attention_a(proj, batch, seq).reshape(batch * seq, A_OUT)
        rb = _retention(proj, batch, seq).reshape(batch * seq, R_V)
        x2d = _merge(x2d, ya, rb, proj, w_a[layer].astype(BF16), w_b[layer].astype(BF16),
                     w_o[layer].astype(BF16))
        x2d = _ffn(x2d, ffn_norm[layer].reshape(1, D_MODEL), w_up[layer].astype(BF16),
                   w_down[layer].astype(BF16), final_gain, final=(layer == depth - 1))
    return x2d.reshape(batch, seq, D_MODEL)
```

```python
import functools

import numpy as np
import jax
import jax.numpy as jnp
from jax import lax
from jax.experimental import pallas as pl
from jax.experimental.pallas import tpu as pltpu

F32 = jnp.float32
BF16 = jnp.bfloat16

D_MODEL = 1024
N_GROUPS = 3
DILATIONS = (1, 4, 16)
A_HEADS = 8
A_HEAD_DIM = 64
A_GROUP_W = A_HEADS * A_HEAD_DIM
A_WIDTH = N_GROUPS * A_GROUP_W
A_OUT = A_GROUP_W
ROT_DIM = A_HEAD_DIM // 4
ROPE_THETA = 500000.0
ATT_BLOCK = 128
NEG_INF = -1e30
R_HEADS = 4
R_KEY_DIM = 128
R_VAL_DIM = 256
R_QK = R_HEADS * R_KEY_DIM
R_V = R_HEADS * R_VAL_DIM
R_CHUNK = 128
R_ROT_BASE = 10000.0
D_FF = 4 * D_MODEL
NORM_EPS = 1e-6

OFF_QA = 0
OFF_KA = A_WIDTH
OFF_VA = 2 * A_WIDTH
OFF_QR = 3 * A_WIDTH
OFF_KR = OFF_QR + R_QK
OFF_VR = OFF_KR + R_QK
OFF_GR = OFF_VR + R_V
OFF_GA = OFF_GR + R_V
OFF_GB = OFF_GA + D_MODEL

LANES = 128
VMEM_LIMIT = 56 * 1024 * 1024

A_SPAN = ATT_BLOCK * max(DILATIONS)
A_UNITS = A_SPAN // ATT_BLOCK

IN_TM = A_SPAN
IN_TN = 512
IN_CHUNKS = (512, 512, 512, 256, 256)
X_CHUNKS = D_MODEL // LANES

P_GA = 0
P_GB = P_GA + D_MODEL
P_VR = P_GB + D_MODEL
P_GR = P_VR + R_V
P_QR = P_GR + R_V
P_KR = P_QR + R_QK
P_QKV = tuple(P_KR + R_QK + g * 3 * A_GROUP_W for g in range(N_GROUPS))
P_WIDTH = P_QKV[-1] + 3 * A_GROUP_W

_TAB_R = 2 * N_GROUPS
_TAB_NONE = 2 * N_GROUPS + 1
TILE_TABLE = ((_TAB_NONE,) * (P_QR // IN_TN) + (_TAB_R,) * (2 * R_QK // IN_TN)
              + tuple(t for g in range(N_GROUPS) for t in (g, N_GROUPS + g, _TAB_NONE)))
LOG2_E = 1.4426950408889634
A_Q_SCALE = A_HEAD_DIM ** -0.5 * LOG2_E
TILE_ORDER = (0,) * (P_QKV[0] // IN_TN) + tuple(g for g in range(N_GROUPS) for _ in range(3))


def _rms(x, gain):
    ms = jnp.mean(x * x, axis=-1, keepdims=True)
    return x * lax.rsqrt(ms + NORM_EPS) * gain


def _step_lookup(j, values):
    out = values[-1]
    for t in range(len(values) - 2, -1, -1):
        if values[t] != values[t + 1]:
            out = jnp.where(j <= t, values[t], out)
    return out


def _in_proj_body(*refs):
    x_refs = refs[:X_CHUNKS]
    g_ref, w_ref, cos_ref, sin_ref, o_ref, h_ref, inv_ref = refs[X_CHUNKS:]
    j = pl.program_id(1)

    @pl.when(j == 0)
    def _norm():
        gains = [g_ref[:, c * LANES:(c + 1) * LANES] for c in range(X_CHUNKS)]
        for u in range(A_UNITS):
            rows = pl.ds(u * ATT_BLOCK, ATT_BLOCK)
            xc = [x_refs[c][rows, :] for c in range(X_CHUNKS)]
            sq = xc[0] * xc[0]
            for c in range(1, X_CHUNKS):
                sq = sq + xc[c] * xc[c]
            inv = lax.rsqrt(jnp.sum(sq, axis=-1, keepdims=True) * (1.0 / D_MODEL) + NORM_EPS)
            inv = jnp.broadcast_to(inv, (ATT_BLOCK, LANES))
            inv_ref[rows, :] = inv
            for c in range(X_CHUNKS):
                h_ref[0, rows, c * LANES:(c + 1) * LANES] = (xc[c] * inv * gains[c]).astype(BF16)
        for g in range(1, N_GROUPS):
            dil = DILATIONS[g]
            span = ATT_BLOCK * dil
            for ss in range(IN_TM // span):
                for r in range(dil):
                    u = ss * dil + r
                    rows = pl.ds(ss * span + r, ATT_BLOCK, stride=dil)
                    dst = pl.ds(u * ATT_BLOCK, ATT_BLOCK)
                    inv = inv_ref[rows, :]
                    for c in range(X_CHUNKS):
                        h_ref[g, dst, c * LANES:(c + 1) * LANES] = (
                            x_refs[c][rows, :] * inv * gains[c]).astype(BF16)

    order = _step_lookup(j, TILE_ORDER)
    starts = np.cumsum((0,) + IN_CHUNKS)
    for start, size in zip(starts, IN_CHUNKS):
        rows = pl.ds(int(start), size)
        acc = jnp.dot(h_ref[order, rows, :], w_ref[...], preferred_element_type=F32)
        for c in range(IN_TN // LANES):
            t = acc[:, c * LANES:(c + 1) * LANES]
            o_ref[rows, c * LANES:(c + 1) * LANES] = (
                t * cos_ref[rows, :] + pltpu.roll(t, LANES // 2, 1) * sin_ref[rows, :]).astype(BF16)


def _pair_layout(t):
    half = ROT_DIM // 2
    a, b = t[..., 0, :], t[..., 1, :]
    return jnp.concatenate([a[..., :half], b[..., :half], a[..., ROT_DIM:],
                            a[..., half:ROT_DIM], b[..., half:ROT_DIM], b[..., ROT_DIM:]], axis=-1)


def _arrange_w_in(w):
    rows = w.shape[0]

    def pairs(cols):
        return _pair_layout(cols.reshape(rows, N_GROUPS, A_GROUP_W // LANES, 2, A_HEAD_DIM)).reshape(
            rows, N_GROUPS, A_GROUP_W)

    qa = pairs(w[:, OFF_QA:OFF_KA])
    ka = pairs(w[:, OFF_KA:OFF_VA])
    va = w[:, OFF_VA:OFF_QR].reshape(rows, N_GROUPS, A_GROUP_W)
    parts = [w[:, OFF_GA:OFF_GB], w[:, OFF_GB:], w[:, OFF_VR:OFF_GR], w[:, OFF_GR:OFF_GA],
             w[:, OFF_QR:OFF_KR], w[:, OFF_KR:OFF_VR]]
    for g in range(N_GROUPS):
        parts += [qa[:, g], ka[:, g], va[:, g]]
    return jnp.concatenate(parts, axis=1).astype(BF16)


def _rotary_tables(seq):
    half = ROT_DIM // 2
    inv = 1.0 / (ROPE_THETA ** (jnp.arange(half, dtype=F32) / half))
    rest = A_HEAD_DIM - ROT_DIM
    ones, zeros = jnp.ones((seq, rest), F32), jnp.zeros((seq, rest), F32)
    cos_tabs, sin_tabs = [], []
    for dil in DILATIONS:
        span = ATT_BLOCK * dil
        pos = jnp.arange(seq, dtype=F32).reshape(seq // span, ATT_BLOCK, dil).transpose(0, 2, 1).reshape(seq, 1)
        ang = pos * inv[None, :]
        cos, sin = jnp.cos(ang), jnp.sin(ang)
        cos_tabs.append(jnp.concatenate([cos, cos, ones, cos, cos, ones], axis=1))
        sin_tabs.append(jnp.concatenate([-sin, -sin, zeros, sin, sin, zeros], axis=1))
    cos_tabs = [t * A_Q_SCALE for t in cos_tabs] + cos_tabs
    sin_tabs = [t * A_Q_SCALE for t in sin_tabs] + sin_tabs
    half_r = R_KEY_DIM // 2
    inv_r = 1.0 / (R_ROT_BASE ** (jnp.arange(half_r, dtype=F32) / half_r))
    ang_r = jnp.arange(seq, dtype=F32)[:, None] * inv_r[None, :]
    cos_tabs += [jnp.concatenate([jnp.cos(ang_r), jnp.cos(ang_r)], axis=1), jnp.ones((seq, LANES), F32)]
    sin_tabs += [jnp.concatenate([-jnp.sin(ang_r), jnp.sin(ang_r)], axis=1), jnp.zeros((seq, LANES), F32)]
    return jnp.stack(cos_tabs), jnp.stack(sin_tabs)


def _in_proj(x2d, gain, w_bf16, tables, seq):
    m = x2d.shape[0]
    tiles_per_seq = seq // IN_TM
    tab_spec = pl.BlockSpec((None, IN_TM, LANES),
                            lambda i, j: (_step_lookup(j, TILE_TABLE), i % tiles_per_seq, 0))
    x_specs = [pl.BlockSpec((IN_TM, LANES), lambda i, j, c=c: (i, c)) for c in range(X_CHUNKS)]
    return pl.pallas_call(
        _in_proj_body,
        grid=(m // IN_TM, P_WIDTH // IN_TN),
        in_specs=x_specs + [
            pl.BlockSpec((1, D_MODEL), lambda i, j: (0, 0)),
            pl.BlockSpec((D_MODEL, IN_TN), lambda i, j: (0, j)),
            tab_spec, tab_spec,
        ],
        out_specs=pl.BlockSpec((IN_TM, IN_TN), lambda i, j: (i, j)),
        out_shape=jax.ShapeDtypeStruct((m, P_WIDTH), BF16),
        scratch_shapes=[pltpu.VMEM((N_GROUPS, IN_TM, D_MODEL), BF16),
                        pltpu.VMEM((IN_TM, LANES), F32)],
        compiler_params=pltpu.CompilerParams(
            dimension_semantics=("parallel", "arbitrary"), vmem_limit_bytes=VMEM_LIMIT),
    )(*([x2d] * X_CHUNKS), gain, w_bf16, *tables)


A_HEADS_PER_STEP = LANES // A_HEAD_DIM


def _attn_body(*refs):
    q_in = refs[0:3]
    kc_in = refs[3:6]
    kp_in = refs[6:9]
    vc_in = refs[9:12]
    vp_in = refs[12:15]
    o_ref = refs[15]
    ro, rl, bias = refs[16:]
    jt = pl.program_id(1)

    qi = lax.broadcasted_iota(jnp.int32, (ATT_BLOCK, 2 * ATT_BLOCK), 0)
    kj = lax.broadcasted_iota(jnp.int32, (ATT_BLOCK, 2 * ATT_BLOCK), 1)
    dist = qi + ATT_BLOCK - kj
    band = (dist >= 0) & (dist <= ATT_BLOCK)
    bias[0] = jnp.where(band & (kj >= ATT_BLOCK), 0.0, NEG_INF).astype(F32)
    bias[1] = jnp.where(band, 0.0, NEG_INF).astype(F32)
    lane = lax.broadcasted_iota(jnp.int32, (ATT_BLOCK, LANES), 1)

    half = ROT_DIM // 2
    first_head = (lane < half) | ((lane >= ROT_DIM) & (lane < A_HEAD_DIM + half))
    head_lanes = (first_head, jnp.logical_not(first_head))
    first = lane < A_HEAD_DIM
    v_lane = lax.broadcasted_iota(jnp.int32, (2 * ATT_BLOCK, LANES), 1)
    v_lanes = (v_lane < A_HEAD_DIM, v_lane >= A_HEAD_DIM)

    def rows_of(u):
        return pl.ds(u * ATT_BLOCK, ATT_BLOCK)

    def scores(u, g, d):
        if u < d:
            k_prev, b = kp_in[g][rows_of(u), :], bias[jnp.minimum(jt, 1)]
        else:
            k_prev, b = kc_in[g][rows_of(u - d), :], bias[1]
        q = q_in[g][rows_of(u), :]
        k2 = jnp.concatenate([k_prev, kc_in[g][rows_of(u), :]], axis=0)
        return [lax.dot_general(jnp.where(head_lanes[h], q, jnp.zeros_like(q)), k2, (((1,), (1,)), ((), ())),
                                preferred_element_type=F32) + b for h in range(A_HEADS_PER_STEP)]

    def finish(u, g, d, scores_u):
        v_prev = vp_in[g][rows_of(u), :] if u < d else vc_in[g][rows_of(u - d), :]
        v2 = jnp.concatenate([v_prev, vc_in[g][rows_of(u), :]], axis=0)
        base = u * ATT_BLOCK
        pvs, ms = [], []
        for h in range(A_HEADS_PER_STEP):
            s = scores_u[h]
            m = jnp.max(s, axis=1, keepdims=True)
            p = jnp.exp2(s - m).astype(BF16)
            vh = jnp.where(v_lanes[h], v2, jnp.ones_like(v2))
            pvs.append(jnp.dot(p, vh, preferred_element_type=F32))
            ms.append(m)
        out = jnp.where(first, pvs[0], pvs[1])
        den = pltpu.roll(jnp.where(first, pvs[1], pvs[0]), A_HEAD_DIM, 1)
        ro[g, pl.ds(base, ATT_BLOCK), :] = out / den
        rl[g, pl.ds(base, ATT_BLOCK), :] = jnp.where(first, ms[0], ms[1]) + jnp.log(den) * LOG2_E

    for g, d in enumerate(DILATIONS):
        pending = scores(0, g, d)
        for u in range(A_UNITS):
            ahead = scores(u + 1, g, d) if u + 1 < A_UNITS else None
            finish(u, g, d, pending)
            pending = ahead

    d2, d3 = DILATIONS[1], DILATIONS[2]
    span2 = ATT_BLOCK * d2
    per_ss2 = ATT_BLOCK * d2 // d3
    for r3 in range(d3):
        blk = slice(r3 * ATT_BLOCK, (r3 + 1) * ATT_BLOCK)
        tok = pl.ds(r3, ATT_BLOCK, stride=d3)
        r2, off = r3 % d2, r3 // d2

        def from_g2(buf):
            return jnp.concatenate(
                [buf[1, pl.ds(ss2 * span2 + r2 * ATT_BLOCK + off, per_ss2, stride=d3 // d2), :]
                 for ss2 in range(A_SPAN // span2)], axis=0)

        o1, l1 = ro[0, tok, :], rl[0, tok, :]
        o2, l2 = from_g2(ro), from_g2(rl)
        o3, l3 = ro[2, blk, :], rl[2, blk, :]
        mx = jnp.maximum(jnp.maximum(l1, l2), l3)
        e1, e2, e3 = jnp.exp2(l1 - mx), jnp.exp2(l2 - mx), jnp.exp2(l3 - mx)
        o_ref[tok, :] = (e1 * o1 + e2 * o2 + e3 * o3) / (e1 + e2 + e3)


def _attention_a(proj, batch, seq):
    p = proj.reshape(batch, seq, P_WIDTH)

    def cur_spec(g, part):
        col = (P_QKV[g] + part * A_GROUP_W) // LANES
        return pl.BlockSpec((None, A_SPAN, LANES), lambda b, j, hp: (b, j, col + hp))

    def prev_spec(g, part):
        col = (P_QKV[g] + part * A_GROUP_W) // LANES
        span = ATT_BLOCK * DILATIONS[g]
        per = A_SPAN // span
        return pl.BlockSpec((None, span, LANES),
                            lambda b, j, hp: (b, jnp.maximum(j * per - 1, 0), col + hp))

    groups = range(N_GROUPS)
    in_specs = ([cur_spec(g, 0) for g in groups] + [cur_spec(g, 1) for g in groups]
                + [prev_spec(g, 1) for g in groups] + [cur_spec(g, 2) for g in groups]
                + [prev_spec(g, 2) for g in groups])
    return pl.pallas_call(
        _attn_body,
        grid=(batch, seq // A_SPAN, A_GROUP_W // LANES),
        in_specs=in_specs,
        out_specs=pl.BlockSpec((None, A_SPAN, LANES), lambda b, j, hp: (b, j, hp)),
        out_shape=jax.ShapeDtypeStruct((batch, seq, A_OUT), F32),
        scratch_shapes=[pltpu.VMEM((N_GROUPS, A_SPAN, LANES), F32),
                        pltpu.VMEM((N_GROUPS, A_SPAN, LANES), F32),
                        pltpu.VMEM((2, ATT_BLOCK, 2 * ATT_BLOCK), F32)],
        compiler_params=pltpu.CompilerParams(
            dimension_semantics=("parallel", "parallel", "parallel"), vmem_limit_bytes=VMEM_LIMIT),
    )(*([p] * 15))


RET_TB = 2048
_LOG_GAMMA = [float(v) for v in np.log1p(-(2.0 ** (-5.0 - np.arange(R_HEADS)))).astype(np.float32)]


def _ret_body(q_ref, k_ref, v_ref, g_ref, o_ref, st_ref):
    h = pl.program_id(1)
    t = pl.program_id(2)

    @pl.when(t == 0)
    def _reset():
        st_ref[...] = jnp.zeros_like(st_ref)

    lg = jnp.float32(_LOG_GAMMA[R_HEADS - 1])
    for hh in range(R_HEADS - 2, -1, -1):
        lg = jnp.where(h == hh, jnp.float32(_LOG_GAMMA[hh]), lg)
    row = lax.broadcasted_iota(jnp.int32, (R_CHUNK, R_CHUNK), 0).astype(F32)
    colv = lax.broadcasted_iota(jnp.int32, (R_CHUNK, R_CHUNK), 1).astype(F32)
    diff = row - colv
    decay = jnp.where(diff >= 0, jnp.exp(diff * lg), 0.0)
    zeta = jnp.exp((R_CHUNK - 1 - row) * lg)
    xi = jnp.exp((row + 1.0) * lg)
    chunk_decay = jnp.exp(jnp.full((R_KEY_DIM, R_VAL_DIM), float(R_CHUNK), F32) * lg)
    kscale = R_KEY_DIM ** -0.5

    for c in range(RET_TB // R_CHUNK):
        sl = slice(c * R_CHUNK, (c + 1) * R_CHUNK)
        q = q_ref[sl, :].astype(F32)
        k = k_ref[sl, :].astype(F32) * kscale
        v = v_ref[sl, :]
        s = lax.dot_general(q.astype(BF16), k.astype(BF16), (((1,), (1,)), ((), ())),
                            preferred_element_type=F32) * decay
        inner = jnp.dot(s.astype(BF16), v, preferred_element_type=F32)
        state = st_ref[...]
        cross = jnp.dot((q * xi).astype(BF16), state.astype(BF16), preferred_element_type=F32)
        kz_t = (k * zeta).T.astype(BF16)
        st_ref[...] = state * chunk_decay + jnp.dot(kz_t, v, preferred_element_type=F32)
        out = inner + cross
        ms = jnp.mean(out * out, axis=-1, keepdims=True)
        gate = g_ref[sl, :].astype(F32)
        o_ref[sl, :] = (gate * jax.nn.sigmoid(gate) * (out * lax.rsqrt(ms + NORM_EPS))).astype(BF16)


def _retention(proj, batch, seq):
    p = proj.reshape(batch, seq, P_WIDTH)
    return pl.pallas_call(
        _ret_body,
        grid=(batch, R_HEADS, seq // RET_TB),
        in_specs=[
            pl.BlockSpec((None, RET_TB, R_KEY_DIM), lambda b, h, t: (b, t, P_QR // R_KEY_DIM + h)),
            pl.BlockSpec((None, RET_TB, R_KEY_DIM), lambda b, h, t: (b, t, P_KR // R_KEY_DIM + h)),
            pl.BlockSpec((None, RET_TB, R_VAL_DIM), lambda b, h, t: (b, t, P_VR // R_VAL_DIM + h)),
            pl.BlockSpec((None, RET_TB, R_VAL_DIM), lambda b, h, t: (b, t, P_GR // R_VAL_DIM + h)),
        ],
        out_specs=pl.BlockSpec((None, RET_TB, R_VAL_DIM), lambda b, h, t: (b, t, h)),
        out_shape=jax.ShapeDtypeStruct((batch, seq, R_V), BF16),
        scratch_shapes=[pltpu.VMEM((R_KEY_DIM, R_VAL_DIM), F32)],
        compiler_params=pltpu.CompilerParams(
            dimension_semantics=("parallel", "parallel", "arbitrary"), vmem_limit_bytes=VMEM_LIMIT),
    )(p, p, p, p)


MERGE_TM = 512


def _merge_body(x_ref, ya_ref, rb_ref, ga_ref, gb_ref, wa_ref, wb_ref, wo_ref, o_ref):
    ya = jnp.dot(ya_ref[...].astype(BF16), wa_ref[...], preferred_element_type=F32)
    yb = jnp.dot(rb_ref[...], wb_ref[...], preferred_element_type=F32)
    ga = ga_ref[...].astype(F32)
    gb = gb_ref[...].astype(F32)
    merged = jax.nn.sigmoid(ga) * ya + jax.nn.sigmoid(gb) * yb
    o_ref[...] = x_ref[...] + jnp.dot(merged.astype(BF16), wo_ref[...], preferred_element_type=F32)


def _merge(x2d, ya2d, rb2d, proj, wa, wb, wo):
    m = x2d.shape[0]

    def full(shape):
        return pl.BlockSpec(shape, lambda i: (0, 0))

    return pl.pallas_call(
        _merge_body,
        grid=(m // MERGE_TM,),
        in_specs=[
            pl.BlockSpec((MERGE_TM, D_MODEL), lambda i: (i, 0)),
            pl.BlockSpec((MERGE_TM, A_OUT), lambda i: (i, 0)),
            pl.BlockSpec((MERGE_TM, R_V), lambda i: (i, 0)),
            pl.BlockSpec((MERGE_TM, D_MODEL), lambda i: (i, P_GA // D_MODEL)),
            pl.BlockSpec((MERGE_TM, D_MODEL), lambda i: (i, P_GB // D_MODEL)),
            full((A_OUT, D_MODEL)), full((R_V, D_MODEL)), full((D_MODEL, D_MODEL)),
        ],
        out_specs=pl.BlockSpec((MERGE_TM, D_MODEL), lambda i: (i, 0)),
        out_shape=jax.ShapeDtypeStruct((m, D_MODEL), F32),
        compiler_params=pltpu.CompilerParams(
            dimension_semantics=("parallel",), vmem_limit_bytes=VMEM_LIMIT),
    )(x2d, ya2d, rb2d, proj, proj, wa, wb, wo)


FFN_TM = 1024
FFN_TF = 1024


def _ffn_body(x_ref, g_ref, wu_ref, wd_ref, fg_ref, o_ref, h_ref, *, final):
    c = pl.program_id(1)

    @pl.when(c == 0)
    def _start():
        x = x_ref[...]
        h_ref[...] = _rms(x, g_ref[...]).astype(BF16)
        o_ref[...] = x

    up = jnp.dot(h_ref[...], wu_ref[...], preferred_element_type=F32)
    up = jnp.square(jnp.maximum(up, 0.0)).astype(BF16)
    o_ref[...] += jnp.dot(up, wd_ref[...], preferred_element_type=F32)

    if final:
        @pl.when(c == pl.num_programs(1) - 1)
        def _final_norm():
            o_ref[...] = _rms(o_ref[...], fg_ref[...])


def _ffn(x2d, gain, wu, wd, final_gain, final):
    m = x2d.shape[0]
    return pl.pallas_call(
        functools.partial(_ffn_body, final=final),
        grid=(m // FFN_TM, D_FF // FFN_TF),
        in_specs=[
            pl.BlockSpec((FFN_TM, D_MODEL), lambda i, c: (i, 0)),
            pl.BlockSpec((1, D_MODEL), lambda i, c: (0, 0)),
            pl.BlockSpec((D_MODEL, FFN_TF), lambda i, c: (0, c)),
            pl.BlockSpec((FFN_TF, D_MODEL), lambda i, c: (c, 0)),
            pl.BlockSpec((1, D_MODEL), lambda i, c: (0, 0)),
        ],
        out_specs=pl.BlockSpec((FFN_TM, D_MODEL), lambda i, c: (i, 0)),
        out_shape=jax.ShapeDtypeStruct((m, D_MODEL), F32),
        scratch_shapes=[pltpu.VMEM((FFN_TM, D_MODEL), BF16)],
        compiler_params=pltpu.CompilerParams(
            dimension_semantics=("parallel", "arbitrary"), vmem_limit_bytes=VMEM_LIMIT),
    )(x2d, gain, wu, wd, final_gain)


def kernel(x, mix_norm, w_in, w_a, w_b, w_o, ffn_norm, w_up, w_down, final_norm):
    batch, seq, _ = x.shape
    depth = w_in.shape[0]
    assert seq % A_SPAN == 0 and seq % RET_TB == 0
    tables = _rotary_tables(seq)
    final_gain = final_norm.reshape(1, D_MODEL)
    x2d = x.reshape(batch * seq, D_MODEL)
    for layer in range(depth):
        proj = _in_proj(x2d, mix_norm[layer].reshape(1, D_MODEL), _arrange_w_in(w_in[layer]), tables, seq)
        ya = _attention_a(proj, batch, seq).reshape(batch * seq, A_OUT)
        rb = _retention(proj, batch, seq).reshape(batch * seq, R_V)
        x2d = _merge(x2d, ya, rb, proj, w_a[layer].astype(BF16), w_b[layer].astype(BF16),
                     w_o[layer].astype(BF16))
        x2d = _ffn(x2d, ffn_norm[layer].reshape(1, D_MODEL), w_up[layer].astype(BF16),
                   w_down[layer].astype(BF16), final_gain, final=(layer == depth - 1))
    return x2d.reshape(batch, seq, D_MODEL)
```

```python
import functools

import numpy as np
import jax
import jax.numpy as jnp
from jax import lax
from jax.experimental import pallas as pl
from jax.experimental.pallas import tpu as pltpu

F32 = jnp.float32
BF16 = jnp.bfloat16

D_MODEL = 1024
N_GROUPS = 3
DILATIONS = (1, 4, 16)
A_HEADS = 8
A_HEAD_DIM = 64
A_GROUP_W = A_HEADS * A_HEAD_DIM
A_WIDTH = N_GROUPS * A_GROUP_W
A_OUT = A_GROUP_W
ROT_DIM = A_HEAD_DIM // 4
ROPE_THETA = 500000.0
ATT_BLOCK = 128
NEG_INF = -1e30
R_HEADS = 4
R_KEY_DIM = 128
R_VAL_DIM = 256
R_QK = R_HEADS * R_KEY_DIM
R_V = R_HEADS * R_VAL_DIM
R_CHUNK = 128
R_ROT_BASE = 10000.0
D_FF = 4 * D_MODEL
NORM_EPS = 1e-6

OFF_QA = 0
OFF_KA = A_WIDTH
OFF_VA = 2 * A_WIDTH
OFF_QR = 3 * A_WIDTH
OFF_KR = OFF_QR + R_QK
OFF_VR = OFF_KR + R_QK
OFF_GR = OFF_VR + R_V
OFF_GA = OFF_GR + R_V
OFF_GB = OFF_GA + D_MODEL

LANES = 128
VMEM_LIMIT = 56 * 1024 * 1024

A_SPAN = ATT_BLOCK * max(DILATIONS)
A_UNITS = A_SPAN // ATT_BLOCK

IN_TM = A_SPAN
IN_TN = 512
IN_CHUNKS = (512, 512, 512, 256, 256)
X_CHUNKS = D_MODEL // LANES

P_GA = 0
P_GB = P_GA + D_MODEL
P_VR = P_GB + D_MODEL
P_GR = P_VR + R_V
P_QR = P_GR + R_V
P_KR = P_QR + R_QK
P_QKV = tuple(P_KR + R_QK + g * 3 * A_GROUP_W for g in range(N_GROUPS))
P_WIDTH = P_QKV[-1] + 3 * A_GROUP_W

_TAB_R = 2 * N_GROUPS
_TAB_NONE = 2 * N_GROUPS + 1
TILE_TABLE = ((_TAB_NONE,) * (P_QR // IN_TN) + (_TAB_R,) * (2 * R_QK // IN_TN)
              + tuple(t for g in range(N_GROUPS) for t in (g, N_GROUPS + g, _TAB_NONE)))
LOG2_E = 1.4426950408889634
A_Q_SCALE = A_HEAD_DIM ** -0.5 * LOG2_E
TILE_ORDER = (0,) * (P_QKV[0] // IN_TN) + tuple(g for g in range(N_GROUPS) for _ in range(3))


def _rms(x, gain):
    ms = jnp.mean(x * x, axis=-1, keepdims=True)
    return x * lax.rsqrt(ms + NORM_EPS) * gain


def _step_lookup(j, values):
    out = values[-1]
    for t in range(len(values) - 2, -1, -1):
        if values[t] != values[t + 1]:
            out = jnp.where(j <= t, values[t], out)
    return out


def _in_proj_body(*refs):
    x_refs = refs[:X_CHUNKS]
    g_ref, w_ref, cos_ref, sin_ref, o_ref, h_ref, inv_ref = refs[X_CHUNKS:]
    j = pl.program_id(1)

    @pl.when(j == 0)
    def _norm():
        gains = [g_ref[:, c * LANES:(c + 1) * LANES] for c in range(X_CHUNKS)]
        for u in range(A_UNITS):
            rows = pl.ds(u * ATT_BLOCK, ATT_BLOCK)
            xc = [x_refs[c][rows, :] for c in range(X_CHUNKS)]
            sq = xc[0] * xc[0]
            for c in range(1, X_CHUNKS):
                sq = sq + xc[c] * xc[c]
            inv = lax.rsqrt(jnp.sum(sq, axis=-1, keepdims=True) * (1.0 / D_MODEL) + NORM_EPS)
            inv = jnp.broadcast_to(inv, (ATT_BLOCK, LANES))
            inv_ref[rows, :] = inv
            for c in range(X_CHUNKS):
                h_ref[0, rows, c * LANES:(c + 1) * LANES] = (xc[c] * inv * gains[c]).astype(BF16)
        for g in range(1, N_GROUPS):
            dil = DILATIONS[g]
            span = ATT_BLOCK * dil
            for ss in range(IN_TM // span):
                for r in range(dil):
                    u = ss * dil + r
                    rows = pl.ds(ss * span + r, ATT_BLOCK, stride=dil)
                    dst = pl.ds(u * ATT_BLOCK, ATT_BLOCK)
                    inv = inv_ref[rows, :]
                    for c in range(X_CHUNKS):
                        h_ref[g, dst, c * LANES:(c + 1) * LANES] = (
                            x_refs[c][rows, :] * inv * gains[c]).astype(BF16)

    order = _step_lookup(j, TILE_ORDER)
    starts = np.cumsum((0,) + IN_CHUNKS)
    for start, size in zip(starts, IN_CHUNKS):
        rows = pl.ds(int(start), size)
        acc = jnp.dot(h_ref[order, rows, :], w_ref[...], preferred_element_type=F32)
        for c in range(IN_TN // LANES):
            t = acc[:, c * LANES:(c + 1) * LANES]
            o_ref[rows, c * LANES:(c + 1) * LANES] = (
                t * cos_ref[rows, :] + pltpu.roll(t, LANES // 2, 1) * sin_ref[rows, :]).astype(BF16)


def _pair_layout(t):
    half = ROT_DIM // 2
    a, b = t[..., 0, :], t[..., 1, :]
    return jnp.concatenate([a[..., :half], b[..., :half], a[..., ROT_DIM:],
                            a[..., half:ROT_DIM], b[..., half:ROT_DIM], b[..., ROT_DIM:]], axis=-1)


def _arrange_w_in(w):
    rows = w.shape[0]

    def pairs(cols):
        return _pair_layout(cols.reshape(rows, N_GROUPS, A_GROUP_W // LANES, 2, A_HEAD_DIM)).reshape(
            rows, N_GROUPS, A_GROUP_W)

    qa = pairs(w[:, OFF_QA:OFF_KA])
    ka = pairs(w[:, OFF_KA:OFF_VA])
    va = w[:, OFF_VA:OFF_QR].reshape(rows, N_GROUPS, A_GROUP_W)
    parts = [w[:, OFF_GA:OFF_GB], w[:, OFF_GB:], w[:, OFF_VR:OFF_GR], w[:, OFF_GR:OFF_GA],
             w[:, OFF_QR:OFF_KR], w[:, OFF_KR:OFF_VR]]
    for g in range(N_GROUPS):
        parts += [qa[:, g], ka[:, g], va[:, g]]
    return jnp.concatenate(parts, axis=1).astype(BF16)


def _rotary_tables(seq):
    half = ROT_DIM // 2
    inv = 1.0 / (ROPE_THETA ** (jnp.arange(half, dtype=F32) / half))
    rest = A_HEAD_DIM - ROT_DIM
    ones, zeros = jnp.ones((seq, rest), F32), jnp.zeros((seq, rest), F32)
    cos_tabs, sin_tabs = [], []
    for dil in DILATIONS:
        span = ATT_BLOCK * dil
        pos = jnp.arange(seq, dtype=F32).reshape(seq // span, ATT_BLOCK, dil).transpose(0, 2, 1).reshape(seq, 1)
        ang = pos * inv[None, :]
        cos, sin = jnp.cos(ang), jnp.sin(ang)
        cos_tabs.append(jnp.concatenate([cos, cos, ones, cos, cos, ones], axis=1))
        sin_tabs.append(jnp.concatenate([-sin, -sin, zeros, sin, sin, zeros], axis=1))
    cos_tabs = [t * A_Q_SCALE for t in cos_tabs] + cos_tabs
    sin_tabs = [t * A_Q_SCALE for t in sin_tabs] + sin_tabs
    half_r = R_KEY_DIM // 2
    inv_r = 1.0 / (R_ROT_BASE ** (jnp.arange(half_r, dtype=F32) / half_r))
    ang_r = jnp.arange(seq, dtype=F32)[:, None] * inv_r[None, :]
    cos_tabs += [jnp.concatenate([jnp.cos(ang_r), jnp.cos(ang_r)], axis=1), jnp.ones((seq, LANES), F32)]
    sin_tabs += [jnp.concatenate([-jnp.sin(ang_r), jnp.sin(ang_r)], axis=1), jnp.zeros((seq, LANES), F32)]
    return jnp.stack(cos_tabs), jnp.stack(sin_tabs)


def _in_proj(x2d, gain, w_bf16, tables, seq):
    m = x2d.shape[0]
    tiles_per_seq = seq // IN_TM
    tab_spec = pl.BlockSpec((None, IN_TM, LANES),
                            lambda i, j: (_step_lookup(j, TILE_TABLE), i % tiles_per_seq, 0))
    x_specs = [pl.BlockSpec((IN_TM, LANES), lambda i, j, c=c: (i, c)) for c in range(X_CHUNKS)]
    return pl.pallas_call(
        _in_proj_body,
        grid=(m // IN_TM, P_WIDTH // IN_TN),
        in_specs=x_specs + [
            pl.BlockSpec((1, D_MODEL), lambda i, j: (0, 0)),
            pl.BlockSpec((D_MODEL, IN_TN), lambda i, j: (0, j)),
            tab_spec, tab_spec,
        ],
        out_specs=pl.BlockSpec((IN_TM, IN_TN), lambda i, j: (i, j)),
        out_shape=jax.ShapeDtypeStruct((m, P_WIDTH), BF16),
        scratch_shapes=[pltpu.VMEM((N_GROUPS, IN_TM, D_MODEL), BF16),
                        pltpu.VMEM((IN_TM, LANES), F32)],
        compiler_params=pltpu.CompilerParams(
            dimension_semantics=("parallel", "arbitrary"), vmem_limit_bytes=VMEM_LIMIT),
    )(*([x2d] * X_CHUNKS), gain, w_bf16, *tables)


A_HEADS_PER_STEP = LANES // A_HEAD_DIM
A_QUAD = 4


def _attn_body(*refs):
    q_in = refs[0:3]
    kc_in = refs[3:6]
    kp_in = refs[6:9]
    vc_in = refs[9:12]
    vp_in = refs[12:15]
    o_ref = refs[15]
    ro, rl, bias, s_buf, p_buf, m_buf = refs[16:22]
    kbuf, vbuf = refs[22:25], refs[25:28]
    jt = pl.program_id(1)

    qi = lax.broadcasted_iota(jnp.int32, (ATT_BLOCK, 2 * ATT_BLOCK), 0)
    kj = lax.broadcasted_iota(jnp.int32, (ATT_BLOCK, 2 * ATT_BLOCK), 1)
    dist = qi + ATT_BLOCK - kj
    band = (dist >= 0) & (dist <= ATT_BLOCK)
    bias[0] = jnp.where(band & (kj >= ATT_BLOCK), 0.0, NEG_INF).astype(F32)
    bias[1] = jnp.where(band, 0.0, NEG_INF).astype(F32)
    lane = lax.broadcasted_iota(jnp.int32, (ATT_BLOCK, LANES), 1)

    half = ROT_DIM // 2
    first_head = (lane < half) | ((lane >= ROT_DIM) & (lane < A_HEAD_DIM + half))
    head_lanes = (first_head, jnp.logical_not(first_head))
    first = lane < A_HEAD_DIM
    v_lane = lax.broadcasted_iota(jnp.int32, (2 * ATT_BLOCK, LANES), 1)
    v_lanes = (v_lane < A_HEAD_DIM, v_lane >= A_HEAD_DIM)

    for g, d in enumerate(DILATIONS):
        span = ATT_BLOCK * d
        kbuf[g][0:span] = kp_in[g][...]
        kbuf[g][span:] = kc_in[g][...]
        vbuf[g][0:span] = vp_in[g][...]
        vbuf[g][span:] = vc_in[g][...]

    def rows_of(u):
        return pl.ds(u * ATT_BLOCK if isinstance(u, int) else pl.multiple_of(u * ATT_BLOCK, ATT_BLOCK), ATT_BLOCK)

    def stage_scores(u, g, d):
        b = bias[jnp.where(jnp.logical_or(jt > 0, u >= d), 1, 0)]
        q = q_in[g][rows_of(u), :]
        k2 = jnp.concatenate([kbuf[g][rows_of(u), :], kbuf[g][rows_of(u + d), :]], axis=0)
        for h in range(A_HEADS_PER_STEP):
            qm = jnp.where(head_lanes[h], q, jnp.zeros_like(q))
            s_buf[u, h] = lax.dot_general(qm, k2, (((1,), (1,)), ((), ())), preferred_element_type=F32) + b

    def stage_softmax(u):
        ms = []
        for h in range(A_HEADS_PER_STEP):
            s = s_buf[u, h]
            m = jnp.max(s, axis=1, keepdims=True)
            p_buf[u, h] = jnp.exp2(s - m).astype(BF16)
            ms.append(m)
        m_buf[u] = jnp.where(first, ms[0], ms[1])

    def stage_out(u, g, d):
        v2 = jnp.concatenate([vbuf[g][rows_of(u), :], vbuf[g][rows_of(u + d), :]], axis=0)
        pvs = [jnp.dot(p_buf[u, h], jnp.where(v_lanes[h], v2, jnp.ones_like(v2)), preferred_element_type=F32)
               for h in range(A_HEADS_PER_STEP)]
        out = jnp.where(first, pvs[0], pvs[1])
        den = pltpu.roll(jnp.where(first, pvs[1], pvs[0]), A_HEAD_DIM, 1)
        ro[g, rows_of(u), :] = out / den
        rl[g, rows_of(u), :] = m_buf[u] + jnp.log(den) * LOG2_E

    def quad(stage, qd, *args):
        for i in range(A_QUAD):
            stage(qd * A_QUAD + i, *args)

    n_quads = A_UNITS // A_QUAD
    for g, d in enumerate(DILATIONS):
        quad(stage_scores, 0, g, d)
        quad(stage_softmax, 0)
        quad(stage_scores, 1, g, d)

        def trip(qd, carry, g=g, d=d):
            quad(stage_out, qd, g, d)
            quad(stage_softmax, qd + 1)
            quad(stage_scores, qd + 2, g, d)
            return carry

        lax.fori_loop(0, n_quads - 2, trip, 0)
        quad(stage_out, n_quads - 2, g, d)
        quad(stage_softmax, n_quads - 1)
        quad(stage_out, n_quads - 1, g, d)

    d2, d3 = DILATIONS[1], DILATIONS[2]
    span2 = ATT_BLOCK * d2
    per_ss2 = ATT_BLOCK * d2 // d3
    for r3 in range(d3):
        blk = slice(r3 * ATT_BLOCK, (r3 + 1) * ATT_BLOCK)
        tok = pl.ds(r3, ATT_BLOCK, stride=d3)
        r2, off = r3 % d2, r3 // d2

        def from_g2(buf):
            return jnp.concatenate(
                [buf[1, pl.ds(ss2 * span2 + r2 * ATT_BLOCK + off, per_ss2, stride=d3 // d2), :]
                 for ss2 in range(A_SPAN // span2)], axis=0)

        o1, l1 = ro[0, tok, :], rl[0, tok, :]
        o2, l2 = from_g2(ro), from_g2(rl)
        o3, l3 = ro[2, blk, :], rl[2, blk, :]
        mx = jnp.maximum(jnp.maximum(l1, l2), l3)
        e1, e2, e3 = jnp.exp2(l1 - mx), jnp.exp2(l2 - mx), jnp.exp2(l3 - mx)
        o_ref[tok, :] = (e1 * o1 + e2 * o2 + e3 * o3) / (e1 + e2 + e3)


def _attention_a(proj, batch, seq):
    p = proj.reshape(batch, seq, P_WIDTH)

    def cur_spec(g, part):
        col = (P_QKV[g] + part * A_GROUP_W) // LANES
        return pl.BlockSpec((None, A_SPAN, LANES), lambda b, j, hp: (b, j, col + hp))

    def prev_spec(g, part):
        col = (P_QKV[g] + part * A_GROUP_W) // LANES
        span = ATT_BLOCK * DILATIONS[g]
        per = A_SPAN // span
        return pl.BlockSpec((None, span, LANES),
                            lambda b, j, hp: (b, jnp.maximum(j * per - 1, 0), col + hp))

    groups = range(N_GROUPS)
    kv_bufs = [pltpu.VMEM((ATT_BLOCK * d + A_SPAN, LANES), BF16) for d in DILATIONS]
    in_specs = ([cur_spec(g, 0) for g in groups] + [cur_spec(g, 1) for g in groups]
                + [prev_spec(g, 1) for g in groups] + [cur_spec(g, 2) for g in groups]
                + [prev_spec(g, 2) for g in groups])
    return pl.pallas_call(
        _attn_body,
        grid=(batch, seq // A_SPAN, A_GROUP_W // LANES),
        in_specs=in_specs,
        out_specs=pl.BlockSpec((None, A_SPAN, LANES), lambda b, j, hp: (b, j, hp)),
        out_shape=jax.ShapeDtypeStruct((batch, seq, A_OUT), F32),
        scratch_shapes=[pltpu.VMEM((N_GROUPS, A_SPAN, LANES), F32),
                        pltpu.VMEM((N_GROUPS, A_SPAN, LANES), F32),
                        pltpu.VMEM((2, ATT_BLOCK, 2 * ATT_BLOCK), F32),
                        pltpu.VMEM((A_UNITS, A_HEADS_PER_STEP, ATT_BLOCK, 2 * ATT_BLOCK), F32),
                        pltpu.VMEM((A_UNITS, A_HEADS_PER_STEP, ATT_BLOCK, 2 * ATT_BLOCK), BF16),
                        pltpu.VMEM((A_UNITS, ATT_BLOCK, LANES), F32)] + kv_bufs + kv_bufs,
        compiler_params=pltpu.CompilerParams(
            dimension_semantics=("parallel", "parallel", "parallel"), vmem_limit_bytes=VMEM_LIMIT),
    )(*([p] * 15))


RET_TB = 2048
_LOG_GAMMA = [float(v) for v in np.log1p(-(2.0 ** (-5.0 - np.arange(R_HEADS)))).astype(np.float32)]


def _ret_body(q_ref, k_ref, v_ref, g_ref, o_ref, st_ref):
    h = pl.program_id(1)
    t = pl.program_id(2)

    @pl.when(t == 0)
    def _reset():
        st_ref[...] = jnp.zeros_like(st_ref)

    lg = jnp.float32(_LOG_GAMMA[R_HEADS - 1])
    for hh in range(R_HEADS - 2, -1, -1):
        lg = jnp.where(h == hh, jnp.float32(_LOG_GAMMA[hh]), lg)
    row = lax.broadcasted_iota(jnp.int32, (R_CHUNK, R_CHUNK), 0).astype(F32)
    colv = lax.broadcasted_iota(jnp.int32, (R_CHUNK, R_CHUNK), 1).astype(F32)
    diff = row - colv
    decay = jnp.where(diff >= 0, jnp.exp(diff * lg), 0.0)
    zeta = jnp.exp((R_CHUNK - 1 - row) * lg)
    xi = jnp.exp((row + 1.0) * lg)
    chunk_decay = jnp.exp(jnp.full((R_KEY_DIM, R_VAL_DIM), float(R_CHUNK), F32) * lg)
    kscale = R_KEY_DIM ** -0.5

    for c in range(RET_TB // R_CHUNK):
        sl = slice(c * R_CHUNK, (c + 1) * R_CHUNK)
        q = q_ref[sl, :].astype(F32)
        k = k_ref[sl, :].astype(F32) * kscale
        v = v_ref[sl, :]
        s = lax.dot_general(q.astype(BF16), k.astype(BF16), (((1,), (1,)), ((), ())),
                            preferred_element_type=F32) * decay
        inner = jnp.dot(s.astype(BF16), v, preferred_element_type=F32)
        state = st_ref[...]
        cross = jnp.dot((q * xi).astype(BF16), state.astype(BF16), preferred_element_type=F32)
        kz_t = (k * zeta).T.astype(BF16)
        st_ref[...] = state * chunk_decay + jnp.dot(kz_t, v, preferred_element_type=F32)
        out = inner + cross
        ms = jnp.mean(out * out, axis=-1, keepdims=True)
        gate = g_ref[sl, :].astype(F32)
        o_ref[sl, :] = (gate * jax.nn.sigmoid(gate) * (out * lax.rsqrt(ms + NORM_EPS))).astype(BF16)


def _retention(proj, batch, seq):
    p = proj.reshape(batch, seq, P_WIDTH)
    return pl.pallas_call(
        _ret_body,
        grid=(batch, R_HEADS, seq // RET_TB),
        in_specs=[
            pl.BlockSpec((None, RET_TB, R_KEY_DIM), lambda b, h, t: (b, t, P_QR // R_KEY_DIM + h)),
            pl.BlockSpec((None, RET_TB, R_KEY_DIM), lambda b, h, t: (b, t, P_KR // R_KEY_DIM + h)),
            pl.BlockSpec((None, RET_TB, R_VAL_DIM), lambda b, h, t: (b, t, P_VR // R_VAL_DIM + h)),
            pl.BlockSpec((None, RET_TB, R_VAL_DIM), lambda b, h, t: (b, t, P_GR // R_VAL_DIM + h)),
        ],
        out_specs=pl.BlockSpec((None, RET_TB, R_VAL_DIM), lambda b, h, t: (b, t, h)),
        out_shape=jax.ShapeDtypeStruct((batch, seq, R_V), BF16),
        scratch_shapes=[pltpu.VMEM((R_KEY_DIM, R_VAL_DIM), F32)],
        compiler_params=pltpu.CompilerParams(
            dimension_semantics=("parallel", "parallel", "arbitrary"), vmem_limit_bytes=VMEM_LIMIT),
    )(p, p, p, p)


MERGE_TM = 512


def _merge_body(x_ref, ya_ref, rb_ref, ga_ref, gb_ref, wa_ref, wb_ref, wo_ref, o_ref):
    ya = jnp.dot(ya_ref[...].astype(BF16), wa_ref[...], preferred_element_type=F32)
    yb = jnp.dot(rb_ref[...], wb_ref[...], preferred_element_type=F32)
    ga = ga_ref[...].astype(F32)
    gb = gb_ref[...].astype(F32)
    merged = jax.nn.sigmoid(ga) * ya + jax.nn.sigmoid(gb) * yb
    o_ref[...] = x_ref[...] + jnp.dot(merged.astype(BF16), wo_ref[...], preferred_element_type=F32)


def _merge(x2d, ya2d, rb2d, proj, wa, wb, wo):
    m = x2d.shape[0]

    def full(shape):
        return pl.BlockSpec(shape, lambda i: (0, 0))

    return pl.pallas_call(
        _merge_body,
        grid=(m // MERGE_TM,),
        in_specs=[
            pl.BlockSpec((MERGE_TM, D_MODEL), lambda i: (i, 0)),
            pl.BlockSpec((MERGE_TM, A_OUT), lambda i: (i, 0)),
            pl.BlockSpec((MERGE_TM, R_V), lambda i: (i, 0)),
            pl.BlockSpec((MERGE_TM, D_MODEL), lambda i: (i, P_GA // D_MODEL)),
            pl.BlockSpec((MERGE_TM, D_MODEL), lambda i: (i, P_GB // D_MODEL)),
            full((A_OUT, D_MODEL)), full((R_V, D_MODEL)), full((D_MODEL, D_MODEL)),
        ],
        out_specs=pl.BlockSpec((MERGE_TM, D_MODEL), lambda i: (i, 0)),
        out_shape=jax.ShapeDtypeStruct((m, D_MODEL), F32),
        compiler_params=pltpu.CompilerParams(
            dimension_semantics=("parallel",), vmem_limit_bytes=VMEM_LIMIT),
    )(x2d, ya2d, rb2d, proj, proj, wa, wb, wo)


FFN_TM = 1024
FFN_TF = 1024


def _ffn_body(x_ref, g_ref, wu_ref, wd_ref, fg_ref, o_ref, h_ref, *, final):
    c = pl.program_id(1)

    @pl.when(c == 0)
    def _start():
        x = x_ref[...]
        h_ref[...] = _rms(x, g_ref[...]).astype(BF16)
        o_ref[...] = x

    up = jnp.dot(h_ref[...], wu_ref[...], preferred_element_type=F32)
    up = jnp.square(jnp.maximum(up, 0.0)).astype(BF16)
    o_ref[...] += jnp.dot(up, wd_ref[...], preferred_element_type=F32)

    if final:
        @pl.when(c == pl.num_programs(1) - 1)
        def _final_norm():
            o_ref[...] = _rms(o_ref[...], fg_ref[...])


def _ffn(x2d, gain, wu, wd, final_gain, final):
    m = x2d.shape[0]
    return pl.pallas_call(
        functools.partial(_ffn_body, final=final),
        grid=(m // FFN_TM, D_FF // FFN_TF),
        in_specs=[
            pl.BlockSpec((FFN_TM, D_MODEL), lambda i, c: (i, 0)),
            pl.BlockSpec((1, D_MODEL), lambda i, c: (0, 0)),
            pl.BlockSpec((D_MODEL, FFN_TF), lambda i, c: (0, c)),
            pl.BlockSpec((FFN_TF, D_MODEL), lambda i, c: (c, 0)),
            pl.BlockSpec((1, D_MODEL), lambda i, c: (0, 0)),
        ],
        out_specs=pl.BlockSpec((FFN_TM, D_MODEL), lambda i, c: (i, 0)),
        out_shape=jax.ShapeDtypeStruct((m, D_MODEL), F32),
        scratch_shapes=[pltpu.VMEM((FFN_TM, D_MODEL), BF16)],
        compiler_params=pltpu.CompilerParams(
            dimension_semantics=("parallel", "arbitrary"), vmem_limit_bytes=VMEM_LIMIT),
    )(x2d, gain, wu, wd, final_gain)


def kernel(x, mix_norm, w_in, w_a, w_b, w_o, ffn_norm, w_up, w_down, final_norm):
    batch, seq, _ = x.shape
    depth = w_in.shape[0]
    assert seq % A_SPAN == 0 and seq % RET_TB == 0
    tables = _rotary_tables(seq)
    final_gain = final_norm.reshape(1, D_MODEL)
    x2d = x.reshape(batch * seq, D_MODEL)
    for layer in range(depth):
        proj = _in_proj(x2d, mix_norm[layer].reshape(1, D_MODEL), _arrange_w_in(w_in[layer]), tables, seq)
        ya = _attention_a(proj, batch, seq).reshape(batch * seq, A_OUT)
        rb = _retention(proj, batch, seq).reshape(batch * seq, R_V)
        x2d = _merge(x2d, ya, rb, proj, w_a[layer].astype(BF16), w_b[layer].astype(BF16),
                     w_o[layer].astype(BF16))
        x2d = _ffn(x2d, ffn_norm[layer].reshape(1, D_MODEL), w_up[layer].astype(BF16),
                   w_down[layer].astype(BF16), final_gain, final=(layer == depth - 1))
    return x2d.reshape(batch, seq, D_MODEL)
```

```python
import functools

import numpy as np
import jax
import jax.numpy as jnp
from jax import lax
from jax.experimental import pallas as pl
from jax.experimental.pallas import tpu as pltpu

F32 = jnp.float32
BF16 = jnp.bfloat16

D_MODEL = 1024
N_GROUPS = 3
DILATIONS = (1, 4, 16)
A_HEADS = 8
A_HEAD_DIM = 64
A_GROUP_W = A_HEADS * A_HEAD_DIM
A_WIDTH = N_GROUPS * A_GROUP_W
A_OUT = A_GROUP_W
ROT_DIM = A_HEAD_DIM // 4
ROPE_THETA = 500000.0
ATT_BLOCK = 128
NEG_INF = -1e30
R_HEADS = 4
R_KEY_DIM = 128
R_VAL_DIM = 256
R_QK = R_HEADS * R_KEY_DIM
R_V = R_HEADS * R_VAL_DIM
R_CHUNK = 128
R_ROT_BASE = 10000.0
D_FF = 4 * D_MODEL
NORM_EPS = 1e-6

OFF_QA = 0
OFF_KA = A_WIDTH
OFF_VA = 2 * A_WIDTH
OFF_QR = 3 * A_WIDTH
OFF_KR = OFF_QR + R_QK
OFF_VR = OFF_KR + R_QK
OFF_GR = OFF_VR + R_V
OFF_GA = OFF_GR + R_V
OFF_GB = OFF_GA + D_MODEL

LANES = 128
VMEM_LIMIT = 56 * 1024 * 1024

A_SPAN = ATT_BLOCK * max(DILATIONS)
A_UNITS = A_SPAN // ATT_BLOCK

IN_TM = A_SPAN
IN_TN = 512
IN_CHUNKS = (512, 512, 512, 256, 256)
X_CHUNKS = D_MODEL // LANES

P_GA = 0
P_GB = P_GA + D_MODEL
P_VR = P_GB + D_MODEL
P_GR = P_VR + R_V
P_QR = P_GR + R_V
P_KR = P_QR + R_QK
P_QKV = tuple(P_KR + R_QK + g * 3 * A_GROUP_W for g in range(N_GROUPS))
P_WIDTH = P_QKV[-1] + 3 * A_GROUP_W

_TAB_R = 2 * N_GROUPS
_TAB_NONE = 2 * N_GROUPS + 1
TILE_TABLE = ((_TAB_NONE,) * (P_QR // IN_TN) + (_TAB_R,) * (2 * R_QK // IN_TN)
              + tuple(t for g in range(N_GROUPS) for t in (g, N_GROUPS + g, _TAB_NONE)))
LOG2_E = 1.4426950408889634
A_Q_SCALE = A_HEAD_DIM ** -0.5 * LOG2_E
TILE_ORDER = (0,) * (P_QKV[0] // IN_TN) + tuple(g for g in range(N_GROUPS) for _ in range(3))


def _rms(x, gain):
    ms = jnp.mean(x * x, axis=-1, keepdims=True)
    return x * lax.rsqrt(ms + NORM_EPS) * gain


def _step_lookup(j, values):
    out = values[-1]
    for t in range(len(values) - 2, -1, -1):
        if values[t] != values[t + 1]:
            out = jnp.where(j <= t, values[t], out)
    return out


def _in_proj_body(*refs):
    x_refs = refs[:X_CHUNKS]
    g_ref, w_ref, cos_ref, sin_ref, o_ref, h_ref, inv_ref = refs[X_CHUNKS:]
    j = pl.program_id(1)

    @pl.when(j == 0)
    def _norm():
        gains = [g_ref[:, c * LANES:(c + 1) * LANES] for c in range(X_CHUNKS)]
        for u in range(A_UNITS):
            rows = pl.ds(u * ATT_BLOCK, ATT_BLOCK)
            xc = [x_refs[c][rows, :] for c in range(X_CHUNKS)]
            sq = xc[0] * xc[0]
            for c in range(1, X_CHUNKS):
                sq = sq + xc[c] * xc[c]
            inv = lax.rsqrt(jnp.sum(sq, axis=-1, keepdims=True) * (1.0 / D_MODEL) + NORM_EPS)
            inv = jnp.broadcast_to(inv, (ATT_BLOCK, LANES))
            inv_ref[rows, :] = inv
            for c in range(X_CHUNKS):
                h_ref[0, rows, c * LANES:(c + 1) * LANES] = (xc[c] * inv * gains[c]).astype(BF16)
        for g in range(1, N_GROUPS):
            dil = DILATIONS[g]
            span = ATT_BLOCK * dil
            for ss in range(IN_TM // span):
                for r in range(dil):
                    u = ss * dil + r
                    rows = pl.ds(ss * span + r, ATT_BLOCK, stride=dil)
                    dst = pl.ds(u * ATT_BLOCK, ATT_BLOCK)
                    inv = inv_ref[rows, :]
                    for c in range(X_CHUNKS):
                        h_ref[g, dst, c * LANES:(c + 1) * LANES] = (
                            x_refs[c][rows, :] * inv * gains[c]).astype(BF16)

    order = _step_lookup(j, TILE_ORDER)
    starts = np.cumsum((0,) + IN_CHUNKS)
    for start, size in zip(starts, IN_CHUNKS):
        rows = pl.ds(int(start), size)
        acc = jnp.dot(h_ref[order, rows, :], w_ref[...], preferred_element_type=F32)
        for c in range(IN_TN // LANES):
            t = acc[:, c * LANES:(c + 1) * LANES]
            o_ref[rows, c * LANES:(c + 1) * LANES] = (
                t * cos_ref[rows, :] + pltpu.roll(t, LANES // 2, 1) * sin_ref[rows, :]).astype(BF16)


def _pair_layout(t):
    half = ROT_DIM // 2
    a, b = t[..., 0, :], t[..., 1, :]
    return jnp.concatenate([a[..., :half], b[..., :half], a[..., ROT_DIM:],
                            a[..., half:ROT_DIM], b[..., half:ROT_DIM], b[..., ROT_DIM:]], axis=-1)


def _arrange_w_in(w):
    rows = w.shape[0]

    def pairs(cols):
        return _pair_layout(cols.reshape(rows, N_GROUPS, A_GROUP_W // LANES, 2, A_HEAD_DIM)).reshape(
            rows, N_GROUPS, A_GROUP_W)

    qa = pairs(w[:, OFF_QA:OFF_KA])
    ka = pairs(w[:, OFF_KA:OFF_VA])
    va = w[:, OFF_VA:OFF_QR].reshape(rows, N_GROUPS, A_GROUP_W)
    parts = [w[:, OFF_GA:OFF_GB], w[:, OFF_GB:], w[:, OFF_VR:OFF_GR], w[:, OFF_GR:OFF_GA],
             w[:, OFF_QR:OFF_KR], w[:, OFF_KR:OFF_VR]]
    for g in range(N_GROUPS):
        parts += [qa[:, g], ka[:, g], va[:, g]]
    return jnp.concatenate(parts, axis=1).astype(BF16)


def _rotary_tables(seq):
    half = ROT_DIM // 2
    inv = 1.0 / (ROPE_THETA ** (jnp.arange(half, dtype=F32) / half))
    rest = A_HEAD_DIM - ROT_DIM
    ones, zeros = jnp.ones((seq, rest), F32), jnp.zeros((seq, rest), F32)
    cos_tabs, sin_tabs = [], []
    for dil in DILATIONS:
        span = ATT_BLOCK * dil
        pos = jnp.arange(seq, dtype=F32).reshape(seq // span, ATT_BLOCK, dil).transpose(0, 2, 1).reshape(seq, 1)
        ang = pos * inv[None, :]
        cos, sin = jnp.cos(ang), jnp.sin(ang)
        cos_tabs.append(jnp.concatenate([cos, cos, ones, cos, cos, ones], axis=1))
        sin_tabs.append(jnp.concatenate([-sin, -sin, zeros, sin, sin, zeros], axis=1))
    cos_tabs = [t * A_Q_SCALE for t in cos_tabs] + cos_tabs
    sin_tabs = [t * A_Q_SCALE for t in sin_tabs] + sin_tabs
    half_r = R_KEY_DIM // 2
    inv_r = 1.0 / (R_ROT_BASE ** (jnp.arange(half_r, dtype=F32) / half_r))
    ang_r = jnp.arange(seq, dtype=F32)[:, None] * inv_r[None, :]
    cos_tabs += [jnp.concatenate([jnp.cos(ang_r), jnp.cos(ang_r)], axis=1), jnp.ones((seq, LANES), F32)]
    sin_tabs += [jnp.concatenate([-jnp.sin(ang_r), jnp.sin(ang_r)], axis=1), jnp.zeros((seq, LANES), F32)]
    return jnp.stack(cos_tabs), jnp.stack(sin_tabs)


def _in_proj(x2d, gain, w_bf16, tables, seq):
    m = x2d.shape[0]
    tiles_per_seq = seq // IN_TM
    tab_spec = pl.BlockSpec((None, IN_TM, LANES),
                            lambda i, j: (_step_lookup(j, TILE_TABLE), i % tiles_per_seq, 0))
    x_specs = [pl.BlockSpec((IN_TM, LANES), lambda i, j, c=c: (i, c)) for c in range(X_CHUNKS)]
    return pl.pallas_call(
        _in_proj_body,
        grid=(m // IN_TM, P_WIDTH // IN_TN),
        in_specs=x_specs + [
            pl.BlockSpec((1, D_MODEL), lambda i, j: (0, 0)),
            pl.BlockSpec((D_MODEL, IN_TN), lambda i, j: (0, j)),
            tab_spec, tab_spec,
        ],
        out_specs=pl.BlockSpec((IN_TM, IN_TN), lambda i, j: (i, j)),
        out_shape=jax.ShapeDtypeStruct((m, P_WIDTH), BF16),
        scratch_shapes=[pltpu.VMEM((N_GROUPS, IN_TM, D_MODEL), BF16),
                        pltpu.VMEM((IN_TM, LANES), F32)],
        compiler_params=pltpu.CompilerParams(
            dimension_semantics=("parallel", "arbitrary"), vmem_limit_bytes=VMEM_LIMIT),
    )(*([x2d] * X_CHUNKS), gain, w_bf16, *tables)


A_HEADS_PER_STEP = LANES // A_HEAD_DIM
A_QUAD = 4


def _attn_body(*refs):
    q_in = refs[0:3]
    kc_in = refs[3:6]
    kp_in = refs[6:9]
    vc_in = refs[9:12]
    vp_in = refs[12:15]
    o_ref = refs[15]
    ro, rl, bias, s_buf, p_buf, m_buf = refs[16:22]
    kbuf, vbuf = refs[22:25], refs[25:28]
    jt = pl.program_id(1)

    qi = lax.broadcasted_iota(jnp.int32, (ATT_BLOCK, 2 * ATT_BLOCK), 0)
    kj = lax.broadcasted_iota(jnp.int32, (ATT_BLOCK, 2 * ATT_BLOCK), 1)
    dist = qi + ATT_BLOCK - kj
    band = (dist >= 0) & (dist <= ATT_BLOCK)
    bias[0] = jnp.where(band & (kj >= ATT_BLOCK), 0.0, NEG_INF).astype(BF16)
    bias[1] = jnp.where(band, 0.0, NEG_INF).astype(BF16)
    lane = lax.broadcasted_iota(jnp.int32, (ATT_BLOCK, LANES), 1)

    half = ROT_DIM // 2
    first_head = (lane < half) | ((lane >= ROT_DIM) & (lane < A_HEAD_DIM + half))
    head_lanes = (first_head, jnp.logical_not(first_head))
    first = lane < A_HEAD_DIM
    v_lane = lax.broadcasted_iota(jnp.int32, (2 * ATT_BLOCK, LANES), 1)
    v_lanes = (v_lane < A_HEAD_DIM, v_lane >= A_HEAD_DIM)

    for g, d in enumerate(DILATIONS):
        span = ATT_BLOCK * d
        kbuf[g][0:span] = kp_in[g][...]
        kbuf[g][span:] = kc_in[g][...]
        vbuf[g][0:span] = vp_in[g][...]
        vbuf[g][span:] = vc_in[g][...]

    def rows_of(u):
        return pl.ds(u * ATT_BLOCK if isinstance(u, int) else pl.multiple_of(u * ATT_BLOCK, ATT_BLOCK), ATT_BLOCK)

    def stage_scores(u, g, d):
        b = bias[jnp.where(jnp.logical_or(jt > 0, u >= d), 1, 0)]
        q = q_in[g][rows_of(u), :]
        k2 = jnp.concatenate([kbuf[g][rows_of(u), :], kbuf[g][rows_of(u + d), :]], axis=0)
        for h in range(A_HEADS_PER_STEP):
            qm = jnp.where(head_lanes[h], q, jnp.zeros_like(q))
            s = lax.dot_general(qm, k2, (((1,), (1,)), ((), ())), preferred_element_type=F32)
            s_buf[u, h] = s.astype(BF16) + b

    def stage_softmax(u):
        ms = []
        for h in range(A_HEADS_PER_STEP):
            s = s_buf[u, h]
            m = jnp.max(s, axis=1, keepdims=True)
            p_buf[u, h] = jnp.exp2(s - m)
            ms.append(m.astype(F32))
        m_buf[u] = jnp.where(first, ms[0], ms[1])

    def stage_out(u, g, d):
        v2 = jnp.concatenate([vbuf[g][rows_of(u), :], vbuf[g][rows_of(u + d), :]], axis=0)
        pvs = [jnp.dot(p_buf[u, h], jnp.where(v_lanes[h], v2, jnp.ones_like(v2)), preferred_element_type=F32)
               for h in range(A_HEADS_PER_STEP)]
        out = jnp.where(first, pvs[0], pvs[1])
        den = pltpu.roll(jnp.where(first, pvs[1], pvs[0]), A_HEAD_DIM, 1)
        ro[g, rows_of(u), :] = out / den
        rl[g, rows_of(u), :] = m_buf[u] + jnp.log(den) * LOG2_E

    def quad(stage, qd, *args):
        for i in range(A_QUAD):
            stage(qd * A_QUAD + i, *args)

    n_quads = A_UNITS // A_QUAD
    for g, d in enumerate(DILATIONS):
        quad(stage_scores, 0, g, d)
        quad(stage_softmax, 0)
        quad(stage_scores, 1, g, d)

        def trip(qd, carry, g=g, d=d):
            quad(stage_out, qd, g, d)
            quad(stage_softmax, qd + 1)
            quad(stage_scores, qd + 2, g, d)
            return carry

        lax.fori_loop(0, n_quads - 2, trip, 0)
        quad(stage_out, n_quads - 2, g, d)
        quad(stage_softmax, n_quads - 1)
        quad(stage_out, n_quads - 1, g, d)

    d2, d3 = DILATIONS[1], DILATIONS[2]
    span2 = ATT_BLOCK * d2
    per_ss2 = ATT_BLOCK * d2 // d3
    for r3 in range(d3):
        blk = slice(r3 * ATT_BLOCK, (r3 + 1) * ATT_BLOCK)
        tok = pl.ds(r3, ATT_BLOCK, stride=d3)
        r2, off = r3 % d2, r3 // d2

        def from_g2(buf):
            return jnp.concatenate(
                [buf[1, pl.ds(ss2 * span2 + r2 * ATT_BLOCK + off, per_ss2, stride=d3 // d2), :]
                 for ss2 in range(A_SPAN // span2)], axis=0)

        o1, l1 = ro[0, tok, :], rl[0, tok, :]
        o2, l2 = from_g2(ro), from_g2(rl)
        o3, l3 = ro[2, blk, :], rl[2, blk, :]
        mx = jnp.maximum(jnp.maximum(l1, l2), l3)
        e1, e2, e3 = jnp.exp2(l1 - mx), jnp.exp2(l2 - mx), jnp.exp2(l3 - mx)
        o_ref[tok, :] = (e1 * o1 + e2 * o2 + e3 * o3) / (e1 + e2 + e3)


def _attention_a(proj, batch, seq):
    p = proj.reshape(batch, seq, P_WIDTH)

    def cur_spec(g, part):
        col = (P_QKV[g] + part * A_GROUP_W) // LANES
        return pl.BlockSpec((None, A_SPAN, LANES), lambda b, j, hp: (b, j, col + hp))

    def prev_spec(g, part):
        col = (P_QKV[g] + part * A_GROUP_W) // LANES
        span = ATT_BLOCK * DILATIONS[g]
        per = A_SPAN // span
        return pl.BlockSpec((None, span, LANES),
                            lambda b, j, hp: (b, jnp.maximum(j * per - 1, 0), col + hp))

    groups = range(N_GROUPS)
    kv_bufs = [pltpu.VMEM((ATT_BLOCK * d + A_SPAN, LANES), BF16) for d in DILATIONS]
    in_specs = ([cur_spec(g, 0) for g in groups] + [cur_spec(g, 1) for g in groups]
                + [prev_spec(g, 1) for g in groups] + [cur_spec(g, 2) for g in groups]
                + [prev_spec(g, 2) for g in groups])
    return pl.pallas_call(
        _attn_body,
        grid=(batch, seq // A_SPAN, A_GROUP_W // LANES),
        in_specs=in_specs,
        out_specs=pl.BlockSpec((None, A_SPAN, LANES), lambda b, j, hp: (b, j, hp)),
        out_shape=jax.ShapeDtypeStruct((batch, seq, A_OUT), F32),
        scratch_shapes=[pltpu.VMEM((N_GROUPS, A_SPAN, LANES), F32),
                        pltpu.VMEM((N_GROUPS, A_SPAN, LANES), F32),
                        pltpu.VMEM((2, ATT_BLOCK, 2 * ATT_BLOCK), BF16),
                        pltpu.VMEM((A_UNITS, A_HEADS_PER_STEP, ATT_BLOCK, 2 * ATT_BLOCK), BF16),
                        pltpu.VMEM((A_UNITS, A_HEADS_PER_STEP, ATT_BLOCK, 2 * ATT_BLOCK), BF16),
                        pltpu.VMEM((A_UNITS, ATT_BLOCK, LANES), F32)] + kv_bufs + kv_bufs,
        compiler_params=pltpu.CompilerParams(
            dimension_semantics=("parallel", "parallel", "parallel"), vmem_limit_bytes=VMEM_LIMIT),
    )(*([p] * 15))


RET_TB = 2048
_LOG_GAMMA = [float(v) for v in np.log1p(-(2.0 ** (-5.0 - np.arange(R_HEADS)))).astype(np.float32)]


def _ret_body(q_ref, k_ref, v_ref, g_ref, o_ref, st_ref):
    h = pl.program_id(1)
    t = pl.program_id(2)

    @pl.when(t == 0)
    def _reset():
        st_ref[...] = jnp.zeros_like(st_ref)

    lg = jnp.float32(_LOG_GAMMA[R_HEADS - 1])
    for hh in range(R_HEADS - 2, -1, -1):
        lg = jnp.where(h == hh, jnp.float32(_LOG_GAMMA[hh]), lg)
    row = lax.broadcasted_iota(jnp.int32, (R_CHUNK, R_CHUNK), 0).astype(F32)
    colv = lax.broadcasted_iota(jnp.int32, (R_CHUNK, R_CHUNK), 1).astype(F32)
    diff = row - colv
    decay = jnp.where(diff >= 0, jnp.exp(diff * lg), 0.0)
    zeta = jnp.exp((R_CHUNK - 1 - row) * lg)
    xi = jnp.exp((row + 1.0) * lg)
    chunk_decay = jnp.exp(jnp.full((R_KEY_DIM, R_VAL_DIM), float(R_CHUNK), F32) * lg)
    kscale = R_KEY_DIM ** -0.5

    for c in range(RET_TB // R_CHUNK):
        sl = slice(c * R_CHUNK, (c + 1) * R_CHUNK)
        q = q_ref[sl, :].astype(F32)
        k = k_ref[sl, :].astype(F32) * kscale
        v = v_ref[sl, :]
        s = lax.dot_general(q.astype(BF16), k.astype(BF16), (((1,), (1,)), ((), ())),
                            preferred_element_type=F32) * decay
        inner = jnp.dot(s.astype(BF16), v, preferred_element_type=F32)
        state = st_ref[...]
        cross = jnp.dot((q * xi).astype(BF16), state.astype(BF16), preferred_element_type=F32)
        kz_t = (k * zeta).T.astype(BF16)
        st_ref[...] = state * chunk_decay + jnp.dot(kz_t, v, preferred_element_type=F32)
        out = inner + cross
        ms = jnp.mean(out * out, axis=-1, keepdims=True)
        gate = g_ref[sl, :].astype(F32)
        o_ref[sl, :] = (gate * jax.nn.sigmoid(gate) * (out * lax.rsqrt(ms + NORM_EPS))).astype(BF16)


def _retention(proj, batch, seq):
    p = proj.reshape(batch, seq, P_WIDTH)
    return pl.pallas_call(
        _ret_body,
        grid=(batch, R_HEADS, seq // RET_TB),
        in_specs=[
            pl.BlockSpec((None, RET_TB, R_KEY_DIM), lambda b, h, t: (b, t, P_QR // R_KEY_DIM + h)),
            pl.BlockSpec((None, RET_TB, R_KEY_DIM), lambda b, h, t: (b, t, P_KR // R_KEY_DIM + h)),
            pl.BlockSpec((None, RET_TB, R_VAL_DIM), lambda b, h, t: (b, t, P_VR // R_VAL_DIM + h)),
            pl.BlockSpec((None, RET_TB, R_VAL_DIM), lambda b, h, t: (b, t, P_GR // R_VAL_DIM + h)),
        ],
        out_specs=pl.BlockSpec((None, RET_TB, R_VAL_DIM), lambda b, h, t: (b, t, h)),
        out_shape=jax.ShapeDtypeStruct((batch, seq, R_V), BF16),
        scratch_shapes=[pltpu.VMEM((R_KEY_DIM, R_VAL_DIM), F32)],
        compiler_params=pltpu.CompilerParams(
            dimension_semantics=("parallel", "parallel", "arbitrary"), vmem_limit_bytes=VMEM_LIMIT),
    )(p, p, p, p)


MERGE_TM = 512


def _merge_body(x_ref, ya_ref, rb_ref, ga_ref, gb_ref, wa_ref, wb_ref, wo_ref, o_ref):
    ya = jnp.dot(ya_ref[...].astype(BF16), wa_ref[...], preferred_element_type=F32)
    yb = jnp.dot(rb_ref[...], wb_ref[...], preferred_element_type=F32)
    ga = ga_ref[...].astype(F32)
    gb = gb_ref[...].astype(F32)
    merged = jax.nn.sigmoid(ga) * ya + jax.nn.sigmoid(gb) * yb
    o_ref[...] = x_ref[...] + jnp.dot(merged.astype(BF16), wo_ref[...], preferred_element_type=F32)


def _merge(x2d, ya2d, rb2d, proj, wa, wb, wo):
    m = x2d.shape[0]

    def full(shape):
        return pl.BlockSpec(shape, lambda i: (0, 0))

    return pl.pallas_call(
        _merge_body,
        grid=(m // MERGE_TM,),
        in_specs=[
            pl.BlockSpec((MERGE_TM, D_MODEL), lambda i: (i, 0)),
            pl.BlockSpec((MERGE_TM, A_OUT), lambda i: (i, 0)),
            pl.BlockSpec((MERGE_TM, R_V), lambda i: (i, 0)),
            pl.BlockSpec((MERGE_TM, D_MODEL), lambda i: (i, P_GA // D_MODEL)),
            pl.BlockSpec((MERGE_TM, D_MODEL), lambda i: (i, P_GB // D_MODEL)),
            full((A_OUT, D_MODEL)), full((R_V, D_MODEL)), full((D_MODEL, D_MODEL)),
        ],
        out_specs=pl.BlockSpec((MERGE_TM, D_MODEL), lambda i: (i, 0)),
        out_shape=jax.ShapeDtypeStruct((m, D_MODEL), F32),
        compiler_params=pltpu.CompilerParams(
            dimension_semantics=("parallel",), vmem_limit_bytes=VMEM_LIMIT),
    )(x2d, ya2d, rb2d, proj, proj, wa, wb, wo)


FFN_TM = 1024
FFN_TF = 1024


def _ffn_body(x_ref, g_ref, wu_ref, wd_ref, fg_ref, o_ref, h_ref, *, final):
    c = pl.program_id(1)

    @pl.when(c == 0)
    def _start():
        x = x_ref[...]
        h_ref[...] = _rms(x, g_ref[...]).astype(BF16)
        o_ref[...] = x

    up = jnp.dot(h_ref[...], wu_ref[...], preferred_element_type=F32)
    up = jnp.square(jnp.maximum(up, 0.0)).astype(BF16)
    o_ref[...] += jnp.dot(up, wd_ref[...], preferred_element_type=F32)

    if final:
        @pl.when(c == pl.num_programs(1) - 1)
        def _final_norm():
            o_ref[...] = _rms(o_ref[...], fg_ref[...])


def _ffn(x2d, gain, wu, wd, final_gain, final):
    m = x2d.shape[0]
    return pl.pallas_call(
        functools.partial(_ffn_body, final=final),
        grid=(m // FFN_TM, D_FF // FFN_TF),
        in_specs=[
            pl.BlockSpec((FFN_TM, D_MODEL), lambda i, c: (i, 0)),
            pl.BlockSpec((1, D_MODEL), lambda i, c: (0, 0)),
            pl.BlockSpec((D_MODEL, FFN_TF), lambda i, c: (0, c)),
            pl.BlockSpec((FFN_TF, D_MODEL), lambda i, c: (c, 0)),
            pl.BlockSpec((1, D_MODEL), lambda i, c: (0, 0)),
        ],
        out_specs=pl.BlockSpec((FFN_TM, D_MODEL), lambda i, c: (i, 0)),
        out_shape=jax.ShapeDtypeStruct((m, D_MODEL), F32),
        scratch_shapes=[pltpu.VMEM((FFN_TM, D_MODEL), BF16)],
        compiler_params=pltpu.CompilerParams(
            dimension_semantics=("parallel", "arbitrary"), vmem_limit_bytes=VMEM_LIMIT),
    )(x2d, gain, wu, wd, final_gain)


def kernel(x, mix_norm, w_in, w_a, w_b, w_o, ffn_norm, w_up, w_down, final_norm):
    batch, seq, _ = x.shape
    depth = w_in.shape[0]
    assert seq % A_SPAN == 0 and seq % RET_TB == 0
    tables = _rotary_tables(seq)
    final_gain = final_norm.reshape(1, D_MODEL)
    x2d = x.reshape(batch * seq, D_MODEL)
    for layer in range(depth):
        proj = _in_proj(x2d, mix_norm[layer].reshape(1, D_MODEL), _arrange_w_in(w_in[layer]), tables, seq)
        ya = _attention_a(proj, batch, seq).reshape(batch * seq, A_OUT)
        rb = _retention(proj, batch, seq).reshape(batch * seq, R_V)
        x2d = _merge(x2d, ya, rb, proj, w_a[layer].astype(BF16), w_b[layer].astype(BF16),
                     w_o[layer].astype(BF16))
        x2d = _ffn(x2d, ffn_norm[layer].reshape(1, D_MODEL), w_up[layer].astype(BF16),
                   w_down[layer].astype(BF16), final_gain, final=(layer == depth - 1))
    return x2d.reshape(batch, seq, D_MODEL)
```

```python
import functools

import numpy as np
import jax
import jax.numpy as jnp
from jax import lax
from jax.experimental import pallas as pl
from jax.experimental.pallas import tpu as pltpu

F32 = jnp.float32
BF16 = jnp.bfloat16

D_MODEL = 1024
N_GROUPS = 3
DILATIONS = (1, 4, 16)
A_HEADS = 8
A_HEAD_DIM = 64
A_GROUP_W = A_HEADS * A_HEAD_DIM
A_WIDTH = N_GROUPS * A_GROUP_W
A_OUT = A_GROUP_W
ROT_DIM = A_HEAD_DIM // 4
ROPE_THETA = 500000.0
ATT_BLOCK = 128
NEG_INF = -1e30
R_HEADS = 4
R_KEY_DIM = 128
R_VAL_DIM = 256
R_QK = R_HEADS * R_KEY_DIM
R_V = R_HEADS * R_VAL_DIM
R_CHUNK = 128
R_ROT_BASE = 10000.0
D_FF = 4 * D_MODEL
NORM_EPS = 1e-6

OFF_QA = 0
OFF_KA = A_WIDTH
OFF_VA = 2 * A_WIDTH
OFF_QR = 3 * A_WIDTH
OFF_KR = OFF_QR + R_QK
OFF_VR = OFF_KR + R_QK
OFF_GR = OFF_VR + R_V
OFF_GA = OFF_GR + R_V
OFF_GB = OFF_GA + D_MODEL

LANES = 128
VMEM_LIMIT = 56 * 1024 * 1024

A_SPAN = ATT_BLOCK * max(DILATIONS)
A_UNITS = A_SPAN // ATT_BLOCK

IN_TM = A_SPAN
IN_TN = 512
IN_STEP_TILES = 2
IN_CHUNKS = (512, 512, 512, 256, 256)
X_CHUNKS = D_MODEL // LANES

P_GA = 0
P_GB = P_GA + D_MODEL
P_VR = P_GB + D_MODEL
P_GR = P_VR + R_V
P_QR = P_GR + R_V
P_KR = P_QR + R_QK
P_QKV = tuple(P_KR + R_QK + g * 3 * A_GROUP_W for g in range(N_GROUPS))
P_WIDTH = P_QKV[-1] + 3 * A_GROUP_W

_TAB_R = 2 * N_GROUPS
_TAB_NONE = 2 * N_GROUPS + 1
TILE_TABLE = ((_TAB_NONE,) * (P_QR // IN_TN) + (_TAB_R,) * (2 * R_QK // IN_TN)
              + tuple(t for g in range(N_GROUPS) for t in (g, N_GROUPS + g, _TAB_NONE)))
LOG2_E = 1.4426950408889634
A_Q_SCALE = A_HEAD_DIM ** -0.5 * LOG2_E
TILE_ORDER = (0,) * (P_QKV[0] // IN_TN) + tuple(g for g in range(N_GROUPS) for _ in range(3))
N_TILES = P_WIDTH // IN_TN


def _rms(x, gain):
    ms = jnp.mean(x * x, axis=-1, keepdims=True)
    return x * lax.rsqrt(ms + NORM_EPS) * gain


def _step_lookup(j, values):
    out = values[-1]
    for t in range(len(values) - 2, -1, -1):
        if values[t] != values[t + 1]:
            out = jnp.where(j <= t, values[t], out)
    return out


def _in_proj_body(*refs):
    x_refs = refs[:X_CHUNKS]
    g_ref, w_ref = refs[X_CHUNKS:X_CHUNKS + 2]
    tab_refs = refs[X_CHUNKS + 2:X_CHUNKS + 2 + 2 * IN_STEP_TILES]
    o_ref, h_ref, inv_ref = refs[X_CHUNKS + 2 + 2 * IN_STEP_TILES:]
    j = pl.program_id(1)

    @pl.when(j == 0)
    def _norm():
        gains = [g_ref[:, c * LANES:(c + 1) * LANES] for c in range(X_CHUNKS)]
        for u in range(A_UNITS):
            rows = pl.ds(u * ATT_BLOCK, ATT_BLOCK)
            xc = [x_refs[c][rows, :] for c in range(X_CHUNKS)]
            sq = xc[0] * xc[0]
            for c in range(1, X_CHUNKS):
                sq = sq + xc[c] * xc[c]
            inv = lax.rsqrt(jnp.sum(sq, axis=-1, keepdims=True) * (1.0 / D_MODEL) + NORM_EPS)
            inv = jnp.broadcast_to(inv, (ATT_BLOCK, LANES))
            inv_ref[rows, :] = inv
            for c in range(X_CHUNKS):
                h_ref[0, rows, c * LANES:(c + 1) * LANES] = (xc[c] * inv * gains[c]).astype(BF16)
        for g in range(1, N_GROUPS):
            dil = DILATIONS[g]
            span = ATT_BLOCK * dil
            for ss in range(IN_TM // span):
                for r in range(dil):
                    u = ss * dil + r
                    rows = pl.ds(ss * span + r, ATT_BLOCK, stride=dil)
                    dst = pl.ds(u * ATT_BLOCK, ATT_BLOCK)
                    inv = inv_ref[rows, :]
                    for c in range(X_CHUNKS):
                        h_ref[g, dst, c * LANES:(c + 1) * LANES] = (
                            x_refs[c][rows, :] * inv * gains[c]).astype(BF16)

    def project(k):
        tile = j * IN_STEP_TILES + k
        order = _step_lookup(tile, TILE_ORDER)
        cos_ref, sin_ref = tab_refs[2 * k], tab_refs[2 * k + 1]
        cols = slice(k * IN_TN, (k + 1) * IN_TN)
        starts = np.cumsum((0,) + IN_CHUNKS)
        for start, size in zip(starts, IN_CHUNKS):
            rows = pl.ds(int(start), size)
            acc = jnp.dot(h_ref[order, rows, :], w_ref[:, cols], preferred_element_type=F32)
            for c in range(IN_TN // LANES):
                t = acc[:, c * LANES:(c + 1) * LANES]
                lanes = slice(k * IN_TN + c * LANES, k * IN_TN + (c + 1) * LANES)
                o_ref[rows, lanes] = (
                    t * cos_ref[rows, :] + pltpu.roll(t, LANES // 2, 1) * sin_ref[rows, :]).astype(BF16)

    project(0)
    for k in range(1, IN_STEP_TILES):
        pl.when(j * IN_STEP_TILES + k < N_TILES)(functools.partial(project, k))


def _pair_layout(t):
    half = ROT_DIM // 2
    a, b = t[..., 0, :], t[..., 1, :]
    return jnp.concatenate([a[..., :half], b[..., :half], a[..., ROT_DIM:],
                            a[..., half:ROT_DIM], b[..., half:ROT_DIM], b[..., ROT_DIM:]], axis=-1)


def _arrange_w_in(w):
    rows = w.shape[0]

    def pairs(cols):
        return _pair_layout(cols.reshape(rows, N_GROUPS, A_GROUP_W // LANES, 2, A_HEAD_DIM)).reshape(
            rows, N_GROUPS, A_GROUP_W)

    qa = pairs(w[:, OFF_QA:OFF_KA])
    ka = pairs(w[:, OFF_KA:OFF_VA])
    va = w[:, OFF_VA:OFF_QR].reshape(rows, N_GROUPS, A_GROUP_W)
    parts = [w[:, OFF_GA:OFF_GB], w[:, OFF_GB:], w[:, OFF_VR:OFF_GR], w[:, OFF_GR:OFF_GA],
             w[:, OFF_QR:OFF_KR], w[:, OFF_KR:OFF_VR]]
    for g in range(N_GROUPS):
        parts += [qa[:, g], ka[:, g], va[:, g]]
    return jnp.concatenate(parts, axis=1).astype(BF16)


def _rotary_tables(seq):
    half = ROT_DIM // 2
    inv = 1.0 / (ROPE_THETA ** (jnp.arange(half, dtype=F32) / half))
    rest = A_HEAD_DIM - ROT_DIM
    ones, zeros = jnp.ones((seq, rest), F32), jnp.zeros((seq, rest), F32)
    cos_tabs, sin_tabs = [], []
    for dil in DILATIONS:
        span = ATT_BLOCK * dil
        pos = jnp.arange(seq, dtype=F32).reshape(seq // span, ATT_BLOCK, dil).transpose(0, 2, 1).reshape(seq, 1)
        ang = pos * inv[None, :]
        cos, sin = jnp.cos(ang), jnp.sin(ang)
        cos_tabs.append(jnp.concatenate([cos, cos, ones, cos, cos, ones], axis=1))
        sin_tabs.append(jnp.concatenate([-sin, -sin, zeros, sin, sin, zeros], axis=1))
    cos_tabs = [t * A_Q_SCALE for t in cos_tabs] + cos_tabs
    sin_tabs = [t * A_Q_SCALE for t in sin_tabs] + sin_tabs
    half_r = R_KEY_DIM // 2
    inv_r = 1.0 / (R_ROT_BASE ** (jnp.arange(half_r, dtype=F32) / half_r))
    ang_r = jnp.arange(seq, dtype=F32)[:, None] * inv_r[None, :]
    cos_tabs += [jnp.concatenate([jnp.cos(ang_r), jnp.cos(ang_r)], axis=1), jnp.ones((seq, LANES), F32)]
    sin_tabs += [jnp.concatenate([-jnp.sin(ang_r), jnp.sin(ang_r)], axis=1), jnp.zeros((seq, LANES), F32)]
    return jnp.stack(cos_tabs), jnp.stack(sin_tabs)


def _in_proj(x2d, gain, w_bf16, tables, seq):
    m = x2d.shape[0]
    tiles_per_seq = seq // IN_TM
    step_w = IN_STEP_TILES * IN_TN

    def tab_spec(k):
        def index(i, j):
            tile = jnp.minimum(j * IN_STEP_TILES + k, N_TILES - 1)
            return _step_lookup(tile, TILE_TABLE), i % tiles_per_seq, 0
        return pl.BlockSpec((None, IN_TM, LANES), index)

    tab_specs = [tab_spec(k) for k in range(IN_STEP_TILES) for _ in range(2)]
    tab_args = [tables[t] for _ in range(IN_STEP_TILES) for t in range(2)]
    x_specs = [pl.BlockSpec((IN_TM, LANES), lambda i, j, c=c: (i, c)) for c in range(X_CHUNKS)]
    return pl.pallas_call(
        _in_proj_body,
        grid=(m // IN_TM, pl.cdiv(P_WIDTH, step_w)),
        in_specs=x_specs + [
            pl.BlockSpec((1, D_MODEL), lambda i, j: (0, 0)),
            pl.BlockSpec((D_MODEL, step_w), lambda i, j: (0, j)),
        ] + tab_specs,
        out_specs=pl.BlockSpec((IN_TM, step_w), lambda i, j: (i, j)),
        out_shape=jax.ShapeDtypeStruct((m, P_WIDTH), BF16),
        scratch_shapes=[pltpu.VMEM((N_GROUPS, IN_TM, D_MODEL), BF16),
                        pltpu.VMEM((IN_TM, LANES), F32)],
        compiler_params=pltpu.CompilerParams(
            dimension_semantics=("parallel", "arbitrary"), vmem_limit_bytes=VMEM_LIMIT),
    )(*([x2d] * X_CHUNKS), gain, w_bf16, *tab_args)


A_HEADS_PER_STEP = LANES // A_HEAD_DIM
A_QUAD = 4


def _attn_body(*refs):
    q_in = refs[0:3]
    kc_in = refs[3:6]
    kp_in = refs[6:9]
    vc_in = refs[9:12]
    vp_in = refs[12:15]
    o_ref = refs[15]
    ro, rl, bias, s_buf, p_buf, m_buf = refs[16:22]
    kbuf, vbuf = refs[22:25], refs[25:28]
    jt = pl.program_id(1)

    qi = lax.broadcasted_iota(jnp.int32, (ATT_BLOCK, 2 * ATT_BLOCK), 0)
    kj = lax.broadcasted_iota(jnp.int32, (ATT_BLOCK, 2 * ATT_BLOCK), 1)
    dist = qi + ATT_BLOCK - kj
    band = (dist >= 0) & (dist <= ATT_BLOCK)
    bias[0] = jnp.where(band & (kj >= ATT_BLOCK), 0.0, NEG_INF).astype(BF16)
    bias[1] = jnp.where(band, 0.0, NEG_INF).astype(BF16)
    lane = lax.broadcasted_iota(jnp.int32, (ATT_BLOCK, LANES), 1)

    half = ROT_DIM // 2
    first_head = (lane < half) | ((lane >= ROT_DIM) & (lane < A_HEAD_DIM + half))
    head_lanes = (first_head, jnp.logical_not(first_head))
    first = lane < A_HEAD_DIM
    v_lane = lax.broadcasted_iota(jnp.int32, (2 * ATT_BLOCK, LANES), 1)
    v_lanes = (v_lane < A_HEAD_DIM, v_lane >= A_HEAD_DIM)

    for g, d in enumerate(DILATIONS):
        span = ATT_BLOCK * d
        kbuf[g][0:span] = kp_in[g][...]
        kbuf[g][span:] = kc_in[g][...]
        vbuf[g][0:span] = vp_in[g][...]
        vbuf[g][span:] = vc_in[g][...]

    def rows_of(u):
        return pl.ds(u * ATT_BLOCK if isinstance(u, int) else pl.multiple_of(u * ATT_BLOCK, ATT_BLOCK), ATT_BLOCK)

    def stage_scores(u, g, d):
        b = bias[jnp.where(jnp.logical_or(jt > 0, u >= d), 1, 0)]
        q = q_in[g][rows_of(u), :]
        k2 = jnp.concatenate([kbuf[g][rows_of(u), :], kbuf[g][rows_of(u + d), :]], axis=0)
        for h in range(A_HEADS_PER_STEP):
            qm = jnp.where(head_lanes[h], q, jnp.zeros_like(q))
            s = lax.dot_general(qm, k2, (((1,), (1,)), ((), ())), preferred_element_type=F32)
            s_buf[u, h] = s.astype(BF16) + b

    def stage_softmax(u):
        ms = []
        for h in range(A_HEADS_PER_STEP):
            s = s_buf[u, h]
            m = jnp.max(s, axis=1, keepdims=True)
            p_buf[u, h] = jnp.exp2(s - m)
            ms.append(m.astype(F32))
        m_buf[u] = jnp.where(first, ms[0], ms[1])

    def stage_out(u, g, d):
        v2 = jnp.concatenate([vbuf[g][rows_of(u), :], vbuf[g][rows_of(u + d), :]], axis=0)
        pvs = [jnp.dot(p_buf[u, h], jnp.where(v_lanes[h], v2, jnp.ones_like(v2)), preferred_element_type=F32)
               for h in range(A_HEADS_PER_STEP)]
        out = jnp.where(first, pvs[0], pvs[1])
        den = pltpu.roll(jnp.where(first, pvs[1], pvs[0]), A_HEAD_DIM, 1)
        ro[g, rows_of(u), :] = out / den
        rl[g, rows_of(u), :] = m_buf[u] + jnp.log(den) * LOG2_E

    def quad(stage, qd, *args):
        for i in range(A_QUAD):
            stage(qd * A_QUAD + i, *args)

    n_quads = A_UNITS // A_QUAD
    for g, d in enumerate(DILATIONS):
        quad(stage_scores, 0, g, d)
        quad(stage_softmax, 0)
        quad(stage_scores, 1, g, d)

        def trip(qd, carry, g=g, d=d):
            quad(stage_out, qd, g, d)
            quad(stage_softmax, qd + 1)
            quad(stage_scores, qd + 2, g, d)
            return carry

        lax.fori_loop(0, n_quads - 2, trip, 0)
        quad(stage_out, n_quads - 2, g, d)
        quad(stage_softmax, n_quads - 1)
        quad(stage_out, n_quads - 1, g, d)

    d2, d3 = DILATIONS[1], DILATIONS[2]
    span2 = ATT_BLOCK * d2
    per_ss2 = ATT_BLOCK * d2 // d3
    for r3 in range(d3):
        blk = slice(r3 * ATT_BLOCK, (r3 + 1) * ATT_BLOCK)
        tok = pl.ds(r3, ATT_BLOCK, stride=d3)
        r2, off = r3 % d2, r3 // d2

        def from_g2(buf):
            return jnp.concatenate(
                [buf[1, pl.ds(ss2 * span2 + r2 * ATT_BLOCK + off, per_ss2, stride=d3 // d2), :]
                 for ss2 in range(A_SPAN // span2)], axis=0)

        o1, l1 = ro[0, tok, :], rl[0, tok, :]
        o2, l2 = from_g2(ro), from_g2(rl)
        o3, l3 = ro[2, blk, :], rl[2, blk, :]
        mx = jnp.maximum(jnp.maximum(l1, l2), l3)
        e1, e2, e3 = jnp.exp2(l1 - mx), jnp.exp2(l2 - mx), jnp.exp2(l3 - mx)
        o_ref[tok, :] = (e1 * o1 + e2 * o2 + e3 * o3) / (e1 + e2 + e3)


def _attention_a(proj, batch, seq):
    p = proj.reshape(batch, seq, P_WIDTH)

    def cur_spec(g, part):
        col = (P_QKV[g] + part * A_GROUP_W) // LANES
        return pl.BlockSpec((None, A_SPAN, LANES), lambda b, j, hp: (b, j, col + hp))

    def prev_spec(g, part):
        col = (P_QKV[g] + part * A_GROUP_W) // LANES
        span = ATT_BLOCK * DILATIONS[g]
        per = A_SPAN // span
        return pl.BlockSpec((None, span, LANES),
                            lambda b, j, hp: (b, jnp.maximum(j * per - 1, 0), col + hp))

    groups = range(N_GROUPS)
    kv_bufs = [pltpu.VMEM((ATT_BLOCK * d + A_SPAN, LANES), BF16) for d in DILATIONS]
    in_specs = ([cur_spec(g, 0) for g in groups] + [cur_spec(g, 1) for g in groups]
                + [prev_spec(g, 1) for g in groups] + [cur_spec(g, 2) for g in groups]
                + [prev_spec(g, 2) for g in groups])
    return pl.pallas_call(
        _attn_body,
        grid=(batch, seq // A_SPAN, A_GROUP_W // LANES),
        in_specs=in_specs,
        out_specs=pl.BlockSpec((None, A_SPAN, LANES), lambda b, j, hp: (b, j, hp)),
        out_shape=jax.ShapeDtypeStruct((batch, seq, A_OUT), F32),
        scratch_shapes=[pltpu.VMEM((N_GROUPS, A_SPAN, LANES), F32),
                        pltpu.VMEM((N_GROUPS, A_SPAN, LANES), F32),
                        pltpu.VMEM((2, ATT_BLOCK, 2 * ATT_BLOCK), BF16),
                        pltpu.VMEM((A_UNITS, A_HEADS_PER_STEP, ATT_BLOCK, 2 * ATT_BLOCK), BF16),
                        pltpu.VMEM((A_UNITS, A_HEADS_PER_STEP, ATT_BLOCK, 2 * ATT_BLOCK), BF16),
                        pltpu.VMEM((A_UNITS, ATT_BLOCK, LANES), F32)] + kv_bufs + kv_bufs,
        compiler_params=pltpu.CompilerParams(
            dimension_semantics=("parallel", "parallel", "parallel"), vmem_limit_bytes=VMEM_LIMIT),
    )(*([p] * 15))


RET_TB = 2048
_LOG_GAMMA = [float(v) for v in np.log1p(-(2.0 ** (-5.0 - np.arange(R_HEADS)))).astype(np.float32)]


def _ret_body(q_ref, k_ref, v_ref, g_ref, o_ref, st_ref):
    h = pl.program_id(1)
    t = pl.program_id(2)

    @pl.when(t == 0)
    def _reset():
        st_ref[...] = jnp.zeros_like(st_ref)

    lg = jnp.float32(_LOG_GAMMA[R_HEADS - 1])
    for hh in range(R_HEADS - 2, -1, -1):
        lg = jnp.where(h == hh, jnp.float32(_LOG_GAMMA[hh]), lg)
    row = lax.broadcasted_iota(jnp.int32, (R_CHUNK, R_CHUNK), 0).astype(F32)
    colv = lax.broadcasted_iota(jnp.int32, (R_CHUNK, R_CHUNK), 1).astype(F32)
    diff = row - colv
    decay = jnp.where(diff >= 0, jnp.exp(diff * lg), 0.0)
    zeta = jnp.exp((R_CHUNK - 1 - row) * lg)
    xi = jnp.exp((row + 1.0) * lg)
    chunk_decay = jnp.exp(jnp.full((R_KEY_DIM, R_VAL_DIM), float(R_CHUNK), F32) * lg)
    kscale = R_KEY_DIM ** -0.5

    for c in range(RET_TB // R_CHUNK):
        sl = slice(c * R_CHUNK, (c + 1) * R_CHUNK)
        q = q_ref[sl, :].astype(F32)
        k = k_ref[sl, :].astype(F32) * kscale
        v = v_ref[sl, :]
        s = lax.dot_general(q.astype(BF16), k.astype(BF16), (((1,), (1,)), ((), ())),
                            preferred_element_type=F32) * decay
        inner = jnp.dot(s.astype(BF16), v, preferred_element_type=F32)
        state = st_ref[...]
        cross = jnp.dot((q * xi).astype(BF16), state.astype(BF16), preferred_element_type=F32)
        kz_t = (k * zeta).T.astype(BF16)
        st_ref[...] = state * chunk_decay + jnp.dot(kz_t, v, preferred_element_type=F32)
        out = inner + cross
        ms = jnp.mean(out * out, axis=-1, keepdims=True)
        gate = g_ref[sl, :].astype(F32)
        o_ref[sl, :] = (gate * jax.nn.sigmoid(gate) * (out * lax.rsqrt(ms + NORM_EPS))).astype(BF16)


def _retention(proj, batch, seq):
    p = proj.reshape(batch, seq, P_WIDTH)
    return pl.pallas_call(
        _ret_body,
        grid=(batch, R_HEADS, seq // RET_TB),
        in_specs=[
            pl.BlockSpec((None, RET_TB, R_KEY_DIM), lambda b, h, t: (b, t, P_QR // R_KEY_DIM + h)),
            pl.BlockSpec((None, RET_TB, R_KEY_DIM), lambda b, h, t: (b, t, P_KR // R_KEY_DIM + h)),
            pl.BlockSpec((None, RET_TB, R_VAL_DIM), lambda b, h, t: (b, t, P_VR // R_VAL_DIM + h)),
            pl.BlockSpec((None, RET_TB, R_VAL_DIM), lambda b, h, t: (b, t, P_GR // R_VAL_DIM + h)),
        ],
        out_specs=pl.BlockSpec((None, RET_TB, R_VAL_DIM), lambda b, h, t: (b, t, h)),
        out_shape=jax.ShapeDtypeStruct((batch, seq, R_V), BF16),
        scratch_shapes=[pltpu.VMEM((R_KEY_DIM, R_VAL_DIM), F32)],
        compiler_params=pltpu.CompilerParams(
            dimension_semantics=("parallel", "parallel", "arbitrary"), vmem_limit_bytes=VMEM_LIMIT),
    )(p, p, p, p)


MERGE_TM = 1024


def _merge_body(x_ref, ya_ref, rb_ref, ga_ref, gb_ref, wa_ref, wb_ref, wo_ref, o_ref):
    ya = jnp.dot(ya_ref[...].astype(BF16), wa_ref[...], preferred_element_type=F32)
    yb = jnp.dot(rb_ref[...], wb_ref[...], preferred_element_type=F32)
    ga = ga_ref[...].astype(F32)
    gb = gb_ref[...].astype(F32)
    merged = jax.nn.sigmoid(ga) * ya + jax.nn.sigmoid(gb) * yb
    o_ref[...] = x_ref[...] + jnp.dot(merged.astype(BF16), wo_ref[...], preferred_element_type=F32)


def _merge(x2d, ya2d, rb2d, proj, wa, wb, wo):
    m = x2d.shape[0]

    def full(shape):
        return pl.BlockSpec(shape, lambda i: (0, 0))

    return pl.pallas_call(
        _merge_body,
        grid=(m // MERGE_TM,),
        in_specs=[
            pl.BlockSpec((MERGE_TM, D_MODEL), lambda i: (i, 0)),
            pl.BlockSpec((MERGE_TM, A_OUT), lambda i: (i, 0)),
            pl.BlockSpec((MERGE_TM, R_V), lambda i: (i, 0)),
            pl.BlockSpec((MERGE_TM, D_MODEL), lambda i: (i, P_GA // D_MODEL)),
            pl.BlockSpec((MERGE_TM, D_MODEL), lambda i: (i, P_GB // D_MODEL)),
            full((A_OUT, D_MODEL)), full((R_V, D_MODEL)), full((D_MODEL, D_MODEL)),
        ],
        out_specs=pl.BlockSpec((MERGE_TM, D_MODEL), lambda i: (i, 0)),
        out_shape=jax.ShapeDtypeStruct((m, D_MODEL), F32),
        compiler_params=pltpu.CompilerParams(
            dimension_semantics=("parallel",), vmem_limit_bytes=VMEM_LIMIT),
    )(x2d, ya2d, rb2d, proj, proj, wa, wb, wo)


FFN_TM = 1024
FFN_TF = 2048


def _ffn_body(x_ref, g_ref, wu_ref, wd_ref, fg_ref, o_ref, h_ref, *, final):
    c = pl.program_id(1)

    @pl.when(c == 0)
    def _start():
        x = x_ref[...]
        h_ref[...] = _rms(x, g_ref[...]).astype(BF16)
        o_ref[...] = x

    up = jnp.dot(h_ref[...], wu_ref[...], preferred_element_type=F32)
    up = jnp.square(jnp.maximum(up, 0.0)).astype(BF16)
    o_ref[...] += jnp.dot(up, wd_ref[...], preferred_element_type=F32)

    if final:
        @pl.when(c == pl.num_programs(1) - 1)
        def _final_norm():
            o_ref[...] = _rms(o_ref[...], fg_ref[...])


def _ffn(x2d, gain, wu, wd, final_gain, final):
    m = x2d.shape[0]
    return pl.pallas_call(
        functools.partial(_ffn_body, final=final),
        grid=(m // FFN_TM, D_FF // FFN_TF),
        in_specs=[
            pl.BlockSpec((FFN_TM, D_MODEL), lambda i, c: (i, 0)),
            pl.BlockSpec((1, D_MODEL), lambda i, c: (0, 0)),
            pl.BlockSpec((D_MODEL, FFN_TF), lambda i, c: (0, c)),
            pl.BlockSpec((FFN_TF, D_MODEL), lambda i, c: (c, 0)),
            pl.BlockSpec((1, D_MODEL), lambda i, c: (0, 0)),
        ],
        out_specs=pl.BlockSpec((FFN_TM, D_MODEL), lambda i, c: (i, 0)),
        out_shape=jax.ShapeDtypeStruct((m, D_MODEL), F32),
        scratch_shapes=[pltpu.VMEM((FFN_TM, D_MODEL), BF16)],
        compiler_params=pltpu.CompilerParams(
            dimension_semantics=("parallel", "arbitrary"), vmem_limit_bytes=VMEM_LIMIT),
    )(x2d, gain, wu, wd, final_gain)


def kernel(x, mix_norm, w_in, w_a, w_b, w_o, ffn_norm, w_up, w_down, final_norm):
    batch, seq, _ = x.shape
    depth = w_in.shape[0]
    assert seq % A_SPAN == 0 and seq % RET_TB == 0
    tables = _rotary_tables(seq)
    final_gain = final_norm.reshape(1, D_MODEL)
    x2d = x.reshape(batch * seq, D_MODEL)
    for layer in range(depth):
        proj = _in_proj(x2d, mix_norm[layer].reshape(1, D_MODEL), _arrange_w_in(w_in[layer]), tables, seq)
        ya = _attention_a(proj, batch, seq).reshape(batch * seq, A_OUT)
        rb = _retention(proj, batch, seq).reshape(batch * seq, R_V)
        x2d = _merge(x2d, ya, rb, proj, w_a[layer].astype(BF16), w_b[layer].astype(BF16),
                     w_o[layer].astype(BF16))
        x2d = _ffn(x2d, ffn_norm[layer].reshape(1, D_MODEL), w_up[layer].astype(BF16),
                   w_down[layer].astype(BF16), final_gain, final=(layer == depth - 1))
    return x2d.reshape(batch, seq, D_MODEL)
```

```python
import functools

import numpy as np
import jax
import jax.numpy as jnp
from jax import lax
from jax.experimental import pallas as pl
from jax.experimental.pallas import tpu as pltpu

F32 = jnp.float32
BF16 = jnp.bfloat16

D_MODEL = 1024
N_GROUPS = 3
DILATIONS = (1, 4, 16)
A_HEADS = 8
A_HEAD_DIM = 64
A_GROUP_W = A_HEADS * A_HEAD_DIM
A_WIDTH = N_GROUPS * A_GROUP_W
A_OUT = A_GROUP_W
ROT_DIM = A_HEAD_DIM // 4
ROPE_THETA = 500000.0
ATT_BLOCK = 128
NEG_INF = -1e30
R_HEADS = 4
R_KEY_DIM = 128
R_VAL_DIM = 256
R_QK = R_HEADS * R_KEY_DIM
R_V = R_HEADS * R_VAL_DIM
R_CHUNK = 128
R_ROT_BASE = 10000.0
D_FF = 4 * D_MODEL
NORM_EPS = 1e-6

OFF_QA = 0
OFF_KA = A_WIDTH
OFF_VA = 2 * A_WIDTH
OFF_QR = 3 * A_WIDTH
OFF_KR = OFF_QR + R_QK
OFF_VR = OFF_KR + R_QK
OFF_GR = OFF_VR + R_V
OFF_GA = OFF_GR + R_V
OFF_GB = OFF_GA + D_MODEL

LANES = 128
VMEM_LIMIT = 56 * 1024 * 1024

A_SPAN = ATT_BLOCK * max(DILATIONS)
A_UNITS = A_SPAN // ATT_BLOCK

IN_TM = A_SPAN
IN_TN = 512
IN_CHUNKS = (512, 512, 512, 256, 256)
X_CHUNKS = D_MODEL // LANES

P_GA = 0
P_GB = P_GA + D_MODEL
P_VR = P_GB + D_MODEL
P_GR = P_VR + R_V
P_QR = P_GR + R_V
P_KR = P_QR + R_QK
P_QKV = tuple(P_KR + R_QK + g * 3 * A_GROUP_W for g in range(N_GROUPS))
P_WIDTH = P_QKV[-1] + 3 * A_GROUP_W

_TAB_R = 2 * N_GROUPS
_TAB_NONE = 2 * N_GROUPS + 1
TILE_TABLE = ((_TAB_NONE,) * (P_QR // IN_TN) + (_TAB_R,) * (2 * R_QK // IN_TN)
              + tuple(t for g in range(N_GROUPS) for t in (g, N_GROUPS + g, _TAB_NONE)))
LOG2_E = 1.4426950408889634
A_Q_SCALE = A_HEAD_DIM ** -0.5 * LOG2_E
TILE_ORDER = (0,) * (P_QKV[0] // IN_TN) + tuple(g for g in range(N_GROUPS) for _ in range(3))


def _rms(x, gain):
    ms = jnp.mean(x * x, axis=-1, keepdims=True)
    return x * lax.rsqrt(ms + NORM_EPS) * gain


def _step_lookup(j, values):
    out = values[-1]
    for t in range(len(values) - 2, -1, -1):
        if values[t] != values[t + 1]:
            out = jnp.where(j <= t, values[t], out)
    return out


def _in_proj_body(*refs):
    x_refs = refs[:X_CHUNKS]
    g_ref, w_ref, cos_ref, sin_ref, o_ref, h_ref, inv_ref = refs[X_CHUNKS:]
    j = pl.program_id(1)

    @pl.when(j == 0)
    def _norm():
        gains = [g_ref[:, c * LANES:(c + 1) * LANES] for c in range(X_CHUNKS)]
        for u in range(A_UNITS):
            rows = pl.ds(u * ATT_BLOCK, ATT_BLOCK)
            xc = [x_refs[c][rows, :] for c in range(X_CHUNKS)]
            sq = xc[0] * xc[0]
            for c in range(1, X_CHUNKS):
                sq = sq + xc[c] * xc[c]
            inv = lax.rsqrt(jnp.sum(sq, axis=-1, keepdims=True) * (1.0 / D_MODEL) + NORM_EPS)
            inv = jnp.broadcast_to(inv, (ATT_BLOCK, LANES))
            inv_ref[rows, :] = inv
            for c in range(X_CHUNKS):
                h_ref[0, rows, c * LANES:(c + 1) * LANES] = (xc[c] * inv * gains[c]).astype(BF16)
        for g in range(1, N_GROUPS):
            dil = DILATIONS[g]
            span = ATT_BLOCK * dil
            for ss in range(IN_TM // span):
                for r in range(dil):
                    u = ss * dil + r
                    rows = pl.ds(ss * span + r, ATT_BLOCK, stride=dil)
                    dst = pl.ds(u * ATT_BLOCK, ATT_BLOCK)
                    inv = inv_ref[rows, :]
                    for c in range(X_CHUNKS):
                        h_ref[g, dst, c * LANES:(c + 1) * LANES] = (
                            x_refs[c][rows, :] * inv * gains[c]).astype(BF16)

    order = _step_lookup(j, TILE_ORDER)
    starts = np.cumsum((0,) + IN_CHUNKS)
    for start, size in zip(starts, IN_CHUNKS):
        rows = pl.ds(int(start), size)
        acc = jnp.dot(h_ref[order, rows, :], w_ref[...], preferred_element_type=F32)
        for c in range(IN_TN // LANES):
            t = acc[:, c * LANES:(c + 1) * LANES]
            o_ref[rows, c * LANES:(c + 1) * LANES] = (
                t * cos_ref[rows, :] + pltpu.roll(t, LANES // 2, 1) * sin_ref[rows, :]).astype(BF16)


def _pair_layout(t):
    half = ROT_DIM // 2
    a, b = t[..., 0, :], t[..., 1, :]
    return jnp.concatenate([a[..., :half], b[..., :half], a[..., ROT_DIM:],
                            a[..., half:ROT_DIM], b[..., half:ROT_DIM], b[..., ROT_DIM:]], axis=-1)


def _arrange_w_in(w):
    rows = w.shape[0]

    def pairs(cols):
        return _pair_layout(cols.reshape(rows, N_GROUPS, A_GROUP_W // LANES, 2, A_HEAD_DIM)).reshape(
            rows, N_GROUPS, A_GROUP_W)

    qa = pairs(w[:, OFF_QA:OFF_KA])
    ka = pairs(w[:, OFF_KA:OFF_VA])
    va = w[:, OFF_VA:OFF_QR].reshape(rows, N_GROUPS, A_GROUP_W)
    parts = [w[:, OFF_GA:OFF_GB], w[:, OFF_GB:], w[:, OFF_VR:OFF_GR], w[:, OFF_GR:OFF_GA],
             w[:, OFF_QR:OFF_KR], w[:, OFF_KR:OFF_VR]]
    for g in range(N_GROUPS):
        parts += [qa[:, g], ka[:, g], va[:, g]]
    return jnp.concatenate(parts, axis=1).astype(BF16)


def _rotary_tables(seq):
    half = ROT_DIM // 2
    inv = 1.0 / (ROPE_THETA ** (jnp.arange(half, dtype=F32) / half))
    rest = A_HEAD_DIM - ROT_DIM
    ones, zeros = jnp.ones((seq, rest), F32), jnp.zeros((seq, rest), F32)
    cos_tabs, sin_tabs = [], []
    for dil in DILATIONS:
        span = ATT_BLOCK * dil
        pos = jnp.arange(seq, dtype=F32).reshape(seq // span, ATT_BLOCK, dil).transpose(0, 2, 1).reshape(seq, 1)
        ang = pos * inv[None, :]
        cos, sin = jnp.cos(ang), jnp.sin(ang)
        cos_tabs.append(jnp.concatenate([cos, cos, ones, cos, cos, ones], axis=1))
        sin_tabs.append(jnp.concatenate([-sin, -sin, zeros, sin, sin, zeros], axis=1))
    cos_tabs = [t * A_Q_SCALE for t in cos_tabs] + cos_tabs
    sin_tabs = [t * A_Q_SCALE for t in sin_tabs] + sin_tabs
    half_r = R_KEY_DIM // 2
    inv_r = 1.0 / (R_ROT_BASE ** (jnp.arange(half_r, dtype=F32) / half_r))
    ang_r = jnp.arange(seq, dtype=F32)[:, None] * inv_r[None, :]
    cos_tabs += [jnp.concatenate([jnp.cos(ang_r), jnp.cos(ang_r)], axis=1), jnp.ones((seq, LANES), F32)]
    sin_tabs += [jnp.concatenate([-jnp.sin(ang_r), jnp.sin(ang_r)], axis=1), jnp.zeros((seq, LANES), F32)]
    return jnp.stack(cos_tabs), jnp.stack(sin_tabs)


def _in_proj(x2d, gain, w_bf16, tables, seq):
    m = x2d.shape[0]
    tiles_per_seq = seq // IN_TM
    tab_spec = pl.BlockSpec((None, IN_TM, LANES),
                            lambda i, j: (_step_lookup(j, TILE_TABLE), i % tiles_per_seq, 0))
    x_specs = [pl.BlockSpec((IN_TM, LANES), lambda i, j, c=c: (i, c)) for c in range(X_CHUNKS)]
    return pl.pallas_call(
        _in_proj_body,
        grid=(m // IN_TM, P_WIDTH // IN_TN),
        in_specs=x_specs + [
            pl.BlockSpec((1, D_MODEL), lambda i, j: (0, 0)),
            pl.BlockSpec((D_MODEL, IN_TN), lambda i, j: (0, j)),
            tab_spec, tab_spec,
        ],
        out_specs=pl.BlockSpec((IN_TM, IN_TN), lambda i, j: (i, j)),
        out_shape=jax.ShapeDtypeStruct((m, P_WIDTH), BF16),
        scratch_shapes=[pltpu.VMEM((N_GROUPS, IN_TM, D_MODEL), BF16),
                        pltpu.VMEM((IN_TM, LANES), F32)],
        compiler_params=pltpu.CompilerParams(
            dimension_semantics=("parallel", "arbitrary"), vmem_limit_bytes=VMEM_LIMIT),
    )(*([x2d] * X_CHUNKS), gain, w_bf16, *tables)


A_HEADS_PER_STEP = LANES // A_HEAD_DIM
A_QUAD = 4


def _attn_body(*refs):
    q_in = refs[0:3]
    kc_in = refs[3:6]
    kp_in = refs[6:9]
    vc_in = refs[9:12]
    vp_in = refs[12:15]
    o_ref = refs[15]
    ro, rl, bias, s_buf, p_buf, m_buf = refs[16:22]
    kbuf, vbuf = refs[22:25], refs[25:28]
    jt = pl.program_id(1)

    qi = lax.broadcasted_iota(jnp.int32, (ATT_BLOCK, 2 * ATT_BLOCK), 0)
    kj = lax.broadcasted_iota(jnp.int32, (ATT_BLOCK, 2 * ATT_BLOCK), 1)
    dist = qi + ATT_BLOCK - kj
    band = (dist >= 0) & (dist <= ATT_BLOCK)
    bias[0] = jnp.where(band & (kj >= ATT_BLOCK), 0.0, NEG_INF).astype(BF16)
    bias[1] = jnp.where(band, 0.0, NEG_INF).astype(BF16)
    lane = lax.broadcasted_iota(jnp.int32, (ATT_BLOCK, LANES), 1)

    half = ROT_DIM // 2
    first_head = (lane < half) | ((lane >= ROT_DIM) & (lane < A_HEAD_DIM + half))
    head_lanes = (first_head, jnp.logical_not(first_head))
    first = lane < A_HEAD_DIM
    v_lane = lax.broadcasted_iota(jnp.int32, (2 * ATT_BLOCK, LANES), 1)
    v_lanes = (v_lane < A_HEAD_DIM, v_lane >= A_HEAD_DIM)

    for g, d in enumerate(DILATIONS):
        span = ATT_BLOCK * d
        kbuf[g][0:span] = kp_in[g][...]
        kbuf[g][span:] = kc_in[g][...]
        vbuf[g][0:span] = vp_in[g][...]
        vbuf[g][span:] = vc_in[g][...]

    def rows_of(u):
        return pl.ds(u * ATT_BLOCK if isinstance(u, int) else pl.multiple_of(u * ATT_BLOCK, ATT_BLOCK), ATT_BLOCK)

    def stage_scores(u, g, d):
        b = bias[jnp.where(jnp.logical_or(jt > 0, u >= d), 1, 0)]
        q = q_in[g][rows_of(u), :]
        k2 = jnp.concatenate([kbuf[g][rows_of(u), :], kbuf[g][rows_of(u + d), :]], axis=0)
        for h in range(A_HEADS_PER_STEP):
            qm = jnp.where(head_lanes[h], q, jnp.zeros_like(q))
            s = lax.dot_general(qm, k2, (((1,), (1,)), ((), ())), preferred_element_type=F32)
            s_buf[u, h] = s.astype(BF16) + b

    def stage_softmax(u):
        ms = []
        for h in range(A_HEADS_PER_STEP):
            s = s_buf[u, h]
            m = jnp.max(s, axis=1, keepdims=True)
            p_buf[u, h] = jnp.exp2(s - m)
            ms.append(m.astype(F32))
        m_buf[u] = jnp.where(first, ms[0], ms[1])

    def stage_out(u, g, d):
        v2 = jnp.concatenate([vbuf[g][rows_of(u), :], vbuf[g][rows_of(u + d), :]], axis=0)
        pvs = [jnp.dot(p_buf[u, h], jnp.where(v_lanes[h], v2, jnp.ones_like(v2)), preferred_element_type=F32)
               for h in range(A_HEADS_PER_STEP)]
        out = jnp.where(first, pvs[0], pvs[1])
        den = pltpu.roll(jnp.where(first, pvs[1], pvs[0]), A_HEAD_DIM, 1)
        ro[g, rows_of(u), :] = out / den
        rl[g, rows_of(u), :] = m_buf[u] + jnp.log(den) * LOG2_E

    def quad(stage, qd, *args):
        for i in range(A_QUAD):
            stage(qd * A_QUAD + i, *args)

    n_quads = A_UNITS // A_QUAD
    for g, d in enumerate(DILATIONS):
        quad(stage_scores, 0, g, d)
        quad(stage_softmax, 0)
        quad(stage_scores, 1, g, d)

        def trip(qd, carry, g=g, d=d):
            quad(stage_out, qd, g, d)
            quad(stage_softmax, qd + 1)
            quad(stage_scores, qd + 2, g, d)
            return carry

        lax.fori_loop(0, n_quads - 2, trip, 0)
        quad(stage_out, n_quads - 2, g, d)
        quad(stage_softmax, n_quads - 1)
        quad(stage_out, n_quads - 1, g, d)

    d2, d3 = DILATIONS[1], DILATIONS[2]
    span2 = ATT_BLOCK * d2
    per_ss2 = ATT_BLOCK * d2 // d3
    for r3 in range(d3):
        blk = slice(r3 * ATT_BLOCK, (r3 + 1) * ATT_BLOCK)
        tok = pl.ds(r3, ATT_BLOCK, stride=d3)
        r2, off = r3 % d2, r3 // d2

        def from_g2(buf):
            return jnp.concatenate(
                [buf[1, pl.ds(ss2 * span2 + r2 * ATT_BLOCK + off, per_ss2, stride=d3 // d2), :]
                 for ss2 in range(A_SPAN // span2)], axis=0)

        o1, l1 = ro[0, tok, :], rl[0, tok, :]
        o2, l2 = from_g2(ro), from_g2(rl)
        o3, l3 = ro[2, blk, :], rl[2, blk, :]
        mx = jnp.maximum(jnp.maximum(l1, l2), l3)
        e1, e2, e3 = jnp.exp2(l1 - mx), jnp.exp2(l2 - mx), jnp.exp2(l3 - mx)
        o_ref[tok, :] = (e1 * o1 + e2 * o2 + e3 * o3) / (e1 + e2 + e3)


def _attention_a(proj, batch, seq):
    p = proj.reshape(batch, seq, P_WIDTH)

    def cur_spec(g, part):
        col = (P_QKV[g] + part * A_GROUP_W) // LANES
        return pl.BlockSpec((None, A_SPAN, LANES), lambda b, j, hp: (b, j, col + hp))

    def prev_spec(g, part):
        col = (P_QKV[g] + part * A_GROUP_W) // LANES
        span = ATT_BLOCK * DILATIONS[g]
        per = A_SPAN // span
        return pl.BlockSpec((None, span, LANES),
                            lambda b, j, hp: (b, jnp.maximum(j * per - 1, 0), col + hp))

    groups = range(N_GROUPS)
    kv_bufs = [pltpu.VMEM((ATT_BLOCK * d + A_SPAN, LANES), BF16) for d in DILATIONS]
    in_specs = ([cur_spec(g, 0) for g in groups] + [cur_spec(g, 1) for g in groups]
                + [prev_spec(g, 1) for g in groups] + [cur_spec(g, 2) for g in groups]
                + [prev_spec(g, 2) for g in groups])
    return pl.pallas_call(
        _attn_body,
        grid=(batch, seq // A_SPAN, A_GROUP_W // LANES),
        in_specs=in_specs,
        out_specs=pl.BlockSpec((None, A_SPAN, LANES), lambda b, j, hp: (b, j, hp)),
        out_shape=jax.ShapeDtypeStruct((batch, seq, A_OUT), F32),
        scratch_shapes=[pltpu.VMEM((N_GROUPS, A_SPAN, LANES), F32),
                        pltpu.VMEM((N_GROUPS, A_SPAN, LANES), F32),
                        pltpu.VMEM((2, ATT_BLOCK, 2 * ATT_BLOCK), BF16),
                        pltpu.VMEM((A_UNITS, A_HEADS_PER_STEP, ATT_BLOCK, 2 * ATT_BLOCK), BF16),
                        pltpu.VMEM((A_UNITS, A_HEADS_PER_STEP, ATT_BLOCK, 2 * ATT_BLOCK), BF16),
                        pltpu.VMEM((A_UNITS, ATT_BLOCK, LANES), F32)] + kv_bufs + kv_bufs,
        compiler_params=pltpu.CompilerParams(
            dimension_semantics=("parallel", "parallel", "parallel"), vmem_limit_bytes=VMEM_LIMIT),
    )(*([p] * 15))


RET_TB = 2048
_LOG_GAMMA = [float(v) for v in np.log1p(-(2.0 ** (-5.0 - np.arange(R_HEADS)))).astype(np.float32)]


def _ret_body(q_ref, k_ref, v_ref, g_ref, o_ref, st_ref):
    h = pl.program_id(1)
    t = pl.program_id(2)

    @pl.when(t == 0)
    def _reset():
        st_ref[...] = jnp.zeros_like(st_ref)

    lg = jnp.float32(_LOG_GAMMA[R_HEADS - 1])
    for hh in range(R_HEADS - 2, -1, -1):
        lg = jnp.where(h == hh, jnp.float32(_LOG_GAMMA[hh]), lg)
    row = lax.broadcasted_iota(jnp.int32, (R_CHUNK, R_CHUNK), 0).astype(F32)
    colv = lax.broadcasted_iota(jnp.int32, (R_CHUNK, R_CHUNK), 1).astype(F32)
    diff = row - colv
    decay = jnp.where(diff >= 0, jnp.exp(diff * lg), 0.0)
    zeta = jnp.exp((R_CHUNK - 1 - row) * lg)
    xi = jnp.exp((row + 1.0) * lg)
    chunk_decay = jnp.exp(jnp.full((R_KEY_DIM, R_VAL_DIM), float(R_CHUNK), F32) * lg)
    kscale = R_KEY_DIM ** -0.5

    for c in range(RET_TB // R_CHUNK):
        sl = slice(c * R_CHUNK, (c + 1) * R_CHUNK)
        q = q_ref[sl, :].astype(F32)
        k = k_ref[sl, :].astype(F32) * kscale
        v = v_ref[sl, :]
        s = lax.dot_general(q.astype(BF16), k.astype(BF16), (((1,), (1,)), ((), ())),
                            preferred_element_type=F32) * decay
        inner = jnp.dot(s.astype(BF16), v, preferred_element_type=F32)
        state = st_ref[...]
        cross = jnp.dot((q * xi).astype(BF16), state.astype(BF16), preferred_element_type=F32)
        kz_t = (k * zeta).T.astype(BF16)
        st_ref[...] = state * chunk_decay + jnp.dot(kz_t, v, preferred_element_type=F32)
        out = inner + cross
        ms = jnp.mean(out * out, axis=-1, keepdims=True)
        gate = g_ref[sl, :].astype(F32)
        o_ref[sl, :] = (gate * jax.nn.sigmoid(gate) * (out * lax.rsqrt(ms + NORM_EPS))).astype(BF16)


def _retention(proj, batch, seq):
    p = proj.reshape(batch, seq, P_WIDTH)
    return pl.pallas_call(
        _ret_body,
        grid=(batch, R_HEADS, seq // RET_TB),
        in_specs=[
            pl.BlockSpec((None, RET_TB, R_KEY_DIM), lambda b, h, t: (b, t, P_QR // R_KEY_DIM + h)),
            pl.BlockSpec((None, RET_TB, R_KEY_DIM), lambda b, h, t: (b, t, P_KR // R_KEY_DIM + h)),
            pl.BlockSpec((None, RET_TB, R_VAL_DIM), lambda b, h, t: (b, t, P_VR // R_VAL_DIM + h)),
            pl.BlockSpec((None, RET_TB, R_VAL_DIM), lambda b, h, t: (b, t, P_GR // R_VAL_DIM + h)),
        ],
        out_specs=pl.BlockSpec((None, RET_TB, R_VAL_DIM), lambda b, h, t: (b, t, h)),
        out_shape=jax.ShapeDtypeStruct((batch, seq, R_V), BF16),
        scratch_shapes=[pltpu.VMEM((R_KEY_DIM, R_VAL_DIM), F32)],
        compiler_params=pltpu.CompilerParams(
            dimension_semantics=("parallel", "parallel", "arbitrary"), vmem_limit_bytes=VMEM_LIMIT),
    )(p, p, p, p)


MERGE_TM = 1024


def _merge_body(x_ref, ya_ref, rb_ref, ga_ref, gb_ref, wa_ref, wb_ref, wo_ref, o_ref):
    ya = jnp.dot(ya_ref[...].astype(BF16), wa_ref[...], preferred_element_type=F32)
    yb = jnp.dot(rb_ref[...], wb_ref[...], preferred_element_type=F32)
    ga = ga_ref[...].astype(F32)
    gb = gb_ref[...].astype(F32)
    merged = jax.nn.sigmoid(ga) * ya + jax.nn.sigmoid(gb) * yb
    o_ref[...] = x_ref[...] + jnp.dot(merged.astype(BF16), wo_ref[...], preferred_element_type=F32)


def _merge(x2d, ya2d, rb2d, proj, wa, wb, wo):
    m = x2d.shape[0]

    def full(shape):
        return pl.BlockSpec(shape, lambda i: (0, 0))

    return pl.pallas_call(
        _merge_body,
        grid=(m // MERGE_TM,),
        in_specs=[
            pl.BlockSpec((MERGE_TM, D_MODEL), lambda i: (i, 0)),
            pl.BlockSpec((MERGE_TM, A_OUT), lambda i: (i, 0)),
            pl.BlockSpec((MERGE_TM, R_V), lambda i: (i, 0)),
            pl.BlockSpec((MERGE_TM, D_MODEL), lambda i: (i, P_GA // D_MODEL)),
            pl.BlockSpec((MERGE_TM, D_MODEL), lambda i: (i, P_GB // D_MODEL)),
            full((A_OUT, D_MODEL)), full((R_V, D_MODEL)), full((D_MODEL, D_MODEL)),
        ],
        out_specs=pl.BlockSpec((MERGE_TM, D_MODEL), lambda i: (i, 0)),
        out_shape=jax.ShapeDtypeStruct((m, D_MODEL), F32),
        compiler_params=pltpu.CompilerParams(
            dimension_semantics=("parallel",), vmem_limit_bytes=VMEM_LIMIT),
    )(x2d, ya2d, rb2d, proj, proj, wa, wb, wo)


FFN_TM = 1024
FFN_TF = 2048


def _ffn_body(x_ref, g_ref, wu_ref, wd_ref, fg_ref, o_ref, h_ref, *, final):
    c = pl.program_id(1)

    @pl.when(c == 0)
    def _start():
        x = x_ref[...]
        h_ref[...] = _rms(x, g_ref[...]).astype(BF16)
        o_ref[...] = x

    up = jnp.dot(h_ref[...], wu_ref[...], preferred_element_type=F32)
    up = jnp.square(jnp.maximum(up, 0.0)).astype(BF16)
    o_ref[...] += jnp.dot(up, wd_ref[...], preferred_element_type=F32)

    if final:
        @pl.when(c == pl.num_programs(1) - 1)
        def _final_norm():
            o_ref[...] = _rms(o_ref[...], fg_ref[...])


def _ffn(x2d, gain, wu, wd, final_gain, final):
    m = x2d.shape[0]
    return pl.pallas_call(
        functools.partial(_ffn_body, final=final),
        grid=(m // FFN_TM, D_FF // FFN_TF),
        in_specs=[
            pl.BlockSpec((FFN_TM, D_MODEL), lambda i, c: (i, 0)),
            pl.BlockSpec((1, D_MODEL), lambda i, c: (0, 0)),
            pl.BlockSpec((D_MODEL, FFN_TF), lambda i, c: (0, c)),
            pl.BlockSpec((FFN_TF, D_MODEL), lambda i, c: (c, 0)),
            pl.BlockSpec((1, D_MODEL), lambda i, c: (0, 0)),
        ],
        out_specs=pl.BlockSpec((FFN_TM, D_MODEL), lambda i, c: (i, 0)),
        out_shape=jax.ShapeDtypeStruct((m, D_MODEL), F32),
        scratch_shapes=[pltpu.VMEM((FFN_TM, D_MODEL), BF16)],
        compiler_params=pltpu.CompilerParams(
            dimension_semantics=("parallel", "arbitrary"), vmem_limit_bytes=VMEM_LIMIT),
    )(x2d, gain, wu, wd, final_gain)


def kernel(x, mix_norm, w_in, w_a, w_b, w_o, ffn_norm, w_up, w_down, final_norm):
    batch, seq, _ = x.shape
    depth = w_in.shape[0]
    assert seq % A_SPAN == 0 and seq % RET_TB == 0
    tables = _rotary_tables(seq)
    final_gain = final_norm.reshape(1, D_MODEL)
    x2d = x.reshape(batch * seq, D_MODEL)
    for layer in range(depth):
        proj = _in_proj(x2d, mix_norm[layer].reshape(1, D_MODEL), _arrange_w_in(w_in[layer]), tables, seq)
        ya = _attention_a(proj, batch, seq).reshape(batch * seq, A_OUT)
        rb = _retention(proj, batch, seq).reshape(batch * seq, R_V)
        x2d = _merge(x2d, ya, rb, proj, w_a[layer].astype(BF16), w_b[layer].astype(BF16),
                     w_o[layer].astype(BF16))
        x2d = _ffn(x2d, ffn_norm[layer].reshape(1, D_MODEL), w_up[layer].astype(BF16),
                   w_down[layer].astype(BF16), final_gain, final=(layer == depth - 1))
    return x2d.reshape(batch, seq, D_MODEL)
```

```python
import functools

import numpy as np
import jax
import jax.numpy as jnp
from jax import lax
from jax.experimental import pallas as pl
from jax.experimental.pallas import tpu as pltpu

F32 = jnp.float32
BF16 = jnp.bfloat16

D_MODEL = 1024
N_GROUPS = 3
DILATIONS = (1, 4, 16)
A_HEADS = 8
A_HEAD_DIM = 64
A_GROUP_W = A_HEADS * A_HEAD_DIM
A_WIDTH = N_GROUPS * A_GROUP_W
A_OUT = A_GROUP_W
ROT_DIM = A_HEAD_DIM // 4
ROPE_THETA = 500000.0
ATT_BLOCK = 128
NEG_INF = -1e30
R_HEADS = 4
R_KEY_DIM = 128
R_VAL_DIM = 256
R_QK = R_HEADS * R_KEY_DIM
R_V = R_HEADS * R_VAL_DIM
R_CHUNK = 128
R_ROT_BASE = 10000.0
D_FF = 4 * D_MODEL
NORM_EPS = 1e-6

OFF_QA = 0
OFF_KA = A_WIDTH
OFF_VA = 2 * A_WIDTH
OFF_QR = 3 * A_WIDTH
OFF_KR = OFF_QR + R_QK
OFF_VR = OFF_KR + R_QK
OFF_GR = OFF_VR + R_V
OFF_GA = OFF_GR + R_V
OFF_GB = OFF_GA + D_MODEL

LANES = 128
VMEM_LIMIT = 56 * 1024 * 1024

A_SPAN = ATT_BLOCK * max(DILATIONS)
A_UNITS = A_SPAN // ATT_BLOCK

IN_TM = A_SPAN
IN_TN = 512
IN_CHUNKS = (512, 512, 512, 256, 256)
X_CHUNKS = D_MODEL // LANES

P_GA = 0
P_GB = P_GA + D_MODEL
P_VR = P_GB + D_MODEL
P_GR = P_VR + R_V
P_QR = P_GR + R_V
P_KR = P_QR + R_QK
P_QKV = tuple(P_KR + R_QK + g * 3 * A_GROUP_W for g in range(N_GROUPS))
P_WIDTH = P_QKV[-1] + 3 * A_GROUP_W

_TAB_R = 2 * N_GROUPS
_TAB_NONE = 2 * N_GROUPS + 1
TILE_TABLE = ((_TAB_NONE,) * (P_QR // IN_TN) + (_TAB_R,) * (2 * R_QK // IN_TN)
              + tuple(t for g in range(N_GROUPS) for t in (g, N_GROUPS + g, _TAB_NONE)))
LOG2_E = 1.4426950408889634
A_Q_SCALE = A_HEAD_DIM ** -0.5 * LOG2_E
TILE_ORDER = (0,) * (P_QKV[0] // IN_TN) + tuple(g for g in range(N_GROUPS) for _ in range(3))


def _rms(x, gain):
    ms = jnp.mean(x * x, axis=-1, keepdims=True)
    return x * lax.rsqrt(ms + NORM_EPS) * gain


def _step_lookup(j, values):
    out = values[-1]
    for t in range(len(values) - 2, -1, -1):
        if values[t] != values[t + 1]:
            out = jnp.where(j <= t, values[t], out)
    return out


def _in_proj_body(*refs):
    x_refs = refs[:X_CHUNKS]
    g_ref, w_ref, cos_ref, sin_ref, o_ref, h_ref, inv_ref = refs[X_CHUNKS:]
    j = pl.program_id(1)

    @pl.when(j == 0)
    def _norm():
        gains = [g_ref[:, c * LANES:(c + 1) * LANES] for c in range(X_CHUNKS)]
        for u in range(A_UNITS):
            rows = pl.ds(u * ATT_BLOCK, ATT_BLOCK)
            xc = [x_refs[c][rows, :] for c in range(X_CHUNKS)]
            sq = xc[0] * xc[0]
            for c in range(1, X_CHUNKS):
                sq = sq + xc[c] * xc[c]
            inv = lax.rsqrt(jnp.sum(sq, axis=-1, keepdims=True) * (1.0 / D_MODEL) + NORM_EPS)
            inv = jnp.broadcast_to(inv, (ATT_BLOCK, LANES))
            inv_ref[rows, :] = inv
            for c in range(X_CHUNKS):
                h_ref[0, rows, c * LANES:(c + 1) * LANES] = (xc[c] * inv * gains[c]).astype(BF16)
        for g in range(1, N_GROUPS):
            dil = DILATIONS[g]
            span = ATT_BLOCK * dil
            for ss in range(IN_TM // span):
                for r in range(dil):
                    u = ss * dil + r
                    rows = pl.ds(ss * span + r, ATT_BLOCK, stride=dil)
                    dst = pl.ds(u * ATT_BLOCK, ATT_BLOCK)
                    inv = inv_ref[rows, :]
                    for c in range(X_CHUNKS):
                        h_ref[g, dst, c * LANES:(c + 1) * LANES] = (
                            x_refs[c][rows, :] * inv * gains[c]).astype(BF16)

    order = _step_lookup(j, TILE_ORDER)
    starts = np.cumsum((0,) + IN_CHUNKS)
    for start, size in zip(starts, IN_CHUNKS):
        rows = pl.ds(int(start), size)
        acc = jnp.dot(h_ref[order, rows, :], w_ref[...], preferred_element_type=F32)
        for c in range(IN_TN // LANES):
            t = acc[:, c * LANES:(c + 1) * LANES]
            o_ref[rows, c * LANES:(c + 1) * LANES] = (
                t * cos_ref[rows, :] + pltpu.roll(t, LANES // 2, 1) * sin_ref[rows, :]).astype(BF16)


def _pair_layout(t):
    half = ROT_DIM // 2
    a, b = t[..., 0, :], t[..., 1, :]
    return jnp.concatenate([a[..., :half], b[..., :half], a[..., ROT_DIM:],
                            a[..., half:ROT_DIM], b[..., half:ROT_DIM], b[..., ROT_DIM:]], axis=-1)


def _arrange_w_in(w):
    rows = w.shape[0]
    w = w.astype(BF16)

    def pairs(cols):
        return _pair_layout(cols.reshape(rows, N_GROUPS, A_GROUP_W // LANES, 2, A_HEAD_DIM)).reshape(
            rows, N_GROUPS, A_GROUP_W)

    qa = pairs(w[:, OFF_QA:OFF_KA])
    ka = pairs(w[:, OFF_KA:OFF_VA])
    va = w[:, OFF_VA:OFF_QR].reshape(rows, N_GROUPS, A_GROUP_W)
    parts = [w[:, OFF_GA:OFF_GB], w[:, OFF_GB:], w[:, OFF_VR:OFF_GR], w[:, OFF_GR:OFF_GA],
             w[:, OFF_QR:OFF_KR], w[:, OFF_KR:OFF_VR]]
    for g in range(N_GROUPS):
        parts += [qa[:, g], ka[:, g], va[:, g]]
    return jnp.concatenate(parts, axis=1)


@functools.lru_cache(maxsize=None)
def _rotary_tables(seq):
    f32 = np.float32
    half = ROT_DIM // 2
    inv = (1.0 / (f32(ROPE_THETA) ** (np.arange(half, dtype=f32) / f32(half)))).astype(f32)
    rest = A_HEAD_DIM - ROT_DIM
    ones, zeros = np.ones((seq, rest), f32), np.zeros((seq, rest), f32)
    cos_tabs, sin_tabs = [], []
    for dil in DILATIONS:
        span = ATT_BLOCK * dil
        pos = np.arange(seq, dtype=f32).reshape(seq // span, ATT_BLOCK, dil).transpose(0, 2, 1).reshape(seq, 1)
        ang = pos * inv[None, :]
        cos, sin = np.cos(ang), np.sin(ang)
        cos_tabs.append(np.concatenate([cos, cos, ones, cos, cos, ones], axis=1))
        sin_tabs.append(np.concatenate([-sin, -sin, zeros, sin, sin, zeros], axis=1))
    cos_tabs = [t * f32(A_Q_SCALE) for t in cos_tabs] + cos_tabs
    sin_tabs = [t * f32(A_Q_SCALE) for t in sin_tabs] + sin_tabs
    half_r = R_KEY_DIM // 2
    inv_r = (1.0 / (f32(R_ROT_BASE) ** (np.arange(half_r, dtype=f32) / f32(half_r)))).astype(f32)
    ang_r = np.arange(seq, dtype=f32)[:, None] * inv_r[None, :]
    cos_tabs += [np.concatenate([np.cos(ang_r), np.cos(ang_r)], axis=1), np.ones((seq, LANES), f32)]
    sin_tabs += [np.concatenate([-np.sin(ang_r), np.sin(ang_r)], axis=1), np.zeros((seq, LANES), f32)]
    return np.stack(cos_tabs).astype(f32), np.stack(sin_tabs).astype(f32)


def _in_proj(x2d, gain, w_bf16, tables, seq):
    m = x2d.shape[0]
    tiles_per_seq = seq // IN_TM
    tab_spec = pl.BlockSpec((None, IN_TM, LANES),
                            lambda i, j: (_step_lookup(j, TILE_TABLE), i % tiles_per_seq, 0))
    x_specs = [pl.BlockSpec((IN_TM, LANES), lambda i, j, c=c: (i, c)) for c in range(X_CHUNKS)]
    return pl.pallas_call(
        _in_proj_body,
        grid=(m // IN_TM, P_WIDTH // IN_TN),
        in_specs=x_specs + [
            pl.BlockSpec((1, D_MODEL), lambda i, j: (0, 0)),
            pl.BlockSpec((D_MODEL, IN_TN), lambda i, j: (0, j)),
            tab_spec, tab_spec,
        ],
        out_specs=pl.BlockSpec((IN_TM, IN_TN), lambda i, j: (i, j)),
        out_shape=jax.ShapeDtypeStruct((m, P_WIDTH), BF16),
        scratch_shapes=[pltpu.VMEM((N_GROUPS, IN_TM, D_MODEL), BF16),
                        pltpu.VMEM((IN_TM, LANES), F32)],
        compiler_params=pltpu.CompilerParams(
            dimension_semantics=("parallel", "arbitrary"), vmem_limit_bytes=VMEM_LIMIT),
    )(*([x2d] * X_CHUNKS), gain, w_bf16, *tables)


A_HEADS_PER_STEP = LANES // A_HEAD_DIM
A_QUAD = 4


def _attn_body(*refs):
    q_in = refs[0:3]
    kc_in = refs[3:6]
    kp_in = refs[6:9]
    vc_in = refs[9:12]
    vp_in = refs[12:15]
    o_ref = refs[15]
    ro, rl, bias, s_buf, p_buf, m_buf = refs[16:22]
    kbuf, vbuf = refs[22:25], refs[25:28]
    jt = pl.program_id(1)

    qi = lax.broadcasted_iota(jnp.int32, (ATT_BLOCK, 2 * ATT_BLOCK), 0)
    kj = lax.broadcasted_iota(jnp.int32, (ATT_BLOCK, 2 * ATT_BLOCK), 1)
    dist = qi + ATT_BLOCK - kj
    band = (dist >= 0) & (dist <= ATT_BLOCK)
    bias[0] = jnp.where(band & (kj >= ATT_BLOCK), 0.0, NEG_INF).astype(BF16)
    bias[1] = jnp.where(band, 0.0, NEG_INF).astype(BF16)
    lane = lax.broadcasted_iota(jnp.int32, (ATT_BLOCK, LANES), 1)

    half = ROT_DIM // 2
    first_head = (lane < half) | ((lane >= ROT_DIM) & (lane < A_HEAD_DIM + half))
    head_lanes = (first_head, jnp.logical_not(first_head))
    first = lane < A_HEAD_DIM
    v_lane = lax.broadcasted_iota(jnp.int32, (2 * ATT_BLOCK, LANES), 1)
    v_lanes = (v_lane < A_HEAD_DIM, v_lane >= A_HEAD_DIM)

    for g, d in enumerate(DILATIONS):
        span = ATT_BLOCK * d
        kbuf[g][0:span] = kp_in[g][...]
        kbuf[g][span:] = kc_in[g][...]
        vbuf[g][0:span] = vp_in[g][...]
        vbuf[g][span:] = vc_in[g][...]

    def rows_of(u):
        return pl.ds(u * ATT_BLOCK if isinstance(u, int) else pl.multiple_of(u * ATT_BLOCK, ATT_BLOCK), ATT_BLOCK)

    def stage_scores(u, g, d):
        b = bias[jnp.where(jnp.logical_or(jt > 0, u >= d), 1, 0)]
        q = q_in[g][rows_of(u), :]
        k2 = jnp.concatenate([kbuf[g][rows_of(u), :], kbuf[g][rows_of(u + d), :]], axis=0)
        for h in range(A_HEADS_PER_STEP):
            qm = jnp.where(head_lanes[h], q, jnp.zeros_like(q))
            s = lax.dot_general(qm, k2, (((1,), (1,)), ((), ())), preferred_element_type=F32)
            s_buf[u, h] = s.astype(BF16) + b

    def stage_softmax(u):
        ms = []
        for h in range(A_HEADS_PER_STEP):
            s = s_buf[u, h]
            m = jnp.max(s, axis=1, keepdims=True)
            p_buf[u, h] = jnp.exp2(s - m)
            ms.append(m.astype(F32))
        m_buf[u] = jnp.where(first, ms[0], ms[1])

    def stage_out(u, g, d):
        v2 = jnp.concatenate([vbuf[g][rows_of(u), :], vbuf[g][rows_of(u + d), :]], axis=0)
        pvs = [jnp.dot(p_buf[u, h], jnp.where(v_lanes[h], v2, jnp.ones_like(v2)), preferred_element_type=F32)
               for h in range(A_HEADS_PER_STEP)]
        out = jnp.where(first, pvs[0], pvs[1])
        den = pltpu.roll(jnp.where(first, pvs[1], pvs[0]), A_HEAD_DIM, 1)
        ro[g, rows_of(u), :] = out / den
        rl[g, rows_of(u), :] = m_buf[u] + jnp.log(den) * LOG2_E

    def quad(stage, qd, *args):
        for i in range(A_QUAD):
            stage(qd * A_QUAD + i, *args)

    n_quads = A_UNITS // A_QUAD
    for g, d in enumerate(DILATIONS):
        quad(stage_scores, 0, g, d)
        quad(stage_softmax, 0)
        quad(stage_scores, 1, g, d)

        def trip(qd, carry, g=g, d=d):
            quad(stage_out, qd, g, d)
            quad(stage_softmax, qd + 1)
            quad(stage_scores, qd + 2, g, d)
            return carry

        lax.fori_loop(0, n_quads - 2, trip, 0)
        quad(stage_out, n_quads - 2, g, d)
        quad(stage_softmax, n_quads - 1)
        quad(stage_out, n_quads - 1, g, d)

    d2, d3 = DILATIONS[1], DILATIONS[2]
    span2 = ATT_BLOCK * d2
    per_ss2 = ATT_BLOCK * d2 // d3
    for r3 in range(d3):
        blk = slice(r3 * ATT_BLOCK, (r3 + 1) * ATT_BLOCK)
        tok = pl.ds(r3, ATT_BLOCK, stride=d3)
        r2, off = r3 % d2, r3 // d2

        def from_g2(buf):
            return jnp.concatenate(
                [buf[1, pl.ds(ss2 * span2 + r2 * ATT_BLOCK + off, per_ss2, stride=d3 // d2), :]
                 for ss2 in range(A_SPAN // span2)], axis=0)

        o1, l1 = ro[0, tok, :], rl[0, tok, :]
        o2, l2 = from_g2(ro), from_g2(rl)
        o3, l3 = ro[2, blk, :], rl[2, blk, :]
        mx = jnp.maximum(jnp.maximum(l1, l2), l3)
        e1, e2, e3 = jnp.exp2(l1 - mx), jnp.exp2(l2 - mx), jnp.exp2(l3 - mx)
        o_ref[tok, :] = (e1 * o1 + e2 * o2 + e3 * o3) / (e1 + e2 + e3)


def _attention_a(proj, batch, seq):
    p = proj.reshape(batch, seq, P_WIDTH)

    def cur_spec(g, part):
        col = (P_QKV[g] + part * A_GROUP_W) // LANES
        return pl.BlockSpec((None, A_SPAN, LANES), lambda b, j, hp: (b, j, col + hp))

    def prev_spec(g, part):
        col = (P_QKV[g] + part * A_GROUP_W) // LANES
        span = ATT_BLOCK * DILATIONS[g]
        per = A_SPAN // span
        return pl.BlockSpec((None, span, LANES),
                            lambda b, j, hp: (b, jnp.maximum(j * per - 1, 0), col + hp))

    groups = range(N_GROUPS)
    kv_bufs = [pltpu.VMEM((ATT_BLOCK * d + A_SPAN, LANES), BF16) for d in DILATIONS]
    in_specs = ([cur_spec(g, 0) for g in groups] + [cur_spec(g, 1) for g in groups]
                + [prev_spec(g, 1) for g in groups] + [cur_spec(g, 2) for g in groups]
                + [prev_spec(g, 2) for g in groups])
    return pl.pallas_call(
        _attn_body,
        grid=(batch, seq // A_SPAN, A_GROUP_W // LANES),
        in_specs=in_specs,
        out_specs=pl.BlockSpec((None, A_SPAN, LANES), lambda b, j, hp: (b, j, hp)),
        out_shape=jax.ShapeDtypeStruct((batch, seq, A_OUT), F32),
        scratch_shapes=[pltpu.VMEM((N_GROUPS, A_SPAN, LANES), F32),
                        pltpu.VMEM((N_GROUPS, A_SPAN, LANES), F32),
                        pltpu.VMEM((2, ATT_BLOCK, 2 * ATT_BLOCK), BF16),
                        pltpu.VMEM((A_UNITS, A_HEADS_PER_STEP, ATT_BLOCK, 2 * ATT_BLOCK), BF16),
                        pltpu.VMEM((A_UNITS, A_HEADS_PER_STEP, ATT_BLOCK, 2 * ATT_BLOCK), BF16),
                        pltpu.VMEM((A_UNITS, ATT_BLOCK, LANES), F32)] + kv_bufs + kv_bufs,
        compiler_params=pltpu.CompilerParams(
            dimension_semantics=("parallel", "parallel", "parallel"), vmem_limit_bytes=VMEM_LIMIT),
    )(*([p] * 15))


RET_TB = 2048
_LOG_GAMMA = [float(v) for v in np.log1p(-(2.0 ** (-5.0 - np.arange(R_HEADS)))).astype(np.float32)]


def _ret_body(q_ref, k_ref, v_ref, g_ref, o_ref, st_ref):
    h = pl.program_id(1)
    t = pl.program_id(2)

    @pl.when(t == 0)
    def _reset():
        st_ref[...] = jnp.zeros_like(st_ref)

    lg = jnp.float32(_LOG_GAMMA[R_HEADS - 1])
    for hh in range(R_HEADS - 2, -1, -1):
        lg = jnp.where(h == hh, jnp.float32(_LOG_GAMMA[hh]), lg)
    row = lax.broadcasted_iota(jnp.int32, (R_CHUNK, R_CHUNK), 0).astype(F32)
    colv = lax.broadcasted_iota(jnp.int32, (R_CHUNK, R_CHUNK), 1).astype(F32)
    diff = row - colv
    decay = jnp.where(diff >= 0, jnp.exp(diff * lg), 0.0)
    zeta = jnp.exp((R_CHUNK - 1 - row) * lg)
    xi = jnp.exp((row + 1.0) * lg)
    chunk_decay = jnp.exp(jnp.full((R_KEY_DIM, R_VAL_DIM), float(R_CHUNK), F32) * lg)
    kscale = R_KEY_DIM ** -0.5

    for c in range(RET_TB // R_CHUNK):
        sl = slice(c * R_CHUNK, (c + 1) * R_CHUNK)
        q = q_ref[sl, :].astype(F32)
        k = k_ref[sl, :].astype(F32) * kscale
        v = v_ref[sl, :]
        s = lax.dot_general(q.astype(BF16), k.astype(BF16), (((1,), (1,)), ((), ())),
                            preferred_element_type=F32) * decay
        inner = jnp.dot(s.astype(BF16), v, preferred_element_type=F32)
        state = st_ref[...]
        cross = jnp.dot((q * xi).astype(BF16), state.astype(BF16), preferred_element_type=F32)
        kz_t = (k * zeta).T.astype(BF16)
        st_ref[...] = state * chunk_decay + jnp.dot(kz_t, v, preferred_element_type=F32)
        out = inner + cross
        ms = jnp.mean(out * out, axis=-1, keepdims=True)
        gate = g_ref[sl, :].astype(F32)
        o_ref[sl, :] = (gate * jax.nn.sigmoid(gate) * (out * lax.rsqrt(ms + NORM_EPS))).astype(BF16)


def _retention(proj, batch, seq):
    p = proj.reshape(batch, seq, P_WIDTH)
    return pl.pallas_call(
        _ret_body,
        grid=(batch, R_HEADS, seq // RET_TB),
        in_specs=[
            pl.BlockSpec((None, RET_TB, R_KEY_DIM), lambda b, h, t: (b, t, P_QR // R_KEY_DIM + h)),
            pl.BlockSpec((None, RET_TB, R_KEY_DIM), lambda b, h, t: (b, t, P_KR // R_KEY_DIM + h)),
            pl.BlockSpec((None, RET_TB, R_VAL_DIM), lambda b, h, t: (b, t, P_VR // R_VAL_DIM + h)),
            pl.BlockSpec((None, RET_TB, R_VAL_DIM), lambda b, h, t: (b, t, P_GR // R_VAL_DIM + h)),
        ],
        out_specs=pl.BlockSpec((None, RET_TB, R_VAL_DIM), lambda b, h, t: (b, t, h)),
        out_shape=jax.ShapeDtypeStruct((batch, seq, R_V), BF16),
        scratch_shapes=[pltpu.VMEM((R_KEY_DIM, R_VAL_DIM), F32)],
        compiler_params=pltpu.CompilerParams(
            dimension_semantics=("parallel", "parallel", "arbitrary"), vmem_limit_bytes=VMEM_LIMIT),
    )(p, p, p, p)


MERGE_TM = 1024


def _merge_body(x_ref, ya_ref, rb_ref, ga_ref, gb_ref, wa_ref, wb_ref, wo_ref, o_ref):
    ya = jnp.dot(ya_ref[...].astype(BF16), wa_ref[...], preferred_element_type=F32)
    yb = jnp.dot(rb_ref[...], wb_ref[...], preferred_element_type=F32)
    ga = ga_ref[...].astype(F32)
    gb = gb_ref[...].astype(F32)
    merged = jax.nn.sigmoid(ga) * ya + jax.nn.sigmoid(gb) * yb
    o_ref[...] = x_ref[...] + jnp.dot(merged.astype(BF16), wo_ref[...], preferred_element_type=F32)


def _merge(x2d, ya2d, rb2d, proj, wa, wb, wo):
    m = x2d.shape[0]

    def full(shape):
        return pl.BlockSpec(shape, lambda i: (0, 0))

    return pl.pallas_call(
        _merge_body,
        grid=(m // MERGE_TM,),
        in_specs=[
            pl.BlockSpec((MERGE_TM, D_MODEL), lambda i: (i, 0)),
            pl.BlockSpec((MERGE_TM, A_OUT), lambda i: (i, 0)),
            pl.BlockSpec((MERGE_TM, R_V), lambda i: (i, 0)),
            pl.BlockSpec((MERGE_TM, D_MODEL), lambda i: (i, P_GA // D_MODEL)),
            pl.BlockSpec((MERGE_TM, D_MODEL), lambda i: (i, P_GB // D_MODEL)),
            full((A_OUT, D_MODEL)), full((R_V, D_MODEL)), full((D_MODEL, D_MODEL)),
        ],
        out_specs=pl.BlockSpec((MERGE_TM, D_MODEL), lambda i: (i, 0)),
        out_shape=jax.ShapeDtypeStruct((m, D_MODEL), F32),
        compiler_params=pltpu.CompilerParams(
            dimension_semantics=("parallel",), vmem_limit_bytes=VMEM_LIMIT),
    )(x2d, ya2d, rb2d, proj, proj, wa, wb, wo)


FFN_TM = 1024
FFN_TF = 2048


def _ffn_body(x_ref, g_ref, wu_ref, wd_ref, fg_ref, o_ref, h_ref, *, final):
    c = pl.program_id(1)

    @pl.when(c == 0)
    def _start():
        x = x_ref[...]
        h_ref[...] = _rms(x, g_ref[...]).astype(BF16)
        o_ref[...] = x

    up = jnp.dot(h_ref[...], wu_ref[...], preferred_element_type=F32)
    up = jnp.square(jnp.maximum(up, 0.0)).astype(BF16)
    o_ref[...] += jnp.dot(up, wd_ref[...], preferred_element_type=F32)

    if final:
        @pl.when(c == pl.num_programs(1) - 1)
        def _final_norm():
            o_ref[...] = _rms(o_ref[...], fg_ref[...])


def _ffn(x2d, gain, wu, wd, final_gain, final):
    m = x2d.shape[0]
    return pl.pallas_call(
        functools.partial(_ffn_body, final=final),
        grid=(m // FFN_TM, D_FF // FFN_TF),
        in_specs=[
            pl.BlockSpec((FFN_TM, D_MODEL), lambda i, c: (i, 0)),
            pl.BlockSpec((1, D_MODEL), lambda i, c: (0, 0)),
            pl.BlockSpec((D_MODEL, FFN_TF), lambda i, c: (0, c)),
            pl.BlockSpec((FFN_TF, D_MODEL), lambda i, c: (c, 0)),
            pl.BlockSpec((1, D_MODEL), lambda i, c: (0, 0)),
        ],
        out_specs=pl.BlockSpec((FFN_TM, D_MODEL), lambda i, c: (i, 0)),
        out_shape=jax.ShapeDtypeStruct((m, D_MODEL), F32),
        scratch_shapes=[pltpu.VMEM((FFN_TM, D_MODEL), BF16)],
        compiler_params=pltpu.CompilerParams(
            dimension_semantics=("parallel", "arbitrary"), vmem_limit_bytes=VMEM_LIMIT),
    )(x2d, gain, wu, wd, final_gain)


def kernel(x, mix_norm, w_in, w_a, w_b, w_o, ffn_norm, w_up, w_down, final_norm):
    batch, seq, _ = x.shape
    depth = w_in.shape[0]
    assert seq % A_SPAN == 0 and seq % RET_TB == 0
    tables = _rotary_tables(seq)
    final_gain = final_norm.reshape(1, D_MODEL)
    x2d = x.reshape(batch * seq, D_MODEL)
    for layer in range(depth):
        proj = _in_proj(x2d, mix_norm[layer].reshape(1, D_MODEL), _arrange_w_in(w_in[layer]), tables, seq)
        ya = _attention_a(proj, batch, seq).reshape(batch * seq, A_OUT)
        rb = _retention(proj, batch, seq).reshape(batch * seq, R_V)
        x2d = _merge(x2d, ya, rb, proj, w_a[layer].astype(BF16), w_b[layer].astype(BF16),
                     w_o[layer].astype(BF16))
        x2d = _ffn(x2d, ffn_norm[layer].reshape(1, D_MODEL), w_up[layer].astype(BF16),
                   w_down[layer].astype(BF16), final_gain, final=(layer == depth - 1))
    return x2d.reshape(batch, seq, D_MODEL)
```

```python
import functools

import numpy as np
import jax
import jax.numpy as jnp
from jax import lax
from jax.experimental import pallas as pl
from jax.experimental.pallas import tpu as pltpu

F32 = jnp.float32
BF16 = jnp.bfloat16

D_MODEL = 1024
N_GROUPS = 3
DILATIONS = (1, 4, 16)
A_HEADS = 8
A_HEAD_DIM = 64
A_GROUP_W = A_HEADS * A_HEAD_DIM
A_WIDTH = N_GROUPS * A_GROUP_W
A_OUT = A_GROUP_W
ROT_DIM = A_HEAD_DIM // 4
ROPE_THETA = 500000.0
ATT_BLOCK = 128
NEG_INF = -1e30
R_HEADS = 4
R_KEY_DIM = 128
R_VAL_DIM = 256
R_QK = R_HEADS * R_KEY_DIM
R_V = R_HEADS * R_VAL_DIM
R_CHUNK = 128
R_ROT_BASE = 10000.0
D_FF = 4 * D_MODEL
NORM_EPS = 1e-6

OFF_QA = 0
OFF_KA = A_WIDTH
OFF_VA = 2 * A_WIDTH
OFF_QR = 3 * A_WIDTH
OFF_KR = OFF_QR + R_QK
OFF_VR = OFF_KR + R_QK
OFF_GR = OFF_VR + R_V
OFF_GA = OFF_GR + R_V
OFF_GB = OFF_GA + D_MODEL

LANES = 128
VMEM_LIMIT = 56 * 1024 * 1024

A_SPAN = ATT_BLOCK * max(DILATIONS)
A_UNITS = A_SPAN // ATT_BLOCK

IN_TM = A_SPAN
IN_TN = 512
IN_CHUNKS = (512, 512, 512, 256, 256)
X_CHUNKS = D_MODEL // LANES

P_GA = 0
P_GB = P_GA + D_MODEL
P_VR = P_GB + D_MODEL
P_GR = P_VR + R_V
P_QR = P_GR + R_V
P_KR = P_QR + R_QK
P_QKV = tuple(P_KR + R_QK + g * 3 * A_GROUP_W for g in range(N_GROUPS))
P_WIDTH = P_QKV[-1] + 3 * A_GROUP_W

_TAB_R = 2 * N_GROUPS
_TAB_NONE = 2 * N_GROUPS + 1
TILE_TABLE = ((_TAB_NONE,) * (P_QR // IN_TN) + (_TAB_R,) * (2 * R_QK // IN_TN)
              + tuple(t for g in range(N_GROUPS) for t in (g, N_GROUPS + g, _TAB_NONE)))
LOG2_E = 1.4426950408889634
A_Q_SCALE = A_HEAD_DIM ** -0.5 * LOG2_E
TILE_ORDER = (0,) * (P_QKV[0] // IN_TN) + tuple(g for g in range(N_GROUPS) for _ in range(3))


def _rms(x, gain):
    ms = jnp.mean(x * x, axis=-1, keepdims=True)
    return x * lax.rsqrt(ms + NORM_EPS) * gain


def _step_lookup(j, values):
    out = values[-1]
    for t in range(len(values) - 2, -1, -1):
        if values[t] != values[t + 1]:
            out = jnp.where(j <= t, values[t], out)
    return out


def _in_proj_body(*refs):
    x_refs = refs[:X_CHUNKS]
    g_ref, w_ref, cos_ref, sin_ref, o_ref, h_ref, inv_ref = refs[X_CHUNKS:]
    j = pl.program_id(1)

    @pl.when(j == 0)
    def _norm():
        gains = [g_ref[:, c * LANES:(c + 1) * LANES] for c in range(X_CHUNKS)]
        for u in range(A_UNITS):
            rows = pl.ds(u * ATT_BLOCK, ATT_BLOCK)
            xc = [x_refs[c][rows, :] for c in range(X_CHUNKS)]
            sq = xc[0] * xc[0]
            for c in range(1, X_CHUNKS):
                sq = sq + xc[c] * xc[c]
            inv = lax.rsqrt(jnp.sum(sq, axis=-1, keepdims=True) * (1.0 / D_MODEL) + NORM_EPS)
            inv = jnp.broadcast_to(inv, (ATT_BLOCK, LANES))
            inv_ref[rows, :] = inv
            for c in range(X_CHUNKS):
                h_ref[0, rows, c * LANES:(c + 1) * LANES] = (xc[c] * inv * gains[c]).astype(BF16)
        for g in range(1, N_GROUPS):
            dil = DILATIONS[g]
            span = ATT_BLOCK * dil
            for ss in range(IN_TM // span):
                for r in range(dil):
                    u = ss * dil + r
                    rows = pl.ds(ss * span + r, ATT_BLOCK, stride=dil)
                    dst = pl.ds(u * ATT_BLOCK, ATT_BLOCK)
                    inv = inv_ref[rows, :]
                    for c in range(X_CHUNKS):
                        h_ref[g, dst, c * LANES:(c + 1) * LANES] = (
                            x_refs[c][rows, :] * inv * gains[c]).astype(BF16)

    order = _step_lookup(j, TILE_ORDER)
    starts = np.cumsum((0,) + IN_CHUNKS)
    for start, size in zip(starts, IN_CHUNKS):
        rows = pl.ds(int(start), size)
        acc = jnp.dot(h_ref[order, rows, :], w_ref[...], preferred_element_type=F32)
        for c in range(IN_TN // LANES):
            t = acc[:, c * LANES:(c + 1) * LANES]
            o_ref[rows, c * LANES:(c + 1) * LANES] = (
                t * cos_ref[rows, :] + pltpu.roll(t, LANES // 2, 1) * sin_ref[rows, :]).astype(BF16)


def _pair_layout(t):
    half = ROT_DIM // 2
    a, b = t[..., 0, :], t[..., 1, :]
    return jnp.concatenate([a[..., :half], b[..., :half], a[..., ROT_DIM:],
                            a[..., half:ROT_DIM], b[..., half:ROT_DIM], b[..., ROT_DIM:]], axis=-1)


def _arrange_w_in(w):
    lead = w.shape[:-1]
    w = w.astype(BF16)

    def pairs(cols):
        return _pair_layout(cols.reshape(lead + (N_GROUPS, A_GROUP_W // LANES, 2, A_HEAD_DIM))).reshape(
            lead + (N_GROUPS, A_GROUP_W))

    qa = pairs(w[..., OFF_QA:OFF_KA])
    ka = pairs(w[..., OFF_KA:OFF_VA])
    va = w[..., OFF_VA:OFF_QR].reshape(lead + (N_GROUPS, A_GROUP_W))
    qkv = jnp.concatenate([qa, ka, va], axis=-1).reshape(lead + (3 * A_WIDTH,))
    return jnp.concatenate([w[..., OFF_GA:], w[..., OFF_VR:OFF_GA], w[..., OFF_QR:OFF_VR], qkv], axis=-1)


@functools.lru_cache(maxsize=None)
def _rotary_tables(seq):
    f32 = np.float32
    half = ROT_DIM // 2
    inv = (1.0 / (f32(ROPE_THETA) ** (np.arange(half, dtype=f32) / f32(half)))).astype(f32)
    rest = A_HEAD_DIM - ROT_DIM
    ones, zeros = np.ones((seq, rest), f32), np.zeros((seq, rest), f32)
    cos_tabs, sin_tabs = [], []
    for dil in DILATIONS:
        span = ATT_BLOCK * dil
        pos = np.arange(seq, dtype=f32).reshape(seq // span, ATT_BLOCK, dil).transpose(0, 2, 1).reshape(seq, 1)
        ang = pos * inv[None, :]
        cos, sin = np.cos(ang), np.sin(ang)
        cos_tabs.append(np.concatenate([cos, cos, ones, cos, cos, ones], axis=1))
        sin_tabs.append(np.concatenate([-sin, -sin, zeros, sin, sin, zeros], axis=1))
    cos_tabs = [t * f32(A_Q_SCALE) for t in cos_tabs] + cos_tabs
    sin_tabs = [t * f32(A_Q_SCALE) for t in sin_tabs] + sin_tabs
    half_r = R_KEY_DIM // 2
    inv_r = (1.0 / (f32(R_ROT_BASE) ** (np.arange(half_r, dtype=f32) / f32(half_r)))).astype(f32)
    ang_r = np.arange(seq, dtype=f32)[:, None] * inv_r[None, :]
    cos_tabs += [np.concatenate([np.cos(ang_r), np.cos(ang_r)], axis=1), np.ones((seq, LANES), f32)]
    sin_tabs += [np.concatenate([-np.sin(ang_r), np.sin(ang_r)], axis=1), np.zeros((seq, LANES), f32)]
    return np.stack(cos_tabs).astype(f32), np.stack(sin_tabs).astype(f32)


def _in_proj(x2d, gain, w_bf16, layer, tables, seq):
    m = x2d.shape[0]
    tiles_per_seq = seq // IN_TM
    tab_spec = pl.BlockSpec((None, IN_TM, LANES),
                            lambda i, j: (_step_lookup(j, TILE_TABLE), i % tiles_per_seq, 0))
    x_specs = [pl.BlockSpec((IN_TM, LANES), lambda i, j, c=c: (i, c)) for c in range(X_CHUNKS)]
    return pl.pallas_call(
        _in_proj_body,
        grid=(m // IN_TM, P_WIDTH // IN_TN),
        in_specs=x_specs + [
            pl.BlockSpec((1, D_MODEL), lambda i, j: (0, 0)),
            pl.BlockSpec((None, D_MODEL, IN_TN), lambda i, j: (layer, 0, j)),
            tab_spec, tab_spec,
        ],
        out_specs=pl.BlockSpec((IN_TM, IN_TN), lambda i, j: (i, j)),
        out_shape=jax.ShapeDtypeStruct((m, P_WIDTH), BF16),
        scratch_shapes=[pltpu.VMEM((N_GROUPS, IN_TM, D_MODEL), BF16),
                        pltpu.VMEM((IN_TM, LANES), F32)],
        compiler_params=pltpu.CompilerParams(
            dimension_semantics=("parallel", "arbitrary"), vmem_limit_bytes=VMEM_LIMIT),
    )(*([x2d] * X_CHUNKS), gain, w_bf16, *tables)


A_HEADS_PER_STEP = LANES // A_HEAD_DIM
A_QUAD = 4


def _attn_body(*refs):
    q_in = refs[0:3]
    kc_in = refs[3:6]
    kp_in = refs[6:9]
    vc_in = refs[9:12]
    vp_in = refs[12:15]
    o_ref = refs[15]
    ro, rl, bias, s_buf, p_buf, m_buf = refs[16:22]
    kbuf, vbuf = refs[22:25], refs[25:28]
    jt = pl.program_id(1)

    qi = lax.broadcasted_iota(jnp.int32, (ATT_BLOCK, 2 * ATT_BLOCK), 0)
    kj = lax.broadcasted_iota(jnp.int32, (ATT_BLOCK, 2 * ATT_BLOCK), 1)
    dist = qi + ATT_BLOCK - kj
    band = (dist >= 0) & (dist <= ATT_BLOCK)
    bias[0] = jnp.where(band & (kj >= ATT_BLOCK), 0.0, NEG_INF).astype(BF16)
    bias[1] = jnp.where(band, 0.0, NEG_INF).astype(BF16)
    lane = lax.broadcasted_iota(jnp.int32, (ATT_BLOCK, LANES), 1)

    half = ROT_DIM // 2
    first_head = (lane < half) | ((lane >= ROT_DIM) & (lane < A_HEAD_DIM + half))
    head_lanes = (first_head, jnp.logical_not(first_head))
    first = lane < A_HEAD_DIM
    v_lane = lax.broadcasted_iota(jnp.int32, (2 * ATT_BLOCK, LANES), 1)
    v_lanes = (v_lane < A_HEAD_DIM, v_lane >= A_HEAD_DIM)

    for g, d in enumerate(DILATIONS):
        span = ATT_BLOCK * d
        kbuf[g][0:span] = kp_in[g][...]
        kbuf[g][span:] = kc_in[g][...]
        vbuf[g][0:span] = vp_in[g][...]
        vbuf[g][span:] = vc_in[g][...]

    def rows_of(u):
        return pl.ds(u * ATT_BLOCK if isinstance(u, int) else pl.multiple_of(u * ATT_BLOCK, ATT_BLOCK), ATT_BLOCK)

    def stage_scores(u, g, d):
        b = bias[jnp.where(jnp.logical_or(jt > 0, u >= d), 1, 0)]
        q = q_in[g][rows_of(u), :]
        k2 = jnp.concatenate([kbuf[g][rows_of(u), :], kbuf[g][rows_of(u + d), :]], axis=0)
        for h in range(A_HEADS_PER_STEP):
            qm = jnp.where(head_lanes[h], q, jnp.zeros_like(q))
            s = lax.dot_general(qm, k2, (((1,), (1,)), ((), ())), preferred_element_type=F32)
            s_buf[u, h] = s.astype(BF16) + b

    def stage_softmax(u):
        ms = []
        for h in range(A_HEADS_PER_STEP):
            s = s_buf[u, h]
            m = jnp.max(s, axis=1, keepdims=True)
            p_buf[u, h] = jnp.exp2(s - m)
            ms.append(m.astype(F32))
        m_buf[u] = jnp.where(first, ms[0], ms[1])

    def stage_out(u, g, d):
        v2 = jnp.concatenate([vbuf[g][rows_of(u), :], vbuf[g][rows_of(u + d), :]], axis=0)
        pvs = [jnp.dot(p_buf[u, h], jnp.where(v_lanes[h], v2, jnp.ones_like(v2)), preferred_element_type=F32)
               for h in range(A_HEADS_PER_STEP)]
        out = jnp.where(first, pvs[0], pvs[1])
        den = pltpu.roll(jnp.where(first, pvs[1], pvs[0]), A_HEAD_DIM, 1)
        ro[g, rows_of(u), :] = out / den
        rl[g, rows_of(u), :] = m_buf[u] + jnp.log(den) * LOG2_E

    def quad(stage, qd, *args):
        for i in range(A_QUAD):
            stage(qd * A_QUAD + i, *args)

    n_quads = A_UNITS // A_QUAD
    for g, d in enumerate(DILATIONS):
        quad(stage_scores, 0, g, d)
        quad(stage_softmax, 0)
        quad(stage_scores, 1, g, d)

        def trip(qd, carry, g=g, d=d):
            quad(stage_out, qd, g, d)
            quad(stage_softmax, qd + 1)
            quad(stage_scores, qd + 2, g, d)
            return carry

        lax.fori_loop(0, n_quads - 2, trip, 0)
        quad(stage_out, n_quads - 2, g, d)
        quad(stage_softmax, n_quads - 1)
        quad(stage_out, n_quads - 1, g, d)

    d2, d3 = DILATIONS[1], DILATIONS[2]
    span2 = ATT_BLOCK * d2
    per_ss2 = ATT_BLOCK * d2 // d3
    for r3 in range(d3):
        blk = slice(r3 * ATT_BLOCK, (r3 + 1) * ATT_BLOCK)
        tok = pl.ds(r3, ATT_BLOCK, stride=d3)
        r2, off = r3 % d2, r3 // d2

        def from_g2(buf):
            return jnp.concatenate(
                [buf[1, pl.ds(ss2 * span2 + r2 * ATT_BLOCK + off, per_ss2, stride=d3 // d2), :]
                 for ss2 in range(A_SPAN // span2)], axis=0)

        o1, l1 = ro[0, tok, :], rl[0, tok, :]
        o2, l2 = from_g2(ro), from_g2(rl)
        o3, l3 = ro[2, blk, :], rl[2, blk, :]
        mx = jnp.maximum(jnp.maximum(l1, l2), l3)
        e1, e2, e3 = jnp.exp2(l1 - mx), jnp.exp2(l2 - mx), jnp.exp2(l3 - mx)
        o_ref[tok, :] = (e1 * o1 + e2 * o2 + e3 * o3) / (e1 + e2 + e3)


def _attention_a(proj, batch, seq):
    p = proj.reshape(batch, seq, P_WIDTH)

    def cur_spec(g, part):
        col = (P_QKV[g] + part * A_GROUP_W) // LANES
        return pl.BlockSpec((None, A_SPAN, LANES), lambda b, j, hp: (b, j, col + hp))

    def prev_spec(g, part):
        col = (P_QKV[g] + part * A_GROUP_W) // LANES
        span = ATT_BLOCK * DILATIONS[g]
        per = A_SPAN // span
        return pl.BlockSpec((None, span, LANES),
                            lambda b, j, hp: (b, jnp.maximum(j * per - 1, 0), col + hp))

    groups = range(N_GROUPS)
    kv_bufs = [pltpu.VMEM((ATT_BLOCK * d + A_SPAN, LANES), BF16) for d in DILATIONS]
    in_specs = ([cur_spec(g, 0) for g in groups] + [cur_spec(g, 1) for g in groups]
                + [prev_spec(g, 1) for g in groups] + [cur_spec(g, 2) for g in groups]
                + [prev_spec(g, 2) for g in groups])
    return pl.pallas_call(
        _attn_body,
        grid=(batch, seq // A_SPAN, A_GROUP_W // LANES),
        in_specs=in_specs,
        out_specs=pl.BlockSpec((None, A_SPAN, LANES), lambda b, j, hp: (b, j, hp)),
        out_shape=jax.ShapeDtypeStruct((batch, seq, A_OUT), F32),
        scratch_shapes=[pltpu.VMEM((N_GROUPS, A_SPAN, LANES), F32),
                        pltpu.VMEM((N_GROUPS, A_SPAN, LANES), F32),
                        pltpu.VMEM((2, ATT_BLOCK, 2 * ATT_BLOCK), BF16),
                        pltpu.VMEM((A_UNITS, A_HEADS_PER_STEP, ATT_BLOCK, 2 * ATT_BLOCK), BF16),
                        pltpu.VMEM((A_UNITS, A_HEADS_PER_STEP, ATT_BLOCK, 2 * ATT_BLOCK), BF16),
                        pltpu.VMEM((A_UNITS, ATT_BLOCK, LANES), F32)] + kv_bufs + kv_bufs,
        compiler_params=pltpu.CompilerParams(
            dimension_semantics=("parallel", "parallel", "parallel"), vmem_limit_bytes=VMEM_LIMIT),
    )(*([p] * 15))


RET_TB = 2048
_LOG_GAMMA = [float(v) for v in np.log1p(-(2.0 ** (-5.0 - np.arange(R_HEADS)))).astype(np.float32)]


def _ret_body(q_ref, k_ref, v_ref, g_ref, o_ref, st_ref):
    h = pl.program_id(1)
    t = pl.program_id(2)

    @pl.when(t == 0)
    def _reset():
        st_ref[...] = jnp.zeros_like(st_ref)

    lg = jnp.float32(_LOG_GAMMA[R_HEADS - 1])
    for hh in range(R_HEADS - 2, -1, -1):
        lg = jnp.where(h == hh, jnp.float32(_LOG_GAMMA[hh]), lg)
    row = lax.broadcasted_iota(jnp.int32, (R_CHUNK, R_CHUNK), 0).astype(F32)
    colv = lax.broadcasted_iota(jnp.int32, (R_CHUNK, R_CHUNK), 1).astype(F32)
    diff = row - colv
    decay = jnp.where(diff >= 0, jnp.exp(diff * lg), 0.0)
    zeta = jnp.exp((R_CHUNK - 1 - row) * lg)
    xi = jnp.exp((row + 1.0) * lg)
    chunk_decay = jnp.exp(jnp.full((R_KEY_DIM, R_VAL_DIM), float(R_CHUNK), F32) * lg)
    kscale = R_KEY_DIM ** -0.5

    for c in range(RET_TB // R_CHUNK):
        sl = slice(c * R_CHUNK, (c + 1) * R_CHUNK)
        q = q_ref[sl, :].astype(F32)
        k = k_ref[sl, :].astype(F32) * kscale
        v = v_ref[sl, :]
        s = lax.dot_general(q.astype(BF16), k.astype(BF16), (((1,), (1,)), ((), ())),
                            preferred_element_type=F32) * decay
        inner = jnp.dot(s.astype(BF16), v, preferred_element_type=F32)
        state = st_ref[...]
        cross = jnp.dot((q * xi).astype(BF16), state.astype(BF16), preferred_element_type=F32)
        kz_t = (k * zeta).T.astype(BF16)
        st_ref[...] = state * chunk_decay + jnp.dot(kz_t, v, preferred_element_type=F32)
        out = inner + cross
        ms = jnp.mean(out * out, axis=-1, keepdims=True)
        gate = g_ref[sl, :].astype(F32)
        o_ref[sl, :] = (gate * jax.nn.sigmoid(gate) * (out * lax.rsqrt(ms + NORM_EPS))).astype(BF16)


def _retention(proj, batch, seq):
    p = proj.reshape(batch, seq, P_WIDTH)
    return pl.pallas_call(
        _ret_body,
        grid=(batch, R_HEADS, seq // RET_TB),
        in_specs=[
            pl.BlockSpec((None, RET_TB, R_KEY_DIM), lambda b, h, t: (b, t, P_QR // R_KEY_DIM + h)),
            pl.BlockSpec((None, RET_TB, R_KEY_DIM), lambda b, h, t: (b, t, P_KR // R_KEY_DIM + h)),
            pl.BlockSpec((None, RET_TB, R_VAL_DIM), lambda b, h, t: (b, t, P_VR // R_VAL_DIM + h)),
            pl.BlockSpec((None, RET_TB, R_VAL_DIM), lambda b, h, t: (b, t, P_GR // R_VAL_DIM + h)),
        ],
        out_specs=pl.BlockSpec((None, RET_TB, R_VAL_DIM), lambda b, h, t: (b, t, h)),
        out_shape=jax.ShapeDtypeStruct((batch, seq, R_V), BF16),
        scratch_shapes=[pltpu.VMEM((R_KEY_DIM, R_VAL_DIM), F32)],
        compiler_params=pltpu.CompilerParams(
            dimension_semantics=("parallel", "parallel", "arbitrary"), vmem_limit_bytes=VMEM_LIMIT),
    )(p, p, p, p)


MERGE_TM = 1024


def _merge_body(x_ref, ya_ref, rb_ref, ga_ref, gb_ref, wa_ref, wb_ref, wo_ref, o_ref):
    ya = jnp.dot(ya_ref[...].astype(BF16), wa_ref[...], preferred_element_type=F32)
    yb = jnp.dot(rb_ref[...], wb_ref[...], preferred_element_type=F32)
    ga = ga_ref[...].astype(F32)
    gb = gb_ref[...].astype(F32)
    merged = jax.nn.sigmoid(ga) * ya + jax.nn.sigmoid(gb) * yb
    o_ref[...] = x_ref[...] + jnp.dot(merged.astype(BF16), wo_ref[...], preferred_element_type=F32)


def _merge(x2d, ya2d, rb2d, proj, wa, wb, wo, layer):
    m = x2d.shape[0]

    def full(shape):
        return pl.BlockSpec((None,) + shape, lambda i: (layer, 0, 0))

    return pl.pallas_call(
        _merge_body,
        grid=(m // MERGE_TM,),
        in_specs=[
            pl.BlockSpec((MERGE_TM, D_MODEL), lambda i: (i, 0)),
            pl.BlockSpec((MERGE_TM, A_OUT), lambda i: (i, 0)),
            pl.BlockSpec((MERGE_TM, R_V), lambda i: (i, 0)),
            pl.BlockSpec((MERGE_TM, D_MODEL), lambda i: (i, P_GA // D_MODEL)),
            pl.BlockSpec((MERGE_TM, D_MODEL), lambda i: (i, P_GB // D_MODEL)),
            full((A_OUT, D_MODEL)), full((R_V, D_MODEL)), full((D_MODEL, D_MODEL)),
        ],
        out_specs=pl.BlockSpec((MERGE_TM, D_MODEL), lambda i: (i, 0)),
        out_shape=jax.ShapeDtypeStruct((m, D_MODEL), F32),
        compiler_params=pltpu.CompilerParams(
            dimension_semantics=("parallel",), vmem_limit_bytes=VMEM_LIMIT),
    )(x2d, ya2d, rb2d, proj, proj, wa, wb, wo)


FFN_TM = 1024
FFN_TF = 2048


def _ffn_body(x_ref, g_ref, wu_ref, wd_ref, fg_ref, o_ref, h_ref, *, final):
    c = pl.program_id(1)

    @pl.when(c == 0)
    def _start():
        x = x_ref[...]
        h_ref[...] = _rms(x, g_ref[...]).astype(BF16)
        o_ref[...] = x

    up = jnp.dot(h_ref[...], wu_ref[...], preferred_element_type=F32)
    up = jnp.square(jnp.maximum(up, 0.0)).astype(BF16)
    o_ref[...] += jnp.dot(up, wd_ref[...], preferred_element_type=F32)

    if final:
        @pl.when(c == pl.num_programs(1) - 1)
        def _final_norm():
            o_ref[...] = _rms(o_ref[...], fg_ref[...])


def _ffn(x2d, gain, wu, wd, layer, final_gain, final):
    m = x2d.shape[0]
    return pl.pallas_call(
        functools.partial(_ffn_body, final=final),
        grid=(m // FFN_TM, D_FF // FFN_TF),
        in_specs=[
            pl.BlockSpec((FFN_TM, D_MODEL), lambda i, c: (i, 0)),
            pl.BlockSpec((1, D_MODEL), lambda i, c: (0, 0)),
            pl.BlockSpec((None, D_MODEL, FFN_TF), lambda i, c: (layer, 0, c)),
            pl.BlockSpec((None, FFN_TF, D_MODEL), lambda i, c: (layer, c, 0)),
            pl.BlockSpec((1, D_MODEL), lambda i, c: (0, 0)),
        ],
        out_specs=pl.BlockSpec((FFN_TM, D_MODEL), lambda i, c: (i, 0)),
        out_shape=jax.ShapeDtypeStruct((m, D_MODEL), F32),
        scratch_shapes=[pltpu.VMEM((FFN_TM, D_MODEL), BF16)],
        compiler_params=pltpu.CompilerParams(
            dimension_semantics=("parallel", "arbitrary"), vmem_limit_bytes=VMEM_LIMIT),
    )(x2d, gain, wu, wd, final_gain)


def kernel(x, mix_norm, w_in, w_a, w_b, w_o, ffn_norm, w_up, w_down, final_norm):
    batch, seq, _ = x.shape
    depth = w_in.shape[0]
    assert seq % A_SPAN == 0 and seq % RET_TB == 0
    tables = _rotary_tables(seq)
    final_gain = final_norm.reshape(1, D_MODEL)
    x2d = x.reshape(batch * seq, D_MODEL)
    w_in, w_a, w_b, w_o, w_up, w_down = (_arrange_w_in(w_in),) + tuple(
        w.astype(BF16) for w in (w_a, w_b, w_o, w_up, w_down))
    for layer in range(depth):
        proj = _in_proj(x2d, mix_norm[layer].reshape(1, D_MODEL), w_in, layer, tables, seq)
        ya = _attention_a(proj, batch, seq).reshape(batch * seq, A_OUT)
        rb = _retention(proj, batch, seq).reshape(batch * seq, R_V)
        x2d = _merge(x2d, ya, rb, proj, w_a, w_b, w_o, layer)
        x2d = _ffn(x2d, ffn_norm[layer].reshape(1, D_MODEL), w_up, w_down, layer, final_gain,
                   final=(layer == depth - 1))
    return x2d.reshape(batch, seq, D_MODEL)
```

```python
import functools

import numpy as np
import jax
import jax.numpy as jnp
from jax import lax
from jax.experimental import pallas as pl
from jax.experimental.pallas import tpu as pltpu

F32 = jnp.float32
BF16 = jnp.bfloat16

D_MODEL = 1024
N_GROUPS = 3
DILATIONS = (1, 4, 16)
A_HEADS = 8
A_HEAD_DIM = 64
A_GROUP_W = A_HEADS * A_HEAD_DIM
A_WIDTH = N_GROUPS * A_GROUP_W
A_OUT = A_GROUP_W
ROT_DIM = A_HEAD_DIM // 4
ROPE_THETA = 500000.0
ATT_BLOCK = 128
NEG_INF = -1e30
R_HEADS = 4
R_KEY_DIM = 128
R_VAL_DIM = 256
R_QK = R_HEADS * R_KEY_DIM
R_V = R_HEADS * R_VAL_DIM
R_CHUNK = 128
R_ROT_BASE = 10000.0
D_FF = 4 * D_MODEL
NORM_EPS = 1e-6

OFF_QA = 0
OFF_KA = A_WIDTH
OFF_VA = 2 * A_WIDTH
OFF_QR = 3 * A_WIDTH
OFF_KR = OFF_QR + R_QK
OFF_VR = OFF_KR + R_QK
OFF_GR = OFF_VR + R_V
OFF_GA = OFF_GR + R_V
OFF_GB = OFF_GA + D_MODEL

LANES = 128
VMEM_LIMIT = 56 * 1024 * 1024

A_SPAN = ATT_BLOCK * max(DILATIONS)
A_UNITS = A_SPAN // ATT_BLOCK

IN_TM = A_SPAN
IN_TN = 512
IN_CHUNKS = (512, 512, 512, 256, 256)
X_CHUNKS = D_MODEL // LANES

P_GA = 0
P_GB = P_GA + D_MODEL
P_VR = P_GB + D_MODEL
P_GR = P_VR + R_V
P_QR = P_GR + R_V
P_KR = P_QR + R_QK
P_QKV = tuple(P_KR + R_QK + g * 3 * A_GROUP_W for g in range(N_GROUPS))
P_WIDTH = P_QKV[-1] + 3 * A_GROUP_W

_TAB_R = 2 * N_GROUPS
_TAB_NONE = 2 * N_GROUPS + 1
TILE_TABLE = ((_TAB_NONE,) * (P_QR // IN_TN) + (_TAB_R,) * (2 * R_QK // IN_TN)
              + tuple(t for g in range(N_GROUPS) for t in (g, N_GROUPS + g, _TAB_NONE)))
LOG2_E = 1.4426950408889634
A_Q_SCALE = A_HEAD_DIM ** -0.5 * LOG2_E
TILE_ORDER = (0,) * (P_QKV[0] // IN_TN) + tuple(g for g in range(N_GROUPS) for _ in range(3))


def _rms(x, gain):
    ms = jnp.mean(x * x, axis=-1, keepdims=True)
    return x * lax.rsqrt(ms + NORM_EPS) * gain


def _step_lookup(j, values):
    out = values[-1]
    for t in range(len(values) - 2, -1, -1):
        if values[t] != values[t + 1]:
            out = jnp.where(j <= t, values[t], out)
    return out


def _in_proj_body(*refs):
    x_refs = refs[:X_CHUNKS]
    g_ref, w_ref, cos_ref, sin_ref, o_ref, h_ref, inv_ref, stage_ref = refs[X_CHUNKS:]
    j = pl.program_id(1)

    @pl.when(j == 0)
    def _norm():
        gains = [g_ref[:, c * LANES:(c + 1) * LANES] for c in range(X_CHUNKS)]
        for u in range(A_UNITS):
            rows = pl.ds(u * ATT_BLOCK, ATT_BLOCK)
            xc = [x_refs[c][rows, :] for c in range(X_CHUNKS)]
            sq = xc[0] * xc[0]
            for c in range(1, X_CHUNKS):
                sq = sq + xc[c] * xc[c]
            inv = lax.rsqrt(jnp.sum(sq, axis=-1, keepdims=True) * (1.0 / D_MODEL) + NORM_EPS)
            inv = jnp.broadcast_to(inv, (ATT_BLOCK, LANES))
            inv_ref[rows, :] = inv
            for c in range(X_CHUNKS):
                h_ref[0, rows, c * LANES:(c + 1) * LANES] = (xc[c] * inv * gains[c]).astype(BF16)
        for g in range(1, N_GROUPS):
            dil, src_dil = DILATIONS[g], DILATIONS[g - 1]
            ratio = dil // src_dil
            piece = ATT_BLOCK // ratio
            for u in range(A_UNITS):
                ss, r = divmod(u, dil)
                for k in range(ratio):
                    src_block = (ss * ratio + k) * src_dil + r % src_dil
                    src = pl.ds(src_block * ATT_BLOCK + r // src_dil, piece, stride=ratio)
                    dst = pl.ds(u * ATT_BLOCK + k * piece, piece)
                    if g == 1:
                        inv = inv_ref[src, :]
                    for c in range(X_CHUNKS):
                        lanes = slice(c * LANES, (c + 1) * LANES)
                        if g == 1:
                            y = x_refs[c][src, :] * inv * gains[c]
                            if g + 1 < N_GROUPS:
                                stage_ref[c, dst, :] = y
                        else:
                            y = stage_ref[c, src, :]
                        h_ref[g, dst, lanes] = y.astype(BF16)

    order = _step_lookup(j, TILE_ORDER)
    starts = np.cumsum((0,) + IN_CHUNKS)
    for start, size in zip(starts, IN_CHUNKS):
        rows = pl.ds(int(start), size)
        acc = jnp.dot(h_ref[order, rows, :], w_ref[...], preferred_element_type=F32)
        for c in range(IN_TN // LANES):
            t = acc[:, c * LANES:(c + 1) * LANES]
            o_ref[rows, c * LANES:(c + 1) * LANES] = (
                t * cos_ref[rows, :] + pltpu.roll(t, LANES // 2, 1) * sin_ref[rows, :]).astype(BF16)


def _pair_layout(t):
    half = ROT_DIM // 2
    a, b = t[..., 0, :], t[..., 1, :]
    return jnp.concatenate([a[..., :half], b[..., :half], a[..., ROT_DIM:],
                            a[..., half:ROT_DIM], b[..., half:ROT_DIM], b[..., ROT_DIM:]], axis=-1)


def _arrange_w_in(w):
    lead = w.shape[:-1]
    w = w.astype(BF16)

    def pairs(cols):
        return _pair_layout(cols.reshape(lead + (N_GROUPS, A_GROUP_W // LANES, 2, A_HEAD_DIM))).reshape(
            lead + (N_GROUPS, A_GROUP_W))

    qa = pairs(w[..., OFF_QA:OFF_KA])
    ka = pairs(w[..., OFF_KA:OFF_VA])
    va = w[..., OFF_VA:OFF_QR].reshape(lead + (N_GROUPS, A_GROUP_W))
    qkv = jnp.concatenate([qa, ka, va], axis=-1).reshape(lead + (3 * A_WIDTH,))
    return jnp.concatenate([w[..., OFF_GA:], w[..., OFF_VR:OFF_GA], w[..., OFF_QR:OFF_VR], qkv], axis=-1)


@functools.lru_cache(maxsize=None)
def _rotary_tables(seq):
    f32 = np.float32
    half = ROT_DIM // 2
    inv = (1.0 / (f32(ROPE_THETA) ** (np.arange(half, dtype=f32) / f32(half)))).astype(f32)
    rest = A_HEAD_DIM - ROT_DIM
    ones, zeros = np.ones((seq, rest), f32), np.zeros((seq, rest), f32)
    cos_tabs, sin_tabs = [], []
    for dil in DILATIONS:
        span = ATT_BLOCK * dil
        pos = np.arange(seq, dtype=f32).reshape(seq // span, ATT_BLOCK, dil).transpose(0, 2, 1).reshape(seq, 1)
        ang = pos * inv[None, :]
        cos, sin = np.cos(ang), np.sin(ang)
        cos_tabs.append(np.concatenate([cos, cos, ones, cos, cos, ones], axis=1))
        sin_tabs.append(np.concatenate([-sin, -sin, zeros, sin, sin, zeros], axis=1))
    cos_tabs = [t * f32(A_Q_SCALE) for t in cos_tabs] + cos_tabs
    sin_tabs = [t * f32(A_Q_SCALE) for t in sin_tabs] + sin_tabs
    half_r = R_KEY_DIM // 2
    inv_r = (1.0 / (f32(R_ROT_BASE) ** (np.arange(half_r, dtype=f32) / f32(half_r)))).astype(f32)
    ang_r = np.arange(seq, dtype=f32)[:, None] * inv_r[None, :]
    cos_tabs += [np.concatenate([np.cos(ang_r), np.cos(ang_r)], axis=1), np.ones((seq, LANES), f32)]
    sin_tabs += [np.concatenate([-np.sin(ang_r), np.sin(ang_r)], axis=1), np.zeros((seq, LANES), f32)]
    return np.stack(cos_tabs).astype(f32), np.stack(sin_tabs).astype(f32)


def _in_proj(x2d, gain, w_bf16, layer, tables, seq):
    m = x2d.shape[0]
    tiles_per_seq = seq // IN_TM
    tab_spec = pl.BlockSpec((None, IN_TM, LANES),
                            lambda i, j: (_step_lookup(j, TILE_TABLE), i % tiles_per_seq, 0))
    x_specs = [pl.BlockSpec((IN_TM, LANES), lambda i, j, c=c: (i, c)) for c in range(X_CHUNKS)]
    return pl.pallas_call(
        _in_proj_body,
        grid=(m // IN_TM, P_WIDTH // IN_TN),
        in_specs=x_specs + [
            pl.BlockSpec((1, D_MODEL), lambda i, j: (0, 0)),
            pl.BlockSpec((None, D_MODEL, IN_TN), lambda i, j: (layer, 0, j)),
            tab_spec, tab_spec,
        ],
        out_specs=pl.BlockSpec((IN_TM, IN_TN), lambda i, j: (i, j)),
        out_shape=jax.ShapeDtypeStruct((m, P_WIDTH), BF16),
        scratch_shapes=[pltpu.VMEM((N_GROUPS, IN_TM, D_MODEL), BF16),
                        pltpu.VMEM((IN_TM, LANES), F32),
                        pltpu.VMEM((X_CHUNKS, IN_TM, LANES), F32)],
        compiler_params=pltpu.CompilerParams(
            dimension_semantics=("parallel", "arbitrary"), vmem_limit_bytes=VMEM_LIMIT),
    )(*([x2d] * X_CHUNKS), gain, w_bf16, *tables)


A_HEADS_PER_STEP = LANES // A_HEAD_DIM
A_QUAD = 4


def _attn_body(*refs):
    q_in = refs[0:3]
    kc_in = refs[3:6]
    kp_in = refs[6:9]
    vc_in = refs[9:12]
    vp_in = refs[12:15]
    o_ref = refs[15]
    ro, rl, bias, s_buf, p_buf, m_buf = refs[16:22]
    kbuf, vbuf = refs[22:25], refs[25:28]
    jt = pl.program_id(1)

    qi = lax.broadcasted_iota(jnp.int32, (ATT_BLOCK, 2 * ATT_BLOCK), 0)
    kj = lax.broadcasted_iota(jnp.int32, (ATT_BLOCK, 2 * ATT_BLOCK), 1)
    dist = qi + ATT_BLOCK - kj
    band = (dist >= 0) & (dist <= ATT_BLOCK)
    bias[0] = jnp.where(band & (kj >= ATT_BLOCK), 0.0, NEG_INF).astype(BF16)
    bias[1] = jnp.where(band, 0.0, NEG_INF).astype(BF16)
    lane = lax.broadcasted_iota(jnp.int32, (ATT_BLOCK, LANES), 1)

    half = ROT_DIM // 2
    first_head = (lane < half) | ((lane >= ROT_DIM) & (lane < A_HEAD_DIM + half))
    head_lanes = (first_head, jnp.logical_not(first_head))
    first = lane < A_HEAD_DIM
    v_lane = lax.broadcasted_iota(jnp.int32, (2 * ATT_BLOCK, LANES), 1)
    v_lanes = (v_lane < A_HEAD_DIM, v_lane >= A_HEAD_DIM)

    for g, d in enumerate(DILATIONS):
        span = ATT_BLOCK * d
        kbuf[g][0:span] = kp_in[g][...]
        kbuf[g][span:] = kc_in[g][...]
        vbuf[g][0:span] = vp_in[g][...]
        vbuf[g][span:] = vc_in[g][...]

    def rows_of(u):
        return pl.ds(u * ATT_BLOCK if isinstance(u, int) else pl.multiple_of(u * ATT_BLOCK, ATT_BLOCK), ATT_BLOCK)

    def stage_scores(u, g, d):
        b = bias[jnp.where(jnp.logical_or(jt > 0, u >= d), 1, 0)]
        q = q_in[g][rows_of(u), :]
        k2 = jnp.concatenate([kbuf[g][rows_of(u), :], kbuf[g][rows_of(u + d), :]], axis=0)
        for h in range(A_HEADS_PER_STEP):
            qm = jnp.where(head_lanes[h], q, jnp.zeros_like(q))
            s = lax.dot_general(qm, k2, (((1,), (1,)), ((), ())), preferred_element_type=F32)
            s_buf[u, h] = s.astype(BF16) + b

    def stage_softmax(u):
        ms = []
        for h in range(A_HEADS_PER_STEP):
            s = s_buf[u, h]
            m = jnp.max(s, axis=1, keepdims=True)
            p_buf[u, h] = jnp.exp2(s - m)
            ms.append(m.astype(F32))
        m_buf[u] = jnp.where(first, ms[0], ms[1])

    def stage_out(u, g, d):
        v2 = jnp.concatenate([vbuf[g][rows_of(u), :], vbuf[g][rows_of(u + d), :]], axis=0)
        pvs = [jnp.dot(p_buf[u, h], jnp.where(v_lanes[h], v2, jnp.ones_like(v2)), preferred_element_type=F32)
               for h in range(A_HEADS_PER_STEP)]
        out = jnp.where(first, pvs[0], pvs[1])
        den = pltpu.roll(jnp.where(first, pvs[1], pvs[0]), A_HEAD_DIM, 1)
        ro[g, rows_of(u), :] = out / den
        rl[g, rows_of(u), :] = m_buf[u] + jnp.log(den) * LOG2_E

    def quad(stage, qd, *args):
        for i in range(A_QUAD):
            stage(qd * A_QUAD + i, *args)

    n_quads = A_UNITS // A_QUAD
    for g, d in enumerate(DILATIONS):
        quad(stage_scores, 0, g, d)
        quad(stage_softmax, 0)
        quad(stage_scores, 1, g, d)

        def trip(qd, carry, g=g, d=d):
            quad(stage_out, qd, g, d)
            quad(stage_softmax, qd + 1)
            quad(stage_scores, qd + 2, g, d)
            return carry

        lax.fori_loop(0, n_quads - 2, trip, 0)
        quad(stage_out, n_quads - 2, g, d)
        quad(stage_softmax, n_quads - 1)
        quad(stage_out, n_quads - 1, g, d)

    d2, d3 = DILATIONS[1], DILATIONS[2]
    span2 = ATT_BLOCK * d2
    per_ss2 = ATT_BLOCK * d2 // d3
    for r3 in range(d3):
        blk = slice(r3 * ATT_BLOCK, (r3 + 1) * ATT_BLOCK)
        tok = pl.ds(r3, ATT_BLOCK, stride=d3)
        r2, off = r3 % d2, r3 // d2

        def from_g2(buf):
            return jnp.concatenate(
                [buf[1, pl.ds(ss2 * span2 + r2 * ATT_BLOCK + off, per_ss2, stride=d3 // d2), :]
                 for ss2 in range(A_SPAN // span2)], axis=0)

        o1, l1 = ro[0, tok, :], rl[0, tok, :]
        o2, l2 = from_g2(ro), from_g2(rl)
        o3, l3 = ro[2, blk, :], rl[2, blk, :]
        mx = jnp.maximum(jnp.maximum(l1, l2), l3)
        e1, e2, e3 = jnp.exp2(l1 - mx), jnp.exp2(l2 - mx), jnp.exp2(l3 - mx)
        o_ref[tok, :] = (e1 * o1 + e2 * o2 + e3 * o3) / (e1 + e2 + e3)


def _attention_a(proj, batch, seq):
    p = proj.reshape(batch, seq, P_WIDTH)

    def cur_spec(g, part):
        col = (P_QKV[g] + part * A_GROUP_W) // LANES
        return pl.BlockSpec((None, A_SPAN, LANES), lambda b, j, hp: (b, j, col + hp))

    def prev_spec(g, part):
        col = (P_QKV[g] + part * A_GROUP_W) // LANES
        span = ATT_BLOCK * DILATIONS[g]
        per = A_SPAN // span
        return pl.BlockSpec((None, span, LANES),
                            lambda b, j, hp: (b, jnp.maximum(j * per - 1, 0), col + hp))

    groups = range(N_GROUPS)
    kv_bufs = [pltpu.VMEM((ATT_BLOCK * d + A_SPAN, LANES), BF16) for d in DILATIONS]
    in_specs = ([cur_spec(g, 0) for g in groups] + [cur_spec(g, 1) for g in groups]
                + [prev_spec(g, 1) for g in groups] + [cur_spec(g, 2) for g in groups]
                + [prev_spec(g, 2) for g in groups])
    return pl.pallas_call(
        _attn_body,
        grid=(batch, seq // A_SPAN, A_GROUP_W // LANES),
        in_specs=in_specs,
        out_specs=pl.BlockSpec((None, A_SPAN, LANES), lambda b, j, hp: (b, j, hp)),
        out_shape=jax.ShapeDtypeStruct((batch, seq, A_OUT), F32),
        scratch_shapes=[pltpu.VMEM((N_GROUPS, A_SPAN, LANES), F32),
                        pltpu.VMEM((N_GROUPS, A_SPAN, LANES), F32),
                        pltpu.VMEM((2, ATT_BLOCK, 2 * ATT_BLOCK), BF16),
                        pltpu.VMEM((A_UNITS, A_HEADS_PER_STEP, ATT_BLOCK, 2 * ATT_BLOCK), BF16),
                        pltpu.VMEM((A_UNITS, A_HEADS_PER_STEP, ATT_BLOCK, 2 * ATT_BLOCK), BF16),
                        pltpu.VMEM((A_UNITS, ATT_BLOCK, LANES), F32)] + kv_bufs + kv_bufs,
        compiler_params=pltpu.CompilerParams(
            dimension_semantics=("parallel", "parallel", "parallel"), vmem_limit_bytes=VMEM_LIMIT),
    )(*([p] * 15))


RET_TB = 2048
_LOG_GAMMA = [float(v) for v in np.log1p(-(2.0 ** (-5.0 - np.arange(R_HEADS)))).astype(np.float32)]


def _ret_body(q_ref, k_ref, v_ref, g_ref, o_ref, st_ref):
    h = pl.program_id(1)
    t = pl.program_id(2)

    @pl.when(t == 0)
    def _reset():
        st_ref[...] = jnp.zeros_like(st_ref)

    lg = jnp.float32(_LOG_GAMMA[R_HEADS - 1])
    for hh in range(R_HEADS - 2, -1, -1):
        lg = jnp.where(h == hh, jnp.float32(_LOG_GAMMA[hh]), lg)
    row = lax.broadcasted_iota(jnp.int32, (R_CHUNK, R_CHUNK), 0).astype(F32)
    colv = lax.broadcasted_iota(jnp.int32, (R_CHUNK, R_CHUNK), 1).astype(F32)
    diff = row - colv
    decay = jnp.where(diff >= 0, jnp.exp(diff * lg), 0.0)
    zeta = jnp.exp((R_CHUNK - 1 - row) * lg)
    xi = jnp.exp((row + 1.0) * lg)
    chunk_decay = jnp.exp(jnp.full((R_KEY_DIM, R_VAL_DIM), float(R_CHUNK), F32) * lg)
    kscale = R_KEY_DIM ** -0.5

    for c in range(RET_TB // R_CHUNK):
        sl = slice(c * R_CHUNK, (c + 1) * R_CHUNK)
        q = q_ref[sl, :].astype(F32)
        k = k_ref[sl, :].astype(F32) * kscale
        v = v_ref[sl, :]
        s = lax.dot_general(q.astype(BF16), k.astype(BF16), (((1,), (1,)), ((), ())),
                            preferred_element_type=F32) * decay
        inner = jnp.dot(s.astype(BF16), v, preferred_element_type=F32)
        state = st_ref[...]
        cross = jnp.dot((q * xi).astype(BF16), state.astype(BF16), preferred_element_type=F32)
        kz_t = (k * zeta).T.astype(BF16)
        st_ref[...] = state * chunk_decay + jnp.dot(kz_t, v, preferred_element_type=F32)
        out = inner + cross
        ms = jnp.mean(out * out, axis=-1, keepdims=True)
        gate = g_ref[sl, :].astype(F32)
        o_ref[sl, :] = (gate * jax.nn.sigmoid(gate) * (out * lax.rsqrt(ms + NORM_EPS))).astype(BF16)


def _retention(proj, batch, seq):
    p = proj.reshape(batch, seq, P_WIDTH)
    return pl.pallas_call(
        _ret_body,
        grid=(batch, R_HEADS, seq // RET_TB),
        in_specs=[
            pl.BlockSpec((None, RET_TB, R_KEY_DIM), lambda b, h, t: (b, t, P_QR // R_KEY_DIM + h)),
            pl.BlockSpec((None, RET_TB, R_KEY_DIM), lambda b, h, t: (b, t, P_KR // R_KEY_DIM + h)),
            pl.BlockSpec((None, RET_TB, R_VAL_DIM), lambda b, h, t: (b, t, P_VR // R_VAL_DIM + h)),
            pl.BlockSpec((None, RET_TB, R_VAL_DIM), lambda b, h, t: (b, t, P_GR // R_VAL_DIM + h)),
        ],
        out_specs=pl.BlockSpec((None, RET_TB, R_VAL_DIM), lambda b, h, t: (b, t, h)),
        out_shape=jax.ShapeDtypeStruct((batch, seq, R_V), BF16),
        scratch_shapes=[pltpu.VMEM((R_KEY_DIM, R_VAL_DIM), F32)],
        compiler_params=pltpu.CompilerParams(
            dimension_semantics=("parallel", "parallel", "arbitrary"), vmem_limit_bytes=VMEM_LIMIT),
    )(p, p, p, p)


MERGE_TM = 1024


def _merge_body(x_ref, ya_ref, rb_ref, ga_ref, gb_ref, wa_ref, wb_ref, wo_ref, o_ref):
    ya = jnp.dot(ya_ref[...].astype(BF16), wa_ref[...], preferred_element_type=F32)
    yb = jnp.dot(rb_ref[...], wb_ref[...], preferred_element_type=F32)
    ga = ga_ref[...].astype(F32)
    gb = gb_ref[...].astype(F32)
    merged = jax.nn.sigmoid(ga) * ya + jax.nn.sigmoid(gb) * yb
    o_ref[...] = x_ref[...] + jnp.dot(merged.astype(BF16), wo_ref[...], preferred_element_type=F32)


def _merge(x2d, ya2d, rb2d, proj, wa, wb, wo, layer):
    m = x2d.shape[0]

    def full(shape):
        return pl.BlockSpec((None,) + shape, lambda i: (layer, 0, 0))

    return pl.pallas_call(
        _merge_body,
        grid=(m // MERGE_TM,),
        in_specs=[
            pl.BlockSpec((MERGE_TM, D_MODEL), lambda i: (i, 0)),
            pl.BlockSpec((MERGE_TM, A_OUT), lambda i: (i, 0)),
            pl.BlockSpec((MERGE_TM, R_V), lambda i: (i, 0)),
            pl.BlockSpec((MERGE_TM, D_MODEL), lambda i: (i, P_GA // D_MODEL)),
            pl.BlockSpec((MERGE_TM, D_MODEL), lambda i: (i, P_GB // D_MODEL)),
            full((A_OUT, D_MODEL)), full((R_V, D_MODEL)), full((D_MODEL, D_MODEL)),
        ],
        out_specs=pl.BlockSpec((MERGE_TM, D_MODEL), lambda i: (i, 0)),
        out_shape=jax.ShapeDtypeStruct((m, D_MODEL), F32),
        compiler_params=pltpu.CompilerParams(
            dimension_semantics=("parallel",), vmem_limit_bytes=VMEM_LIMIT),
    )(x2d, ya2d, rb2d, proj, proj, wa, wb, wo)


FFN_TM = 1024
FFN_TF = 2048


def _ffn_body(x_ref, g_ref, wu_ref, wd_ref, fg_ref, o_ref, h_ref, *, final):
    c = pl.program_id(1)

    @pl.when(c == 0)
    def _start():
        x = x_ref[...]
        h_ref[...] = _rms(x, g_ref[...]).astype(BF16)
        o_ref[...] = x

    up = jnp.dot(h_ref[...], wu_ref[...], preferred_element_type=F32)
    up = jnp.square(jnp.maximum(up, 0.0)).astype(BF16)
    o_ref[...] += jnp.dot(up, wd_ref[...], preferred_element_type=F32)

    if final:
        @pl.when(c == pl.num_programs(1) - 1)
        def _final_norm():
            o_ref[...] = _rms(o_ref[...], fg_ref[...])


def _ffn(x2d, gain, wu, wd, layer, final_gain, final):
    m = x2d.shape[0]
    return pl.pallas_call(
        functools.partial(_ffn_body, final=final),
        grid=(m // FFN_TM, D_FF // FFN_TF),
        in_specs=[
            pl.BlockSpec((FFN_TM, D_MODEL), lambda i, c: (i, 0)),
            pl.BlockSpec((1, D_MODEL), lambda i, c: (0, 0)),
            pl.BlockSpec((None, D_MODEL, FFN_TF), lambda i, c: (layer, 0, c)),
            pl.BlockSpec((None, FFN_TF, D_MODEL), lambda i, c: (layer, c, 0)),
            pl.BlockSpec((1, D_MODEL), lambda i, c: (0, 0)),
        ],
        out_specs=pl.BlockSpec((FFN_TM, D_MODEL), lambda i, c: (i, 0)),
        out_shape=jax.ShapeDtypeStruct((m, D_MODEL), F32),
        scratch_shapes=[pltpu.VMEM((FFN_TM, D_MODEL), BF16)],
        compiler_params=pltpu.CompilerParams(
            dimension_semantics=("parallel", "arbitrary"), vmem_limit_bytes=VMEM_LIMIT),
    )(x2d, gain, wu, wd, final_gain)


def kernel(x, mix_norm, w_in, w_a, w_b, w_o, ffn_norm, w_up, w_down, final_norm):
    batch, seq, _ = x.shape
    depth = w_in.shape[0]
    assert seq % A_SPAN == 0 and seq % RET_TB == 0
    tables = _rotary_tables(seq)
    final_gain = final_norm.reshape(1, D_MODEL)
    x2d = x.reshape(batch * seq, D_MODEL)
    w_in, w_a, w_b, w_o, w_up, w_down = (_arrange_w_in(w_in),) + tuple(
        w.astype(BF16) for w in (w_a, w_b, w_o, w_up, w_down))
    for layer in range(depth):
        proj = _in_proj(x2d, mix_norm[layer].reshape(1, D_MODEL), w_in, layer, tables, seq)
        ya = _attention_a(proj, batch, seq).reshape(batch * seq, A_OUT)
        rb = _retention(proj, batch, seq).reshape(batch * seq, R_V)
        x2d = _merge(x2d, ya, rb, proj, w_a, w_b, w_o, layer)
        x2d = _ffn(x2d, ffn_norm[layer].reshape(1, D_MODEL), w_up, w_down, layer, final_gain,
                   final=(layer == depth - 1))
    return x2d.reshape(batch, seq, D_MODEL)
```

```python
import functools

import numpy as np
import jax
import jax.numpy as jnp
from jax import lax
from jax.experimental import pallas as pl
from jax.experimental.pallas import tpu as pltpu

F32 = jnp.float32
BF16 = jnp.bfloat16

D_MODEL = 1024
N_GROUPS = 3
DILATIONS = (1, 4, 16)
A_HEADS = 8
A_HEAD_DIM = 64
A_GROUP_W = A_HEADS * A_HEAD_DIM
A_WIDTH = N_GROUPS * A_GROUP_W
A_OUT = A_GROUP_W
ROT_DIM = A_HEAD_DIM // 4
ROPE_THETA = 500000.0
ATT_BLOCK = 128
NEG_INF = -1e30
R_HEADS = 4
R_KEY_DIM = 128
R_VAL_DIM = 256
R_QK = R_HEADS * R_KEY_DIM
R_V = R_HEADS * R_VAL_DIM
R_CHUNK = 128
R_ROT_BASE = 10000.0
D_FF = 4 * D_MODEL
NORM_EPS = 1e-6

OFF_QA = 0
OFF_KA = A_WIDTH
OFF_VA = 2 * A_WIDTH
OFF_QR = 3 * A_WIDTH
OFF_KR = OFF_QR + R_QK
OFF_VR = OFF_KR + R_QK
OFF_GR = OFF_VR + R_V
OFF_GA = OFF_GR + R_V
OFF_GB = OFF_GA + D_MODEL

LANES = 128
VMEM_LIMIT = 56 * 1024 * 1024

A_SPAN = ATT_BLOCK * max(DILATIONS)
A_UNITS = A_SPAN // ATT_BLOCK

IN_TM = A_SPAN
IN_TN = 512
IN_CHUNKS = (512, 512, 512, 256, 256)
X_CHUNKS = D_MODEL // LANES

P_GA = 0
P_GB = P_GA + D_MODEL
P_VR = P_GB + D_MODEL
P_GR = P_VR + R_V
P_QR = P_GR + R_V
P_KR = P_QR + R_QK
P_QKV = tuple(P_KR + R_QK + g * 3 * A_GROUP_W for g in range(N_GROUPS))
P_WIDTH = P_QKV[-1] + 3 * A_GROUP_W

_TAB_R = 2 * N_GROUPS
_TAB_NONE = 2 * N_GROUPS + 1
TILE_TABLE = ((_TAB_NONE,) * (P_QR // IN_TN) + (_TAB_R,) * (2 * R_QK // IN_TN)
              + tuple(t for g in range(N_GROUPS) for t in (g, N_GROUPS + g, _TAB_NONE)))
LOG2_E = 1.4426950408889634
A_Q_SCALE = A_HEAD_DIM ** -0.5 * LOG2_E
TILE_ORDER = (0,) * (P_QKV[0] // IN_TN) + tuple(g for g in range(N_GROUPS) for _ in range(3))


def _rms(x, gain):
    ms = jnp.mean(x * x, axis=-1, keepdims=True)
    return x * lax.rsqrt(ms + NORM_EPS) * gain


def _step_lookup(j, values):
    out = values[-1]
    for t in range(len(values) - 2, -1, -1):
        if values[t] != values[t + 1]:
            out = jnp.where(j <= t, values[t], out)
    return out


def _residue_pieces(dil, src_dil, u):
    ratio = dil // src_dil
    piece = ATT_BLOCK // ratio
    ss, r = divmod(u, dil)
    for k in range(ratio):
        src_block = (ss * ratio + k) * src_dil + r % src_dil
        yield (pl.ds(src_block * ATT_BLOCK + r // src_dil, piece, stride=ratio),
               pl.ds(u * ATT_BLOCK + k * piece, piece))


def _in_proj_body(*refs):
    x_refs = refs[:X_CHUNKS]
    g_ref, w_ref, cos_ref, sin_ref, o_ref, h_ref, inv_ref, stage_ref = refs[X_CHUNKS:]
    j = pl.program_id(1)

    @pl.when(j == 0)
    def _norm():
        gains = [g_ref[:, c * LANES:(c + 1) * LANES] for c in range(X_CHUNKS)]
        for u in range(A_UNITS):
            rows = pl.ds(u * ATT_BLOCK, ATT_BLOCK)
            xc = [x_refs[c][rows, :] for c in range(X_CHUNKS)]
            sq = xc[0] * xc[0]
            for c in range(1, X_CHUNKS):
                sq = sq + xc[c] * xc[c]
            inv = lax.rsqrt(jnp.sum(sq, axis=-1, keepdims=True) * (1.0 / D_MODEL) + NORM_EPS)
            inv = jnp.broadcast_to(inv, (ATT_BLOCK, LANES))
            inv_ref[rows, :] = inv
            for c in range(X_CHUNKS):
                h_ref[0, rows, c * LANES:(c + 1) * LANES] = (xc[c] * inv * gains[c]).astype(BF16)
        for g in range(1, N_GROUPS):
            for u in range(A_UNITS):
                for src, dst in _residue_pieces(DILATIONS[g], DILATIONS[g - 1], u):
                    if g == 1:
                        inv = inv_ref[src, :]
                    for c in range(X_CHUNKS):
                        lanes = slice(c * LANES, (c + 1) * LANES)
                        if g == 1:
                            y = x_refs[c][src, :] * inv * gains[c]
                            if g + 1 < N_GROUPS:
                                stage_ref[c, dst, :] = y
                        else:
                            y = stage_ref[c, src, :]
                        h_ref[g, dst, lanes] = y.astype(BF16)

    order = _step_lookup(j, TILE_ORDER)
    starts = np.cumsum((0,) + IN_CHUNKS)
    for start, size in zip(starts, IN_CHUNKS):
        rows = pl.ds(int(start), size)
        acc = jnp.dot(h_ref[order, rows, :], w_ref[...], preferred_element_type=F32)
        for c in range(IN_TN // LANES):
            t = acc[:, c * LANES:(c + 1) * LANES]
            o_ref[rows, c * LANES:(c + 1) * LANES] = (
                t * cos_ref[rows, :] + pltpu.roll(t, LANES // 2, 1) * sin_ref[rows, :]).astype(BF16)


def _pair_layout(t):
    half = ROT_DIM // 2
    a, b = t[..., 0, :], t[..., 1, :]
    return jnp.concatenate([a[..., :half], b[..., :half], a[..., ROT_DIM:],
                            a[..., half:ROT_DIM], b[..., half:ROT_DIM], b[..., ROT_DIM:]], axis=-1)


def _arrange_w_in(w):
    lead = w.shape[:-1]
    w = w.astype(BF16)

    def pairs(cols):
        return _pair_layout(cols.reshape(lead + (N_GROUPS, A_GROUP_W // LANES, 2, A_HEAD_DIM))).reshape(
            lead + (N_GROUPS, A_GROUP_W))

    qa = pairs(w[..., OFF_QA:OFF_KA])
    ka = pairs(w[..., OFF_KA:OFF_VA])
    va = w[..., OFF_VA:OFF_QR].reshape(lead + (N_GROUPS, A_GROUP_W))
    qkv = jnp.concatenate([qa, ka, va], axis=-1).reshape(lead + (3 * A_WIDTH,))
    return jnp.concatenate([w[..., OFF_GA:], w[..., OFF_VR:OFF_GA], w[..., OFF_QR:OFF_VR], qkv], axis=-1)


@functools.lru_cache(maxsize=None)
def _rotary_tables(seq):
    f32 = np.float32
    half = ROT_DIM // 2
    inv = (1.0 / (f32(ROPE_THETA) ** (np.arange(half, dtype=f32) / f32(half)))).astype(f32)
    rest = A_HEAD_DIM - ROT_DIM
    ones, zeros = np.ones((seq, rest), f32), np.zeros((seq, rest), f32)
    cos_tabs, sin_tabs = [], []
    for dil in DILATIONS:
        span = ATT_BLOCK * dil
        pos = np.arange(seq, dtype=f32).reshape(seq // span, ATT_BLOCK, dil).transpose(0, 2, 1).reshape(seq, 1)
        ang = pos * inv[None, :]
        cos, sin = np.cos(ang), np.sin(ang)
        cos_tabs.append(np.concatenate([cos, cos, ones, cos, cos, ones], axis=1))
        sin_tabs.append(np.concatenate([-sin, -sin, zeros, sin, sin, zeros], axis=1))
    cos_tabs = [t * f32(A_Q_SCALE) for t in cos_tabs] + cos_tabs
    sin_tabs = [t * f32(A_Q_SCALE) for t in sin_tabs] + sin_tabs
    half_r = R_KEY_DIM // 2
    inv_r = (1.0 / (f32(R_ROT_BASE) ** (np.arange(half_r, dtype=f32) / f32(half_r)))).astype(f32)
    ang_r = np.arange(seq, dtype=f32)[:, None] * inv_r[None, :]
    cos_tabs += [np.concatenate([np.cos(ang_r), np.cos(ang_r)], axis=1), np.ones((seq, LANES), f32)]
    sin_tabs += [np.concatenate([-np.sin(ang_r), np.sin(ang_r)], axis=1), np.zeros((seq, LANES), f32)]
    return np.stack(cos_tabs).astype(f32), np.stack(sin_tabs).astype(f32)


def _in_proj(x2d, gain, w_bf16, layer, tables, seq):
    m = x2d.shape[0]
    tiles_per_seq = seq // IN_TM
    tab_spec = pl.BlockSpec((None, IN_TM, LANES),
                            lambda i, j: (_step_lookup(j, TILE_TABLE), i % tiles_per_seq, 0))
    x_specs = [pl.BlockSpec((IN_TM, LANES), lambda i, j, c=c: (i, c)) for c in range(X_CHUNKS)]
    return pl.pallas_call(
        _in_proj_body,
        grid=(m // IN_TM, P_WIDTH // IN_TN),
        in_specs=x_specs + [
            pl.BlockSpec((1, D_MODEL), lambda i, j: (0, 0)),
            pl.BlockSpec((None, D_MODEL, IN_TN), lambda i, j: (layer, 0, j)),
            tab_spec, tab_spec,
        ],
        out_specs=pl.BlockSpec((IN_TM, IN_TN), lambda i, j: (i, j)),
        out_shape=jax.ShapeDtypeStruct((m, P_WIDTH), BF16),
        scratch_shapes=[pltpu.VMEM((N_GROUPS, IN_TM, D_MODEL), BF16),
                        pltpu.VMEM((IN_TM, LANES), F32),
                        pltpu.VMEM((X_CHUNKS, IN_TM, LANES), F32)],
        compiler_params=pltpu.CompilerParams(
            dimension_semantics=("parallel", "arbitrary"), vmem_limit_bytes=VMEM_LIMIT),
    )(*([x2d] * X_CHUNKS), gain, w_bf16, *tables)


A_HEADS_PER_STEP = LANES // A_HEAD_DIM
A_QUAD = 4


def _attn_body(*refs):
    q_in = refs[0:3]
    kc_in = refs[3:6]
    kp_in = refs[6:9]
    vc_in = refs[9:12]
    vp_in = refs[12:15]
    o_ref = refs[15]
    ro, rl, bias, s_buf, p_buf, m_buf = refs[16:22]
    kbuf, vbuf = refs[22:25], refs[25:28]
    so_buf, sl_buf, sy_buf = refs[28:31]
    jt = pl.program_id(1)

    qi = lax.broadcasted_iota(jnp.int32, (ATT_BLOCK, 2 * ATT_BLOCK), 0)
    kj = lax.broadcasted_iota(jnp.int32, (ATT_BLOCK, 2 * ATT_BLOCK), 1)
    dist = qi + ATT_BLOCK - kj
    band = (dist >= 0) & (dist <= ATT_BLOCK)
    bias[0] = jnp.where(band & (kj >= ATT_BLOCK), 0.0, NEG_INF).astype(BF16)
    bias[1] = jnp.where(band, 0.0, NEG_INF).astype(BF16)
    lane = lax.broadcasted_iota(jnp.int32, (ATT_BLOCK, LANES), 1)

    half = ROT_DIM // 2
    first_head = (lane < half) | ((lane >= ROT_DIM) & (lane < A_HEAD_DIM + half))
    head_lanes = (first_head, jnp.logical_not(first_head))
    first = lane < A_HEAD_DIM
    v_lane = lax.broadcasted_iota(jnp.int32, (2 * ATT_BLOCK, LANES), 1)
    v_lanes = (v_lane < A_HEAD_DIM, v_lane >= A_HEAD_DIM)

    for g, d in enumerate(DILATIONS):
        span = ATT_BLOCK * d
        kbuf[g][0:span] = kp_in[g][...]
        kbuf[g][span:] = kc_in[g][...]
        vbuf[g][0:span] = vp_in[g][...]
        vbuf[g][span:] = vc_in[g][...]

    def rows_of(u):
        return pl.ds(u * ATT_BLOCK if isinstance(u, int) else pl.multiple_of(u * ATT_BLOCK, ATT_BLOCK), ATT_BLOCK)

    def stage_scores(u, g, d):
        b = bias[jnp.where(jnp.logical_or(jt > 0, u >= d), 1, 0)]
        q = q_in[g][rows_of(u), :]
        k2 = jnp.concatenate([kbuf[g][rows_of(u), :], kbuf[g][rows_of(u + d), :]], axis=0)
        for h in range(A_HEADS_PER_STEP):
            qm = jnp.where(head_lanes[h], q, jnp.zeros_like(q))
            s = lax.dot_general(qm, k2, (((1,), (1,)), ((), ())), preferred_element_type=F32)
            s_buf[u, h] = s.astype(BF16) + b

    def stage_softmax(u):
        ms = []
        for h in range(A_HEADS_PER_STEP):
            s = s_buf[u, h]
            m = jnp.max(s, axis=1, keepdims=True)
            p_buf[u, h] = jnp.exp2(s - m)
            ms.append(m.astype(F32))
        m_buf[u] = jnp.where(first, ms[0], ms[1])

    def stage_out(u, g, d):
        v2 = jnp.concatenate([vbuf[g][rows_of(u), :], vbuf[g][rows_of(u + d), :]], axis=0)
        pvs = [jnp.dot(p_buf[u, h], jnp.where(v_lanes[h], v2, jnp.ones_like(v2)), preferred_element_type=F32)
               for h in range(A_HEADS_PER_STEP)]
        out = jnp.where(first, pvs[0], pvs[1])
        den = pltpu.roll(jnp.where(first, pvs[1], pvs[0]), A_HEAD_DIM, 1)
        ro[g, rows_of(u), :] = out / den
        rl[g, rows_of(u), :] = m_buf[u] + jnp.log(den) * LOG2_E

    def quad(stage, qd, *args):
        for i in range(A_QUAD):
            stage(qd * A_QUAD + i, *args)

    n_quads = A_UNITS // A_QUAD
    for g, d in enumerate(DILATIONS):
        quad(stage_scores, 0, g, d)
        quad(stage_softmax, 0)
        quad(stage_scores, 1, g, d)

        def trip(qd, carry, g=g, d=d):
            quad(stage_out, qd, g, d)
            quad(stage_softmax, qd + 1)
            quad(stage_scores, qd + 2, g, d)
            return carry

        lax.fori_loop(0, n_quads - 2, trip, 0)
        quad(stage_out, n_quads - 2, g, d)
        quad(stage_softmax, n_quads - 1)
        quad(stage_out, n_quads - 1, g, d)

    d2, d3 = DILATIONS[1], DILATIONS[2]
    for u in range(A_UNITS):
        for src, dst in _residue_pieces(d2, DILATIONS[0], u):
            so_buf[dst, :] = ro[0, src, :]
            sl_buf[dst, :] = rl[0, src, :]
    for r3 in range(A_UNITS):
        blk = pl.ds(r3 * ATT_BLOCK, ATT_BLOCK)
        pieces = list(_residue_pieces(d3, d2, r3))
        o1 = jnp.concatenate([so_buf[src, :] for src, _ in pieces], axis=0)
        l1 = jnp.concatenate([sl_buf[src, :] for src, _ in pieces], axis=0)
        o2 = jnp.concatenate([ro[1, src, :] for src, _ in pieces], axis=0)
        l2 = jnp.concatenate([rl[1, src, :] for src, _ in pieces], axis=0)
        o3, l3 = ro[2, blk, :], rl[2, blk, :]
        mx = jnp.maximum(jnp.maximum(l1, l2), l3)
        e1, e2, e3 = jnp.exp2(l1 - mx), jnp.exp2(l2 - mx), jnp.exp2(l3 - mx)
        y = (e1 * o1 + e2 * o2 + e3 * o3) / (e1 + e2 + e3)
        for k, (src, _) in enumerate(pieces):
            sy_buf[src, :] = y[k * ATT_BLOCK // len(pieces):(k + 1) * ATT_BLOCK // len(pieces), :]
    for u in range(A_UNITS):
        for src, dst in _residue_pieces(d2, DILATIONS[0], u):
            o_ref[src, :] = sy_buf[dst, :]


def _attention_a(proj, batch, seq):
    p = proj.reshape(batch, seq, P_WIDTH)

    def cur_spec(g, part):
        col = (P_QKV[g] + part * A_GROUP_W) // LANES
        return pl.BlockSpec((None, A_SPAN, LANES), lambda b, j, hp: (b, j, col + hp))

    def prev_spec(g, part):
        col = (P_QKV[g] + part * A_GROUP_W) // LANES
        span = ATT_BLOCK * DILATIONS[g]
        per = A_SPAN // span
        return pl.BlockSpec((None, span, LANES),
                            lambda b, j, hp: (b, jnp.maximum(j * per - 1, 0), col + hp))

    groups = range(N_GROUPS)
    kv_bufs = [pltpu.VMEM((ATT_BLOCK * d + A_SPAN, LANES), BF16) for d in DILATIONS]
    in_specs = ([cur_spec(g, 0) for g in groups] + [cur_spec(g, 1) for g in groups]
                + [prev_spec(g, 1) for g in groups] + [cur_spec(g, 2) for g in groups]
                + [prev_spec(g, 2) for g in groups])
    return pl.pallas_call(
        _attn_body,
        grid=(batch, seq // A_SPAN, A_GROUP_W // LANES),
        in_specs=in_specs,
        out_specs=pl.BlockSpec((None, A_SPAN, LANES), lambda b, j, hp: (b, j, hp)),
        out_shape=jax.ShapeDtypeStruct((batch, seq, A_OUT), F32),
        scratch_shapes=[pltpu.VMEM((N_GROUPS, A_SPAN, LANES), F32),
                        pltpu.VMEM((N_GROUPS, A_SPAN, LANES), F32),
                        pltpu.VMEM((2, ATT_BLOCK, 2 * ATT_BLOCK), BF16),
                        pltpu.VMEM((A_UNITS, A_HEADS_PER_STEP, ATT_BLOCK, 2 * ATT_BLOCK), BF16),
                        pltpu.VMEM((A_UNITS, A_HEADS_PER_STEP, ATT_BLOCK, 2 * ATT_BLOCK), BF16),
                        pltpu.VMEM((A_UNITS, ATT_BLOCK, LANES), F32)] + kv_bufs + kv_bufs
                       + [pltpu.VMEM((A_SPAN, LANES), F32)] * 3,
        compiler_params=pltpu.CompilerParams(
            dimension_semantics=("parallel", "parallel", "parallel"), vmem_limit_bytes=VMEM_LIMIT),
    )(*([p] * 15))


RET_TB = 2048
_LOG_GAMMA = [float(v) for v in np.log1p(-(2.0 ** (-5.0 - np.arange(R_HEADS)))).astype(np.float32)]


def _ret_body(q_ref, k_ref, v_ref, g_ref, o_ref, st_ref):
    h = pl.program_id(1)
    t = pl.program_id(2)

    @pl.when(t == 0)
    def _reset():
        st_ref[...] = jnp.zeros_like(st_ref)

    lg = jnp.float32(_LOG_GAMMA[R_HEADS - 1])
    for hh in range(R_HEADS - 2, -1, -1):
        lg = jnp.where(h == hh, jnp.float32(_LOG_GAMMA[hh]), lg)
    row = lax.broadcasted_iota(jnp.int32, (R_CHUNK, R_CHUNK), 0).astype(F32)
    colv = lax.broadcasted_iota(jnp.int32, (R_CHUNK, R_CHUNK), 1).astype(F32)
    diff = row - colv
    decay = jnp.where(diff >= 0, jnp.exp(diff * lg), 0.0)
    zeta = jnp.exp((R_CHUNK - 1 - row) * lg)
    xi = jnp.exp((row + 1.0) * lg)
    chunk_decay = jnp.exp(jnp.full((R_KEY_DIM, R_VAL_DIM), float(R_CHUNK), F32) * lg)
    kscale = R_KEY_DIM ** -0.5

    for c in range(RET_TB // R_CHUNK):
        sl = slice(c * R_CHUNK, (c + 1) * R_CHUNK)
        q = q_ref[sl, :].astype(F32)
        k = k_ref[sl, :].astype(F32) * kscale
        v = v_ref[sl, :]
        s = lax.dot_general(q.astype(BF16), k.astype(BF16), (((1,), (1,)), ((), ())),
                            preferred_element_type=F32) * decay
        inner = jnp.dot(s.astype(BF16), v, preferred_element_type=F32)
        state = st_ref[...]
        cross = jnp.dot((q * xi).astype(BF16), state.astype(BF16), preferred_element_type=F32)
        kz_t = (k * zeta).T.astype(BF16)
        st_ref[...] = state * chunk_decay + jnp.dot(kz_t, v, preferred_element_type=F32)
        out = inner + cross
        ms = jnp.mean(out * out, axis=-1, keepdims=True)
        gate = g_ref[sl, :].astype(F32)
        o_ref[sl, :] = (gate * jax.nn.sigmoid(gate) * (out * lax.rsqrt(ms + NORM_EPS))).astype(BF16)


def _retention(proj, batch, seq):
    p = proj.reshape(batch, seq, P_WIDTH)
    return pl.pallas_call(
        _ret_body,
        grid=(batch, R_HEADS, seq // RET_TB),
        in_specs=[
            pl.BlockSpec((None, RET_TB, R_KEY_DIM), lambda b, h, t: (b, t, P_QR // R_KEY_DIM + h)),
            pl.BlockSpec((None, RET_TB, R_KEY_DIM), lambda b, h, t: (b, t, P_KR // R_KEY_DIM + h)),
            pl.BlockSpec((None, RET_TB, R_VAL_DIM), lambda b, h, t: (b, t, P_VR // R_VAL_DIM + h)),
            pl.BlockSpec((None, RET_TB, R_VAL_DIM), lambda b, h, t: (b, t, P_GR // R_VAL_DIM + h)),
        ],
        out_specs=pl.BlockSpec((None, RET_TB, R_VAL_DIM), lambda b, h, t: (b, t, h)),
        out_shape=jax.ShapeDtypeStruct((batch, seq, R_V), BF16),
        scratch_shapes=[pltpu.VMEM((R_KEY_DIM, R_VAL_DIM), F32)],
        compiler_params=pltpu.CompilerParams(
            dimension_semantics=("parallel", "parallel", "arbitrary"), vmem_limit_bytes=VMEM_LIMIT),
    )(p, p, p, p)


MERGE_TM = 1024


def _merge_body(x_ref, ya_ref, rb_ref, ga_ref, gb_ref, wa_ref, wb_ref, wo_ref, o_ref):
    ya = jnp.dot(ya_ref[...].astype(BF16), wa_ref[...], preferred_element_type=F32)
    yb = jnp.dot(rb_ref[...], wb_ref[...], preferred_element_type=F32)
    ga = ga_ref[...].astype(F32)
    gb = gb_ref[...].astype(F32)
    merged = jax.nn.sigmoid(ga) * ya + jax.nn.sigmoid(gb) * yb
    o_ref[...] = x_ref[...] + jnp.dot(merged.astype(BF16), wo_ref[...], preferred_element_type=F32)


def _merge(x2d, ya2d, rb2d, proj, wa, wb, wo, layer):
    m = x2d.shape[0]

    def full(shape):
        return pl.BlockSpec((None,) + shape, lambda i: (layer, 0, 0))

    return pl.pallas_call(
        _merge_body,
        grid=(m // MERGE_TM,),
        in_specs=[
            pl.BlockSpec((MERGE_TM, D_MODEL), lambda i: (i, 0)),
            pl.BlockSpec((MERGE_TM, A_OUT), lambda i: (i, 0)),
            pl.BlockSpec((MERGE_TM, R_V), lambda i: (i, 0)),
            pl.BlockSpec((MERGE_TM, D_MODEL), lambda i: (i, P_GA // D_MODEL)),
            pl.BlockSpec((MERGE_TM, D_MODEL), lambda i: (i, P_GB // D_MODEL)),
            full((A_OUT, D_MODEL)), full((R_V, D_MODEL)), full((D_MODEL, D_MODEL)),
        ],
        out_specs=pl.BlockSpec((MERGE_TM, D_MODEL), lambda i: (i, 0)),
        out_shape=jax.ShapeDtypeStruct((m, D_MODEL), F32),
        compiler_params=pltpu.CompilerParams(
            dimension_semantics=("parallel",), vmem_limit_bytes=VMEM_LIMIT),
    )(x2d, ya2d, rb2d, proj, proj, wa, wb, wo)


FFN_TM = 1024
FFN_TF = 2048


def _ffn_body(x_ref, g_ref, wu_ref, wd_ref, fg_ref, o_ref, h_ref, *, final):
    c = pl.program_id(1)

    @pl.when(c == 0)
    def _start():
        x = x_ref[...]
        h_ref[...] = _rms(x, g_ref[...]).astype(BF16)
        o_ref[...] = x

    up = jnp.dot(h_ref[...], wu_ref[...], preferred_element_type=F32)
    up = jnp.square(jnp.maximum(up, 0.0)).astype(BF16)
    o_ref[...] += jnp.dot(up, wd_ref[...], preferred_element_type=F32)

    if final:
        @pl.when(c == pl.num_programs(1) - 1)
        def _final_norm():
            o_ref[...] = _rms(o_ref[...], fg_ref[...])


def _ffn(x2d, gain, wu, wd, layer, final_gain, final):
    m = x2d.shape[0]
    return pl.pallas_call(
        functools.partial(_ffn_body, final=final),
        grid=(m // FFN_TM, D_FF // FFN_TF),
        in_specs=[
            pl.BlockSpec((FFN_TM, D_MODEL), lambda i, c: (i, 0)),
            pl.BlockSpec((1, D_MODEL), lambda i, c: (0, 0)),
            pl.BlockSpec((None, D_MODEL, FFN_TF), lambda i, c: (layer, 0, c)),
            pl.BlockSpec((None, FFN_TF, D_MODEL), lambda i, c: (layer, c, 0)),
            pl.BlockSpec((1, D_MODEL), lambda i, c: (0, 0)),
        ],
        out_specs=pl.BlockSpec((FFN_TM, D_MODEL), lambda i, c: (i, 0)),
        out_shape=jax.ShapeDtypeStruct((m, D_MODEL), F32),
        scratch_shapes=[pltpu.VMEM((FFN_TM, D_MODEL), BF16)],
        compiler_params=pltpu.CompilerParams(
            dimension_semantics=("parallel", "arbitrary"), vmem_limit_bytes=VMEM_LIMIT),
    )(x2d, gain, wu, wd, final_gain)


def kernel(x, mix_norm, w_in, w_a, w_b, w_o, ffn_norm, w_up, w_down, final_norm):
    batch, seq, _ = x.shape
    depth = w_in.shape[0]
    assert seq % A_SPAN == 0 and seq % RET_TB == 0
    tables = _rotary_tables(seq)
    final_gain = final_norm.reshape(1, D_MODEL)
    x2d = x.reshape(batch * seq, D_MODEL)
    w_in, w_a, w_b, w_o, w_up, w_down = (_arrange_w_in(w_in),) + tuple(
        w.astype(BF16) for w in (w_a, w_b, w_o, w_up, w_down))
    for layer in range(depth):
        proj = _in_proj(x2d, mix_norm[layer].reshape(1, D_MODEL), w_in, layer, tables, seq)
        ya = _attention_a(proj, batch, seq).reshape(batch * seq, A_OUT)
        rb = _retention(proj, batch, seq).reshape(batch * seq, R_V)
        x2d = _merge(x2d, ya, rb, proj, w_a, w_b, w_o, layer)
        x2d = _ffn(x2d, ffn_norm[layer].reshape(1, D_MODEL), w_up, w_down, layer, final_gain,
                   final=(layer == depth - 1))
    return x2d.reshape(batch, seq, D_MODEL)
```

```python
import functools

import numpy as np
import jax
import jax.numpy as jnp
from jax import lax
from jax.experimental import pallas as pl
from jax.experimental.pallas import tpu as pltpu

F32 = jnp.float32
BF16 = jnp.bfloat16

D_MODEL = 1024
N_GROUPS = 3
DILATIONS = (1, 4, 16)
A_HEADS = 8
A_HEAD_DIM = 64
A_GROUP_W = A_HEADS * A_HEAD_DIM
A_WIDTH = N_GROUPS * A_GROUP_W
A_OUT = A_GROUP_W
ROT_DIM = A_HEAD_DIM // 4
ROPE_THETA = 500000.0
ATT_BLOCK = 128
NEG_INF = -1e30
R_HEADS = 4
R_KEY_DIM = 128
R_VAL_DIM = 256
R_QK = R_HEADS * R_KEY_DIM
R_V = R_HEADS * R_VAL_DIM
R_CHUNK = 128
R_ROT_BASE = 10000.0
D_FF = 4 * D_MODEL
NORM_EPS = 1e-6

OFF_QA = 0
OFF_KA = A_WIDTH
OFF_VA = 2 * A_WIDTH
OFF_QR = 3 * A_WIDTH
OFF_KR = OFF_QR + R_QK
OFF_VR = OFF_KR + R_QK
OFF_GR = OFF_VR + R_V
OFF_GA = OFF_GR + R_V
OFF_GB = OFF_GA + D_MODEL

LANES = 128
VMEM_LIMIT = 56 * 1024 * 1024

A_SPAN = ATT_BLOCK * max(DILATIONS)
A_UNITS = A_SPAN // ATT_BLOCK

IN_TM = A_SPAN
IN_TN = 512
IN_CHUNKS = (512, 512, 512, 256, 256)
X_CHUNKS = D_MODEL // LANES

P_GA = 0
P_GB = P_GA + D_MODEL
P_VR = P_GB + D_MODEL
P_GR = P_VR + R_V
P_QR = P_GR + R_V
P_KR = P_QR + R_QK
P_QKV = tuple(P_KR + R_QK + g * 3 * A_GROUP_W for g in range(N_GROUPS))
P_WIDTH = P_QKV[-1] + 3 * A_GROUP_W

_TAB_R = 2 * N_GROUPS
_TAB_NONE = 2 * N_GROUPS + 1
TILE_TABLE = ((_TAB_NONE,) * (P_QR // IN_TN) + (_TAB_R,) * (2 * R_QK // IN_TN)
              + tuple(t for g in range(N_GROUPS) for t in (g, N_GROUPS + g, _TAB_NONE)))
LOG2_E = 1.4426950408889634
A_Q_SCALE = A_HEAD_DIM ** -0.5 * LOG2_E
TILE_ORDER = (0,) * (P_QKV[0] // IN_TN) + tuple(g for g in range(N_GROUPS) for _ in range(3))


def _rms(x, gain):
    ms = jnp.mean(x * x, axis=-1, keepdims=True)
    return x * lax.rsqrt(ms + NORM_EPS) * gain


def _step_lookup(j, values):
    out = values[-1]
    for t in range(len(values) - 2, -1, -1):
        if values[t] != values[t + 1]:
            out = jnp.where(j <= t, values[t], out)
    return out


def _residue_pieces(dil, src_dil, u):
    ratio = dil // src_dil
    piece = ATT_BLOCK // ratio
    ss, r = divmod(u, dil)
    for k in range(ratio):
        src_block = (ss * ratio + k) * src_dil + r % src_dil
        yield (pl.ds(src_block * ATT_BLOCK + r // src_dil, piece, stride=ratio),
               pl.ds(u * ATT_BLOCK + k * piece, piece))


def _in_proj_body(*refs):
    x_refs = refs[:X_CHUNKS]
    g_ref, w_ref, cos_ref, sin_ref, o_ref, h_ref, inv_ref, stage_ref = refs[X_CHUNKS:]
    j = pl.program_id(1)

    @pl.when(j == 0)
    def _norm():
        gains = [g_ref[:, c * LANES:(c + 1) * LANES] for c in range(X_CHUNKS)]
        for u in range(A_UNITS):
            rows = pl.ds(u * ATT_BLOCK, ATT_BLOCK)
            xc = [x_refs[c][rows, :] for c in range(X_CHUNKS)]
            sq = xc[0] * xc[0]
            for c in range(1, X_CHUNKS):
                sq = sq + xc[c] * xc[c]
            inv = lax.rsqrt(jnp.sum(sq, axis=-1, keepdims=True) * (1.0 / D_MODEL) + NORM_EPS)
            inv = jnp.broadcast_to(inv, (ATT_BLOCK, LANES))
            inv_ref[rows, :] = inv
            for c in range(X_CHUNKS):
                h_ref[0, rows, c * LANES:(c + 1) * LANES] = (xc[c] * inv * gains[c]).astype(BF16)
        for g in range(1, N_GROUPS):
            for u in range(A_UNITS):
                for src, dst in _residue_pieces(DILATIONS[g], DILATIONS[g - 1], u):
                    if g == 1:
                        inv = inv_ref[src, :]
                    for c in range(X_CHUNKS):
                        lanes = slice(c * LANES, (c + 1) * LANES)
                        if g == 1:
                            y = x_refs[c][src, :] * inv * gains[c]
                            if g + 1 < N_GROUPS:
                                stage_ref[c, dst, :] = y
                        else:
                            y = stage_ref[c, src, :]
                        h_ref[g, dst, lanes] = y.astype(BF16)

    def project(rotary, order):
        starts = np.cumsum((0,) + IN_CHUNKS)
        for start, size in zip(starts, IN_CHUNKS):
            rows = pl.ds(int(start), size)
            acc = jnp.dot(h_ref[order, rows, :], w_ref[...], preferred_element_type=F32)
            if not rotary:
                o_ref[rows, :] = acc.astype(BF16)
                continue
            for c in range(IN_TN // LANES):
                t = acc[:, c * LANES:(c + 1) * LANES]
                o_ref[rows, c * LANES:(c + 1) * LANES] = (
                    t * cos_ref[rows, :] + pltpu.roll(t, LANES // 2, 1) * sin_ref[rows, :]).astype(BF16)

    order = _step_lookup(j, TILE_ORDER)
    has_rotary = _step_lookup(j, tuple(int(t != _TAB_NONE) for t in TILE_TABLE)) == 1
    pl.when(has_rotary)(functools.partial(project, True, order))
    pl.when(jnp.logical_not(has_rotary))(functools.partial(project, False, order))


def _pair_layout(t):
    half = ROT_DIM // 2
    a, b = t[..., 0, :], t[..., 1, :]
    return jnp.concatenate([a[..., :half], b[..., :half], a[..., ROT_DIM:],
                            a[..., half:ROT_DIM], b[..., half:ROT_DIM], b[..., ROT_DIM:]], axis=-1)


def _arrange_w_in(w):
    lead = w.shape[:-1]
    w = w.astype(BF16)

    def pairs(cols):
        return _pair_layout(cols.reshape(lead + (N_GROUPS, A_GROUP_W // LANES, 2, A_HEAD_DIM))).reshape(
            lead + (N_GROUPS, A_GROUP_W))

    qa = pairs(w[..., OFF_QA:OFF_KA])
    ka = pairs(w[..., OFF_KA:OFF_VA])
    va = w[..., OFF_VA:OFF_QR].reshape(lead + (N_GROUPS, A_GROUP_W))
    qkv = jnp.concatenate([qa, ka, va], axis=-1).reshape(lead + (3 * A_WIDTH,))
    return jnp.concatenate([w[..., OFF_GA:], w[..., OFF_VR:OFF_GA], w[..., OFF_QR:OFF_VR], qkv], axis=-1)


@functools.lru_cache(maxsize=None)
def _rotary_tables(seq):
    f32 = np.float32
    half = ROT_DIM // 2
    inv = (1.0 / (f32(ROPE_THETA) ** (np.arange(half, dtype=f32) / f32(half)))).astype(f32)
    rest = A_HEAD_DIM - ROT_DIM
    ones, zeros = np.ones((seq, rest), f32), np.zeros((seq, rest), f32)
    cos_tabs, sin_tabs = [], []
    for dil in DILATIONS:
        span = ATT_BLOCK * dil
        pos = np.arange(seq, dtype=f32).reshape(seq // span, ATT_BLOCK, dil).transpose(0, 2, 1).reshape(seq, 1)
        ang = pos * inv[None, :]
        cos, sin = np.cos(ang), np.sin(ang)
        cos_tabs.append(np.concatenate([cos, cos, ones, cos, cos, ones], axis=1))
        sin_tabs.append(np.concatenate([-sin, -sin, zeros, sin, sin, zeros], axis=1))
    cos_tabs = [t * f32(A_Q_SCALE) for t in cos_tabs] + cos_tabs
    sin_tabs = [t * f32(A_Q_SCALE) for t in sin_tabs] + sin_tabs
    half_r = R_KEY_DIM // 2
    inv_r = (1.0 / (f32(R_ROT_BASE) ** (np.arange(half_r, dtype=f32) / f32(half_r)))).astype(f32)
    ang_r = np.arange(seq, dtype=f32)[:, None] * inv_r[None, :]
    cos_tabs += [np.concatenate([np.cos(ang_r), np.cos(ang_r)], axis=1), np.ones((seq, LANES), f32)]
    sin_tabs += [np.concatenate([-np.sin(ang_r), np.sin(ang_r)], axis=1), np.zeros((seq, LANES), f32)]
    return np.stack(cos_tabs).astype(f32), np.stack(sin_tabs).astype(f32)


def _in_proj(x2d, gain, w_bf16, layer, tables, seq):
    m = x2d.shape[0]
    tiles_per_seq = seq // IN_TM
    tab_spec = pl.BlockSpec((None, IN_TM, LANES),
                            lambda i, j: (_step_lookup(j, TILE_TABLE), i % tiles_per_seq, 0))
    x_specs = [pl.BlockSpec((IN_TM, LANES), lambda i, j, c=c: (i, c)) for c in range(X_CHUNKS)]
    return pl.pallas_call(
        _in_proj_body,
        grid=(m // IN_TM, P_WIDTH // IN_TN),
        in_specs=x_specs + [
            pl.BlockSpec((1, D_MODEL), lambda i, j: (0, 0)),
            pl.BlockSpec((None, D_MODEL, IN_TN), lambda i, j: (layer, 0, j)),
            tab_spec, tab_spec,
        ],
        out_specs=pl.BlockSpec((IN_TM, IN_TN), lambda i, j: (i, j)),
        out_shape=jax.ShapeDtypeStruct((m, P_WIDTH), BF16),
        scratch_shapes=[pltpu.VMEM((N_GROUPS, IN_TM, D_MODEL), BF16),
                        pltpu.VMEM((IN_TM, LANES), F32),
                        pltpu.VMEM((X_CHUNKS, IN_TM, LANES), F32)],
        compiler_params=pltpu.CompilerParams(
            dimension_semantics=("parallel", "arbitrary"), vmem_limit_bytes=VMEM_LIMIT),
    )(*([x2d] * X_CHUNKS), gain, w_bf16, *tables)


A_HEADS_PER_STEP = LANES // A_HEAD_DIM
A_QUAD = 4


def _attn_body(*refs):
    q_in = refs[0:3]
    kc_in = refs[3:6]
    kp_in = refs[6:9]
    vc_in = refs[9:12]
    vp_in = refs[12:15]
    o_ref = refs[15]
    ro, rl, bias, s_buf, p_buf, m_buf = refs[16:22]
    kbuf, vbuf = refs[22:25], refs[25:28]
    so_buf, sl_buf, sy_buf = refs[28:31]
    jt = pl.program_id(1)

    qi = lax.broadcasted_iota(jnp.int32, (ATT_BLOCK, 2 * ATT_BLOCK), 0)
    kj = lax.broadcasted_iota(jnp.int32, (ATT_BLOCK, 2 * ATT_BLOCK), 1)
    dist = qi + ATT_BLOCK - kj
    band = (dist >= 0) & (dist <= ATT_BLOCK)
    bias[0] = jnp.where(band & (kj >= ATT_BLOCK), 0.0, NEG_INF).astype(BF16)
    bias[1] = jnp.where(band, 0.0, NEG_INF).astype(BF16)
    lane = lax.broadcasted_iota(jnp.int32, (ATT_BLOCK, LANES), 1)

    half = ROT_DIM // 2
    first_head = (lane < half) | ((lane >= ROT_DIM) & (lane < A_HEAD_DIM + half))
    head_lanes = (first_head, jnp.logical_not(first_head))
    first = lane < A_HEAD_DIM
    v_lane = lax.broadcasted_iota(jnp.int32, (2 * ATT_BLOCK, LANES), 1)
    v_lanes = (v_lane < A_HEAD_DIM, v_lane >= A_HEAD_DIM)

    for g, d in enumerate(DILATIONS):
        span = ATT_BLOCK * d
        kbuf[g][0:span] = kp_in[g][...]
        kbuf[g][span:] = kc_in[g][...]
        vbuf[g][0:span] = vp_in[g][...]
        vbuf[g][span:] = vc_in[g][...]

    def rows_of(u):
        return pl.ds(u * ATT_BLOCK if isinstance(u, int) else pl.multiple_of(u * ATT_BLOCK, ATT_BLOCK), ATT_BLOCK)

    def stage_scores(u, g, d):
        b = bias[jnp.where(jnp.logical_or(jt > 0, u >= d), 1, 0)]
        q = q_in[g][rows_of(u), :]
        k2 = jnp.concatenate([kbuf[g][rows_of(u), :], kbuf[g][rows_of(u + d), :]], axis=0)
        for h in range(A_HEADS_PER_STEP):
            qm = jnp.where(head_lanes[h], q, jnp.zeros_like(q))
            s = lax.dot_general(qm, k2, (((1,), (1,)), ((), ())), preferred_element_type=F32)
            s_buf[u, h] = s.astype(BF16) + b

    def stage_softmax(u):
        ms = []
        for h in range(A_HEADS_PER_STEP):
            s = s_buf[u, h]
            m = jnp.max(s, axis=1, keepdims=True)
            p_buf[u, h] = jnp.exp2(s - m)
            ms.append(m.astype(F32))
        m_buf[u] = jnp.where(first, ms[0], ms[1])

    def stage_out(u, g, d):
        v2 = jnp.concatenate([vbuf[g][rows_of(u), :], vbuf[g][rows_of(u + d), :]], axis=0)
        pvs = [jnp.dot(p_buf[u, h], jnp.where(v_lanes[h], v2, jnp.ones_like(v2)), preferred_element_type=F32)
               for h in range(A_HEADS_PER_STEP)]
        out = jnp.where(first, pvs[0], pvs[1])
        den = pltpu.roll(jnp.where(first, pvs[1], pvs[0]), A_HEAD_DIM, 1)
        ro[g, rows_of(u), :] = out / den
        rl[g, rows_of(u), :] = m_buf[u] + jnp.log(den) * LOG2_E

    def quad(stage, qd, *args):
        for i in range(A_QUAD):
            stage(qd * A_QUAD + i, *args)

    n_quads = A_UNITS // A_QUAD
    for g, d in enumerate(DILATIONS):
        quad(stage_scores, 0, g, d)
        quad(stage_softmax, 0)
        quad(stage_scores, 1, g, d)

        def trip(qd, carry, g=g, d=d):
            quad(stage_out, qd, g, d)
            quad(stage_softmax, qd + 1)
            quad(stage_scores, qd + 2, g, d)
            return carry

        lax.fori_loop(0, n_quads - 2, trip, 0)
        quad(stage_out, n_quads - 2, g, d)
        quad(stage_softmax, n_quads - 1)
        quad(stage_out, n_quads - 1, g, d)

    d2, d3 = DILATIONS[1], DILATIONS[2]
    for u in range(A_UNITS):
        for src, dst in _residue_pieces(d2, DILATIONS[0], u):
            so_buf[dst, :] = ro[0, src, :]
            sl_buf[dst, :] = rl[0, src, :]
    for r3 in range(A_UNITS):
        blk = pl.ds(r3 * ATT_BLOCK, ATT_BLOCK)
        pieces = list(_residue_pieces(d3, d2, r3))
        o1 = jnp.concatenate([so_buf[src, :] for src, _ in pieces], axis=0)
        l1 = jnp.concatenate([sl_buf[src, :] for src, _ in pieces], axis=0)
        o2 = jnp.concatenate([ro[1, src, :] for src, _ in pieces], axis=0)
        l2 = jnp.concatenate([rl[1, src, :] for src, _ in pieces], axis=0)
        o3, l3 = ro[2, blk, :], rl[2, blk, :]
        mx = jnp.maximum(jnp.maximum(l1, l2), l3)
        e1, e2, e3 = jnp.exp2(l1 - mx), jnp.exp2(l2 - mx), jnp.exp2(l3 - mx)
        y = (e1 * o1 + e2 * o2 + e3 * o3) / (e1 + e2 + e3)
        for k, (src, _) in enumerate(pieces):
            sy_buf[src, :] = y[k * ATT_BLOCK // len(pieces):(k + 1) * ATT_BLOCK // len(pieces), :]
    for u in range(A_UNITS):
        for src, dst in _residue_pieces(d2, DILATIONS[0], u):
            o_ref[src, :] = sy_buf[dst, :]


def _attention_a(proj, batch, seq):
    p = proj.reshape(batch, seq, P_WIDTH)

    def cur_spec(g, part):
        col = (P_QKV[g] + part * A_GROUP_W) // LANES
        return pl.BlockSpec((None, A_SPAN, LANES), lambda b, j, hp: (b, j, col + hp))

    def prev_spec(g, part):
        col = (P_QKV[g] + part * A_GROUP_W) // LANES
        span = ATT_BLOCK * DILATIONS[g]
        per = A_SPAN // span
        return pl.BlockSpec((None, span, LANES),
                            lambda b, j, hp: (b, jnp.maximum(j * per - 1, 0), col + hp))

    groups = range(N_GROUPS)
    kv_bufs = [pltpu.VMEM((ATT_BLOCK * d + A_SPAN, LANES), BF16) for d in DILATIONS]
    in_specs = ([cur_spec(g, 0) for g in groups] + [cur_spec(g, 1) for g in groups]
                + [prev_spec(g, 1) for g in groups] + [cur_spec(g, 2) for g in groups]
                + [prev_spec(g, 2) for g in groups])
    return pl.pallas_call(
        _attn_body,
        grid=(batch, seq // A_SPAN, A_GROUP_W // LANES),
        in_specs=in_specs,
        out_specs=pl.BlockSpec((None, A_SPAN, LANES), lambda b, j, hp: (b, j, hp)),
        out_shape=jax.ShapeDtypeStruct((batch, seq, A_OUT), F32),
        scratch_shapes=[pltpu.VMEM((N_GROUPS, A_SPAN, LANES), F32),
                        pltpu.VMEM((N_GROUPS, A_SPAN, LANES), F32),
                        pltpu.VMEM((2, ATT_BLOCK, 2 * ATT_BLOCK), BF16),
                        pltpu.VMEM((A_UNITS, A_HEADS_PER_STEP, ATT_BLOCK, 2 * ATT_BLOCK), BF16),
                        pltpu.VMEM((A_UNITS, A_HEADS_PER_STEP, ATT_BLOCK, 2 * ATT_BLOCK), BF16),
                        pltpu.VMEM((A_UNITS, ATT_BLOCK, LANES), F32)] + kv_bufs + kv_bufs
                       + [pltpu.VMEM((A_SPAN, LANES), F32)] * 3,
        compiler_params=pltpu.CompilerParams(
            dimension_semantics=("parallel", "parallel", "parallel"), vmem_limit_bytes=VMEM_LIMIT),
    )(*([p] * 15))


RET_TB = 2048
_LOG_GAMMA = [float(v) for v in np.log1p(-(2.0 ** (-5.0 - np.arange(R_HEADS)))).astype(np.float32)]


def _ret_body(q_ref, k_ref, v_ref, g_ref, o_ref, st_ref):
    h = pl.program_id(1)
    t = pl.program_id(2)

    @pl.when(t == 0)
    def _reset():
        st_ref[...] = jnp.zeros_like(st_ref)

    lg = jnp.float32(_LOG_GAMMA[R_HEADS - 1])
    for hh in range(R_HEADS - 2, -1, -1):
        lg = jnp.where(h == hh, jnp.float32(_LOG_GAMMA[hh]), lg)
    row = lax.broadcasted_iota(jnp.int32, (R_CHUNK, R_CHUNK), 0).astype(F32)
    colv = lax.broadcasted_iota(jnp.int32, (R_CHUNK, R_CHUNK), 1).astype(F32)
    diff = row - colv
    decay = jnp.where(diff >= 0, jnp.exp(diff * lg), 0.0)
    zeta = jnp.exp((R_CHUNK - 1 - row) * lg)
    xi = jnp.exp((row + 1.0) * lg)
    chunk_decay = jnp.exp(jnp.full((R_KEY_DIM, R_VAL_DIM), float(R_CHUNK), F32) * lg)
    kscale = R_KEY_DIM ** -0.5

    for c in range(RET_TB // R_CHUNK):
        sl = slice(c * R_CHUNK, (c + 1) * R_CHUNK)
        q = q_ref[sl, :].astype(F32)
        k = k_ref[sl, :].astype(F32) * kscale
        v = v_ref[sl, :]
        s = lax.dot_general(q.astype(BF16), k.astype(BF16), (((1,), (1,)), ((), ())),
                            preferred_element_type=F32) * decay
        inner = jnp.dot(s.astype(BF16), v, preferred_element_type=F32)
        state = st_ref[...]
        cross = jnp.dot((q * xi).astype(BF16), state.astype(BF16), preferred_element_type=F32)
        kz_t = (k * zeta).T.astype(BF16)
        st_ref[...] = state * chunk_decay + jnp.dot(kz_t, v, preferred_element_type=F32)
        out = inner + cross
        ms = jnp.mean(out * out, axis=-1, keepdims=True)
        gate = g_ref[sl, :].astype(F32)
        o_ref[sl, :] = (gate * jax.nn.sigmoid(gate) * (out * lax.rsqrt(ms + NORM_EPS))).astype(BF16)


def _retention(proj, batch, seq):
    p = proj.reshape(batch, seq, P_WIDTH)
    return pl.pallas_call(
        _ret_body,
        grid=(batch, R_HEADS, seq // RET_TB),
        in_specs=[
            pl.BlockSpec((None, RET_TB, R_KEY_DIM), lambda b, h, t: (b, t, P_QR // R_KEY_DIM + h)),
            pl.BlockSpec((None, RET_TB, R_KEY_DIM), lambda b, h, t: (b, t, P_KR // R_KEY_DIM + h)),
            pl.BlockSpec((None, RET_TB, R_VAL_DIM), lambda b, h, t: (b, t, P_VR // R_VAL_DIM + h)),
            pl.BlockSpec((None, RET_TB, R_VAL_DIM), lambda b, h, t: (b, t, P_GR // R_VAL_DIM + h)),
        ],
        out_specs=pl.BlockSpec((None, RET_TB, R_VAL_DIM), lambda b, h, t: (b, t, h)),
        out_shape=jax.ShapeDtypeStruct((batch, seq, R_V), BF16),
        scratch_shapes=[pltpu.VMEM((R_KEY_DIM, R_VAL_DIM), F32)],
        compiler_params=pltpu.CompilerParams(
            dimension_semantics=("parallel", "parallel", "arbitrary"), vmem_limit_bytes=VMEM_LIMIT),
    )(p, p, p, p)


MERGE_TM = 1024


def _merge_body(x_ref, ya_ref, rb_ref, ga_ref, gb_ref, wa_ref, wb_ref, wo_ref, o_ref):
    ya = jnp.dot(ya_ref[...].astype(BF16), wa_ref[...], preferred_element_type=F32)
    yb = jnp.dot(rb_ref[...], wb_ref[...], preferred_element_type=F32)
    ga = ga_ref[...].astype(F32)
    gb = gb_ref[...].astype(F32)
    merged = jax.nn.sigmoid(ga) * ya + jax.nn.sigmoid(gb) * yb
    o_ref[...] = x_ref[...] + jnp.dot(merged.astype(BF16), wo_ref[...], preferred_element_type=F32)


def _merge(x2d, ya2d, rb2d, proj, wa, wb, wo, layer):
    m = x2d.shape[0]

    def full(shape):
        return pl.BlockSpec((None,) + shape, lambda i: (layer, 0, 0))

    return pl.pallas_call(
        _merge_body,
        grid=(m // MERGE_TM,),
        in_specs=[
            pl.BlockSpec((MERGE_TM, D_MODEL), lambda i: (i, 0)),
            pl.BlockSpec((MERGE_TM, A_OUT), lambda i: (i, 0)),
            pl.BlockSpec((MERGE_TM, R_V), lambda i: (i, 0)),
            pl.BlockSpec((MERGE_TM, D_MODEL), lambda i: (i, P_GA // D_MODEL)),
            pl.BlockSpec((MERGE_TM, D_MODEL), lambda i: (i, P_GB // D_MODEL)),
            full((A_OUT, D_MODEL)), full((R_V, D_MODEL)), full((D_MODEL, D_MODEL)),
        ],
        out_specs=pl.BlockSpec((MERGE_TM, D_MODEL), lambda i: (i, 0)),
        out_shape=jax.ShapeDtypeStruct((m, D_MODEL), F32),
        compiler_params=pltpu.CompilerParams(
            dimension_semantics=("parallel",), vmem_limit_bytes=VMEM_LIMIT),
    )(x2d, ya2d, rb2d, proj, proj, wa, wb, wo)


FFN_TM = 1024
FFN_TF = 2048


def _ffn_body(x_ref, g_ref, wu_ref, wd_ref, fg_ref, o_ref, h_ref, *, final):
    c = pl.program_id(1)

    @pl.when(c == 0)
    def _start():
        x = x_ref[...]
        h_ref[...] = _rms(x, g_ref[...]).astype(BF16)
        o_ref[...] = x

    up = jnp.dot(h_ref[...], wu_ref[...], preferred_element_type=F32)
    up = jnp.square(jnp.maximum(up, 0.0)).astype(BF16)
    o_ref[...] += jnp.dot(up, wd_ref[...], preferred_element_type=F32)

    if final:
        @pl.when(c == pl.num_programs(1) - 1)
        def _final_norm():
            o_ref[...] = _rms(o_ref[...], fg_ref[...])


def _ffn(x2d, gain, wu, wd, layer, final_gain, final):
    m = x2d.shape[0]
    return pl.pallas_call(
        functools.partial(_ffn_body, final=final),
        grid=(m // FFN_TM, D_FF // FFN_TF),
        in_specs=[
            pl.BlockSpec((FFN_TM, D_MODEL), lambda i, c: (i, 0)),
            pl.BlockSpec((1, D_MODEL), lambda i, c: (0, 0)),
            pl.BlockSpec((None, D_MODEL, FFN_TF), lambda i, c: (layer, 0, c)),
            pl.BlockSpec((None, FFN_TF, D_MODEL), lambda i, c: (layer, c, 0)),
            pl.BlockSpec((1, D_MODEL), lambda i, c: (0, 0)),
        ],
        out_specs=pl.BlockSpec((FFN_TM, D_MODEL), lambda i, c: (i, 0)),
        out_shape=jax.ShapeDtypeStruct((m, D_MODEL), F32),
        scratch_shapes=[pltpu.VMEM((FFN_TM, D_MODEL), BF16)],
        compiler_params=pltpu.CompilerParams(
            dimension_semantics=("parallel", "arbitrary"), vmem_limit_bytes=VMEM_LIMIT),
    )(x2d, gain, wu, wd, final_gain)


def kernel(x, mix_norm, w_in, w_a, w_b, w_o, ffn_norm, w_up, w_down, final_norm):
    batch, seq, _ = x.shape
    depth = w_in.shape[0]
    assert seq % A_SPAN == 0 and seq % RET_TB == 0
    tables = _rotary_tables(seq)
    final_gain = final_norm.reshape(1, D_MODEL)
    x2d = x.reshape(batch * seq, D_MODEL)
    w_in, w_a, w_b, w_o, w_up, w_down = (_arrange_w_in(w_in),) + tuple(
        w.astype(BF16) for w in (w_a, w_b, w_o, w_up, w_down))
    for layer in range(depth):
        proj = _in_proj(x2d, mix_norm[layer].reshape(1, D_MODEL), w_in, layer, tables, seq)
        ya = _attention_a(proj, batch, seq).reshape(batch * seq, A_OUT)
        rb = _retention(proj, batch, seq).reshape(batch * seq, R_V)
        x2d = _merge(x2d, ya, rb, proj, w_a, w_b, w_o, layer)
        x2d = _ffn(x2d, ffn_norm[layer].reshape(1, D_MODEL), w_up, w_down, layer, final_gain,
                   final=(layer == depth - 1))
    return x2d.reshape(batch, seq, D_MODEL)
```

```python
import functools

import numpy as np
import jax
import jax.numpy as jnp
from jax import lax
from jax.experimental import pallas as pl
from jax.experimental.pallas import tpu as pltpu

F32 = jnp.float32
BF16 = jnp.bfloat16

D_MODEL = 1024
N_GROUPS = 3
DILATIONS = (1, 4, 16)
A_HEADS = 8
A_HEAD_DIM = 64
A_GROUP_W = A_HEADS * A_HEAD_DIM
A_WIDTH = N_GROUPS * A_GROUP_W
A_OUT = A_GROUP_W
ROT_DIM = A_HEAD_DIM // 4
ROPE_THETA = 500000.0
ATT_BLOCK = 128
NEG_INF = -1e30
R_HEADS = 4
R_KEY_DIM = 128
R_VAL_DIM = 256
R_QK = R_HEADS * R_KEY_DIM
R_V = R_HEADS * R_VAL_DIM
R_CHUNK = 128
R_ROT_BASE = 10000.0
D_FF = 4 * D_MODEL
NORM_EPS = 1e-6

OFF_QA = 0
OFF_KA = A_WIDTH
OFF_VA = 2 * A_WIDTH
OFF_QR = 3 * A_WIDTH
OFF_KR = OFF_QR + R_QK
OFF_VR = OFF_KR + R_QK
OFF_GR = OFF_VR + R_V
OFF_GA = OFF_GR + R_V
OFF_GB = OFF_GA + D_MODEL

LANES = 128
VMEM_LIMIT = 56 * 1024 * 1024

A_SPAN = ATT_BLOCK * max(DILATIONS)
A_UNITS = A_SPAN // ATT_BLOCK

IN_TM = A_SPAN
IN_TN = 512
IN_CHUNKS = (512, 512, 512, 256, 256)
X_CHUNKS = D_MODEL // LANES

P_GA = 0
P_GB = P_GA + D_MODEL
P_VR = P_GB + D_MODEL
P_GR = P_VR + R_V
P_QR = P_GR + R_V
P_KR = P_QR + R_QK
P_QKV = tuple(P_KR + R_QK + g * 3 * A_GROUP_W for g in range(N_GROUPS))
P_WIDTH = P_QKV[-1] + 3 * A_GROUP_W

_TAB_R = 2 * N_GROUPS
_TAB_NONE = 2 * N_GROUPS + 1
TILE_TABLE = ((_TAB_NONE,) * (P_QR // IN_TN) + (_TAB_R,) * (2 * R_QK // IN_TN)
              + tuple(t for g in range(N_GROUPS) for t in (g, N_GROUPS + g, _TAB_NONE)))
LOG2_E = 1.4426950408889634
A_Q_SCALE = A_HEAD_DIM ** -0.5 * LOG2_E
TILE_ORDER = (0,) * (P_QKV[0] // IN_TN) + tuple(g for g in range(N_GROUPS) for _ in range(3))
N_TILES = P_WIDTH // IN_TN


def _rms(x, gain):
    ms = jnp.mean(x * x, axis=-1, keepdims=True)
    return x * lax.rsqrt(ms + NORM_EPS) * gain


def _step_lookup(j, values):
    out = values[-1]
    for t in range(len(values) - 2, -1, -1):
        if values[t] != values[t + 1]:
            out = jnp.where(j <= t, values[t], out)
    return out


def _residue_pieces(dil, src_dil, u):
    ratio = dil // src_dil
    piece = ATT_BLOCK // ratio
    ss, r = divmod(u, dil)
    for k in range(ratio):
        src_block = (ss * ratio + k) * src_dil + r % src_dil
        yield (pl.ds(src_block * ATT_BLOCK + r // src_dil, piece, stride=ratio),
               pl.ds(u * ATT_BLOCK + k * piece, piece))


def _in_proj_body(*refs):
    x_refs = refs[:X_CHUNKS]
    g_ref, w_ref, cos_ref, sin_ref, o_ref, h_ref, inv_ref, stage_ref = refs[X_CHUNKS:]
    j = pl.program_id(1)

    @pl.when(j == 0)
    def _norm():
        gains = [g_ref[:, c * LANES:(c + 1) * LANES] for c in range(X_CHUNKS)]
        for u in range(A_UNITS):
            rows = pl.ds(u * ATT_BLOCK, ATT_BLOCK)
            xc = [x_refs[c][rows, :] for c in range(X_CHUNKS)]
            sq = xc[0] * xc[0]
            for c in range(1, X_CHUNKS):
                sq = sq + xc[c] * xc[c]
            inv = lax.rsqrt(jnp.sum(sq, axis=-1, keepdims=True) * (1.0 / D_MODEL) + NORM_EPS)
            inv = jnp.broadcast_to(inv, (ATT_BLOCK, LANES))
            inv_ref[rows, :] = inv
            for c in range(X_CHUNKS):
                h_ref[0, rows, c * LANES:(c + 1) * LANES] = (xc[c] * inv * gains[c]).astype(BF16)
        for g in range(1, N_GROUPS):
            for u in range(A_UNITS):
                for src, dst in _residue_pieces(DILATIONS[g], DILATIONS[g - 1], u):
                    if g == 1:
                        inv = inv_ref[src, :]
                    for c in range(X_CHUNKS):
                        lanes = slice(c * LANES, (c + 1) * LANES)
                        if g == 1:
                            y = x_refs[c][src, :] * inv * gains[c]
                            if g + 1 < N_GROUPS:
                                stage_ref[c, dst, :] = y
                        else:
                            y = stage_ref[c, src, :]
                        h_ref[g, dst, lanes] = y.astype(BF16)

    def project(rotary, order):
        starts = np.cumsum((0,) + IN_CHUNKS)
        for start, size in zip(starts, IN_CHUNKS):
            rows = pl.ds(int(start), size)
            acc = jnp.dot(h_ref[order, rows, :], w_ref[...], preferred_element_type=F32)
            if not rotary:
                o_ref[rows, :] = acc.astype(BF16)
                continue
            for c in range(IN_TN // LANES):
                t = acc[:, c * LANES:(c + 1) * LANES]
                o_ref[rows, c * LANES:(c + 1) * LANES] = (
                    t * cos_ref[rows, :] + pltpu.roll(t, LANES // 2, 1) * sin_ref[rows, :]).astype(BF16)

    order = _step_lookup(j, TILE_ORDER)
    has_rotary = _step_lookup(j, tuple(int(t != _TAB_NONE) for t in TILE_TABLE)) == 1
    pl.when(has_rotary)(functools.partial(project, True, order))
    pl.when(jnp.logical_not(has_rotary))(functools.partial(project, False, order))


_SRC_TILE = tuple(off // IN_TN + t for off, n in ((OFF_GA, 2), (OFF_GB, 2), (OFF_VR, 2), (OFF_GR, 2), (OFF_QR, 1),
                                                   (OFF_KR, 1)) for t in range(n)) + tuple(
    (off + g * A_GROUP_W) // IN_TN for g in range(N_GROUPS) for off in (OFF_QA, OFF_KA, OFF_VA))
_PAIR_TILE = tuple(int(t < 2 * N_GROUPS) for t in TILE_TABLE)


def _arrange_body(w_ref, o_ref):
    j = pl.program_id(1)
    pair = _step_lookup(j, _PAIR_TILE) == 1

    @pl.when(jnp.logical_not(pair))
    def _plain():
        o_ref[...] = w_ref[...].astype(BF16)

    @pl.when(pair)
    def _pair():
        half = ROT_DIM // 2
        lane = lax.broadcasted_iota(jnp.int32, (D_MODEL, LANES), 1)
        to_b_x1 = (lane >= half) & (lane < ROT_DIM)
        to_a_x2 = (lane >= A_HEAD_DIM) & (lane < A_HEAD_DIM + half)
        for c in range(IN_TN // LANES):
            t = w_ref[:, c * LANES:(c + 1) * LANES]
            moved = jnp.where(to_b_x1, pltpu.roll(t, LANES - A_HEAD_DIM + half, 1),
                              pltpu.roll(t, A_HEAD_DIM - half, 1))
            o_ref[:, c * LANES:(c + 1) * LANES] = jnp.where(to_b_x1 | to_a_x2, moved, t).astype(BF16)


def _arrange_w_in(w):
    depth = w.shape[0]
    return pl.pallas_call(
        _arrange_body,
        grid=(depth, N_TILES),
        in_specs=[pl.BlockSpec((None, D_MODEL, IN_TN), lambda l, j: (l, 0, _step_lookup(j, _SRC_TILE)))],
        out_specs=pl.BlockSpec((None, D_MODEL, IN_TN), lambda l, j: (l, 0, j)),
        out_shape=jax.ShapeDtypeStruct((depth, D_MODEL, P_WIDTH), BF16),
        compiler_params=pltpu.CompilerParams(
            dimension_semantics=("parallel", "parallel"), vmem_limit_bytes=VMEM_LIMIT),
    )(w)


@functools.lru_cache(maxsize=None)
def _rotary_tables(seq):
    f32 = np.float32
    half = ROT_DIM // 2
    inv = (1.0 / (f32(ROPE_THETA) ** (np.arange(half, dtype=f32) / f32(half)))).astype(f32)
    rest = A_HEAD_DIM - ROT_DIM
    ones, zeros = np.ones((seq, rest), f32), np.zeros((seq, rest), f32)
    cos_tabs, sin_tabs = [], []
    for dil in DILATIONS:
        span = ATT_BLOCK * dil
        pos = np.arange(seq, dtype=f32).reshape(seq // span, ATT_BLOCK, dil).transpose(0, 2, 1).reshape(seq, 1)
        ang = pos * inv[None, :]
        cos, sin = np.cos(ang), np.sin(ang)
        cos_tabs.append(np.concatenate([cos, cos, ones, cos, cos, ones], axis=1))
        sin_tabs.append(np.concatenate([-sin, -sin, zeros, sin, sin, zeros], axis=1))
    cos_tabs = [t * f32(A_Q_SCALE) for t in cos_tabs] + cos_tabs
    sin_tabs = [t * f32(A_Q_SCALE) for t in sin_tabs] + sin_tabs
    half_r = R_KEY_DIM // 2
    inv_r = (1.0 / (f32(R_ROT_BASE) ** (np.arange(half_r, dtype=f32) / f32(half_r)))).astype(f32)
    ang_r = np.arange(seq, dtype=f32)[:, None] * inv_r[None, :]
    cos_tabs += [np.concatenate([np.cos(ang_r), np.cos(ang_r)], axis=1), np.ones((seq, LANES), f32)]
    sin_tabs += [np.concatenate([-np.sin(ang_r), np.sin(ang_r)], axis=1), np.zeros((seq, LANES), f32)]
    return np.stack(cos_tabs).astype(f32), np.stack(sin_tabs).astype(f32)


def _in_proj(x2d, gain, w_bf16, layer, tables, seq):
    m = x2d.shape[0]
    tiles_per_seq = seq // IN_TM
    tab_spec = pl.BlockSpec((None, IN_TM, LANES),
                            lambda i, j: (_step_lookup(j, TILE_TABLE), i % tiles_per_seq, 0))
    x_specs = [pl.BlockSpec((IN_TM, LANES), lambda i, j, c=c: (i, c)) for c in range(X_CHUNKS)]
    return pl.pallas_call(
        _in_proj_body,
        grid=(m // IN_TM, P_WIDTH // IN_TN),
        in_specs=x_specs + [
            pl.BlockSpec((1, D_MODEL), lambda i, j: (0, 0)),
            pl.BlockSpec((None, D_MODEL, IN_TN), lambda i, j: (layer, 0, j)),
            tab_spec, tab_spec,
        ],
        out_specs=pl.BlockSpec((IN_TM, IN_TN), lambda i, j: (i, j)),
        out_shape=jax.ShapeDtypeStruct((m, P_WIDTH), BF16),
        scratch_shapes=[pltpu.VMEM((N_GROUPS, IN_TM, D_MODEL), BF16),
                        pltpu.VMEM((IN_TM, LANES), F32),
                        pltpu.VMEM((X_CHUNKS, IN_TM, LANES), F32)],
        compiler_params=pltpu.CompilerParams(
            dimension_semantics=("parallel", "arbitrary"), vmem_limit_bytes=VMEM_LIMIT),
    )(*([x2d] * X_CHUNKS), gain, w_bf16, *tables)


A_HEADS_PER_STEP = LANES // A_HEAD_DIM
A_QUAD = 4


def _attn_body(*refs):
    q_in = refs[0:3]
    kc_in = refs[3:6]
    kp_in = refs[6:9]
    vc_in = refs[9:12]
    vp_in = refs[12:15]
    o_ref = refs[15]
    ro, rl, bias, s_buf, p_buf, m_buf = refs[16:22]
    kbuf, vbuf = refs[22:25], refs[25:28]
    so_buf, sl_buf, sy_buf = refs[28:31]
    jt = pl.program_id(1)

    qi = lax.broadcasted_iota(jnp.int32, (ATT_BLOCK, 2 * ATT_BLOCK), 0)
    kj = lax.broadcasted_iota(jnp.int32, (ATT_BLOCK, 2 * ATT_BLOCK), 1)
    dist = qi + ATT_BLOCK - kj
    band = (dist >= 0) & (dist <= ATT_BLOCK)
    bias[0] = jnp.where(band & (kj >= ATT_BLOCK), 0.0, NEG_INF).astype(BF16)
    bias[1] = jnp.where(band, 0.0, NEG_INF).astype(BF16)
    lane = lax.broadcasted_iota(jnp.int32, (ATT_BLOCK, LANES), 1)

    half = ROT_DIM // 2
    first_head = (lane < half) | ((lane >= ROT_DIM) & (lane < A_HEAD_DIM + half))
    head_lanes = (first_head, jnp.logical_not(first_head))
    first = lane < A_HEAD_DIM
    v_lane = lax.broadcasted_iota(jnp.int32, (2 * ATT_BLOCK, LANES), 1)
    v_lanes = (v_lane < A_HEAD_DIM, v_lane >= A_HEAD_DIM)

    for g, d in enumerate(DILATIONS):
        span = ATT_BLOCK * d
        kbuf[g][0:span] = kp_in[g][...]
        kbuf[g][span:] = kc_in[g][...]
        vbuf[g][0:span] = vp_in[g][...]
        vbuf[g][span:] = vc_in[g][...]

    def rows_of(u):
        return pl.ds(u * ATT_BLOCK if isinstance(u, int) else pl.multiple_of(u * ATT_BLOCK, ATT_BLOCK), ATT_BLOCK)

    def stage_scores(u, g, d):
        b = bias[jnp.where(jnp.logical_or(jt > 0, u >= d), 1, 0)]
        q = q_in[g][rows_of(u), :]
        k2 = jnp.concatenate([kbuf[g][rows_of(u), :], kbuf[g][rows_of(u + d), :]], axis=0)
        for h in range(A_HEADS_PER_STEP):
            qm = jnp.where(head_lanes[h], q, jnp.zeros_like(q))
            s = lax.dot_general(qm, k2, (((1,), (1,)), ((), ())), preferred_element_type=F32)
            s_buf[u, h] = s.astype(BF16) + b

    def stage_softmax(u):
        ms = []
        for h in range(A_HEADS_PER_STEP):
            s = s_buf[u, h]
            m = jnp.max(s, axis=1, keepdims=True)
            p_buf[u, h] = jnp.exp2(s - m)
            ms.append(m.astype(F32))
        m_buf[u] = jnp.where(first, ms[0], ms[1])

    def stage_out(u, g, d):
        v2 = jnp.concatenate([vbuf[g][rows_of(u), :], vbuf[g][rows_of(u + d), :]], axis=0)
        pvs = [jnp.dot(p_buf[u, h], jnp.where(v_lanes[h], v2, jnp.ones_like(v2)), preferred_element_type=F32)
               for h in range(A_HEADS_PER_STEP)]
        out = jnp.where(first, pvs[0], pvs[1])
        den = pltpu.roll(jnp.where(first, pvs[1], pvs[0]), A_HEAD_DIM, 1)
        ro[g, rows_of(u), :] = out / den
        rl[g, rows_of(u), :] = m_buf[u] + jnp.log(den) * LOG2_E

    def quad(stage, qd, *args):
        for i in range(A_QUAD):
            stage(qd * A_QUAD + i, *args)

    n_quads = A_UNITS // A_QUAD
    for g, d in enumerate(DILATIONS):
        quad(stage_scores, 0, g, d)
        quad(stage_softmax, 0)
        quad(stage_scores, 1, g, d)

        def trip(qd, carry, g=g, d=d):
            quad(stage_out, qd, g, d)
            quad(stage_softmax, qd + 1)
            quad(stage_scores, qd + 2, g, d)
            return carry

        lax.fori_loop(0, n_quads - 2, trip, 0)
        quad(stage_out, n_quads - 2, g, d)
        quad(stage_softmax, n_quads - 1)
        quad(stage_out, n_quads - 1, g, d)

    d2, d3 = DILATIONS[1], DILATIONS[2]
    for u in range(A_UNITS):
        for src, dst in _residue_pieces(d2, DILATIONS[0], u):
            so_buf[dst, :] = ro[0, src, :]
            sl_buf[dst, :] = rl[0, src, :]
    for r3 in range(A_UNITS):
        blk = pl.ds(r3 * ATT_BLOCK, ATT_BLOCK)
        pieces = list(_residue_pieces(d3, d2, r3))
        o1 = jnp.concatenate([so_buf[src, :] for src, _ in pieces], axis=0)
        l1 = jnp.concatenate([sl_buf[src, :] for src, _ in pieces], axis=0)
        o2 = jnp.concatenate([ro[1, src, :] for src, _ in pieces], axis=0)
        l2 = jnp.concatenate([rl[1, src, :] for src, _ in pieces], axis=0)
        o3, l3 = ro[2, blk, :], rl[2, blk, :]
        mx = jnp.maximum(jnp.maximum(l1, l2), l3)
        e1, e2, e3 = jnp.exp2(l1 - mx), jnp.exp2(l2 - mx), jnp.exp2(l3 - mx)
        y = (e1 * o1 + e2 * o2 + e3 * o3) / (e1 + e2 + e3)
        for k, (src, _) in enumerate(pieces):
            sy_buf[src, :] = y[k * ATT_BLOCK // len(pieces):(k + 1) * ATT_BLOCK // len(pieces), :]
    for u in range(A_UNITS):
        for src, dst in _residue_pieces(d2, DILATIONS[0], u):
            o_ref[src, :] = sy_buf[dst, :]


def _attention_a(proj, batch, seq):
    p = proj.reshape(batch, seq, P_WIDTH)

    def cur_spec(g, part):
        col = (P_QKV[g] + part * A_GROUP_W) // LANES
        return pl.BlockSpec((None, A_SPAN, LANES), lambda b, j, hp: (b, j, col + hp))

    def prev_spec(g, part):
        col = (P_QKV[g] + part * A_GROUP_W) // LANES
        span = ATT_BLOCK * DILATIONS[g]
        per = A_SPAN // span
        return pl.BlockSpec((None, span, LANES),
                            lambda b, j, hp: (b, jnp.maximum(j * per - 1, 0), col + hp))

    groups = range(N_GROUPS)
    kv_bufs = [pltpu.VMEM((ATT_BLOCK * d + A_SPAN, LANES), BF16) for d in DILATIONS]
    in_specs = ([cur_spec(g, 0) for g in groups] + [cur_spec(g, 1) for g in groups]
                + [prev_spec(g, 1) for g in groups] + [cur_spec(g, 2) for g in groups]
                + [prev_spec(g, 2) for g in groups])
    return pl.pallas_call(
        _attn_body,
        grid=(batch, seq // A_SPAN, A_GROUP_W // LANES),
        in_specs=in_specs,
        out_specs=pl.BlockSpec((None, A_SPAN, LANES), lambda b, j, hp: (b, j, hp)),
        out_shape=jax.ShapeDtypeStruct((batch, seq, A_OUT), F32),
        scratch_shapes=[pltpu.VMEM((N_GROUPS, A_SPAN, LANES), F32),
                        pltpu.VMEM((N_GROUPS, A_SPAN, LANES), F32),
                        pltpu.VMEM((2, ATT_BLOCK, 2 * ATT_BLOCK), BF16),
                        pltpu.VMEM((A_UNITS, A_HEADS_PER_STEP, ATT_BLOCK, 2 * ATT_BLOCK), BF16),
                        pltpu.VMEM((A_UNITS, A_HEADS_PER_STEP, ATT_BLOCK, 2 * ATT_BLOCK), BF16),
                        pltpu.VMEM((A_UNITS, ATT_BLOCK, LANES), F32)] + kv_bufs + kv_bufs
                       + [pltpu.VMEM((A_SPAN, LANES), F32)] * 3,
        compiler_params=pltpu.CompilerParams(
            dimension_semantics=("parallel", "parallel", "parallel"), vmem_limit_bytes=VMEM_LIMIT),
    )(*([p] * 15))


RET_TB = 2048
_LOG_GAMMA = [float(v) for v in np.log1p(-(2.0 ** (-5.0 - np.arange(R_HEADS)))).astype(np.float32)]


def _ret_body(q_ref, k_ref, v_ref, g_ref, o_ref, st_ref):
    h = pl.program_id(1)
    t = pl.program_id(2)

    @pl.when(t == 0)
    def _reset():
        st_ref[...] = jnp.zeros_like(st_ref)

    lg = jnp.float32(_LOG_GAMMA[R_HEADS - 1])
    for hh in range(R_HEADS - 2, -1, -1):
        lg = jnp.where(h == hh, jnp.float32(_LOG_GAMMA[hh]), lg)
    row = lax.broadcasted_iota(jnp.int32, (R_CHUNK, R_CHUNK), 0).astype(F32)
    colv = lax.broadcasted_iota(jnp.int32, (R_CHUNK, R_CHUNK), 1).astype(F32)
    diff = row - colv
    decay = jnp.where(diff >= 0, jnp.exp(diff * lg), 0.0)
    zeta = jnp.exp((R_CHUNK - 1 - row) * lg)
    xi = jnp.exp((row + 1.0) * lg)
    chunk_decay = jnp.exp(jnp.full((R_KEY_DIM, R_VAL_DIM), float(R_CHUNK), F32) * lg)
    kscale = R_KEY_DIM ** -0.5

    for c in range(RET_TB // R_CHUNK):
        sl = slice(c * R_CHUNK, (c + 1) * R_CHUNK)
        q = q_ref[sl, :].astype(F32)
        k = k_ref[sl, :].astype(F32) * kscale
        v = v_ref[sl, :]
        s = lax.dot_general(q.astype(BF16), k.astype(BF16), (((1,), (1,)), ((), ())),
                            preferred_element_type=F32) * decay
        inner = jnp.dot(s.astype(BF16), v, preferred_element_type=F32)
        state = st_ref[...]
        cross = jnp.dot((q * xi).astype(BF16), state.astype(BF16), preferred_element_type=F32)
        kz_t = (k * zeta).T.astype(BF16)
        st_ref[...] = state * chunk_decay + jnp.dot(kz_t, v, preferred_element_type=F32)
        out = inner + cross
        ms = jnp.mean(out * out, axis=-1, keepdims=True)
        gate = g_ref[sl, :].astype(F32)
        o_ref[sl, :] = (gate * jax.nn.sigmoid(gate) * (out * lax.rsqrt(ms + NORM_EPS))).astype(BF16)


def _retention(proj, batch, seq):
    p = proj.reshape(batch, seq, P_WIDTH)
    return pl.pallas_call(
        _ret_body,
        grid=(batch, R_HEADS, seq // RET_TB),
        in_specs=[
            pl.BlockSpec((None, RET_TB, R_KEY_DIM), lambda b, h, t: (b, t, P_QR // R_KEY_DIM + h)),
            pl.BlockSpec((None, RET_TB, R_KEY_DIM), lambda b, h, t: (b, t, P_KR // R_KEY_DIM + h)),
            pl.BlockSpec((None, RET_TB, R_VAL_DIM), lambda b, h, t: (b, t, P_VR // R_VAL_DIM + h)),
            pl.BlockSpec((None, RET_TB, R_VAL_DIM), lambda b, h, t: (b, t, P_GR // R_VAL_DIM + h)),
        ],
        out_specs=pl.BlockSpec((None, RET_TB, R_VAL_DIM), lambda b, h, t: (b, t, h)),
        out_shape=jax.ShapeDtypeStruct((batch, seq, R_V), BF16),
        scratch_shapes=[pltpu.VMEM((R_KEY_DIM, R_VAL_DIM), F32)],
        compiler_params=pltpu.CompilerParams(
            dimension_semantics=("parallel", "parallel", "arbitrary"), vmem_limit_bytes=VMEM_LIMIT),
    )(p, p, p, p)


MERGE_TM = 1024


def _merge_body(x_ref, ya_ref, rb_ref, ga_ref, gb_ref, wa_ref, wb_ref, wo_ref, o_ref):
    ya = jnp.dot(ya_ref[...].astype(BF16), wa_ref[...], preferred_element_type=F32)
    yb = jnp.dot(rb_ref[...], wb_ref[...], preferred_element_type=F32)
    ga = ga_ref[...].astype(F32)
    gb = gb_ref[...].astype(F32)
    merged = jax.nn.sigmoid(ga) * ya + jax.nn.sigmoid(gb) * yb
    o_ref[...] = x_ref[...] + jnp.dot(merged.astype(BF16), wo_ref[...], preferred_element_type=F32)


def _merge(x2d, ya2d, rb2d, proj, wa, wb, wo, layer):
    m = x2d.shape[0]

    def full(shape):
        return pl.BlockSpec((None,) + shape, lambda i: (layer, 0, 0))

    return pl.pallas_call(
        _merge_body,
        grid=(m // MERGE_TM,),
        in_specs=[
            pl.BlockSpec((MERGE_TM, D_MODEL), lambda i: (i, 0)),
            pl.BlockSpec((MERGE_TM, A_OUT), lambda i: (i, 0)),
            pl.BlockSpec((MERGE_TM, R_V), lambda i: (i, 0)),
            pl.BlockSpec((MERGE_TM, D_MODEL), lambda i: (i, P_GA // D_MODEL)),
            pl.BlockSpec((MERGE_TM, D_MODEL), lambda i: (i, P_GB // D_MODEL)),
            full((A_OUT, D_MODEL)), full((R_V, D_MODEL)), full((D_MODEL, D_MODEL)),
        ],
        out_specs=pl.BlockSpec((MERGE_TM, D_MODEL), lambda i: (i, 0)),
        out_shape=jax.ShapeDtypeStruct((m, D_MODEL), F32),
        compiler_params=pltpu.CompilerParams(
            dimension_semantics=("parallel",), vmem_limit_bytes=VMEM_LIMIT),
    )(x2d, ya2d, rb2d, proj, proj, wa, wb, wo)


FFN_TM = 1024
FFN_TF = 2048


def _ffn_body(x_ref, g_ref, wu_ref, wd_ref, fg_ref, o_ref, h_ref, *, final):
    c = pl.program_id(1)

    @pl.when(c == 0)
    def _start():
        x = x_ref[...]
        h_ref[...] = _rms(x, g_ref[...]).astype(BF16)
        o_ref[...] = x

    up = jnp.dot(h_ref[...], wu_ref[...], preferred_element_type=F32)
    up = jnp.square(jnp.maximum(up, 0.0)).astype(BF16)
    o_ref[...] += jnp.dot(up, wd_ref[...], preferred_element_type=F32)

    if final:
        @pl.when(c == pl.num_programs(1) - 1)
        def _final_norm():
            o_ref[...] = _rms(o_ref[...], fg_ref[...])


def _ffn(x2d, gain, wu, wd, layer, final_gain, final):
    m = x2d.shape[0]
    return pl.pallas_call(
        functools.partial(_ffn_body, final=final),
        grid=(m // FFN_TM, D_FF // FFN_TF),
        in_specs=[
            pl.BlockSpec((FFN_TM, D_MODEL), lambda i, c: (i, 0)),
            pl.BlockSpec((1, D_MODEL), lambda i, c: (0, 0)),
            pl.BlockSpec((None, D_MODEL, FFN_TF), lambda i, c: (layer, 0, c)),
            pl.BlockSpec((None, FFN_TF, D_MODEL), lambda i, c: (layer, c, 0)),
            pl.BlockSpec((1, D_MODEL), lambda i, c: (0, 0)),
        ],
        out_specs=pl.BlockSpec((FFN_TM, D_MODEL), lambda i, c: (i, 0)),
        out_shape=jax.ShapeDtypeStruct((m, D_MODEL), F32),
        scratch_shapes=[pltpu.VMEM((FFN_TM, D_MODEL), BF16)],
        compiler_params=pltpu.CompilerParams(
            dimension_semantics=("parallel", "arbitrary"), vmem_limit_bytes=VMEM_LIMIT),
    )(x2d, gain, wu, wd, final_gain)


def kernel(x, mix_norm, w_in, w_a, w_b, w_o, ffn_norm, w_up, w_down, final_norm):
    batch, seq, _ = x.shape
    depth = w_in.shape[0]
    assert seq % A_SPAN == 0 and seq % RET_TB == 0
    tables = _rotary_tables(seq)
    final_gain = final_norm.reshape(1, D_MODEL)
    x2d = x.reshape(batch * seq, D_MODEL)
    w_in, w_a, w_b, w_o, w_up, w_down = (_arrange_w_in(w_in),) + tuple(
        w.astype(BF16) for w in (w_a, w_b, w_o, w_up, w_down))
    for layer in range(depth):
        proj = _in_proj(x2d, mix_norm[layer].reshape(1, D_MODEL), w_in, layer, tables, seq)
        ya = _attention_a(proj, batch, seq).reshape(batch * seq, A_OUT)
        rb = _retention(proj, batch, seq).reshape(batch * seq, R_V)
        x2d = _merge(x2d, ya, rb, proj, w_a, w_b, w_o, layer)
        x2d = _ffn(x2d, ffn_norm[layer].reshape(1, D_MODEL), w_up, w_down, layer, final_gain,
                   final=(layer == depth - 1))
    return x2d.reshape(batch, seq, D_MODEL)
```

```python
import functools

import numpy as np
import jax
import jax.numpy as jnp
from jax import lax
from jax.experimental import pallas as pl
from jax.experimental.pallas import tpu as pltpu

F32 = jnp.float32
BF16 = jnp.bfloat16

D_MODEL = 1024
N_GROUPS = 3
DILATIONS = (1, 4, 16)
A_HEADS = 8
A_HEAD_DIM = 64
A_GROUP_W = A_HEADS * A_HEAD_DIM
A_WIDTH = N_GROUPS * A_GROUP_W
A_OUT = A_GROUP_W
ROT_DIM = A_HEAD_DIM // 4
ROPE_THETA = 500000.0
ATT_BLOCK = 128
NEG_INF = -1e30
R_HEADS = 4
R_KEY_DIM = 128
R_VAL_DIM = 256
R_QK = R_HEADS * R_KEY_DIM
R_V = R_HEADS * R_VAL_DIM
R_CHUNK = 128
R_ROT_BASE = 10000.0
D_FF = 4 * D_MODEL
NORM_EPS = 1e-6

OFF_QA = 0
OFF_KA = A_WIDTH
OFF_VA = 2 * A_WIDTH
OFF_QR = 3 * A_WIDTH
OFF_KR = OFF_QR + R_QK
OFF_VR = OFF_KR + R_QK
OFF_GR = OFF_VR + R_V
OFF_GA = OFF_GR + R_V
OFF_GB = OFF_GA + D_MODEL

LANES = 128
VMEM_LIMIT = 56 * 1024 * 1024

A_SPAN = ATT_BLOCK * max(DILATIONS)
A_UNITS = A_SPAN // ATT_BLOCK

IN_TM = A_SPAN
IN_TN = 512
IN_CHUNKS = (512, 512, 512, 256, 256)
X_CHUNKS = D_MODEL // LANES

P_GA = 0
P_GB = P_GA + D_MODEL
P_VR = P_GB + D_MODEL
P_GR = P_VR + R_V
P_QR = P_GR + R_V
P_KR = P_QR + R_QK
P_QKV = tuple(P_KR + R_QK + g * 3 * A_GROUP_W for g in range(N_GROUPS))
P_WIDTH = P_QKV[-1] + 3 * A_GROUP_W

_TAB_R = 2 * N_GROUPS
_TAB_NONE = 2 * N_GROUPS + 1
TILE_TABLE = ((_TAB_NONE,) * (P_QR // IN_TN) + (_TAB_R,) * (2 * R_QK // IN_TN)
              + tuple(t for g in range(N_GROUPS) for t in (g, N_GROUPS + g, _TAB_NONE)))
LOG2_E = 1.4426950408889634
A_Q_SCALE = A_HEAD_DIM ** -0.5 * LOG2_E
TILE_ORDER = (0,) * (P_QKV[0] // IN_TN) + tuple(g for g in range(N_GROUPS) for _ in range(3))
N_TILES = P_WIDTH // IN_TN


def _rms(x, gain):
    ms = jnp.mean(x * x, axis=-1, keepdims=True)
    return x * lax.rsqrt(ms + NORM_EPS) * gain


def _step_lookup(j, values):
    out = values[-1]
    for t in range(len(values) - 2, -1, -1):
        if values[t] != values[t + 1]:
            out = jnp.where(j <= t, values[t], out)
    return out


def _residue_pieces(dil, src_dil, u):
    ratio = dil // src_dil
    piece = ATT_BLOCK // ratio
    ss, r = divmod(u, dil)
    for k in range(ratio):
        src_block = (ss * ratio + k) * src_dil + r % src_dil
        yield (pl.ds(src_block * ATT_BLOCK + r // src_dil, piece, stride=ratio),
               pl.ds(u * ATT_BLOCK + k * piece, piece))


def _in_proj_body(*refs):
    x_refs = refs[:X_CHUNKS]
    g_ref, w_ref, cos_ref, sin_ref, o_ref, h_ref, inv_ref, stage_ref = refs[X_CHUNKS:]
    j = pl.program_id(1)

    @pl.when(j == 0)
    def _norm():
        gains = [g_ref[:, c * LANES:(c + 1) * LANES] for c in range(X_CHUNKS)]
        for u in range(A_UNITS):
            rows = pl.ds(u * ATT_BLOCK, ATT_BLOCK)
            xc = [x_refs[c][rows, :] for c in range(X_CHUNKS)]
            sq = xc[0] * xc[0]
            for c in range(1, X_CHUNKS):
                sq = sq + xc[c] * xc[c]
            inv = lax.rsqrt(jnp.sum(sq, axis=-1, keepdims=True) * (1.0 / D_MODEL) + NORM_EPS)
            inv = jnp.broadcast_to(inv, (ATT_BLOCK, LANES))
            inv_ref[rows, :] = inv
            for c in range(X_CHUNKS):
                h_ref[0, rows, c * LANES:(c + 1) * LANES] = (xc[c] * inv * gains[c]).astype(BF16)
        for g in range(1, N_GROUPS):
            for u in range(A_UNITS):
                for src, dst in _residue_pieces(DILATIONS[g], DILATIONS[g - 1], u):
                    if g == 1:
                        inv = inv_ref[src, :]
                    for c in range(X_CHUNKS):
                        lanes = slice(c * LANES, (c + 1) * LANES)
                        if g == 1:
                            y = x_refs[c][src, :] * inv * gains[c]
                            if g + 1 < N_GROUPS:
                                stage_ref[c, dst, :] = y
                        else:
                            y = stage_ref[c, src, :]
                        h_ref[g, dst, lanes] = y.astype(BF16)

    def project(rotary, order):
        starts = np.cumsum((0,) + IN_CHUNKS)
        for start, size in zip(starts, IN_CHUNKS):
            rows = pl.ds(int(start), size)
            acc = jnp.dot(h_ref[order, rows, :], w_ref[...], preferred_element_type=F32)
            if not rotary:
                o_ref[rows, :] = acc.astype(BF16)
                continue
            for c in range(IN_TN // LANES):
                t = acc[:, c * LANES:(c + 1) * LANES]
                o_ref[rows, c * LANES:(c + 1) * LANES] = (
                    t * cos_ref[rows, :] + pltpu.roll(t, LANES // 2, 1) * sin_ref[rows, :]).astype(BF16)

    order = _step_lookup(j, TILE_ORDER)
    has_rotary = _step_lookup(j, tuple(int(t != _TAB_NONE) for t in TILE_TABLE)) == 1
    pl.when(has_rotary)(functools.partial(project, True, order))
    pl.when(jnp.logical_not(has_rotary))(functools.partial(project, False, order))


_SRC_TILE = tuple(off // IN_TN + t for off, n in ((OFF_GA, 2), (OFF_GB, 2), (OFF_VR, 2), (OFF_GR, 2), (OFF_QR, 1),
                                                   (OFF_KR, 1)) for t in range(n)) + tuple(
    (off + g * A_GROUP_W) // IN_TN for g in range(N_GROUPS) for off in (OFF_QA, OFF_KA, OFF_VA))
_PAIR_TILE = tuple(int(t < 2 * N_GROUPS) for t in TILE_TABLE)


def _arrange_body(w_ref, o_ref):
    j = pl.program_id(1)
    pair = _step_lookup(j, _PAIR_TILE) == 1

    @pl.when(jnp.logical_not(pair))
    def _plain():
        o_ref[...] = w_ref[...].astype(BF16)

    @pl.when(pair)
    def _pair():
        half = ROT_DIM // 2
        lane = lax.broadcasted_iota(jnp.int32, (D_MODEL, LANES), 1)
        to_b_x1 = (lane >= half) & (lane < ROT_DIM)
        to_a_x2 = (lane >= A_HEAD_DIM) & (lane < A_HEAD_DIM + half)
        for c in range(IN_TN // LANES):
            t = w_ref[:, c * LANES:(c + 1) * LANES]
            moved = jnp.where(to_b_x1, pltpu.roll(t, LANES - A_HEAD_DIM + half, 1),
                              pltpu.roll(t, A_HEAD_DIM - half, 1))
            o_ref[:, c * LANES:(c + 1) * LANES] = jnp.where(to_b_x1 | to_a_x2, moved, t).astype(BF16)


def _arrange_w_in(w):
    depth = w.shape[0]
    return pl.pallas_call(
        _arrange_body,
        grid=(depth, N_TILES),
        in_specs=[pl.BlockSpec((None, D_MODEL, IN_TN), lambda l, j: (l, 0, _step_lookup(j, _SRC_TILE)))],
        out_specs=pl.BlockSpec((None, D_MODEL, IN_TN), lambda l, j: (l, 0, j)),
        out_shape=jax.ShapeDtypeStruct((depth, D_MODEL, P_WIDTH), BF16),
        compiler_params=pltpu.CompilerParams(
            dimension_semantics=("parallel", "parallel"), vmem_limit_bytes=VMEM_LIMIT),
    )(w)


@functools.lru_cache(maxsize=None)
def _rotary_tables(seq):
    f32 = np.float32
    half = ROT_DIM // 2
    inv = (1.0 / (f32(ROPE_THETA) ** (np.arange(half, dtype=f32) / f32(half)))).astype(f32)
    rest = A_HEAD_DIM - ROT_DIM
    ones, zeros = np.ones((seq, rest), f32), np.zeros((seq, rest), f32)
    cos_tabs, sin_tabs = [], []
    for dil in DILATIONS:
        span = ATT_BLOCK * dil
        pos = np.arange(seq, dtype=f32).reshape(seq // span, ATT_BLOCK, dil).transpose(0, 2, 1).reshape(seq, 1)
        ang = pos * inv[None, :]
        cos, sin = np.cos(ang), np.sin(ang)
        cos_tabs.append(np.concatenate([cos, cos, ones, cos, cos, ones], axis=1))
        sin_tabs.append(np.concatenate([-sin, -sin, zeros, sin, sin, zeros], axis=1))
    cos_tabs = [t * f32(A_Q_SCALE) for t in cos_tabs] + cos_tabs
    sin_tabs = [t * f32(A_Q_SCALE) for t in sin_tabs] + sin_tabs
    half_r = R_KEY_DIM // 2
    inv_r = (1.0 / (f32(R_ROT_BASE) ** (np.arange(half_r, dtype=f32) / f32(half_r)))).astype(f32)
    ang_r = np.arange(seq, dtype=f32)[:, None] * inv_r[None, :]
    cos_tabs += [np.concatenate([np.cos(ang_r), np.cos(ang_r)], axis=1), np.ones((seq, LANES), f32)]
    sin_tabs += [np.concatenate([-np.sin(ang_r), np.sin(ang_r)], axis=1), np.zeros((seq, LANES), f32)]
    return np.stack(cos_tabs).astype(f32), np.stack(sin_tabs).astype(f32)


def _in_proj(x2d, gain, w_bf16, layer, tables, seq):
    m = x2d.shape[0]
    tiles_per_seq = seq // IN_TM
    tab_spec = pl.BlockSpec((None, IN_TM, LANES),
                            lambda i, j: (_step_lookup(j, TILE_TABLE), i % tiles_per_seq, 0))
    x_specs = [pl.BlockSpec((IN_TM, LANES), lambda i, j, c=c: (i, c)) for c in range(X_CHUNKS)]
    return pl.pallas_call(
        _in_proj_body,
        grid=(m // IN_TM, P_WIDTH // IN_TN),
        in_specs=x_specs + [
            pl.BlockSpec((1, D_MODEL), lambda i, j: (0, 0)),
            pl.BlockSpec((None, D_MODEL, IN_TN), lambda i, j: (layer, 0, j)),
            tab_spec, tab_spec,
        ],
        out_specs=pl.BlockSpec((IN_TM, IN_TN), lambda i, j: (i, j)),
        out_shape=jax.ShapeDtypeStruct((m, P_WIDTH), BF16),
        scratch_shapes=[pltpu.VMEM((N_GROUPS, IN_TM, D_MODEL), BF16),
                        pltpu.VMEM((IN_TM, LANES), F32),
                        pltpu.VMEM((X_CHUNKS, IN_TM, LANES), F32)],
        compiler_params=pltpu.CompilerParams(
            dimension_semantics=("parallel", "arbitrary"), vmem_limit_bytes=VMEM_LIMIT),
    )(*([x2d] * X_CHUNKS), gain, w_bf16, *tables)


A_HEADS_PER_STEP = LANES // A_HEAD_DIM
A_QUAD = 8


def _attn_body(*refs):
    q_in = refs[0:3]
    kc_in = refs[3:6]
    kp_in = refs[6:9]
    vc_in = refs[9:12]
    vp_in = refs[12:15]
    o_ref = refs[15]
    ro, rl, bias, s_buf, p_buf, m_buf = refs[16:22]
    so_buf, sl_buf, sy_buf = refs[22:25]
    jt = pl.program_id(1)

    qi = lax.broadcasted_iota(jnp.int32, (ATT_BLOCK, 2 * ATT_BLOCK), 0)
    kj = lax.broadcasted_iota(jnp.int32, (ATT_BLOCK, 2 * ATT_BLOCK), 1)
    dist = qi + ATT_BLOCK - kj
    band = (dist >= 0) & (dist <= ATT_BLOCK)
    bias[0] = jnp.where(band & (kj >= ATT_BLOCK), 0.0, NEG_INF).astype(BF16)
    bias[1] = jnp.where(band, 0.0, NEG_INF).astype(BF16)
    lane = lax.broadcasted_iota(jnp.int32, (ATT_BLOCK, LANES), 1)

    half = ROT_DIM // 2
    first_head = (lane < half) | ((lane >= ROT_DIM) & (lane < A_HEAD_DIM + half))
    head_lanes = (first_head, jnp.logical_not(first_head))
    first = lane < A_HEAD_DIM
    v_lane = lax.broadcasted_iota(jnp.int32, (2 * ATT_BLOCK, LANES), 1)
    v_lanes = (v_lane < A_HEAD_DIM, v_lane >= A_HEAD_DIM)

    def rows_of(u):
        return pl.ds(u * ATT_BLOCK, ATT_BLOCK)

    def previous(cur, prev, d, u):
        return prev[rows_of(u), :] if u < d else cur[rows_of(u - d), :]

    def stage_scores(slot, g, u):
        d = DILATIONS[g]
        b = bias[jnp.minimum(jt, 1)] if u < d else bias[1]
        q = q_in[g][rows_of(u), :]
        k2 = jnp.concatenate([previous(kc_in[g], kp_in[g], d, u), kc_in[g][rows_of(u), :]], axis=0)
        for h in range(A_HEADS_PER_STEP):
            qm = jnp.where(head_lanes[h], q, jnp.zeros_like(q))
            s = lax.dot_general(qm, k2, (((1,), (1,)), ((), ())), preferred_element_type=F32)
            s_buf[slot, h] = s.astype(BF16) + b

    def stage_softmax(slot, g, u):
        ms = []
        for h in range(A_HEADS_PER_STEP):
            s = s_buf[slot, h]
            m = jnp.max(s, axis=1, keepdims=True)
            p_buf[slot, h] = jnp.exp2(s - m)
            ms.append(m.astype(F32))
        m_buf[slot] = jnp.where(first, ms[0], ms[1])

    def stage_out(slot, g, u):
        d = DILATIONS[g]
        v2 = jnp.concatenate([previous(vc_in[g], vp_in[g], d, u), vc_in[g][rows_of(u), :]], axis=0)
        pvs = [jnp.dot(p_buf[slot, h], jnp.where(v_lanes[h], v2, jnp.ones_like(v2)), preferred_element_type=F32)
               for h in range(A_HEADS_PER_STEP)]
        out = jnp.where(first, pvs[0], pvs[1])
        den = pltpu.roll(jnp.where(first, pvs[1], pvs[0]), A_HEAD_DIM, 1)
        ro[g, rows_of(u), :] = out / den
        rl[g, rows_of(u), :] = m_buf[slot] + jnp.log(den) * LOG2_E

    quads = [(g, qd) for g in range(N_GROUPS) for qd in range(A_UNITS // A_QUAD)]
    separate_block = pl.program_id(0) >= 0
    stages = (stage_scores, stage_softmax, stage_out)
    for trip in range(len(quads) + len(stages) - 1):

        @pl.when(separate_block)
        def _trip(trip=trip):
            for age in range(len(stages) - 1, -1, -1):
                n = trip - age
                if 0 <= n < len(quads):
                    g, qd = quads[n]
                    for i in range(A_QUAD):
                        stages[age]((n % 2) * A_QUAD + i, g, qd * A_QUAD + i)

    d2, d3 = DILATIONS[1], DILATIONS[2]
    for u in range(A_UNITS):
        for src, dst in _residue_pieces(d2, DILATIONS[0], u):
            so_buf[dst, :] = ro[0, src, :]
            sl_buf[dst, :] = rl[0, src, :]
    for r3 in range(A_UNITS):
        blk = pl.ds(r3 * ATT_BLOCK, ATT_BLOCK)
        pieces = list(_residue_pieces(d3, d2, r3))
        o1 = jnp.concatenate([so_buf[src, :] for src, _ in pieces], axis=0)
        l1 = jnp.concatenate([sl_buf[src, :] for src, _ in pieces], axis=0)
        o2 = jnp.concatenate([ro[1, src, :] for src, _ in pieces], axis=0)
        l2 = jnp.concatenate([rl[1, src, :] for src, _ in pieces], axis=0)
        o3, l3 = ro[2, blk, :], rl[2, blk, :]
        mx = jnp.maximum(jnp.maximum(l1, l2), l3)
        e1, e2, e3 = jnp.exp2(l1 - mx), jnp.exp2(l2 - mx), jnp.exp2(l3 - mx)
        y = (e1 * o1 + e2 * o2 + e3 * o3) / (e1 + e2 + e3)
        for k, (src, _) in enumerate(pieces):
            sy_buf[src, :] = y[k * ATT_BLOCK // len(pieces):(k + 1) * ATT_BLOCK // len(pieces), :]
    for u in range(A_UNITS):
        for src, dst in _residue_pieces(d2, DILATIONS[0], u):
            o_ref[src, :] = sy_buf[dst, :]


def _attention_a(proj, batch, seq):
    p = proj.reshape(batch, seq, P_WIDTH)

    def cur_spec(g, part):
        col = (P_QKV[g] + part * A_GROUP_W) // LANES
        return pl.BlockSpec((None, A_SPAN, LANES), lambda b, j, hp: (b, j, col + hp))

    def prev_spec(g, part):
        col = (P_QKV[g] + part * A_GROUP_W) // LANES
        span = ATT_BLOCK * DILATIONS[g]
        per = A_SPAN // span
        return pl.BlockSpec((None, span, LANES),
                            lambda b, j, hp: (b, jnp.maximum(j * per - 1, 0), col + hp))

    groups = range(N_GROUPS)
    in_specs = ([cur_spec(g, 0) for g in groups] + [cur_spec(g, 1) for g in groups]
                + [prev_spec(g, 1) for g in groups] + [cur_spec(g, 2) for g in groups]
                + [prev_spec(g, 2) for g in groups])
    return pl.pallas_call(
        _attn_body,
        grid=(batch, seq // A_SPAN, A_GROUP_W // LANES),
        in_specs=in_specs,
        out_specs=pl.BlockSpec((None, A_SPAN, LANES), lambda b, j, hp: (b, j, hp)),
        out_shape=jax.ShapeDtypeStruct((batch, seq, A_OUT), F32),
        scratch_shapes=[pltpu.VMEM((N_GROUPS, A_SPAN, LANES), F32),
                        pltpu.VMEM((N_GROUPS, A_SPAN, LANES), F32),
                        pltpu.VMEM((2, ATT_BLOCK, 2 * ATT_BLOCK), BF16),
                        pltpu.VMEM((2 * A_QUAD, A_HEADS_PER_STEP, ATT_BLOCK, 2 * ATT_BLOCK), BF16),
                        pltpu.VMEM((2 * A_QUAD, A_HEADS_PER_STEP, ATT_BLOCK, 2 * ATT_BLOCK), BF16),
                        pltpu.VMEM((2 * A_QUAD, ATT_BLOCK, LANES), F32)]
                       + [pltpu.VMEM((A_SPAN, LANES), F32)] * 3,
        compiler_params=pltpu.CompilerParams(
            dimension_semantics=("parallel", "parallel", "parallel"), vmem_limit_bytes=VMEM_LIMIT),
    )(*([p] * 15))


RET_TB = 2048
_LOG_GAMMA = [float(v) for v in np.log1p(-(2.0 ** (-5.0 - np.arange(R_HEADS)))).astype(np.float32)]


def _ret_body(q_ref, k_ref, v_ref, g_ref, o_ref, st_ref):
    h = pl.program_id(1)
    t = pl.program_id(2)

    @pl.when(t == 0)
    def _reset():
        st_ref[...] = jnp.zeros_like(st_ref)

    lg = jnp.float32(_LOG_GAMMA[R_HEADS - 1])
    for hh in range(R_HEADS - 2, -1, -1):
        lg = jnp.where(h == hh, jnp.float32(_LOG_GAMMA[hh]), lg)
    row = lax.broadcasted_iota(jnp.int32, (R_CHUNK, R_CHUNK), 0).astype(F32)
    colv = lax.broadcasted_iota(jnp.int32, (R_CHUNK, R_CHUNK), 1).astype(F32)
    diff = row - colv
    decay = jnp.where(diff >= 0, jnp.exp(diff * lg), 0.0)
    zeta = jnp.exp((R_CHUNK - 1 - row) * lg)
    xi = jnp.exp((row + 1.0) * lg)
    chunk_decay = jnp.exp(jnp.full((R_KEY_DIM, R_VAL_DIM), float(R_CHUNK), F32) * lg)
    kscale = R_KEY_DIM ** -0.5

    for c in range(RET_TB // R_CHUNK):
        sl = slice(c * R_CHUNK, (c + 1) * R_CHUNK)
        q = q_ref[sl, :].astype(F32)
        k = k_ref[sl, :].astype(F32) * kscale
        v = v_ref[sl, :]
        s = lax.dot_general(q.astype(BF16), k.astype(BF16), (((1,), (1,)), ((), ())),
                            preferred_element_type=F32) * decay
        inner = jnp.dot(s.astype(BF16), v, preferred_element_type=F32)
        state = st_ref[...]
        cross = jnp.dot((q * xi).astype(BF16), state.astype(BF16), preferred_element_type=F32)
        kz_t = (k * zeta).T.astype(BF16)
        st_ref[...] = state * chunk_decay + jnp.dot(kz_t, v, preferred_element_type=F32)
        out = inner + cross
        ms = jnp.mean(out * out, axis=-1, keepdims=True)
        gate = g_ref[sl, :].astype(F32)
        o_ref[sl, :] = (gate * jax.nn.sigmoid(gate) * (out * lax.rsqrt(ms + NORM_EPS))).astype(BF16)


def _retention(proj, batch, seq):
    p = proj.reshape(batch, seq, P_WIDTH)
    return pl.pallas_call(
        _ret_body,
        grid=(batch, R_HEADS, seq // RET_TB),
        in_specs=[
            pl.BlockSpec((None, RET_TB, R_KEY_DIM), lambda b, h, t: (b, t, P_QR // R_KEY_DIM + h)),
            pl.BlockSpec((None, RET_TB, R_KEY_DIM), lambda b, h, t: (b, t, P_KR // R_KEY_DIM + h)),
            pl.BlockSpec((None, RET_TB, R_VAL_DIM), lambda b, h, t: (b, t, P_VR // R_VAL_DIM + h)),
            pl.BlockSpec((None, RET_TB, R_VAL_DIM), lambda b, h, t: (b, t, P_GR // R_VAL_DIM + h)),
        ],
        out_specs=pl.BlockSpec((None, RET_TB, R_VAL_DIM), lambda b, h, t: (b, t, h)),
        out_shape=jax.ShapeDtypeStruct((batch, seq, R_V), BF16),
        scratch_shapes=[pltpu.VMEM((R_KEY_DIM, R_VAL_DIM), F32)],
        compiler_params=pltpu.CompilerParams(
            dimension_semantics=("parallel", "parallel", "arbitrary"), vmem_limit_bytes=VMEM_LIMIT),
    )(p, p, p, p)


MERGE_TM = 1024


def _merge_body(x_ref, ya_ref, rb_ref, ga_ref, gb_ref, wa_ref, wb_ref, wo_ref, o_ref):
    ya = jnp.dot(ya_ref[...].astype(BF16), wa_ref[...], preferred_element_type=F32)
    yb = jnp.dot(rb_ref[...], wb_ref[...], preferred_element_type=F32)
    ga = ga_ref[...].astype(F32)
    gb = gb_ref[...].astype(F32)
    merged = jax.nn.sigmoid(ga) * ya + jax.nn.sigmoid(gb) * yb
    o_ref[...] = x_ref[...] + jnp.dot(merged.astype(BF16), wo_ref[...], preferred_element_type=F32)


def _merge(x2d, ya2d, rb2d, proj, wa, wb, wo, layer):
    m = x2d.shape[0]

    def full(shape):
        return pl.BlockSpec((None,) + shape, lambda i: (layer, 0, 0))

    return pl.pallas_call(
        _merge_body,
        grid=(m // MERGE_TM,),
        in_specs=[
            pl.BlockSpec((MERGE_TM, D_MODEL), lambda i: (i, 0)),
            pl.BlockSpec((MERGE_TM, A_OUT), lambda i: (i, 0)),
            pl.BlockSpec((MERGE_TM, R_V), lambda i: (i, 0)),
            pl.BlockSpec((MERGE_TM, D_MODEL), lambda i: (i, P_GA // D_MODEL)),
            pl.BlockSpec((MERGE_TM, D_MODEL), lambda i: (i, P_GB // D_MODEL)),
            full((A_OUT, D_MODEL)), full((R_V, D_MODEL)), full((D_MODEL, D_MODEL)),
        ],
        out_specs=pl.BlockSpec((MERGE_TM, D_MODEL), lambda i: (i, 0)),
        out_shape=jax.ShapeDtypeStruct((m, D_MODEL), F32),
        compiler_params=pltpu.CompilerParams(
            dimension_semantics=("parallel",), vmem_limit_bytes=VMEM_LIMIT),
    )(x2d, ya2d, rb2d, proj, proj, wa, wb, wo)


FFN_TM = 1024
FFN_TF = 2048


def _ffn_body(x_ref, g_ref, wu_ref, wd_ref, fg_ref, o_ref, h_ref, *, final):
    c = pl.program_id(1)

    @pl.when(c == 0)
    def _start():
        x = x_ref[...]
        h_ref[...] = _rms(x, g_ref[...]).astype(BF16)
        o_ref[...] = x

    up = jnp.dot(h_ref[...], wu_ref[...], preferred_element_type=F32)
    up = jnp.square(jnp.maximum(up, 0.0)).astype(BF16)
    o_ref[...] += jnp.dot(up, wd_ref[...], preferred_element_type=F32)

    if final:
        @pl.when(c == pl.num_programs(1) - 1)
        def _final_norm():
            o_ref[...] = _rms(o_ref[...], fg_ref[...])


def _ffn(x2d, gain, wu, wd, layer, final_gain, final):
    m = x2d.shape[0]
    return pl.pallas_call(
        functools.partial(_ffn_body, final=final),
        grid=(m // FFN_TM, D_FF // FFN_TF),
        in_specs=[
            pl.BlockSpec((FFN_TM, D_MODEL), lambda i, c: (i, 0)),
            pl.BlockSpec((1, D_MODEL), lambda i, c: (0, 0)),
            pl.BlockSpec((None, D_MODEL, FFN_TF), lambda i, c: (layer, 0, c)),
            pl.BlockSpec((None, FFN_TF, D_MODEL), lambda i, c: (layer, c, 0)),
            pl.BlockSpec((1, D_MODEL), lambda i, c: (0, 0)),
        ],
        out_specs=pl.BlockSpec((FFN_TM, D_MODEL), lambda i, c: (i, 0)),
        out_shape=jax.ShapeDtypeStruct((m, D_MODEL), F32),
        scratch_shapes=[pltpu.VMEM((FFN_TM, D_MODEL), BF16)],
        compiler_params=pltpu.CompilerParams(
            dimension_semantics=("parallel", "arbitrary"), vmem_limit_bytes=VMEM_LIMIT),
    )(x2d, gain, wu, wd, final_gain)


def kernel(x, mix_norm, w_in, w_a, w_b, w_o, ffn_norm, w_up, w_down, final_norm):
    batch, seq, _ = x.shape
    depth = w_in.shape[0]
    assert seq % A_SPAN == 0 and seq % RET_TB == 0
    tables = _rotary_tables(seq)
    final_gain = final_norm.reshape(1, D_MODEL)
    x2d = x.reshape(batch * seq, D_MODEL)
    w_in, w_a, w_b, w_o, w_up, w_down = (_arrange_w_in(w_in),) + tuple(
        w.astype(BF16) for w in (w_a, w_b, w_o, w_up, w_down))
    for layer in range(depth):
        proj = _in_proj(x2d, mix_norm[layer].reshape(1, D_MODEL), w_in, layer, tables, seq)
        ya = _attention_a(proj, batch, seq).reshape(batch * seq, A_OUT)
        rb = _retention(proj, batch, seq).reshape(batch * seq, R_V)
        x2d = _merge(x2d, ya, rb, proj, w_a, w_b, w_o, layer)
        x2d = _ffn(x2d, ffn_norm[layer].reshape(1, D_MODEL), w_up, w_down, layer, final_gain,
                   final=(layer == depth - 1))
    return x2d.reshape(batch, seq, D_MODEL)
```

```python
import functools

import numpy as np
import jax
import jax.numpy as jnp
from jax import lax
from jax.experimental import pallas as pl
from jax.experimental.pallas import tpu as pltpu

F32 = jnp.float32
BF16 = jnp.bfloat16

D_MODEL = 1024
N_GROUPS = 3
DILATIONS = (1, 4, 16)
A_HEADS = 8
A_HEAD_DIM = 64
A_GROUP_W = A_HEADS * A_HEAD_DIM
A_WIDTH = N_GROUPS * A_GROUP_W
A_OUT = A_GROUP_W
ROT_DIM = A_HEAD_DIM // 4
ROPE_THETA = 500000.0
ATT_BLOCK = 128
NEG_INF = -1e30
R_HEADS = 4
R_KEY_DIM = 128
R_VAL_DIM = 256
R_QK = R_HEADS * R_KEY_DIM
R_V = R_HEADS * R_VAL_DIM
R_CHUNK = 128
R_ROT_BASE = 10000.0
D_FF = 4 * D_MODEL
NORM_EPS = 1e-6

OFF_QA = 0
OFF_KA = A_WIDTH
OFF_VA = 2 * A_WIDTH
OFF_QR = 3 * A_WIDTH
OFF_KR = OFF_QR + R_QK
OFF_VR = OFF_KR + R_QK
OFF_GR = OFF_VR + R_V
OFF_GA = OFF_GR + R_V
OFF_GB = OFF_GA + D_MODEL

LANES = 128
VMEM_LIMIT = 56 * 1024 * 1024

A_SPAN = ATT_BLOCK * max(DILATIONS)
A_UNITS = A_SPAN // ATT_BLOCK

IN_TM = A_SPAN
IN_TN = 512
IN_CHUNKS = (512, 512, 512, 256, 256)
X_CHUNKS = D_MODEL // LANES

P_GA = 0
P_GB = P_GA + D_MODEL
P_VR = P_GB + D_MODEL
P_GR = P_VR + R_V
P_QR = P_GR + R_V
P_KR = P_QR + R_QK
P_QKV = tuple(P_KR + R_QK + g * 3 * A_GROUP_W for g in range(N_GROUPS))
P_WIDTH = P_QKV[-1] + 3 * A_GROUP_W

_TAB_R = 2 * N_GROUPS
_TAB_NONE = 2 * N_GROUPS + 1
TILE_TABLE = ((_TAB_NONE,) * (P_QR // IN_TN) + (_TAB_R,) * (2 * R_QK // IN_TN)
              + tuple(t for g in range(N_GROUPS) for t in (g, N_GROUPS + g, _TAB_NONE)))
LOG2_E = 1.4426950408889634
A_Q_SCALE = A_HEAD_DIM ** -0.5 * LOG2_E
TILE_ORDER = (0,) * (P_QKV[0] // IN_TN) + tuple(g for g in range(N_GROUPS) for _ in range(3))
N_TILES = P_WIDTH // IN_TN


def _rms(x, gain):
    ms = jnp.mean(x * x, axis=-1, keepdims=True)
    return x * lax.rsqrt(ms + NORM_EPS) * gain


def _step_lookup(j, values):
    out = values[-1]
    for t in range(len(values) - 2, -1, -1):
        if values[t] != values[t + 1]:
            out = jnp.where(j <= t, values[t], out)
    return out


def _residue_pieces(dil, src_dil, u):
    ratio = dil // src_dil
    piece = ATT_BLOCK // ratio
    ss, r = divmod(u, dil)
    for k in range(ratio):
        src_block = (ss * ratio + k) * src_dil + r % src_dil
        yield (pl.ds(src_block * ATT_BLOCK + r // src_dil, piece, stride=ratio),
               pl.ds(u * ATT_BLOCK + k * piece, piece))


def _in_proj_body(*refs):
    x_refs = refs[:X_CHUNKS]
    g_ref, w_ref, cos_ref, sin_ref, o_ref, h_ref, inv_ref, stage_ref = refs[X_CHUNKS:]
    j = pl.program_id(1)

    @pl.when(j == 0)
    def _norm():
        gains = [g_ref[:, c * LANES:(c + 1) * LANES] for c in range(X_CHUNKS)]
        for u in range(A_UNITS):
            rows = pl.ds(u * ATT_BLOCK, ATT_BLOCK)
            xc = [x_refs[c][rows, :] for c in range(X_CHUNKS)]
            sq = xc[0] * xc[0]
            for c in range(1, X_CHUNKS):
                sq = sq + xc[c] * xc[c]
            inv = lax.rsqrt(jnp.sum(sq, axis=-1, keepdims=True) * (1.0 / D_MODEL) + NORM_EPS)
            inv = jnp.broadcast_to(inv, (ATT_BLOCK, LANES))
            inv_ref[rows, :] = inv
            for c in range(X_CHUNKS):
                h_ref[0, rows, c * LANES:(c + 1) * LANES] = (xc[c] * inv * gains[c]).astype(BF16)
        for g in range(1, N_GROUPS):
            for u in range(A_UNITS):
                for src, dst in _residue_pieces(DILATIONS[g], DILATIONS[g - 1], u):
                    if g == 1:
                        inv = inv_ref[src, :]
                    for c in range(X_CHUNKS):
                        lanes = slice(c * LANES, (c + 1) * LANES)
                        if g == 1:
                            y = x_refs[c][src, :] * inv * gains[c]
                            if g + 1 < N_GROUPS:
                                stage_ref[c, dst, :] = y
                        else:
                            y = stage_ref[c, src, :]
                        h_ref[g, dst, lanes] = y.astype(BF16)

    def project(rotary, order):
        starts = np.cumsum((0,) + IN_CHUNKS)
        for start, size in zip(starts, IN_CHUNKS):
            rows = pl.ds(int(start), size)
            acc = jnp.dot(h_ref[order, rows, :], w_ref[...], preferred_element_type=F32)
            if not rotary:
                o_ref[rows, :] = acc.astype(BF16)
                continue
            for c in range(IN_TN // LANES):
                t = acc[:, c * LANES:(c + 1) * LANES]
                o_ref[rows, c * LANES:(c + 1) * LANES] = (
                    t * cos_ref[rows, :] + pltpu.roll(t, LANES // 2, 1) * sin_ref[rows, :]).astype(BF16)

    order = _step_lookup(j, TILE_ORDER)
    has_rotary = _step_lookup(j, tuple(int(t != _TAB_NONE) for t in TILE_TABLE)) == 1
    pl.when(has_rotary)(functools.partial(project, True, order))
    pl.when(jnp.logical_not(has_rotary))(functools.partial(project, False, order))


_SRC_TILE = tuple(off // IN_TN + t for off, n in ((OFF_GA, 2), (OFF_GB, 2), (OFF_VR, 2), (OFF_GR, 2), (OFF_QR, 1),
                                                   (OFF_KR, 1)) for t in range(n)) + tuple(
    (off + g * A_GROUP_W) // IN_TN for g in range(N_GROUPS) for off in (OFF_QA, OFF_KA, OFF_VA))
_PAIR_TILE = tuple(int(t < 2 * N_GROUPS) for t in TILE_TABLE)


def _arrange_body(w_ref, o_ref):
    j = pl.program_id(1)
    pair = _step_lookup(j, _PAIR_TILE) == 1

    @pl.when(jnp.logical_not(pair))
    def _plain():
        o_ref[...] = w_ref[...].astype(BF16)

    @pl.when(pair)
    def _pair():
        half = ROT_DIM // 2
        lane = lax.broadcasted_iota(jnp.int32, (D_MODEL, LANES), 1)
        to_b_x1 = (lane >= half) & (lane < ROT_DIM)
        to_a_x2 = (lane >= A_HEAD_DIM) & (lane < A_HEAD_DIM + half)
        for c in range(IN_TN // LANES):
            t = w_ref[:, c * LANES:(c + 1) * LANES]
            moved = jnp.where(to_b_x1, pltpu.roll(t, LANES - A_HEAD_DIM + half, 1),
                              pltpu.roll(t, A_HEAD_DIM - half, 1))
            o_ref[:, c * LANES:(c + 1) * LANES] = jnp.where(to_b_x1 | to_a_x2, moved, t).astype(BF16)


def _arrange_w_in(w):
    depth = w.shape[0]
    return pl.pallas_call(
        _arrange_body,
        grid=(depth, N_TILES),
        in_specs=[pl.BlockSpec((None, D_MODEL, IN_TN), lambda l, j: (l, 0, _step_lookup(j, _SRC_TILE)))],
        out_specs=pl.BlockSpec((None, D_MODEL, IN_TN), lambda l, j: (l, 0, j)),
        out_shape=jax.ShapeDtypeStruct((depth, D_MODEL, P_WIDTH), BF16),
        compiler_params=pltpu.CompilerParams(
            dimension_semantics=("parallel", "parallel"), vmem_limit_bytes=VMEM_LIMIT),
    )(w)


@functools.lru_cache(maxsize=None)
def _rotary_tables(seq):
    f32 = np.float32
    half = ROT_DIM // 2
    inv = (1.0 / (f32(ROPE_THETA) ** (np.arange(half, dtype=f32) / f32(half)))).astype(f32)
    rest = A_HEAD_DIM - ROT_DIM
    ones, zeros = np.ones((seq, rest), f32), np.zeros((seq, rest), f32)
    cos_tabs, sin_tabs = [], []
    for dil in DILATIONS:
        span = ATT_BLOCK * dil
        pos = np.arange(seq, dtype=f32).reshape(seq // span, ATT_BLOCK, dil).transpose(0, 2, 1).reshape(seq, 1)
        ang = pos * inv[None, :]
        cos, sin = np.cos(ang), np.sin(ang)
        cos_tabs.append(np.concatenate([cos, cos, ones, cos, cos, ones], axis=1))
        sin_tabs.append(np.concatenate([-sin, -sin, zeros, sin, sin, zeros], axis=1))
    cos_tabs = [t * f32(A_Q_SCALE) for t in cos_tabs] + cos_tabs
    sin_tabs = [t * f32(A_Q_SCALE) for t in sin_tabs] + sin_tabs
    half_r = R_KEY_DIM // 2
    inv_r = (1.0 / (f32(R_ROT_BASE) ** (np.arange(half_r, dtype=f32) / f32(half_r)))).astype(f32)
    ang_r = np.arange(seq, dtype=f32)[:, None] * inv_r[None, :]
    cos_tabs += [np.concatenate([np.cos(ang_r), np.cos(ang_r)], axis=1), np.ones((seq, LANES), f32)]
    sin_tabs += [np.concatenate([-np.sin(ang_r), np.sin(ang_r)], axis=1), np.zeros((seq, LANES), f32)]
    return np.stack(cos_tabs).astype(f32), np.stack(sin_tabs).astype(f32)


def _in_proj(x2d, gain, w_bf16, layer, tables, seq):
    m = x2d.shape[0]
    tiles_per_seq = seq // IN_TM
    tab_spec = pl.BlockSpec((None, IN_TM, LANES),
                            lambda i, j: (_step_lookup(j, TILE_TABLE), i % tiles_per_seq, 0))
    x_specs = [pl.BlockSpec((IN_TM, LANES), lambda i, j, c=c: (i, c)) for c in range(X_CHUNKS)]
    return pl.pallas_call(
        _in_proj_body,
        grid=(m // IN_TM, P_WIDTH // IN_TN),
        in_specs=x_specs + [
            pl.BlockSpec((1, D_MODEL), lambda i, j: (0, 0)),
            pl.BlockSpec((None, D_MODEL, IN_TN), lambda i, j: (layer, 0, j)),
            tab_spec, tab_spec,
        ],
        out_specs=pl.BlockSpec((IN_TM, IN_TN), lambda i, j: (i, j)),
        out_shape=jax.ShapeDtypeStruct((m, P_WIDTH), BF16),
        scratch_shapes=[pltpu.VMEM((N_GROUPS, IN_TM, D_MODEL), BF16),
                        pltpu.VMEM((IN_TM, LANES), F32),
                        pltpu.VMEM((X_CHUNKS, IN_TM, LANES), F32)],
        compiler_params=pltpu.CompilerParams(
            dimension_semantics=("parallel", "arbitrary"), vmem_limit_bytes=VMEM_LIMIT),
    )(*([x2d] * X_CHUNKS), gain, w_bf16, *tables)


A_HEADS_PER_STEP = LANES // A_HEAD_DIM
A_QUAD = 8


def _attn_body(*refs):
    q_in = refs[0:3]
    kc_in = refs[3:6]
    kp_in = refs[6:9]
    vc_in = refs[9:12]
    vp_in = refs[12:15]
    o_ref = refs[15]
    ro, rl, bias, s_buf, p_buf, m_buf = refs[16:22]
    so_buf, sl_buf, sy_buf = refs[22:25]
    jt = pl.program_id(1)

    qi = lax.broadcasted_iota(jnp.int32, (ATT_BLOCK, 2 * ATT_BLOCK), 0)
    kj = lax.broadcasted_iota(jnp.int32, (ATT_BLOCK, 2 * ATT_BLOCK), 1)
    dist = qi + ATT_BLOCK - kj
    band = (dist >= 0) & (dist <= ATT_BLOCK)
    bias[0] = jnp.where(band & (kj >= ATT_BLOCK), 0.0, NEG_INF).astype(BF16)
    bias[1] = jnp.where(band, 0.0, NEG_INF).astype(BF16)
    lane = lax.broadcasted_iota(jnp.int32, (ATT_BLOCK, LANES), 1)

    half = ROT_DIM // 2
    first_head = (lane < half) | ((lane >= ROT_DIM) & (lane < A_HEAD_DIM + half))
    head_lanes = (first_head, jnp.logical_not(first_head))
    first = lane < A_HEAD_DIM
    v_lane = lax.broadcasted_iota(jnp.int32, (2 * ATT_BLOCK, LANES), 1)
    v_lanes = (v_lane < A_HEAD_DIM, v_lane >= A_HEAD_DIM)

    def rows_of(u):
        return pl.ds(u * ATT_BLOCK, ATT_BLOCK)

    def previous(cur, prev, d, u):
        return prev[rows_of(u), :] if u < d else cur[rows_of(u - d), :]

    def stage_scores(slot, g, u):
        d = DILATIONS[g]
        b = bias[jnp.minimum(jt, 1)] if u < d else bias[1]
        q = q_in[g][rows_of(u), :]
        k2 = jnp.concatenate([previous(kc_in[g], kp_in[g], d, u), kc_in[g][rows_of(u), :]], axis=0)
        for h in range(A_HEADS_PER_STEP):
            qm = jnp.where(head_lanes[h], q, jnp.zeros_like(q))
            s = lax.dot_general(qm, k2, (((1,), (1,)), ((), ())), preferred_element_type=F32)
            s_buf[slot, h] = s.astype(BF16) + b

    def stage_softmax(slot, g, u):
        ms = []
        for h in range(A_HEADS_PER_STEP):
            s = s_buf[slot, h]
            m = jnp.max(s, axis=1, keepdims=True)
            p_buf[slot, h] = jnp.exp2(s - m)
            ms.append(m.astype(F32))
        m_buf[slot] = jnp.where(first, ms[0], ms[1])

    def stage_out(slot, g, u):
        d = DILATIONS[g]
        v2 = jnp.concatenate([previous(vc_in[g], vp_in[g], d, u), vc_in[g][rows_of(u), :]], axis=0)
        pvs = [jnp.dot(p_buf[slot, h], jnp.where(v_lanes[h], v2, jnp.ones_like(v2)), preferred_element_type=F32)
               for h in range(A_HEADS_PER_STEP)]
        out = jnp.where(first, pvs[0], pvs[1])
        den = pltpu.roll(jnp.where(first, pvs[1], pvs[0]), A_HEAD_DIM, 1)
        ro[g, rows_of(u), :] = out / den
        rl[g, rows_of(u), :] = m_buf[slot] + jnp.log(den) * LOG2_E

    quads = [(g, qd) for g in range(N_GROUPS) for qd in range(A_UNITS // A_QUAD)]
    stages = (stage_scores, stage_softmax, stage_out)
    for trip in range(len(quads) + len(stages) - 1):
        for age in range(len(stages) - 1, -1, -1):
            n = trip - age
            if 0 <= n < len(quads):
                g, qd = quads[n]
                for i in range(A_QUAD):
                    stages[age]((n % 2) * A_QUAD + i, g, qd * A_QUAD + i)

    d2, d3 = DILATIONS[1], DILATIONS[2]
    for u in range(A_UNITS):
        for src, dst in _residue_pieces(d2, DILATIONS[0], u):
            so_buf[dst, :] = ro[0, src, :]
            sl_buf[dst, :] = rl[0, src, :]
    for r3 in range(A_UNITS):
        blk = pl.ds(r3 * ATT_BLOCK, ATT_BLOCK)
        pieces = list(_residue_pieces(d3, d2, r3))
        o1 = jnp.concatenate([so_buf[src, :] for src, _ in pieces], axis=0)
        l1 = jnp.concatenate([sl_buf[src, :] for src, _ in pieces], axis=0)
        o2 = jnp.concatenate([ro[1, src, :] for src, _ in pieces], axis=0)
        l2 = jnp.concatenate([rl[1, src, :] for src, _ in pieces], axis=0)
        o3, l3 = ro[2, blk, :], rl[2, blk, :]
        mx = jnp.maximum(jnp.maximum(l1, l2), l3)
        e1, e2, e3 = jnp.exp2(l1 - mx), jnp.exp2(l2 - mx), jnp.exp2(l3 - mx)
        y = (e1 * o1 + e2 * o2 + e3 * o3) / (e1 + e2 + e3)
        for k, (src, _) in enumerate(pieces):
            sy_buf[src, :] = y[k * ATT_BLOCK // len(pieces):(k + 1) * ATT_BLOCK // len(pieces), :]
    for u in range(A_UNITS):
        for src, dst in _residue_pieces(d2, DILATIONS[0], u):
            o_ref[src, :] = sy_buf[dst, :]


def _attention_a(proj, batch, seq):
    p = proj.reshape(batch, seq, P_WIDTH)

    def cur_spec(g, part):
        col = (P_QKV[g] + part * A_GROUP_W) // LANES
        return pl.BlockSpec((None, A_SPAN, LANES), lambda b, j, hp: (b, j, col + hp))

    def prev_spec(g, part):
        col = (P_QKV[g] + part * A_GROUP_W) // LANES
        span = ATT_BLOCK * DILATIONS[g]
        per = A_SPAN // span
        return pl.BlockSpec((None, span, LANES),
                            lambda b, j, hp: (b, jnp.maximum(j * per - 1, 0), col + hp))

    groups = range(N_GROUPS)
    in_specs = ([cur_spec(g, 0) for g in groups] + [cur_spec(g, 1) for g in groups]
                + [prev_spec(g, 1) for g in groups] + [cur_spec(g, 2) for g in groups]
                + [prev_spec(g, 2) for g in groups])
    return pl.pallas_call(
        _attn_body,
        grid=(batch, seq // A_SPAN, A_GROUP_W // LANES),
        in_specs=in_specs,
        out_specs=pl.BlockSpec((None, A_SPAN, LANES), lambda b, j, hp: (b, j, hp)),
        out_shape=jax.ShapeDtypeStruct((batch, seq, A_OUT), F32),
        scratch_shapes=[pltpu.VMEM((N_GROUPS, A_SPAN, LANES), F32),
                        pltpu.VMEM((N_GROUPS, A_SPAN, LANES), F32),
                        pltpu.VMEM((2, ATT_BLOCK, 2 * ATT_BLOCK), BF16),
                        pltpu.VMEM((2 * A_QUAD, A_HEADS_PER_STEP, ATT_BLOCK, 2 * ATT_BLOCK), BF16),
                        pltpu.VMEM((2 * A_QUAD, A_HEADS_PER_STEP, ATT_BLOCK, 2 * ATT_BLOCK), BF16),
                        pltpu.VMEM((2 * A_QUAD, ATT_BLOCK, LANES), F32)]
                       + [pltpu.VMEM((A_SPAN, LANES), F32)] * 3,
        compiler_params=pltpu.CompilerParams(
            dimension_semantics=("parallel", "parallel", "parallel"), vmem_limit_bytes=VMEM_LIMIT),
    )(*([p] * 15))


RET_TB = 2048
_LOG_GAMMA = [float(v) for v in np.log1p(-(2.0 ** (-5.0 - np.arange(R_HEADS)))).astype(np.float32)]


def _ret_body(q_ref, k_ref, v_ref, g_ref, o_ref, st_ref):
    h = pl.program_id(1)
    t = pl.program_id(2)

    @pl.when(t == 0)
    def _reset():
        st_ref[...] = jnp.zeros_like(st_ref)

    lg = jnp.float32(_LOG_GAMMA[R_HEADS - 1])
    for hh in range(R_HEADS - 2, -1, -1):
        lg = jnp.where(h == hh, jnp.float32(_LOG_GAMMA[hh]), lg)
    row = lax.broadcasted_iota(jnp.int32, (R_CHUNK, R_CHUNK), 0).astype(F32)
    colv = lax.broadcasted_iota(jnp.int32, (R_CHUNK, R_CHUNK), 1).astype(F32)
    diff = row - colv
    decay = jnp.where(diff >= 0, jnp.exp(diff * lg), 0.0)
    zeta = jnp.exp((R_CHUNK - 1 - row) * lg)
    xi = jnp.exp((row + 1.0) * lg)
    chunk_decay = jnp.exp(jnp.full((R_KEY_DIM, R_VAL_DIM), float(R_CHUNK), F32) * lg)
    kscale = R_KEY_DIM ** -0.5

    for c in range(RET_TB // R_CHUNK):
        sl = slice(c * R_CHUNK, (c + 1) * R_CHUNK)
        q = q_ref[sl, :].astype(F32)
        k = k_ref[sl, :].astype(F32) * kscale
        v = v_ref[sl, :]
        s = lax.dot_general(q.astype(BF16), k.astype(BF16), (((1,), (1,)), ((), ())),
                            preferred_element_type=F32) * decay
        inner = jnp.dot(s.astype(BF16), v, preferred_element_type=F32)
        state = st_ref[...]
        cross = jnp.dot((q * xi).astype(BF16), state.astype(BF16), preferred_element_type=F32)
        kz_t = (k * zeta).T.astype(BF16)
        st_ref[...] = state * chunk_decay + jnp.dot(kz_t, v, preferred_element_type=F32)
        out = inner + cross
        ms = jnp.mean(out * out, axis=-1, keepdims=True)
        gate = g_ref[sl, :].astype(F32)
        o_ref[sl, :] = (gate * jax.nn.sigmoid(gate) * (out * lax.rsqrt(ms + NORM_EPS))).astype(BF16)


def _retention(proj, batch, seq):
    p = proj.reshape(batch, seq, P_WIDTH)
    return pl.pallas_call(
        _ret_body,
        grid=(batch, R_HEADS, seq // RET_TB),
        in_specs=[
            pl.BlockSpec((None, RET_TB, R_KEY_DIM), lambda b, h, t: (b, t, P_QR // R_KEY_DIM + h)),
            pl.BlockSpec((None, RET_TB, R_KEY_DIM), lambda b, h, t: (b, t, P_KR // R_KEY_DIM + h)),
            pl.BlockSpec((None, RET_TB, R_VAL_DIM), lambda b, h, t: (b, t, P_VR // R_VAL_DIM + h)),
            pl.BlockSpec((None, RET_TB, R_VAL_DIM), lambda b, h, t: (b, t, P_GR // R_VAL_DIM + h)),
        ],
        out_specs=pl.BlockSpec((None, RET_TB, R_VAL_DIM), lambda b, h, t: (b, t, h)),
        out_shape=jax.ShapeDtypeStruct((batch, seq, R_V), BF16),
        scratch_shapes=[pltpu.VMEM((R_KEY_DIM, R_VAL_DIM), F32)],
        compiler_params=pltpu.CompilerParams(
            dimension_semantics=("parallel", "parallel", "arbitrary"), vmem_limit_bytes=VMEM_LIMIT),
    )(p, p, p, p)


MERGE_TM = 1024


def _merge_body(x_ref, ya_ref, rb_ref, ga_ref, gb_ref, wa_ref, wb_ref, wo_ref, o_ref):
    ya = jnp.dot(ya_ref[...].astype(BF16), wa_ref[...], preferred_element_type=F32)
    yb = jnp.dot(rb_ref[...], wb_ref[...], preferred_element_type=F32)
    ga = ga_ref[...].astype(F32)
    gb = gb_ref[...].astype(F32)
    merged = jax.nn.sigmoid(ga) * ya + jax.nn.sigmoid(gb) * yb
    o_ref[...] = x_ref[...] + jnp.dot(merged.astype(BF16), wo_ref[...], preferred_element_type=F32)


def _merge(x2d, ya2d, rb2d, proj, wa, wb, wo, layer):
    m = x2d.shape[0]

    def full(shape):
        return pl.BlockSpec((None,) + shape, lambda i: (layer, 0, 0))

    return pl.pallas_call(
        _merge_body,
        grid=(m // MERGE_TM,),
        in_specs=[
            pl.BlockSpec((MERGE_TM, D_MODEL), lambda i: (i, 0)),
            pl.BlockSpec((MERGE_TM, A_OUT), lambda i: (i, 0)),
            pl.BlockSpec((MERGE_TM, R_V), lambda i: (i, 0)),
            pl.BlockSpec((MERGE_TM, D_MODEL), lambda i: (i, P_GA // D_MODEL)),
            pl.BlockSpec((MERGE_TM, D_MODEL), lambda i: (i, P_GB // D_MODEL)),
            full((A_OUT, D_MODEL)), full((R_V, D_MODEL)), full((D_MODEL, D_MODEL)),
        ],
        out_specs=pl.BlockSpec((MERGE_TM, D_MODEL), lambda i: (i, 0)),
        out_shape=jax.ShapeDtypeStruct((m, D_MODEL), F32),
        compiler_params=pltpu.CompilerParams(
            dimension_semantics=("parallel",), vmem_limit_bytes=VMEM_LIMIT),
    )(x2d, ya2d, rb2d, proj, proj, wa, wb, wo)


FFN_TM = 1024
FFN_TF = 2048


def _ffn_body(x_ref, g_ref, wu_ref, wd_ref, fg_ref, o_ref, h_ref, *, final):
    c = pl.program_id(1)

    @pl.when(c == 0)
    def _start():
        x = x_ref[...]
        h_ref[...] = _rms(x, g_ref[...]).astype(BF16)
        o_ref[...] = x

    up = jnp.dot(h_ref[...], wu_ref[...], preferred_element_type=F32)
    up = jnp.square(jnp.maximum(up, 0.0)).astype(BF16)
    o_ref[...] += jnp.dot(up, wd_ref[...], preferred_element_type=F32)

    if final:
        @pl.when(c == pl.num_programs(1) - 1)
        def _final_norm():
            o_ref[...] = _rms(o_ref[...], fg_ref[...])


def _ffn(x2d, gain, wu, wd, layer, final_gain, final):
    m = x2d.shape[0]
    return pl.pallas_call(
        functools.partial(_ffn_body, final=final),
        grid=(m // FFN_TM, D_FF // FFN_TF),
        in_specs=[
            pl.BlockSpec((FFN_TM, D_MODEL), lambda i, c: (i, 0)),
            pl.BlockSpec((1, D_MODEL), lambda i, c: (0, 0)),
            pl.BlockSpec((None, D_MODEL, FFN_TF), lambda i, c: (layer, 0, c)),
            pl.BlockSpec((None, FFN_TF, D_MODEL), lambda i, c: (layer, c, 0)),
            pl.BlockSpec((1, D_MODEL), lambda i, c: (0, 0)),
        ],
        out_specs=pl.BlockSpec((FFN_TM, D_MODEL), lambda i, c: (i, 0)),
        out_shape=jax.ShapeDtypeStruct((m, D_MODEL), F32),
        scratch_shapes=[pltpu.VMEM((FFN_TM, D_MODEL), BF16)],
        compiler_params=pltpu.CompilerParams(
            dimension_semantics=("parallel", "arbitrary"), vmem_limit_bytes=VMEM_LIMIT),
    )(x2d, gain, wu, wd, final_gain)


def kernel(x, mix_norm, w_in, w_a, w_b, w_o, ffn_norm, w_up, w_down, final_norm):
    batch, seq, _ = x.shape
    depth = w_in.shape[0]
    assert seq % A_SPAN == 0 and seq % RET_TB == 0
    tables = _rotary_tables(seq)
    final_gain = final_norm.reshape(1, D_MODEL)
    x2d = x.reshape(batch * seq, D_MODEL)
    w_in, w_a, w_b, w_o, w_up, w_down = (_arrange_w_in(w_in),) + tuple(
        w.astype(BF16) for w in (w_a, w_b, w_o, w_up, w_down))
    for layer in range(depth):
        proj = _in_proj(x2d, mix_norm[layer].reshape(1, D_MODEL), w_in, layer, tables, seq)
        ya = _attention_a(proj, batch, seq).reshape(batch * seq, A_OUT)
        rb = _retention(proj, batch, seq).reshape(batch * seq, R_V)
        x2d = _merge(x2d, ya, rb, proj, w_a, w_b, w_o, layer)
        x2d = _ffn(x2d, ffn_norm[layer].reshape(1, D_MODEL), w_up, w_down, layer, final_gain,
                   final=(layer == depth - 1))
    return x2d.reshape(batch, seq, D_MODEL)
```

```python
import functools

import numpy as np
import jax
import jax.numpy as jnp
from jax import lax
from jax.experimental import pallas as pl
from jax.experimental.pallas import tpu as pltpu

F32 = jnp.float32
BF16 = jnp.bfloat16

D_MODEL = 1024
N_GROUPS = 3
DILATIONS = (1, 4, 16)
A_HEADS = 8
A_HEAD_DIM = 64
A_GROUP_W = A_HEADS * A_HEAD_DIM
A_WIDTH = N_GROUPS * A_GROUP_W
A_OUT = A_GROUP_W
ROT_DIM = A_HEAD_DIM // 4
ROPE_THETA = 500000.0
ATT_BLOCK = 128
NEG_INF = -1e30
R_HEADS = 4
R_KEY_DIM = 128
R_VAL_DIM = 256
R_QK = R_HEADS * R_KEY_DIM
R_V = R_HEADS * R_VAL_DIM
R_CHUNK = 128
R_ROT_BASE = 10000.0
D_FF = 4 * D_MODEL
NORM_EPS = 1e-6

OFF_QA = 0
OFF_KA = A_WIDTH
OFF_VA = 2 * A_WIDTH
OFF_QR = 3 * A_WIDTH
OFF_KR = OFF_QR + R_QK
OFF_VR = OFF_KR + R_QK
OFF_GR = OFF_VR + R_V
OFF_GA = OFF_GR + R_V
OFF_GB = OFF_GA + D_MODEL

LANES = 128
VMEM_LIMIT = 56 * 1024 * 1024

A_SPAN = ATT_BLOCK * max(DILATIONS)
A_UNITS = A_SPAN // ATT_BLOCK

IN_TM = A_SPAN
IN_TN = 512
IN_CHUNKS = (512, 512, 512, 256, 256)
X_CHUNKS = D_MODEL // LANES

P_GA = 0
P_GB = P_GA + D_MODEL
P_VR = P_GB + D_MODEL
P_GR = P_VR + R_V
P_QR = P_GR + R_V
P_KR = P_QR + R_QK
P_QKV = tuple(P_KR + R_QK + g * 3 * A_GROUP_W for g in range(N_GROUPS))
P_WIDTH = P_QKV[-1] + 3 * A_GROUP_W

_TAB_R = 2 * N_GROUPS
_TAB_NONE = 2 * N_GROUPS + 1
TILE_TABLE = ((_TAB_NONE,) * (P_QR // IN_TN) + (_TAB_R,) * (2 * R_QK // IN_TN)
              + tuple(t for g in range(N_GROUPS) for t in (g, N_GROUPS + g, _TAB_NONE)))
LOG2_E = 1.4426950408889634
A_Q_SCALE = A_HEAD_DIM ** -0.5 * LOG2_E
TILE_ORDER = (0,) * (P_QKV[0] // IN_TN) + tuple(g for g in range(N_GROUPS) for _ in range(3))
N_TILES = P_WIDTH // IN_TN


def _rms(x, gain):
    ms = jnp.mean(x * x, axis=-1, keepdims=True)
    return x * lax.rsqrt(ms + NORM_EPS) * gain


def _step_lookup(j, values):
    out = values[-1]
    for t in range(len(values) - 2, -1, -1):
        if values[t] != values[t + 1]:
            out = jnp.where(j <= t, values[t], out)
    return out


def _residue_pieces(dil, src_dil, u):
    ratio = dil // src_dil
    piece = ATT_BLOCK // ratio
    ss, r = divmod(u, dil)
    for k in range(ratio):
        src_block = (ss * ratio + k) * src_dil + r % src_dil
        yield (pl.ds(src_block * ATT_BLOCK + r // src_dil, piece, stride=ratio),
               pl.ds(u * ATT_BLOCK + k * piece, piece))


def _in_proj_body(*refs):
    x_refs = refs[:X_CHUNKS]
    g_ref, w_ref, cos_ref, sin_ref, o_ref, h_ref, inv_ref, stage_ref = refs[X_CHUNKS:]
    j = pl.program_id(1)

    @pl.when(j == 0)
    def _norm():
        gains = [g_ref[:, c * LANES:(c + 1) * LANES] for c in range(X_CHUNKS)]
        for u in range(A_UNITS):
            rows = pl.ds(u * ATT_BLOCK, ATT_BLOCK)
            xc = [x_refs[c][rows, :] for c in range(X_CHUNKS)]
            sq = xc[0] * xc[0]
            for c in range(1, X_CHUNKS):
                sq = sq + xc[c] * xc[c]
            inv = lax.rsqrt(jnp.sum(sq, axis=-1, keepdims=True) * (1.0 / D_MODEL) + NORM_EPS)
            inv = jnp.broadcast_to(inv, (ATT_BLOCK, LANES))
            inv_ref[rows, :] = inv
            for c in range(X_CHUNKS):
                h_ref[0, rows, c * LANES:(c + 1) * LANES] = (xc[c] * inv * gains[c]).astype(BF16)
        for g in range(1, N_GROUPS):
            for u in range(A_UNITS):
                for src, dst in _residue_pieces(DILATIONS[g], DILATIONS[g - 1], u):
                    if g == 1:
                        inv = inv_ref[src, :]
                    for c in range(X_CHUNKS):
                        lanes = slice(c * LANES, (c + 1) * LANES)
                        if g == 1:
                            y = x_refs[c][src, :] * inv * gains[c]
                            if g + 1 < N_GROUPS:
                                stage_ref[c, dst, :] = y
                        else:
                            y = stage_ref[c, src, :]
                        h_ref[g, dst, lanes] = y.astype(BF16)

    def project(rotary, order):
        starts = np.cumsum((0,) + IN_CHUNKS)
        for start, size in zip(starts, IN_CHUNKS):
            rows = pl.ds(int(start), size)
            acc = jnp.dot(h_ref[order, rows, :], w_ref[...], preferred_element_type=F32)
            if not rotary:
                o_ref[rows, :] = acc.astype(BF16)
                continue
            for c in range(IN_TN // LANES):
                t = acc[:, c * LANES:(c + 1) * LANES]
                o_ref[rows, c * LANES:(c + 1) * LANES] = (
                    t * cos_ref[rows, :] + pltpu.roll(t, LANES // 2, 1) * sin_ref[rows, :]).astype(BF16)

    order = _step_lookup(j, TILE_ORDER)
    has_rotary = _step_lookup(j, tuple(int(t != _TAB_NONE) for t in TILE_TABLE)) == 1
    pl.when(has_rotary)(functools.partial(project, True, order))
    pl.when(jnp.logical_not(has_rotary))(functools.partial(project, False, order))


_SRC_TILE = tuple(off // IN_TN + t for off, n in ((OFF_GA, 2), (OFF_GB, 2), (OFF_VR, 2), (OFF_GR, 2), (OFF_QR, 1),
                                                   (OFF_KR, 1)) for t in range(n)) + tuple(
    (off + g * A_GROUP_W) // IN_TN for g in range(N_GROUPS) for off in (OFF_QA, OFF_KA, OFF_VA))
_PAIR_TILE = tuple(int(t < 2 * N_GROUPS) for t in TILE_TABLE)


def _arrange_body(w_ref, o_ref):
    j = pl.program_id(1)
    pair = _step_lookup(j, _PAIR_TILE) == 1

    @pl.when(jnp.logical_not(pair))
    def _plain():
        o_ref[...] = w_ref[...].astype(BF16)

    @pl.when(pair)
    def _pair():
        half = ROT_DIM // 2
        lane = lax.broadcasted_iota(jnp.int32, (D_MODEL, LANES), 1)
        to_b_x1 = (lane >= half) & (lane < ROT_DIM)
        to_a_x2 = (lane >= A_HEAD_DIM) & (lane < A_HEAD_DIM + half)
        for c in range(IN_TN // LANES):
            t = w_ref[:, c * LANES:(c + 1) * LANES]
            moved = jnp.where(to_b_x1, pltpu.roll(t, LANES - A_HEAD_DIM + half, 1),
                              pltpu.roll(t, A_HEAD_DIM - half, 1))
            o_ref[:, c * LANES:(c + 1) * LANES] = jnp.where(to_b_x1 | to_a_x2, moved, t).astype(BF16)


def _arrange_w_in(w):
    depth = w.shape[0]
    return pl.pallas_call(
        _arrange_body,
        grid=(depth, N_TILES),
        in_specs=[pl.BlockSpec((None, D_MODEL, IN_TN), lambda l, j: (l, 0, _step_lookup(j, _SRC_TILE)))],
        out_specs=pl.BlockSpec((None, D_MODEL, IN_TN), lambda l, j: (l, 0, j)),
        out_shape=jax.ShapeDtypeStruct((depth, D_MODEL, P_WIDTH), BF16),
        compiler_params=pltpu.CompilerParams(
            dimension_semantics=("parallel", "parallel"), vmem_limit_bytes=VMEM_LIMIT),
    )(w)


@functools.lru_cache(maxsize=None)
def _rotary_tables(seq):
    f32 = np.float32
    half = ROT_DIM // 2
    inv = (1.0 / (f32(ROPE_THETA) ** (np.arange(half, dtype=f32) / f32(half)))).astype(f32)
    rest = A_HEAD_DIM - ROT_DIM
    ones, zeros = np.ones((seq, rest), f32), np.zeros((seq, rest), f32)
    cos_tabs, sin_tabs = [], []
    for dil in DILATIONS:
        span = ATT_BLOCK * dil
        pos = np.arange(seq, dtype=f32).reshape(seq // span, ATT_BLOCK, dil).transpose(0, 2, 1).reshape(seq, 1)
        ang = pos * inv[None, :]
        cos, sin = np.cos(ang), np.sin(ang)
        cos_tabs.append(np.concatenate([cos, cos, ones, cos, cos, ones], axis=1))
        sin_tabs.append(np.concatenate([-sin, -sin, zeros, sin, sin, zeros], axis=1))
    cos_tabs = [t * f32(A_Q_SCALE) for t in cos_tabs] + cos_tabs
    sin_tabs = [t * f32(A_Q_SCALE) for t in sin_tabs] + sin_tabs
    half_r = R_KEY_DIM // 2
    inv_r = (1.0 / (f32(R_ROT_BASE) ** (np.arange(half_r, dtype=f32) / f32(half_r)))).astype(f32)
    ang_r = np.arange(seq, dtype=f32)[:, None] * inv_r[None, :]
    cos_tabs += [np.concatenate([np.cos(ang_r), np.cos(ang_r)], axis=1), np.ones((seq, LANES), f32)]
    sin_tabs += [np.concatenate([-np.sin(ang_r), np.sin(ang_r)], axis=1), np.zeros((seq, LANES), f32)]
    return np.stack(cos_tabs).astype(f32), np.stack(sin_tabs).astype(f32)


def _in_proj(x2d, gain, w_bf16, layer, tables, seq):
    m = x2d.shape[0]
    tiles_per_seq = seq // IN_TM
    tab_spec = pl.BlockSpec((None, IN_TM, LANES),
                            lambda i, j: (_step_lookup(j, TILE_TABLE), i % tiles_per_seq, 0))
    x_specs = [pl.BlockSpec((IN_TM, LANES), lambda i, j, c=c: (i, c)) for c in range(X_CHUNKS)]
    return pl.pallas_call(
        _in_proj_body,
        grid=(m // IN_TM, P_WIDTH // IN_TN),
        in_specs=x_specs + [
            pl.BlockSpec((1, D_MODEL), lambda i, j: (0, 0)),
            pl.BlockSpec((None, D_MODEL, IN_TN), lambda i, j: (layer, 0, j)),
            tab_spec, tab_spec,
        ],
        out_specs=pl.BlockSpec((IN_TM, IN_TN), lambda i, j: (i, j)),
        out_shape=jax.ShapeDtypeStruct((m, P_WIDTH), BF16),
        scratch_shapes=[pltpu.VMEM((N_GROUPS, IN_TM, D_MODEL), BF16),
                        pltpu.VMEM((IN_TM, LANES), F32),
                        pltpu.VMEM((X_CHUNKS, IN_TM, LANES), F32)],
        compiler_params=pltpu.CompilerParams(
            dimension_semantics=("parallel", "arbitrary"), vmem_limit_bytes=VMEM_LIMIT),
    )(*([x2d] * X_CHUNKS), gain, w_bf16, *tables)


A_HEADS_PER_STEP = LANES // A_HEAD_DIM
A_QUAD = 8


def _attn_body(*refs):
    q_in = refs[0:3]
    kc_in = refs[3:6]
    kp_in = refs[6:9]
    vc_in = refs[9:12]
    vp_in = refs[12:15]
    o_ref = refs[15]
    ro, rl, bias, s_buf, p_buf, m_buf = refs[16:22]
    so_buf, sl_buf, sy_buf = refs[22:25]
    jt = pl.program_id(1)

    qi = lax.broadcasted_iota(jnp.int32, (ATT_BLOCK, 2 * ATT_BLOCK), 0)
    kj = lax.broadcasted_iota(jnp.int32, (ATT_BLOCK, 2 * ATT_BLOCK), 1)
    dist = qi + ATT_BLOCK - kj
    band = (dist >= 0) & (dist <= ATT_BLOCK)
    bias[0] = jnp.where(band & (kj >= ATT_BLOCK), 0.0, NEG_INF).astype(BF16)
    bias[1] = jnp.where(band, 0.0, NEG_INF).astype(BF16)
    lane = lax.broadcasted_iota(jnp.int32, (ATT_BLOCK, LANES), 1)

    half = ROT_DIM // 2
    first_head = (lane < half) | ((lane >= ROT_DIM) & (lane < A_HEAD_DIM + half))
    head_lanes = (first_head, jnp.logical_not(first_head))
    first = lane < A_HEAD_DIM
    v_lane = lax.broadcasted_iota(jnp.int32, (2 * ATT_BLOCK, LANES), 1)
    v_lanes = (v_lane < A_HEAD_DIM, v_lane >= A_HEAD_DIM)

    def rows_of(u):
        return pl.ds(u * ATT_BLOCK, ATT_BLOCK)

    def previous(cur, prev, d, u):
        return prev[rows_of(u), :] if u < d else cur[rows_of(u - d), :]

    def stage_scores(slot, g, u):
        d = DILATIONS[g]
        b = bias[jnp.minimum(jt, 1)] if u < d else bias[1]
        q = q_in[g][rows_of(u), :]
        k2 = jnp.concatenate([previous(kc_in[g], kp_in[g], d, u), kc_in[g][rows_of(u), :]], axis=0)
        for h in range(A_HEADS_PER_STEP):
            qm = jnp.where(head_lanes[h], q, jnp.zeros_like(q))
            s = lax.dot_general(qm, k2, (((1,), (1,)), ((), ())), preferred_element_type=F32)
            s_buf[slot, h] = s.astype(BF16) + b

    def stage_softmax(slot, g, u):
        ms = []
        for h in range(A_HEADS_PER_STEP):
            s = s_buf[slot, h]
            m = jnp.max(s, axis=1, keepdims=True)
            p_buf[slot, h] = jnp.exp2(s - m)
            ms.append(m.astype(F32))
        m_buf[slot] = jnp.where(first, ms[0], ms[1])

    def stage_out(slot, g, u):
        d = DILATIONS[g]
        v2 = jnp.concatenate([previous(vc_in[g], vp_in[g], d, u), vc_in[g][rows_of(u), :]], axis=0)
        pvs = [jnp.dot(p_buf[slot, h], jnp.where(v_lanes[h], v2, jnp.ones_like(v2)), preferred_element_type=F32)
               for h in range(A_HEADS_PER_STEP)]
        out = jnp.where(first, pvs[0], pvs[1])
        den = pltpu.roll(jnp.where(first, pvs[1], pvs[0]), A_HEAD_DIM, 1)
        ro[g, rows_of(u), :] = out / den
        rl[g, rows_of(u), :] = m_buf[slot] + jnp.log(den) * LOG2_E

    quads = [(g, qd) for g in range(N_GROUPS) for qd in range(A_UNITS // A_QUAD)]
    always = pl.program_id(0) >= 0
    stages = (stage_scores, stage_softmax, stage_out)
    for trip in range(len(quads) + len(stages) - 1):

        @pl.when(always)
        def _trip(trip=trip):
            for age in range(len(stages) - 1, -1, -1):
                n = trip - age
                if 0 <= n < len(quads):
                    g, qd = quads[n]
                    for i in range(A_QUAD):
                        stages[age]((n % 2) * A_QUAD + i, g, qd * A_QUAD + i)

    d2, d3 = DILATIONS[1], DILATIONS[2]
    for u in range(A_UNITS):
        for src, dst in _residue_pieces(d2, DILATIONS[0], u):
            so_buf[dst, :] = ro[0, src, :]
            sl_buf[dst, :] = rl[0, src, :]
    for r3 in range(A_UNITS):
        blk = pl.ds(r3 * ATT_BLOCK, ATT_BLOCK)
        pieces = list(_residue_pieces(d3, d2, r3))
        o1 = jnp.concatenate([so_buf[src, :] for src, _ in pieces], axis=0)
        l1 = jnp.concatenate([sl_buf[src, :] for src, _ in pieces], axis=0)
        o2 = jnp.concatenate([ro[1, src, :] for src, _ in pieces], axis=0)
        l2 = jnp.concatenate([rl[1, src, :] for src, _ in pieces], axis=0)
        o3, l3 = ro[2, blk, :], rl[2, blk, :]
        mx = jnp.maximum(jnp.maximum(l1, l2), l3)
        e1, e2, e3 = jnp.exp2(l1 - mx), jnp.exp2(l2 - mx), jnp.exp2(l3 - mx)
        y = (e1 * o1 + e2 * o2 + e3 * o3) / (e1 + e2 + e3)
        for k, (src, _) in enumerate(pieces):
            sy_buf[src, :] = y[k * ATT_BLOCK // len(pieces):(k + 1) * ATT_BLOCK // len(pieces), :]
    for u in range(A_UNITS):
        for src, dst in _residue_pieces(d2, DILATIONS[0], u):
            o_ref[src, :] = sy_buf[dst, :]


def _attention_a(proj, batch, seq):
    p = proj.reshape(batch, seq, P_WIDTH)

    def cur_spec(g, part):
        col = (P_QKV[g] + part * A_GROUP_W) // LANES
        return pl.BlockSpec((None, A_SPAN, LANES), lambda b, j, hp: (b, j, col + hp))

    def prev_spec(g, part):
        col = (P_QKV[g] + part * A_GROUP_W) // LANES
        span = ATT_BLOCK * DILATIONS[g]
        per = A_SPAN // span
        return pl.BlockSpec((None, span, LANES),
                            lambda b, j, hp: (b, jnp.maximum(j * per - 1, 0), col + hp))

    groups = range(N_GROUPS)
    in_specs = ([cur_spec(g, 0) for g in groups] + [cur_spec(g, 1) for g in groups]
                + [prev_spec(g, 1) for g in groups] + [cur_spec(g, 2) for g in groups]
                + [prev_spec(g, 2) for g in groups])
    return pl.pallas_call(
        _attn_body,
        grid=(batch, seq // A_SPAN, A_GROUP_W // LANES),
        in_specs=in_specs,
        out_specs=pl.BlockSpec((None, A_SPAN, LANES), lambda b, j, hp: (b, j, hp)),
        out_shape=jax.ShapeDtypeStruct((batch, seq, A_OUT), F32),
        scratch_shapes=[pltpu.VMEM((N_GROUPS, A_SPAN, LANES), F32),
                        pltpu.VMEM((N_GROUPS, A_SPAN, LANES), F32),
                        pltpu.VMEM((2, ATT_BLOCK, 2 * ATT_BLOCK), BF16),
                        pltpu.VMEM((2 * A_QUAD, A_HEADS_PER_STEP, ATT_BLOCK, 2 * ATT_BLOCK), BF16),
                        pltpu.VMEM((2 * A_QUAD, A_HEADS_PER_STEP, ATT_BLOCK, 2 * ATT_BLOCK), BF16),
                        pltpu.VMEM((2 * A_QUAD, ATT_BLOCK, LANES), F32)]
                       + [pltpu.VMEM((A_SPAN, LANES), F32)] * 3,
        compiler_params=pltpu.CompilerParams(
            dimension_semantics=("parallel", "parallel", "parallel"), vmem_limit_bytes=VMEM_LIMIT),
    )(*([p] * 15))


RET_TB = 2048
_LOG_GAMMA = [float(v) for v in np.log1p(-(2.0 ** (-5.0 - np.arange(R_HEADS)))).astype(np.float32)]


def _ret_body(q_ref, k_ref, v_ref, g_ref, o_ref, st_ref):
    h = pl.program_id(1)
    t = pl.program_id(2)

    @pl.when(t == 0)
    def _reset():
        st_ref[...] = jnp.zeros_like(st_ref)

    lg = jnp.float32(_LOG_GAMMA[R_HEADS - 1])
    for hh in range(R_HEADS - 2, -1, -1):
        lg = jnp.where(h == hh, jnp.float32(_LOG_GAMMA[hh]), lg)
    row = lax.broadcasted_iota(jnp.int32, (R_CHUNK, R_CHUNK), 0).astype(F32)
    colv = lax.broadcasted_iota(jnp.int32, (R_CHUNK, R_CHUNK), 1).astype(F32)
    diff = row - colv
    decay = jnp.where(diff >= 0, jnp.exp(diff * lg), 0.0)
    zeta = jnp.exp((R_CHUNK - 1 - row) * lg)
    xi = jnp.exp((row + 1.0) * lg)
    chunk_decay = jnp.exp(jnp.full((R_KEY_DIM, R_VAL_DIM), float(R_CHUNK), F32) * lg)
    kscale = R_KEY_DIM ** -0.5

    for c in range(RET_TB // R_CHUNK):
        sl = slice(c * R_CHUNK, (c + 1) * R_CHUNK)
        q = q_ref[sl, :].astype(F32)
        k = k_ref[sl, :].astype(F32) * kscale
        v = v_ref[sl, :]
        s = lax.dot_general(q.astype(BF16), k.astype(BF16), (((1,), (1,)), ((), ())),
                            preferred_element_type=F32) * decay
        inner = jnp.dot(s.astype(BF16), v, preferred_element_type=F32)
        state = st_ref[...]
        cross = jnp.dot((q * xi).astype(BF16), state.astype(BF16), preferred_element_type=F32)
        kz_t = (k * zeta).T.astype(BF16)
        st_ref[...] = state * chunk_decay + jnp.dot(kz_t, v, preferred_element_type=F32)
        out = inner + cross
        ms = jnp.mean(out * out, axis=-1, keepdims=True)
        gate = g_ref[sl, :].astype(F32)
        o_ref[sl, :] = (gate * jax.nn.sigmoid(gate) * (out * lax.rsqrt(ms + NORM_EPS))).astype(BF16)


def _retention(proj, batch, seq):
    p = proj.reshape(batch, seq, P_WIDTH)
    return pl.pallas_call(
        _ret_body,
        grid=(batch, R_HEADS, seq // RET_TB),
        in_specs=[
            pl.BlockSpec((None, RET_TB, R_KEY_DIM), lambda b, h, t: (b, t, P_QR // R_KEY_DIM + h)),
            pl.BlockSpec((None, RET_TB, R_KEY_DIM), lambda b, h, t: (b, t, P_KR // R_KEY_DIM + h)),
            pl.BlockSpec((None, RET_TB, R_VAL_DIM), lambda b, h, t: (b, t, P_VR // R_VAL_DIM + h)),
            pl.BlockSpec((None, RET_TB, R_VAL_DIM), lambda b, h, t: (b, t, P_GR // R_VAL_DIM + h)),
        ],
        out_specs=pl.BlockSpec((None, RET_TB, R_VAL_DIM), lambda b, h, t: (b, t, h)),
        out_shape=jax.ShapeDtypeStruct((batch, seq, R_V), BF16),
        scratch_shapes=[pltpu.VMEM((R_KEY_DIM, R_VAL_DIM), F32)],
        compiler_params=pltpu.CompilerParams(
            dimension_semantics=("parallel", "parallel", "arbitrary"), vmem_limit_bytes=VMEM_LIMIT),
    )(p, p, p, p)


MERGE_TM = 1024


def _merge_body(x_ref, ya_ref, rb_ref, ga_ref, gb_ref, wa_ref, wb_ref, wo_ref, o_ref):
    ya = jnp.dot(ya_ref[...].astype(BF16), wa_ref[...], preferred_element_type=F32)
    yb = jnp.dot(rb_ref[...], wb_ref[...], preferred_element_type=F32)
    ga = ga_ref[...].astype(F32)
    gb = gb_ref[...].astype(F32)
    merged = jax.nn.sigmoid(ga) * ya + jax.nn.sigmoid(gb) * yb
    o_ref[...] = x_ref[...] + jnp.dot(merged.astype(BF16), wo_ref[...], preferred_element_type=F32)


def _merge(x2d, ya2d, rb2d, proj, wa, wb, wo, layer):
    m = x2d.shape[0]

    def full(shape):
        return pl.BlockSpec((None,) + shape, lambda i: (layer, 0, 0))

    return pl.pallas_call(
        _merge_body,
        grid=(m // MERGE_TM,),
        in_specs=[
            pl.BlockSpec((MERGE_TM, D_MODEL), lambda i: (i, 0)),
            pl.BlockSpec((MERGE_TM, A_OUT), lambda i: (i, 0)),
            pl.BlockSpec((MERGE_TM, R_V), lambda i: (i, 0)),
            pl.BlockSpec((MERGE_TM, D_MODEL), lambda i: (i, P_GA // D_MODEL)),
            pl.BlockSpec((MERGE_TM, D_MODEL), lambda i: (i, P_GB // D_MODEL)),
            full((A_OUT, D_MODEL)), full((R_V, D_MODEL)), full((D_MODEL, D_MODEL)),
        ],
        out_specs=pl.BlockSpec((MERGE_TM, D_MODEL), lambda i: (i, 0)),
        out_shape=jax.ShapeDtypeStruct((m, D_MODEL), F32),
        compiler_params=pltpu.CompilerParams(
            dimension_semantics=("parallel",), vmem_limit_bytes=VMEM_LIMIT),
    )(x2d, ya2d, rb2d, proj, proj, wa, wb, wo)


FFN_TM = 1024
FFN_TF = 2048


def _ffn_body(x_ref, g_ref, wu_ref, wd_ref, fg_ref, o_ref, h_ref, *, final):
    c = pl.program_id(1)

    @pl.when(c == 0)
    def _start():
        x = x_ref[...]
        h_ref[...] = _rms(x, g_ref[...]).astype(BF16)
        o_ref[...] = x

    up = jnp.dot(h_ref[...], wu_ref[...], preferred_element_type=F32)
    up = jnp.square(jnp.maximum(up, 0.0)).astype(BF16)
    o_ref[...] += jnp.dot(up, wd_ref[...], preferred_element_type=F32)

    if final:
        @pl.when(c == pl.num_programs(1) - 1)
        def _final_norm():
            o_ref[...] = _rms(o_ref[...], fg_ref[...])


def _ffn(x2d, gain, wu, wd, layer, final_gain, final):
    m = x2d.shape[0]
    return pl.pallas_call(
        functools.partial(_ffn_body, final=final),
        grid=(m // FFN_TM, D_FF // FFN_TF),
        in_specs=[
            pl.BlockSpec((FFN_TM, D_MODEL), lambda i, c: (i, 0)),
            pl.BlockSpec((1, D_MODEL), lambda i, c: (0, 0)),
            pl.BlockSpec((None, D_MODEL, FFN_TF), lambda i, c: (layer, 0, c)),
            pl.BlockSpec((None, FFN_TF, D_MODEL), lambda i, c: (layer, c, 0)),
            pl.BlockSpec((1, D_MODEL), lambda i, c: (0, 0)),
        ],
        out_specs=pl.BlockSpec((FFN_TM, D_MODEL), lambda i, c: (i, 0)),
        out_shape=jax.ShapeDtypeStruct((m, D_MODEL), F32),
        scratch_shapes=[pltpu.VMEM((FFN_TM, D_MODEL), BF16)],
        compiler_params=pltpu.CompilerParams(
            dimension_semantics=("parallel", "arbitrary"), vmem_limit_bytes=VMEM_LIMIT),
    )(x2d, gain, wu, wd, final_gain)


def kernel(x, mix_norm, w_in, w_a, w_b, w_o, ffn_norm, w_up, w_down, final_norm):
    batch, seq, _ = x.shape
    depth = w_in.shape[0]
    assert seq % A_SPAN == 0 and seq % RET_TB == 0
    tables = _rotary_tables(seq)
    final_gain = final_norm.reshape(1, D_MODEL)
    x2d = x.reshape(batch * seq, D_MODEL)
    w_in, w_a, w_b, w_o, w_up, w_down = (_arrange_w_in(w_in),) + tuple(
        w.astype(BF16) for w in (w_a, w_b, w_o, w_up, w_down))
    for layer in range(depth):
        proj = _in_proj(x2d, mix_norm[layer].reshape(1, D_MODEL), w_in, layer, tables, seq)
        ya = _attention_a(proj, batch, seq).reshape(batch * seq, A_OUT)
        rb = _retention(proj, batch, seq).reshape(batch * seq, R_V)
        x2d = _merge(x2d, ya, rb, proj, w_a, w_b, w_o, layer)
        x2d = _ffn(x2d, ffn_norm[layer].reshape(1, D_MODEL), w_up, w_down, layer, final_gain,
                   final=(layer == depth - 1))
    return x2d.reshape(batch, seq, D_MODEL)
```

```python
import functools

import numpy as np
import jax
import jax.numpy as jnp
from jax import lax
from jax.experimental import pallas as pl
from jax.experimental.pallas import tpu as pltpu

F32 = jnp.float32
BF16 = jnp.bfloat16

D_MODEL = 1024
N_GROUPS = 3
DILATIONS = (1, 4, 16)
A_HEADS = 8
A_HEAD_DIM = 64
A_GROUP_W = A_HEADS * A_HEAD_DIM
A_WIDTH = N_GROUPS * A_GROUP_W
A_OUT = A_GROUP_W
ROT_DIM = A_HEAD_DIM // 4
ROPE_THETA = 500000.0
ATT_BLOCK = 128
NEG_INF = -1e30
R_HEADS = 4
R_KEY_DIM = 128
R_VAL_DIM = 256
R_QK = R_HEADS * R_KEY_DIM
R_V = R_HEADS * R_VAL_DIM
R_CHUNK = 128
R_ROT_BASE = 10000.0
D_FF = 4 * D_MODEL
NORM_EPS = 1e-6

OFF_QA = 0
OFF_KA = A_WIDTH
OFF_VA = 2 * A_WIDTH
OFF_QR = 3 * A_WIDTH
OFF_KR = OFF_QR + R_QK
OFF_VR = OFF_KR + R_QK
OFF_GR = OFF_VR + R_V
OFF_GA = OFF_GR + R_V
OFF_GB = OFF_GA + D_MODEL

LANES = 128
VMEM_LIMIT = 56 * 1024 * 1024

A_SPAN = ATT_BLOCK * max(DILATIONS)
A_UNITS = A_SPAN // ATT_BLOCK

IN_TM = A_SPAN
IN_TN = 512
IN_CHUNKS = (512, 512, 512, 256, 256)
X_CHUNKS = D_MODEL // LANES

P_GA = 0
P_GB = P_GA + D_MODEL
P_VR = P_GB + D_MODEL
P_GR = P_VR + R_V
P_QR = P_GR + R_V
P_KR = P_QR + R_QK
P_QKV = tuple(P_KR + R_QK + g * 3 * A_GROUP_W for g in range(N_GROUPS))
P_WIDTH = P_QKV[-1] + 3 * A_GROUP_W

_TAB_R = 2 * N_GROUPS
_TAB_NONE = 2 * N_GROUPS + 1
TILE_TABLE = ((_TAB_NONE,) * (P_QR // IN_TN) + (_TAB_R,) * (2 * R_QK // IN_TN)
              + tuple(t for g in range(N_GROUPS) for t in (g, N_GROUPS + g, _TAB_NONE)))
LOG2_E = 1.4426950408889634
A_Q_SCALE = A_HEAD_DIM ** -0.5 * LOG2_E
TILE_ORDER = (0,) * (P_QKV[0] // IN_TN) + tuple(g for g in range(N_GROUPS) for _ in range(3))
N_TILES = P_WIDTH // IN_TN


def _rms(x, gain):
    ms = jnp.mean(x * x, axis=-1, keepdims=True)
    return x * lax.rsqrt(ms + NORM_EPS) * gain


def _step_lookup(j, values):
    out = values[-1]
    for t in range(len(values) - 2, -1, -1):
        if values[t] != values[t + 1]:
            out = jnp.where(j <= t, values[t], out)
    return out


def _residue_pieces(dil, src_dil, u):
    ratio = dil // src_dil
    piece = ATT_BLOCK // ratio
    ss, r = divmod(u, dil)
    for k in range(ratio):
        src_block = (ss * ratio + k) * src_dil + r % src_dil
        yield (pl.ds(src_block * ATT_BLOCK + r // src_dil, piece, stride=ratio),
               pl.ds(u * ATT_BLOCK + k * piece, piece))


def _in_proj_body(*refs):
    x_refs = refs[:X_CHUNKS]
    g_ref, w_ref, cos_ref, sin_ref, o_ref, h_ref, inv_ref, stage_ref = refs[X_CHUNKS:]
    j = pl.program_id(1)

    @pl.when(j == 0)
    def _norm():
        gains = [g_ref[:, c * LANES:(c + 1) * LANES] for c in range(X_CHUNKS)]
        for u in range(A_UNITS):
            rows = pl.ds(u * ATT_BLOCK, ATT_BLOCK)
            xc = [x_refs[c][rows, :] for c in range(X_CHUNKS)]
            sq = xc[0] * xc[0]
            for c in range(1, X_CHUNKS):
                sq = sq + xc[c] * xc[c]
            inv = lax.rsqrt(jnp.sum(sq, axis=-1, keepdims=True) * (1.0 / D_MODEL) + NORM_EPS)
            inv = jnp.broadcast_to(inv, (ATT_BLOCK, LANES))
            inv_ref[rows, :] = inv
            for c in range(X_CHUNKS):
                h_ref[0, rows, c * LANES:(c + 1) * LANES] = (xc[c] * inv * gains[c]).astype(BF16)
        for g in range(1, N_GROUPS):
            for u in range(A_UNITS):
                for src, dst in _residue_pieces(DILATIONS[g], DILATIONS[g - 1], u):
                    if g == 1:
                        inv = inv_ref[src, :]
                    for c in range(X_CHUNKS):
                        lanes = slice(c * LANES, (c + 1) * LANES)
                        if g == 1:
                            y = x_refs[c][src, :] * inv * gains[c]
                            if g + 1 < N_GROUPS:
                                stage_ref[c, dst, :] = y
                        else:
                            y = stage_ref[c, src, :]
                        h_ref[g, dst, lanes] = y.astype(BF16)

    def project(rotary, order):
        starts = np.cumsum((0,) + IN_CHUNKS)
        for start, size in zip(starts, IN_CHUNKS):
            rows = pl.ds(int(start), size)
            acc = jnp.dot(h_ref[order, rows, :], w_ref[...], preferred_element_type=F32)
            if not rotary:
                o_ref[rows, :] = acc.astype(BF16)
                continue
            for c in range(IN_TN // LANES):
                t = acc[:, c * LANES:(c + 1) * LANES]
                o_ref[rows, c * LANES:(c + 1) * LANES] = (
                    t * cos_ref[rows, :] + pltpu.roll(t, LANES // 2, 1) * sin_ref[rows, :]).astype(BF16)

    order = _step_lookup(j, TILE_ORDER)
    has_rotary = _step_lookup(j, tuple(int(t != _TAB_NONE) for t in TILE_TABLE)) == 1
    pl.when(has_rotary)(functools.partial(project, True, order))
    pl.when(jnp.logical_not(has_rotary))(functools.partial(project, False, order))


_SRC_TILE = tuple(off // IN_TN + t for off, n in ((OFF_GA, 2), (OFF_GB, 2), (OFF_VR, 2), (OFF_GR, 2), (OFF_QR, 1),
                                                   (OFF_KR, 1)) for t in range(n)) + tuple(
    (off + g * A_GROUP_W) // IN_TN for g in range(N_GROUPS) for off in (OFF_QA, OFF_KA, OFF_VA))
_PAIR_TILE = tuple(int(t < 2 * N_GROUPS) for t in TILE_TABLE)


def _arrange_body(w_ref, o_ref):
    j = pl.program_id(1)
    pair = _step_lookup(j, _PAIR_TILE) == 1

    @pl.when(jnp.logical_not(pair))
    def _plain():
        o_ref[...] = w_ref[...].astype(BF16)

    @pl.when(pair)
    def _pair():
        half = ROT_DIM // 2
        lane = lax.broadcasted_iota(jnp.int32, (D_MODEL, LANES), 1)
        to_b_x1 = (lane >= half) & (lane < ROT_DIM)
        to_a_x2 = (lane >= A_HEAD_DIM) & (lane < A_HEAD_DIM + half)
        for c in range(IN_TN // LANES):
            t = w_ref[:, c * LANES:(c + 1) * LANES]
            moved = jnp.where(to_b_x1, pltpu.roll(t, LANES - A_HEAD_DIM + half, 1),
                              pltpu.roll(t, A_HEAD_DIM - half, 1))
            o_ref[:, c * LANES:(c + 1) * LANES] = jnp.where(to_b_x1 | to_a_x2, moved, t).astype(BF16)


def _arrange_w_in(w):
    depth = w.shape[0]
    return pl.pallas_call(
        _arrange_body,
        grid=(depth, N_TILES),
        in_specs=[pl.BlockSpec((None, D_MODEL, IN_TN), lambda l, j: (l, 0, _step_lookup(j, _SRC_TILE)))],
        out_specs=pl.BlockSpec((None, D_MODEL, IN_TN), lambda l, j: (l, 0, j)),
        out_shape=jax.ShapeDtypeStruct((depth, D_MODEL, P_WIDTH), BF16),
        compiler_params=pltpu.CompilerParams(
            dimension_semantics=("parallel", "parallel"), vmem_limit_bytes=VMEM_LIMIT),
    )(w)


@functools.lru_cache(maxsize=None)
def _rotary_tables(seq):
    f32 = np.float32
    half = ROT_DIM // 2
    inv = (1.0 / (f32(ROPE_THETA) ** (np.arange(half, dtype=f32) / f32(half)))).astype(f32)
    rest = A_HEAD_DIM - ROT_DIM
    ones, zeros = np.ones((seq, rest), f32), np.zeros((seq, rest), f32)
    cos_tabs, sin_tabs = [], []
    for dil in DILATIONS:
        span = ATT_BLOCK * dil
        pos = np.arange(seq, dtype=f32).reshape(seq // span, ATT_BLOCK, dil).transpose(0, 2, 1).reshape(seq, 1)
        ang = pos * inv[None, :]
        cos, sin = np.cos(ang), np.sin(ang)
        cos_tabs.append(np.concatenate([cos, cos, ones, cos, cos, ones], axis=1))
        sin_tabs.append(np.concatenate([-sin, -sin, zeros, sin, sin, zeros], axis=1))
    cos_tabs = [t * f32(A_Q_SCALE) for t in cos_tabs] + cos_tabs
    sin_tabs = [t * f32(A_Q_SCALE) for t in sin_tabs] + sin_tabs
    half_r = R_KEY_DIM // 2
    inv_r = (1.0 / (f32(R_ROT_BASE) ** (np.arange(half_r, dtype=f32) / f32(half_r)))).astype(f32)
    ang_r = np.arange(seq, dtype=f32)[:, None] * inv_r[None, :]
    cos_tabs += [np.concatenate([np.cos(ang_r), np.cos(ang_r)], axis=1), np.ones((seq, LANES), f32)]
    sin_tabs += [np.concatenate([-np.sin(ang_r), np.sin(ang_r)], axis=1), np.zeros((seq, LANES), f32)]
    return np.stack(cos_tabs).astype(f32), np.stack(sin_tabs).astype(f32)


def _in_proj(x2d, gain, w_bf16, layer, tables, seq):
    m = x2d.shape[0]
    tiles_per_seq = seq // IN_TM
    tab_spec = pl.BlockSpec((None, IN_TM, LANES),
                            lambda i, j: (_step_lookup(j, TILE_TABLE), i % tiles_per_seq, 0))
    x_specs = [pl.BlockSpec((IN_TM, LANES), lambda i, j, c=c: (i, c)) for c in range(X_CHUNKS)]
    return pl.pallas_call(
        _in_proj_body,
        grid=(m // IN_TM, P_WIDTH // IN_TN),
        in_specs=x_specs + [
            pl.BlockSpec((1, D_MODEL), lambda i, j: (0, 0)),
            pl.BlockSpec((None, D_MODEL, IN_TN), lambda i, j: (layer, 0, j)),
            tab_spec, tab_spec,
        ],
        out_specs=pl.BlockSpec((IN_TM, IN_TN), lambda i, j: (i, j)),
        out_shape=jax.ShapeDtypeStruct((m, P_WIDTH), BF16),
        scratch_shapes=[pltpu.VMEM((N_GROUPS, IN_TM, D_MODEL), BF16),
                        pltpu.VMEM((IN_TM, LANES), F32),
                        pltpu.VMEM((X_CHUNKS, IN_TM, LANES), F32)],
        compiler_params=pltpu.CompilerParams(
            dimension_semantics=("parallel", "arbitrary"), vmem_limit_bytes=VMEM_LIMIT),
    )(*([x2d] * X_CHUNKS), gain, w_bf16, *tables)


A_HEADS_PER_STEP = LANES // A_HEAD_DIM
A_QUAD = 8


def _attn_body(*refs):
    q_in = refs[0:3]
    kc_in = refs[3:6]
    kp_in = refs[6:9]
    vc_in = refs[9:12]
    vp_in = refs[12:15]
    o_ref = refs[15]
    ro, rl, bias, s_buf, p_buf, m_buf = refs[16:22]
    so_buf, sl_buf, sy_buf = refs[22:25]
    jt = pl.program_id(1)

    qi = lax.broadcasted_iota(jnp.int32, (ATT_BLOCK, 2 * ATT_BLOCK), 0)
    kj = lax.broadcasted_iota(jnp.int32, (ATT_BLOCK, 2 * ATT_BLOCK), 1)
    dist = qi + ATT_BLOCK - kj
    band = (dist >= 0) & (dist <= ATT_BLOCK)
    bias[0] = jnp.where(band & (kj >= ATT_BLOCK), 0.0, NEG_INF).astype(BF16)
    bias[1] = jnp.where(band, 0.0, NEG_INF).astype(BF16)
    lane = lax.broadcasted_iota(jnp.int32, (ATT_BLOCK, LANES), 1)

    half = ROT_DIM // 2
    first_head = (lane < half) | ((lane >= ROT_DIM) & (lane < A_HEAD_DIM + half))
    head_lanes = (first_head, jnp.logical_not(first_head))
    first = lane < A_HEAD_DIM
    v_lane = lax.broadcasted_iota(jnp.int32, (2 * ATT_BLOCK, LANES), 1)
    v_lanes = (v_lane < A_HEAD_DIM, v_lane >= A_HEAD_DIM)

    def rows_of(u):
        return pl.ds(u * ATT_BLOCK, ATT_BLOCK)

    def previous(cur, prev, d, u):
        return prev[rows_of(u), :] if u < d else cur[rows_of(u - d), :]

    def stage_scores(slot, g, u):
        d = DILATIONS[g]
        b = bias[jnp.minimum(jt, 1)] if u < d else bias[1]
        q = q_in[g][rows_of(u), :]
        k2 = jnp.concatenate([previous(kc_in[g], kp_in[g], d, u), kc_in[g][rows_of(u), :]], axis=0)
        for h in range(A_HEADS_PER_STEP):
            qm = jnp.where(head_lanes[h], q, jnp.zeros_like(q))
            s = lax.dot_general(qm, k2, (((1,), (1,)), ((), ())), preferred_element_type=F32)
            s_buf[slot, h] = s.astype(BF16) + b

    def stage_softmax(slot, g, u):
        ms = []
        for h in range(A_HEADS_PER_STEP):
            s = s_buf[slot, h]
            m = jnp.max(s, axis=1, keepdims=True)
            p_buf[slot, h] = jnp.exp2(s - m)
            ms.append(m.astype(F32))
        m_buf[slot] = jnp.where(first, ms[0], ms[1])

    def stage_out(slot, g, u):
        d = DILATIONS[g]
        v2 = jnp.concatenate([previous(vc_in[g], vp_in[g], d, u), vc_in[g][rows_of(u), :]], axis=0)
        pvs = [jnp.dot(p_buf[slot, h], jnp.where(v_lanes[h], v2, jnp.ones_like(v2)), preferred_element_type=F32)
               for h in range(A_HEADS_PER_STEP)]
        out = jnp.where(first, pvs[0], pvs[1])
        den = pltpu.roll(jnp.where(first, pvs[1], pvs[0]), A_HEAD_DIM, 1)
        ro[g, rows_of(u), :] = out / den
        rl[g, rows_of(u), :] = m_buf[slot] + jnp.log(den) * LOG2_E

    quads = [(g, qd) for g in range(N_GROUPS) for qd in range(A_UNITS // A_QUAD)]
    always = pl.program_id(0) >= 0
    stages = (stage_scores, stage_softmax, stage_out)
    for trip in range(len(quads) + len(stages) - 1):

        @pl.when(always)
        def _trip(trip=trip):
            for age in range(len(stages) - 1, -1, -1):
                n = trip - age
                if 0 <= n < len(quads):
                    g, qd = quads[n]
                    for i in range(A_QUAD):
                        stages[age]((n % 2) * A_QUAD + i, g, qd * A_QUAD + i)

    d2, d3 = DILATIONS[1], DILATIONS[2]
    for u in range(A_UNITS):
        for src, dst in _residue_pieces(d2, DILATIONS[0], u):
            so_buf[dst, :] = ro[0, src, :]
            sl_buf[dst, :] = rl[0, src, :]
    for r3 in range(A_UNITS):
        blk = pl.ds(r3 * ATT_BLOCK, ATT_BLOCK)
        pieces = list(_residue_pieces(d3, d2, r3))
        o1 = jnp.concatenate([so_buf[src, :] for src, _ in pieces], axis=0)
        l1 = jnp.concatenate([sl_buf[src, :] for src, _ in pieces], axis=0)
        o2 = jnp.concatenate([ro[1, src, :] for src, _ in pieces], axis=0)
        l2 = jnp.concatenate([rl[1, src, :] for src, _ in pieces], axis=0)
        o3, l3 = ro[2, blk, :], rl[2, blk, :]
        mx = jnp.maximum(jnp.maximum(l1, l2), l3)
        e1, e2, e3 = jnp.exp2(l1 - mx), jnp.exp2(l2 - mx), jnp.exp2(l3 - mx)
        y = (e1 * o1 + e2 * o2 + e3 * o3) / (e1 + e2 + e3)
        for k, (src, _) in enumerate(pieces):
            sy_buf[src, :] = y[k * ATT_BLOCK // len(pieces):(k + 1) * ATT_BLOCK // len(pieces), :]
    for u in range(A_UNITS):
        for src, dst in _residue_pieces(d2, DILATIONS[0], u):
            o_ref[src, :] = sy_buf[dst, :]


def _attention_a(proj, batch, seq):
    p = proj.reshape(batch, seq, P_WIDTH)

    def cur_spec(g, part):
        col = (P_QKV[g] + part * A_GROUP_W) // LANES
        return pl.BlockSpec((None, A_SPAN, LANES), lambda b, j, hp: (b, j, col + hp))

    def prev_spec(g, part):
        col = (P_QKV[g] + part * A_GROUP_W) // LANES
        span = ATT_BLOCK * DILATIONS[g]
        per = A_SPAN // span
        return pl.BlockSpec((None, span, LANES),
                            lambda b, j, hp: (b, jnp.maximum(j * per - 1, 0), col + hp))

    groups = range(N_GROUPS)
    in_specs = ([cur_spec(g, 0) for g in groups] + [cur_spec(g, 1) for g in groups]
                + [prev_spec(g, 1) for g in groups] + [cur_spec(g, 2) for g in groups]
                + [prev_spec(g, 2) for g in groups])
    return pl.pallas_call(
        _attn_body,
        grid=(batch, seq // A_SPAN, A_GROUP_W // LANES),
        in_specs=in_specs,
        out_specs=pl.BlockSpec((None, A_SPAN, LANES), lambda b, j, hp: (b, j, hp)),
        out_shape=jax.ShapeDtypeStruct((batch, seq, A_OUT), F32),
        scratch_shapes=[pltpu.VMEM((N_GROUPS, A_SPAN, LANES), F32),
                        pltpu.VMEM((N_GROUPS, A_SPAN, LANES), F32),
                        pltpu.VMEM((2, ATT_BLOCK, 2 * ATT_BLOCK), BF16),
                        pltpu.VMEM((2 * A_QUAD, A_HEADS_PER_STEP, ATT_BLOCK, 2 * ATT_BLOCK), BF16),
                        pltpu.VMEM((2 * A_QUAD, A_HEADS_PER_STEP, ATT_BLOCK, 2 * ATT_BLOCK), BF16),
                        pltpu.VMEM((2 * A_QUAD, ATT_BLOCK, LANES), F32)]
                       + [pltpu.VMEM((A_SPAN, LANES), F32)] * 3,
        compiler_params=pltpu.CompilerParams(
            dimension_semantics=("parallel", "parallel", "parallel"), vmem_limit_bytes=VMEM_LIMIT),
    )(*([p] * 15))


RET_TB = 4096
_LOG_GAMMA = [float(v) for v in np.log1p(-(2.0 ** (-5.0 - np.arange(R_HEADS)))).astype(np.float32)]


def _ret_body(q_ref, k_ref, v_ref, g_ref, o_ref, st_ref):
    h = pl.program_id(1)
    t = pl.program_id(2)

    @pl.when(t == 0)
    def _reset():
        st_ref[...] = jnp.zeros_like(st_ref)

    lg = jnp.float32(_LOG_GAMMA[R_HEADS - 1])
    for hh in range(R_HEADS - 2, -1, -1):
        lg = jnp.where(h == hh, jnp.float32(_LOG_GAMMA[hh]), lg)
    row = lax.broadcasted_iota(jnp.int32, (R_CHUNK, R_CHUNK), 0).astype(F32)
    colv = lax.broadcasted_iota(jnp.int32, (R_CHUNK, R_CHUNK), 1).astype(F32)
    diff = row - colv
    decay = jnp.where(diff >= 0, jnp.exp(diff * lg), 0.0)
    zeta = jnp.exp((R_CHUNK - 1 - row) * lg)
    xi = jnp.exp((row + 1.0) * lg)
    chunk_decay = jnp.exp(jnp.full((R_KEY_DIM, R_VAL_DIM), float(R_CHUNK), F32) * lg)
    kscale = R_KEY_DIM ** -0.5

    for c in range(RET_TB // R_CHUNK):
        sl = slice(c * R_CHUNK, (c + 1) * R_CHUNK)
        q = q_ref[sl, :].astype(F32)
        k = k_ref[sl, :].astype(F32) * kscale
        v = v_ref[sl, :]
        s = lax.dot_general(q.astype(BF16), k.astype(BF16), (((1,), (1,)), ((), ())),
                            preferred_element_type=F32) * decay
        inner = jnp.dot(s.astype(BF16), v, preferred_element_type=F32)
        state = st_ref[...]
        cross = jnp.dot((q * xi).astype(BF16), state.astype(BF16), preferred_element_type=F32)
        kz_t = (k * zeta).T.astype(BF16)
        st_ref[...] = state * chunk_decay + jnp.dot(kz_t, v, preferred_element_type=F32)
        out = inner + cross
        ms = jnp.mean(out * out, axis=-1, keepdims=True)
        gate = g_ref[sl, :].astype(F32)
        o_ref[sl, :] = (gate * jax.nn.sigmoid(gate) * (out * lax.rsqrt(ms + NORM_EPS))).astype(BF16)


def _retention(proj, batch, seq):
    p = proj.reshape(batch, seq, P_WIDTH)
    return pl.pallas_call(
        _ret_body,
        grid=(batch, R_HEADS, seq // RET_TB),
        in_specs=[
            pl.BlockSpec((None, RET_TB, R_KEY_DIM), lambda b, h, t: (b, t, P_QR // R_KEY_DIM + h)),
            pl.BlockSpec((None, RET_TB, R_KEY_DIM), lambda b, h, t: (b, t, P_KR // R_KEY_DIM + h)),
            pl.BlockSpec((None, RET_TB, R_VAL_DIM), lambda b, h, t: (b, t, P_VR // R_VAL_DIM + h)),
            pl.BlockSpec((None, RET_TB, R_VAL_DIM), lambda b, h, t: (b, t, P_GR // R_VAL_DIM + h)),
        ],
        out_specs=pl.BlockSpec((None, RET_TB, R_VAL_DIM), lambda b, h, t: (b, t, h)),
        out_shape=jax.ShapeDtypeStruct((batch, seq, R_V), BF16),
        scratch_shapes=[pltpu.VMEM((R_KEY_DIM, R_VAL_DIM), F32)],
        compiler_params=pltpu.CompilerParams(
            dimension_semantics=("parallel", "parallel", "arbitrary"), vmem_limit_bytes=VMEM_LIMIT),
    )(p, p, p, p)


MERGE_TM = 1024


def _merge_body(x_ref, ya_ref, rb_ref, ga_ref, gb_ref, wa_ref, wb_ref, wo_ref, o_ref):
    ya = jnp.dot(ya_ref[...].astype(BF16), wa_ref[...], preferred_element_type=F32)
    yb = jnp.dot(rb_ref[...], wb_ref[...], preferred_element_type=F32)
    ga = ga_ref[...].astype(F32)
    gb = gb_ref[...].astype(F32)
    merged = jax.nn.sigmoid(ga) * ya + jax.nn.sigmoid(gb) * yb
    o_ref[...] = x_ref[...] + jnp.dot(merged.astype(BF16), wo_ref[...], preferred_element_type=F32)


def _merge(x2d, ya2d, rb2d, proj, wa, wb, wo, layer):
    m = x2d.shape[0]

    def full(shape):
        return pl.BlockSpec((None,) + shape, lambda i: (layer, 0, 0))

    return pl.pallas_call(
        _merge_body,
        grid=(m // MERGE_TM,),
        in_specs=[
            pl.BlockSpec((MERGE_TM, D_MODEL), lambda i: (i, 0)),
            pl.BlockSpec((MERGE_TM, A_OUT), lambda i: (i, 0)),
            pl.BlockSpec((MERGE_TM, R_V), lambda i: (i, 0)),
            pl.BlockSpec((MERGE_TM, D_MODEL), lambda i: (i, P_GA // D_MODEL)),
            pl.BlockSpec((MERGE_TM, D_MODEL), lambda i: (i, P_GB // D_MODEL)),
            full((A_OUT, D_MODEL)), full((R_V, D_MODEL)), full((D_MODEL, D_MODEL)),
        ],
        out_specs=pl.BlockSpec((MERGE_TM, D_MODEL), lambda i: (i, 0)),
        out_shape=jax.ShapeDtypeStruct((m, D_MODEL), F32),
        compiler_params=pltpu.CompilerParams(
            dimension_semantics=("parallel",), vmem_limit_bytes=VMEM_LIMIT),
    )(x2d, ya2d, rb2d, proj, proj, wa, wb, wo)


FFN_TM = 1024
FFN_TF = 2048


def _ffn_body(x_ref, g_ref, wu_ref, wd_ref, fg_ref, o_ref, h_ref, *, final):
    c = pl.program_id(1)

    @pl.when(c == 0)
    def _start():
        x = x_ref[...]
        h_ref[...] = _rms(x, g_ref[...]).astype(BF16)
        o_ref[...] = x

    up = jnp.dot(h_ref[...], wu_ref[...], preferred_element_type=F32)
    up = jnp.square(jnp.maximum(up, 0.0)).astype(BF16)
    o_ref[...] += jnp.dot(up, wd_ref[...], preferred_element_type=F32)

    if final:
        @pl.when(c == pl.num_programs(1) - 1)
        def _final_norm():
            o_ref[...] = _rms(o_ref[...], fg_ref[...])


def _ffn(x2d, gain, wu, wd, layer, final_gain, final):
    m = x2d.shape[0]
    return pl.pallas_call(
        functools.partial(_ffn_body, final=final),
        grid=(m // FFN_TM, D_FF // FFN_TF),
        in_specs=[
            pl.BlockSpec((FFN_TM, D_MODEL), lambda i, c: (i, 0)),
            pl.BlockSpec((1, D_MODEL), lambda i, c: (0, 0)),
            pl.BlockSpec((None, D_MODEL, FFN_TF), lambda i, c: (layer, 0, c)),
            pl.BlockSpec((None, FFN_TF, D_MODEL), lambda i, c: (layer, c, 0)),
            pl.BlockSpec((1, D_MODEL), lambda i, c: (0, 0)),
        ],
        out_specs=pl.BlockSpec((FFN_TM, D_MODEL), lambda i, c: (i, 0)),
        out_shape=jax.ShapeDtypeStruct((m, D_MODEL), F32),
        scratch_shapes=[pltpu.VMEM((FFN_TM, D_MODEL), BF16)],
        compiler_params=pltpu.CompilerParams(
            dimension_semantics=("parallel", "arbitrary"), vmem_limit_bytes=VMEM_LIMIT),
    )(x2d, gain, wu, wd, final_gain)


def kernel(x, mix_norm, w_in, w_a, w_b, w_o, ffn_norm, w_up, w_down, final_norm):
    batch, seq, _ = x.shape
    depth = w_in.shape[0]
    assert seq % A_SPAN == 0 and seq % RET_TB == 0
    tables = _rotary_tables(seq)
    final_gain = final_norm.reshape(1, D_MODEL)
    x2d = x.reshape(batch * seq, D_MODEL)
    w_in, w_a, w_b, w_o, w_up, w_down = (_arrange_w_in(w_in),) + tuple(
        w.astype(BF16) for w in (w_a, w_b, w_o, w_up, w_down))
    for layer in range(depth):
        proj = _in_proj(x2d, mix_norm[layer].reshape(1, D_MODEL), w_in, layer, tables, seq)
        ya = _attention_a(proj, batch, seq).reshape(batch * seq, A_OUT)
        rb = _retention(proj, batch, seq).reshape(batch * seq, R_V)
        x2d = _merge(x2d, ya, rb, proj, w_a, w_b, w_o, layer)
        x2d = _ffn(x2d, ffn_norm[layer].reshape(1, D_MODEL), w_up, w_down, layer, final_gain,
                   final=(layer == depth - 1))
    return x2d.reshape(batch, seq, D_MODEL)
```

```python
import functools

import numpy as np
import jax
import jax.numpy as jnp
from jax import lax
from jax.experimental import pallas as pl
from jax.experimental.pallas import tpu as pltpu

F32 = jnp.float32
BF16 = jnp.bfloat16

D_MODEL = 1024
N_GROUPS = 3
DILATIONS = (1, 4, 16)
A_HEADS = 8
A_HEAD_DIM = 64
A_GROUP_W = A_HEADS * A_HEAD_DIM
A_WIDTH = N_GROUPS * A_GROUP_W
A_OUT = A_GROUP_W
ROT_DIM = A_HEAD_DIM // 4
ROPE_THETA = 500000.0
ATT_BLOCK = 128
NEG_INF = -1e30
R_HEADS = 4
R_KEY_DIM = 128
R_VAL_DIM = 256
R_QK = R_HEADS * R_KEY_DIM
R_V = R_HEADS * R_VAL_DIM
R_CHUNK = 128
R_ROT_BASE = 10000.0
D_FF = 4 * D_MODEL
NORM_EPS = 1e-6

OFF_QA = 0
OFF_KA = A_WIDTH
OFF_VA = 2 * A_WIDTH
OFF_QR = 3 * A_WIDTH
OFF_KR = OFF_QR + R_QK
OFF_VR = OFF_KR + R_QK
OFF_GR = OFF_VR + R_V
OFF_GA = OFF_GR + R_V
OFF_GB = OFF_GA + D_MODEL

LANES = 128
VMEM_LIMIT = 56 * 1024 * 1024

A_SPAN = ATT_BLOCK * max(DILATIONS)
A_UNITS = A_SPAN // ATT_BLOCK

IN_TM = A_SPAN
IN_TN = 512
IN_CHUNKS = (512, 512, 512, 256, 256)
X_CHUNKS = D_MODEL // LANES

P_GA = 0
P_GB = P_GA + D_MODEL
P_VR = P_GB + D_MODEL
P_GR = P_VR + R_V
P_QR = P_GR + R_V
P_KR = P_QR + R_QK
P_QKV = tuple(P_KR + R_QK + g * 3 * A_GROUP_W for g in range(N_GROUPS))
P_WIDTH = P_QKV[-1] + 3 * A_GROUP_W

_TAB_R = 2 * N_GROUPS
_TAB_NONE = 2 * N_GROUPS + 1
TILE_TABLE = ((_TAB_NONE,) * (P_QR // IN_TN) + (_TAB_R,) * (2 * R_QK // IN_TN)
              + tuple(t for g in range(N_GROUPS) for t in (g, N_GROUPS + g, _TAB_NONE)))
LOG2_E = 1.4426950408889634
A_Q_SCALE = A_HEAD_DIM ** -0.5 * LOG2_E
TILE_ORDER = (0,) * (P_QKV[0] // IN_TN) + tuple(g for g in range(N_GROUPS) for _ in range(3))
N_TILES = P_WIDTH // IN_TN


def _rms(x, gain):
    ms = jnp.mean(x * x, axis=-1, keepdims=True)
    return x * lax.rsqrt(ms + NORM_EPS) * gain


def _step_lookup(j, values):
    out = values[-1]
    for t in range(len(values) - 2, -1, -1):
        if values[t] != values[t + 1]:
            out = jnp.where(j <= t, values[t], out)
    return out


def _residue_pieces(dil, src_dil, u):
    ratio = dil // src_dil
    piece = ATT_BLOCK // ratio
    ss, r = divmod(u, dil)
    for k in range(ratio):
        src_block = (ss * ratio + k) * src_dil + r % src_dil
        yield (pl.ds(src_block * ATT_BLOCK + r // src_dil, piece, stride=ratio),
               pl.ds(u * ATT_BLOCK + k * piece, piece))


def _in_proj_body(*refs):
    x_refs = refs[:X_CHUNKS]
    g_ref, w_ref, cos_ref, sin_ref, o_ref, h_ref, inv_ref, stage_ref = refs[X_CHUNKS:]
    j = pl.program_id(1)

    @pl.when(j == 0)
    def _norm():
        gains = [g_ref[:, c * LANES:(c + 1) * LANES] for c in range(X_CHUNKS)]
        for u in range(A_UNITS):
            rows = pl.ds(u * ATT_BLOCK, ATT_BLOCK)
            xc = [x_refs[c][rows, :] for c in range(X_CHUNKS)]
            sq = xc[0] * xc[0]
            for c in range(1, X_CHUNKS):
                sq = sq + xc[c] * xc[c]
            inv = lax.rsqrt(jnp.sum(sq, axis=-1, keepdims=True) * (1.0 / D_MODEL) + NORM_EPS)
            inv = jnp.broadcast_to(inv, (ATT_BLOCK, LANES))
            inv_ref[rows, :] = inv
            for c in range(X_CHUNKS):
                h_ref[0, rows, c * LANES:(c + 1) * LANES] = (xc[c] * inv * gains[c]).astype(BF16)
        for g in range(1, N_GROUPS):
            for u in range(A_UNITS):
                for src, dst in _residue_pieces(DILATIONS[g], DILATIONS[g - 1], u):
                    if g == 1:
                        inv = inv_ref[src, :]
                    for c in range(X_CHUNKS):
                        lanes = slice(c * LANES, (c + 1) * LANES)
                        if g == 1:
                            y = x_refs[c][src, :] * inv * gains[c]
                            if g + 1 < N_GROUPS:
                                stage_ref[c, dst, :] = y
                        else:
                            y = stage_ref[c, src, :]
                        h_ref[g, dst, lanes] = y.astype(BF16)

    def project(rotary, order):
        starts = np.cumsum((0,) + IN_CHUNKS)
        for start, size in zip(starts, IN_CHUNKS):
            rows = pl.ds(int(start), size)
            acc = jnp.dot(h_ref[order, rows, :], w_ref[...], preferred_element_type=F32)
            if not rotary:
                o_ref[rows, :] = acc.astype(BF16)
                continue
            for c in range(IN_TN // LANES):
                t = acc[:, c * LANES:(c + 1) * LANES]
                o_ref[rows, c * LANES:(c + 1) * LANES] = (
                    t * cos_ref[rows, :] + pltpu.roll(t, LANES // 2, 1) * sin_ref[rows, :]).astype(BF16)

    order = _step_lookup(j, TILE_ORDER)
    has_rotary = _step_lookup(j, tuple(int(t != _TAB_NONE) for t in TILE_TABLE)) == 1
    pl.when(has_rotary)(functools.partial(project, True, order))
    pl.when(jnp.logical_not(has_rotary))(functools.partial(project, False, order))


_SRC_TILE = tuple(off // IN_TN + t for off, n in ((OFF_GA, 2), (OFF_GB, 2), (OFF_VR, 2), (OFF_GR, 2), (OFF_QR, 1),
                                                   (OFF_KR, 1)) for t in range(n)) + tuple(
    (off + g * A_GROUP_W) // IN_TN for g in range(N_GROUPS) for off in (OFF_QA, OFF_KA, OFF_VA))
_PAIR_TILE = tuple(int(t < 2 * N_GROUPS) for t in TILE_TABLE)


def _arrange_body(w_ref, o_ref):
    j = pl.program_id(1)
    pair = _step_lookup(j, _PAIR_TILE) == 1

    @pl.when(jnp.logical_not(pair))
    def _plain():
        o_ref[...] = w_ref[...].astype(BF16)

    @pl.when(pair)
    def _pair():
        half = ROT_DIM // 2
        lane = lax.broadcasted_iota(jnp.int32, (D_MODEL, LANES), 1)
        to_b_x1 = (lane >= half) & (lane < ROT_DIM)
        to_a_x2 = (lane >= A_HEAD_DIM) & (lane < A_HEAD_DIM + half)
        for c in range(IN_TN // LANES):
            t = w_ref[:, c * LANES:(c + 1) * LANES]
            moved = jnp.where(to_b_x1, pltpu.roll(t, LANES - A_HEAD_DIM + half, 1),
                              pltpu.roll(t, A_HEAD_DIM - half, 1))
            o_ref[:, c * LANES:(c + 1) * LANES] = jnp.where(to_b_x1 | to_a_x2, moved, t).astype(BF16)


def _arrange_w_in(w):
    depth = w.shape[0]
    return pl.pallas_call(
        _arrange_body,
        grid=(depth, N_TILES),
        in_specs=[pl.BlockSpec((None, D_MODEL, IN_TN), lambda l, j: (l, 0, _step_lookup(j, _SRC_TILE)))],
        out_specs=pl.BlockSpec((None, D_MODEL, IN_TN), lambda l, j: (l, 0, j)),
        out_shape=jax.ShapeDtypeStruct((depth, D_MODEL, P_WIDTH), BF16),
        compiler_params=pltpu.CompilerParams(
            dimension_semantics=("parallel", "parallel"), vmem_limit_bytes=VMEM_LIMIT),
    )(w)


@functools.lru_cache(maxsize=None)
def _rotary_tables(seq):
    f32 = np.float32
    half = ROT_DIM // 2
    inv = (1.0 / (f32(ROPE_THETA) ** (np.arange(half, dtype=f32) / f32(half)))).astype(f32)
    rest = A_HEAD_DIM - ROT_DIM
    ones, zeros = np.ones((seq, rest), f32), np.zeros((seq, rest), f32)
    cos_tabs, sin_tabs = [], []
    for dil in DILATIONS:
        span = ATT_BLOCK * dil
        pos = np.arange(seq, dtype=f32).reshape(seq // span, ATT_BLOCK, dil).transpose(0, 2, 1).reshape(seq, 1)
        ang = pos * inv[None, :]
        cos, sin = np.cos(ang), np.sin(ang)
        cos_tabs.append(np.concatenate([cos, cos, ones, cos, cos, ones], axis=1))
        sin_tabs.append(np.concatenate([-sin, -sin, zeros, sin, sin, zeros], axis=1))
    cos_tabs = [t * f32(A_Q_SCALE) for t in cos_tabs] + cos_tabs
    sin_tabs = [t * f32(A_Q_SCALE) for t in sin_tabs] + sin_tabs
    half_r = R_KEY_DIM // 2
    inv_r = (1.0 / (f32(R_ROT_BASE) ** (np.arange(half_r, dtype=f32) / f32(half_r)))).astype(f32)
    ang_r = np.arange(seq, dtype=f32)[:, None] * inv_r[None, :]
    cos_tabs += [np.concatenate([np.cos(ang_r), np.cos(ang_r)], axis=1), np.ones((seq, LANES), f32)]
    sin_tabs += [np.concatenate([-np.sin(ang_r), np.sin(ang_r)], axis=1), np.zeros((seq, LANES), f32)]
    return np.stack(cos_tabs).astype(f32), np.stack(sin_tabs).astype(f32)


def _in_proj(x2d, gain, w_bf16, layer, tables, seq):
    m = x2d.shape[0]
    tiles_per_seq = seq // IN_TM
    tab_spec = pl.BlockSpec((None, IN_TM, LANES),
                            lambda i, j: (_step_lookup(j, TILE_TABLE), i % tiles_per_seq, 0))
    x_specs = [pl.BlockSpec((IN_TM, LANES), lambda i, j, c=c: (i, c)) for c in range(X_CHUNKS)]
    return pl.pallas_call(
        _in_proj_body,
        grid=(m // IN_TM, P_WIDTH // IN_TN),
        in_specs=x_specs + [
            pl.BlockSpec((1, D_MODEL), lambda i, j: (0, 0)),
            pl.BlockSpec((None, D_MODEL, IN_TN), lambda i, j: (layer, 0, j)),
            tab_spec, tab_spec,
        ],
        out_specs=pl.BlockSpec((IN_TM, IN_TN), lambda i, j: (i, j)),
        out_shape=jax.ShapeDtypeStruct((m, P_WIDTH), BF16),
        scratch_shapes=[pltpu.VMEM((N_GROUPS, IN_TM, D_MODEL), BF16),
                        pltpu.VMEM((IN_TM, LANES), F32),
                        pltpu.VMEM((X_CHUNKS, IN_TM, LANES), F32)],
        compiler_params=pltpu.CompilerParams(
            dimension_semantics=("parallel", "arbitrary"), vmem_limit_bytes=VMEM_LIMIT),
    )(*([x2d] * X_CHUNKS), gain, w_bf16, *tables)


A_HEADS_PER_STEP = LANES // A_HEAD_DIM
A_QUAD = 8


def _attn_body(*refs):
    q_in = refs[0:3]
    kc_in = refs[3:6]
    kp_in = refs[6:9]
    vc_in = refs[9:12]
    vp_in = refs[12:15]
    o_ref = refs[15]
    ro, rd, rm, bias, s_buf, p_buf = refs[16:22]
    so_buf, sd_buf, sm_buf, sy_buf = refs[22:26]
    jt = pl.program_id(1)

    qi = lax.broadcasted_iota(jnp.int32, (ATT_BLOCK, 2 * ATT_BLOCK), 0)
    kj = lax.broadcasted_iota(jnp.int32, (ATT_BLOCK, 2 * ATT_BLOCK), 1)
    dist = qi + ATT_BLOCK - kj
    band = (dist >= 0) & (dist <= ATT_BLOCK)
    bias[0] = jnp.where(band & (kj >= ATT_BLOCK), 0.0, NEG_INF).astype(BF16)
    bias[1] = jnp.where(band, 0.0, NEG_INF).astype(BF16)
    lane = lax.broadcasted_iota(jnp.int32, (ATT_BLOCK, LANES), 1)

    half = ROT_DIM // 2
    first_head = (lane < half) | ((lane >= ROT_DIM) & (lane < A_HEAD_DIM + half))
    head_lanes = (first_head, jnp.logical_not(first_head))
    first = lane < A_HEAD_DIM
    v_lane = lax.broadcasted_iota(jnp.int32, (2 * ATT_BLOCK, LANES), 1)
    v_lanes = (v_lane < A_HEAD_DIM, v_lane >= A_HEAD_DIM)

    def rows_of(u):
        return pl.ds(u * ATT_BLOCK, ATT_BLOCK)

    def previous(cur, prev, d, u):
        return prev[rows_of(u), :] if u < d else cur[rows_of(u - d), :]

    def stage_scores(slot, g, u):
        d = DILATIONS[g]
        b = bias[jnp.minimum(jt, 1)] if u < d else bias[1]
        q = q_in[g][rows_of(u), :]
        k2 = jnp.concatenate([previous(kc_in[g], kp_in[g], d, u), kc_in[g][rows_of(u), :]], axis=0)
        for h in range(A_HEADS_PER_STEP):
            qm = jnp.where(head_lanes[h], q, jnp.zeros_like(q))
            s = lax.dot_general(qm, k2, (((1,), (1,)), ((), ())), preferred_element_type=F32)
            s_buf[slot, h] = s.astype(BF16) + b

    def stage_softmax(slot, g, u):
        ms = []
        for h in range(A_HEADS_PER_STEP):
            s = s_buf[slot, h]
            m = jnp.max(s, axis=1, keepdims=True)
            p_buf[slot, h] = jnp.exp2(s - m)
            ms.append(m.astype(F32))
        rm[g, rows_of(u), :] = jnp.where(first, ms[0], ms[1])

    def stage_out(slot, g, u):
        d = DILATIONS[g]
        v2 = jnp.concatenate([previous(vc_in[g], vp_in[g], d, u), vc_in[g][rows_of(u), :]], axis=0)
        pvs = [jnp.dot(p_buf[slot, h], jnp.where(v_lanes[h], v2, jnp.ones_like(v2)), preferred_element_type=F32)
               for h in range(A_HEADS_PER_STEP)]
        out = jnp.where(first, pvs[0], pvs[1])
        den = pltpu.roll(jnp.where(first, pvs[1], pvs[0]), A_HEAD_DIM, 1)
        ro[g, rows_of(u), :] = out
        rd[g, rows_of(u), :] = den

    quads = [(g, qd) for g in range(N_GROUPS) for qd in range(A_UNITS // A_QUAD)]
    always = pl.program_id(0) >= 0
    stages = (stage_scores, stage_softmax, stage_out)
    for trip in range(len(quads) + len(stages) - 1):

        @pl.when(always)
        def _trip(trip=trip):
            for age in range(len(stages) - 1, -1, -1):
                n = trip - age
                if 0 <= n < len(quads):
                    g, qd = quads[n]
                    for i in range(A_QUAD):
                        stages[age]((n % 2) * A_QUAD + i, g, qd * A_QUAD + i)

    d2, d3 = DILATIONS[1], DILATIONS[2]
    for u in range(A_UNITS):
        for src, dst in _residue_pieces(d2, DILATIONS[0], u):
            so_buf[dst, :] = ro[0, src, :]
            sd_buf[dst, :] = rd[0, src, :]
            sm_buf[dst, :] = rm[0, src, :]
    for r3 in range(A_UNITS):
        blk = pl.ds(r3 * ATT_BLOCK, ATT_BLOCK)
        pieces = list(_residue_pieces(d3, d2, r3))

        def by_group(staged, res):
            return (jnp.concatenate([staged[src, :] for src, _ in pieces], axis=0),
                    jnp.concatenate([res[1, src, :] for src, _ in pieces], axis=0), res[2, blk, :])

        outs, dens, maxes = by_group(so_buf, ro), by_group(sd_buf, rd), by_group(sm_buf, rm)
        mx = jnp.maximum(jnp.maximum(maxes[0], maxes[1]), maxes[2])
        e = [jnp.exp2(m - mx) for m in maxes]
        y = (e[0] * outs[0] + e[1] * outs[1] + e[2] * outs[2]) / (e[0] * dens[0] + e[1] * dens[1] + e[2] * dens[2])
        for k, (src, _) in enumerate(pieces):
            sy_buf[src, :] = y[k * ATT_BLOCK // len(pieces):(k + 1) * ATT_BLOCK // len(pieces), :]
    for u in range(A_UNITS):
        for src, dst in _residue_pieces(d2, DILATIONS[0], u):
            o_ref[src, :] = sy_buf[dst, :]


def _attention_a(proj, batch, seq):
    p = proj.reshape(batch, seq, P_WIDTH)

    def cur_spec(g, part):
        col = (P_QKV[g] + part * A_GROUP_W) // LANES
        return pl.BlockSpec((None, A_SPAN, LANES), lambda b, j, hp: (b, j, col + hp))

    def prev_spec(g, part):
        col = (P_QKV[g] + part * A_GROUP_W) // LANES
        span = ATT_BLOCK * DILATIONS[g]
        per = A_SPAN // span
        return pl.BlockSpec((None, span, LANES),
                            lambda b, j, hp: (b, jnp.maximum(j * per - 1, 0), col + hp))

    groups = range(N_GROUPS)
    in_specs = ([cur_spec(g, 0) for g in groups] + [cur_spec(g, 1) for g in groups]
                + [prev_spec(g, 1) for g in groups] + [cur_spec(g, 2) for g in groups]
                + [prev_spec(g, 2) for g in groups])
    return pl.pallas_call(
        _attn_body,
        grid=(batch, seq // A_SPAN, A_GROUP_W // LANES),
        in_specs=in_specs,
        out_specs=pl.BlockSpec((None, A_SPAN, LANES), lambda b, j, hp: (b, j, hp)),
        out_shape=jax.ShapeDtypeStruct((batch, seq, A_OUT), F32),
        scratch_shapes=[pltpu.VMEM((N_GROUPS, A_SPAN, LANES), F32)] * 3
                       + [pltpu.VMEM((2, ATT_BLOCK, 2 * ATT_BLOCK), BF16),
                          pltpu.VMEM((2 * A_QUAD, A_HEADS_PER_STEP, ATT_BLOCK, 2 * ATT_BLOCK), BF16),
                          pltpu.VMEM((2 * A_QUAD, A_HEADS_PER_STEP, ATT_BLOCK, 2 * ATT_BLOCK), BF16)]
                       + [pltpu.VMEM((A_SPAN, LANES), F32)] * 4,
        compiler_params=pltpu.CompilerParams(
            dimension_semantics=("parallel", "parallel", "parallel"), vmem_limit_bytes=VMEM_LIMIT),
    )(*([p] * 15))


RET_TB = 4096
_LOG_GAMMA = [float(v) for v in np.log1p(-(2.0 ** (-5.0 - np.arange(R_HEADS)))).astype(np.float32)]


def _ret_body(q_ref, k_ref, v_ref, g_ref, o_ref, st_ref):
    h = pl.program_id(1)
    t = pl.program_id(2)

    @pl.when(t == 0)
    def _reset():
        st_ref[...] = jnp.zeros_like(st_ref)

    lg = jnp.float32(_LOG_GAMMA[R_HEADS - 1])
    for hh in range(R_HEADS - 2, -1, -1):
        lg = jnp.where(h == hh, jnp.float32(_LOG_GAMMA[hh]), lg)
    row = lax.broadcasted_iota(jnp.int32, (R_CHUNK, R_CHUNK), 0).astype(F32)
    colv = lax.broadcasted_iota(jnp.int32, (R_CHUNK, R_CHUNK), 1).astype(F32)
    diff = row - colv
    decay = jnp.where(diff >= 0, jnp.exp(diff * lg), 0.0)
    zeta = jnp.exp((R_CHUNK - 1 - row) * lg)
    xi = jnp.exp((row + 1.0) * lg)
    chunk_decay = jnp.exp(jnp.full((R_KEY_DIM, R_VAL_DIM), float(R_CHUNK), F32) * lg)
    kscale = R_KEY_DIM ** -0.5

    for c in range(RET_TB // R_CHUNK):
        sl = slice(c * R_CHUNK, (c + 1) * R_CHUNK)
        q = q_ref[sl, :].astype(F32)
        k = k_ref[sl, :].astype(F32) * kscale
        v = v_ref[sl, :]
        s = lax.dot_general(q.astype(BF16), k.astype(BF16), (((1,), (1,)), ((), ())),
                            preferred_element_type=F32) * decay
        inner = jnp.dot(s.astype(BF16), v, preferred_element_type=F32)
        state = st_ref[...]
        cross = jnp.dot((q * xi).astype(BF16), state.astype(BF16), preferred_element_type=F32)
        kz_t = (k * zeta).T.astype(BF16)
        st_ref[...] = state * chunk_decay + jnp.dot(kz_t, v, preferred_element_type=F32)
        out = inner + cross
        ms = jnp.mean(out * out, axis=-1, keepdims=True)
        gate = g_ref[sl, :].astype(F32)
        o_ref[sl, :] = (gate * jax.nn.sigmoid(gate) * (out * lax.rsqrt(ms + NORM_EPS))).astype(BF16)


def _retention(proj, batch, seq):
    p = proj.reshape(batch, seq, P_WIDTH)
    return pl.pallas_call(
        _ret_body,
        grid=(batch, R_HEADS, seq // RET_TB),
        in_specs=[
            pl.BlockSpec((None, RET_TB, R_KEY_DIM), lambda b, h, t: (b, t, P_QR // R_KEY_DIM + h)),
            pl.BlockSpec((None, RET_TB, R_KEY_DIM), lambda b, h, t: (b, t, P_KR // R_KEY_DIM + h)),
            pl.BlockSpec((None, RET_TB, R_VAL_DIM), lambda b, h, t: (b, t, P_VR // R_VAL_DIM + h)),
            pl.BlockSpec((None, RET_TB, R_VAL_DIM), lambda b, h, t: (b, t, P_GR // R_VAL_DIM + h)),
        ],
        out_specs=pl.BlockSpec((None, RET_TB, R_VAL_DIM), lambda b, h, t: (b, t, h)),
        out_shape=jax.ShapeDtypeStruct((batch, seq, R_V), BF16),
        scratch_shapes=[pltpu.VMEM((R_KEY_DIM, R_VAL_DIM), F32)],
        compiler_params=pltpu.CompilerParams(
            dimension_semantics=("parallel", "parallel", "arbitrary"), vmem_limit_bytes=VMEM_LIMIT),
    )(p, p, p, p)


MERGE_TM = 1024


def _merge_body(x_ref, ya_ref, rb_ref, ga_ref, gb_ref, wa_ref, wb_ref, wo_ref, o_ref):
    ya = jnp.dot(ya_ref[...].astype(BF16), wa_ref[...], preferred_element_type=F32)
    yb = jnp.dot(rb_ref[...], wb_ref[...], preferred_element_type=F32)
    ga = ga_ref[...].astype(F32)
    gb = gb_ref[...].astype(F32)
    merged = jax.nn.sigmoid(ga) * ya + jax.nn.sigmoid(gb) * yb
    o_ref[...] = x_ref[...] + jnp.dot(merged.astype(BF16), wo_ref[...], preferred_element_type=F32)


def _merge(x2d, ya2d, rb2d, proj, wa, wb, wo, layer):
    m = x2d.shape[0]

    def full(shape):
        return pl.BlockSpec((None,) + shape, lambda i: (layer, 0, 0))

    return pl.pallas_call(
        _merge_body,
        grid=(m // MERGE_TM,),
        in_specs=[
            pl.BlockSpec((MERGE_TM, D_MODEL), lambda i: (i, 0)),
            pl.BlockSpec((MERGE_TM, A_OUT), lambda i: (i, 0)),
            pl.BlockSpec((MERGE_TM, R_V), lambda i: (i, 0)),
            pl.BlockSpec((MERGE_TM, D_MODEL), lambda i: (i, P_GA // D_MODEL)),
            pl.BlockSpec((MERGE_TM, D_MODEL), lambda i: (i, P_GB // D_MODEL)),
            full((A_OUT, D_MODEL)), full((R_V, D_MODEL)), full((D_MODEL, D_MODEL)),
        ],
        out_specs=pl.BlockSpec((MERGE_TM, D_MODEL), lambda i: (i, 0)),
        out_shape=jax.ShapeDtypeStruct((m, D_MODEL), F32),
        compiler_params=pltpu.CompilerParams(
            dimension_semantics=("parallel",), vmem_limit_bytes=VMEM_LIMIT),
    )(x2d, ya2d, rb2d, proj, proj, wa, wb, wo)


FFN_TM = 1024
FFN_TF = 2048


def _ffn_body(x_ref, g_ref, wu_ref, wd_ref, fg_ref, o_ref, h_ref, *, final):
    c = pl.program_id(1)

    @pl.when(c == 0)
    def _start():
        x = x_ref[...]
        h_ref[...] = _rms(x, g_ref[...]).astype(BF16)
        o_ref[...] = x

    up = jnp.dot(h_ref[...], wu_ref[...], preferred_element_type=F32)
    up = jnp.square(jnp.maximum(up, 0.0)).astype(BF16)
    o_ref[...] += jnp.dot(up, wd_ref[...], preferred_element_type=F32)

    if final:
        @pl.when(c == pl.num_programs(1) - 1)
        def _final_norm():
            o_ref[...] = _rms(o_ref[...], fg_ref[...])


def _ffn(x2d, gain, wu, wd, layer, final_gain, final):
    m = x2d.shape[0]
    return pl.pallas_call(
        functools.partial(_ffn_body, final=final),
        grid=(m // FFN_TM, D_FF // FFN_TF),
        in_specs=[
            pl.BlockSpec((FFN_TM, D_MODEL), lambda i, c: (i, 0)),
            pl.BlockSpec((1, D_MODEL), lambda i, c: (0, 0)),
            pl.BlockSpec((None, D_MODEL, FFN_TF), lambda i, c: (layer, 0, c)),
            pl.BlockSpec((None, FFN_TF, D_MODEL), lambda i, c: (layer, c, 0)),
            pl.BlockSpec((1, D_MODEL), lambda i, c: (0, 0)),
        ],
        out_specs=pl.BlockSpec((FFN_TM, D_MODEL), lambda i, c: (i, 0)),
        out_shape=jax.ShapeDtypeStruct((m, D_MODEL), F32),
        scratch_shapes=[pltpu.VMEM((FFN_TM, D_MODEL), BF16)],
        compiler_params=pltpu.CompilerParams(
            dimension_semantics=("parallel", "arbitrary"), vmem_limit_bytes=VMEM_LIMIT),
    )(x2d, gain, wu, wd, final_gain)


def kernel(x, mix_norm, w_in, w_a, w_b, w_o, ffn_norm, w_up, w_down, final_norm):
    batch, seq, _ = x.shape
    depth = w_in.shape[0]
    assert seq % A_SPAN == 0 and seq % RET_TB == 0
    tables = _rotary_tables(seq)
    final_gain = final_norm.reshape(1, D_MODEL)
    x2d = x.reshape(batch * seq, D_MODEL)
    w_in, w_a, w_b, w_o, w_up, w_down = (_arrange_w_in(w_in),) + tuple(
        w.astype(BF16) for w in (w_a, w_b, w_o, w_up, w_down))
    for layer in range(depth):
        proj = _in_proj(x2d, mix_norm[layer].reshape(1, D_MODEL), w_in, layer, tables, seq)
        ya = _attention_a(proj, batch, seq).reshape(batch * seq, A_OUT)
        rb = _retention(proj, batch, seq).reshape(batch * seq, R_V)
        x2d = _merge(x2d, ya, rb, proj, w_a, w_b, w_o, layer)
        x2d = _ffn(x2d, ffn_norm[layer].reshape(1, D_MODEL), w_up, w_down, layer, final_gain,
                   final=(layer == depth - 1))
    return x2d.reshape(batch, seq, D_MODEL)
```

```python
import functools

import numpy as np
import jax
import jax.numpy as jnp
from jax import lax
from jax.experimental import pallas as pl
from jax.experimental.pallas import tpu as pltpu

F32 = jnp.float32
BF16 = jnp.bfloat16

D_MODEL = 1024
N_GROUPS = 3
DILATIONS = (1, 4, 16)
A_HEADS = 8
A_HEAD_DIM = 64
A_GROUP_W = A_HEADS * A_HEAD_DIM
A_WIDTH = N_GROUPS * A_GROUP_W
A_OUT = A_GROUP_W
ROT_DIM = A_HEAD_DIM // 4
ROPE_THETA = 500000.0
ATT_BLOCK = 128
NEG_INF = -1e30
R_HEADS = 4
R_KEY_DIM = 128
R_VAL_DIM = 256
R_QK = R_HEADS * R_KEY_DIM
R_V = R_HEADS * R_VAL_DIM
R_CHUNK = 128
R_ROT_BASE = 10000.0
D_FF = 4 * D_MODEL
NORM_EPS = 1e-6

OFF_QA = 0
OFF_KA = A_WIDTH
OFF_VA = 2 * A_WIDTH
OFF_QR = 3 * A_WIDTH
OFF_KR = OFF_QR + R_QK
OFF_VR = OFF_KR + R_QK
OFF_GR = OFF_VR + R_V
OFF_GA = OFF_GR + R_V
OFF_GB = OFF_GA + D_MODEL

LANES = 128
VMEM_LIMIT = 56 * 1024 * 1024

A_SPAN = ATT_BLOCK * max(DILATIONS)
A_UNITS = A_SPAN // ATT_BLOCK

IN_TM = A_SPAN
IN_TN = 512
IN_CHUNKS = (512, 512, 512, 256, 256)
X_CHUNKS = D_MODEL // LANES

P_GA = 0
P_GB = P_GA + D_MODEL
P_VR = P_GB + D_MODEL
P_GR = P_VR + R_V
P_QR = P_GR + R_V
P_KR = P_QR + R_QK
P_QKV = tuple(P_KR + R_QK + g * 3 * A_GROUP_W for g in range(N_GROUPS))
P_WIDTH = P_QKV[-1] + 3 * A_GROUP_W

_TAB_R = 2 * N_GROUPS
_TAB_NONE = 2 * N_GROUPS + 1
TILE_TABLE = ((_TAB_NONE,) * (P_QR // IN_TN) + (_TAB_R,) * (2 * R_QK // IN_TN)
              + tuple(t for g in range(N_GROUPS) for t in (g, N_GROUPS + g, _TAB_NONE)))
LOG2_E = 1.4426950408889634
A_Q_SCALE = A_HEAD_DIM ** -0.5 * LOG2_E
TILE_ORDER = (0,) * (P_QKV[0] // IN_TN) + tuple(g for g in range(N_GROUPS) for _ in range(3))
N_TILES = P_WIDTH // IN_TN


def _rms(x, gain):
    ms = jnp.mean(x * x, axis=-1, keepdims=True)
    return x * lax.rsqrt(ms + NORM_EPS) * gain


def _step_lookup(j, values):
    out = values[-1]
    for t in range(len(values) - 2, -1, -1):
        if values[t] != values[t + 1]:
            out = jnp.where(j <= t, values[t], out)
    return out


def _residue_pieces(dil, src_dil, u):
    ratio = dil // src_dil
    piece = ATT_BLOCK // ratio
    ss, r = divmod(u, dil)
    for k in range(ratio):
        src_block = (ss * ratio + k) * src_dil + r % src_dil
        yield (pl.ds(src_block * ATT_BLOCK + r // src_dil, piece, stride=ratio),
               pl.ds(u * ATT_BLOCK + k * piece, piece))


def _in_proj_body(*refs):
    x_refs = refs[:X_CHUNKS]
    g_ref, w_ref, cos_ref, sin_ref, o_ref, h_ref, inv_ref, stage_ref = refs[X_CHUNKS:]
    j = pl.program_id(1)

    @pl.when(j == 0)
    def _norm():
        gains = [g_ref[:, c * LANES:(c + 1) * LANES] for c in range(X_CHUNKS)]
        for u in range(A_UNITS):
            rows = pl.ds(u * ATT_BLOCK, ATT_BLOCK)
            xc = [x_refs[c][rows, :] for c in range(X_CHUNKS)]
            sq = xc[0] * xc[0]
            for c in range(1, X_CHUNKS):
                sq = sq + xc[c] * xc[c]
            inv = lax.rsqrt(jnp.sum(sq, axis=-1, keepdims=True) * (1.0 / D_MODEL) + NORM_EPS)
            inv = jnp.broadcast_to(inv, (ATT_BLOCK, LANES))
            inv_ref[rows, :] = inv
            for c in range(X_CHUNKS):
                h_ref[0, rows, c * LANES:(c + 1) * LANES] = (xc[c] * inv * gains[c]).astype(BF16)
        for g in range(1, N_GROUPS):
            for u in range(A_UNITS):
                for src, dst in _residue_pieces(DILATIONS[g], DILATIONS[g - 1], u):
                    if g == 1:
                        inv = inv_ref[src, :]
                    for c in range(X_CHUNKS):
                        lanes = slice(c * LANES, (c + 1) * LANES)
                        if g == 1:
                            y = x_refs[c][src, :] * inv * gains[c]
                            if g + 1 < N_GROUPS:
                                stage_ref[c, dst, :] = y
                        else:
                            y = stage_ref[c, src, :]
                        h_ref[g, dst, lanes] = y.astype(BF16)

    def project(rotary, order):
        starts = np.cumsum((0,) + IN_CHUNKS)
        for start, size in zip(starts, IN_CHUNKS):
            rows = pl.ds(int(start), size)
            acc = jnp.dot(h_ref[order, rows, :], w_ref[...], preferred_element_type=F32)
            if not rotary:
                o_ref[rows, :] = acc.astype(BF16)
                continue
            for c in range(IN_TN // LANES):
                t = acc[:, c * LANES:(c + 1) * LANES]
                o_ref[rows, c * LANES:(c + 1) * LANES] = (
                    t * cos_ref[rows, :] + pltpu.roll(t, LANES // 2, 1) * sin_ref[rows, :]).astype(BF16)

    order = _step_lookup(j, TILE_ORDER)
    has_rotary = _step_lookup(j, tuple(int(t != _TAB_NONE) for t in TILE_TABLE)) == 1
    pl.when(has_rotary)(functools.partial(project, True, order))
    pl.when(jnp.logical_not(has_rotary))(functools.partial(project, False, order))


_SRC_TILE = tuple(off // IN_TN + t for off, n in ((OFF_GA, 2), (OFF_GB, 2), (OFF_VR, 2), (OFF_GR, 2), (OFF_QR, 1),
                                                   (OFF_KR, 1)) for t in range(n)) + tuple(
    (off + g * A_GROUP_W) // IN_TN for g in range(N_GROUPS) for off in (OFF_QA, OFF_KA, OFF_VA))
_PAIR_TILE = tuple(int(t < 2 * N_GROUPS) for t in TILE_TABLE)


def _arrange_body(w_ref, o_ref):
    j = pl.program_id(1)
    pair = _step_lookup(j, _PAIR_TILE) == 1

    @pl.when(jnp.logical_not(pair))
    def _plain():
        o_ref[...] = w_ref[...].astype(BF16)

    @pl.when(pair)
    def _pair():
        half = ROT_DIM // 2
        lane = lax.broadcasted_iota(jnp.int32, (D_MODEL, LANES), 1)
        to_b_x1 = (lane >= half) & (lane < ROT_DIM)
        to_a_x2 = (lane >= A_HEAD_DIM) & (lane < A_HEAD_DIM + half)
        for c in range(IN_TN // LANES):
            t = w_ref[:, c * LANES:(c + 1) * LANES]
            moved = jnp.where(to_b_x1, pltpu.roll(t, LANES - A_HEAD_DIM + half, 1),
                              pltpu.roll(t, A_HEAD_DIM - half, 1))
            o_ref[:, c * LANES:(c + 1) * LANES] = jnp.where(to_b_x1 | to_a_x2, moved, t).astype(BF16)


def _arrange_w_in(w):
    depth = w.shape[0]
    return pl.pallas_call(
        _arrange_body,
        grid=(depth, N_TILES),
        in_specs=[pl.BlockSpec((None, D_MODEL, IN_TN), lambda l, j: (l, 0, _step_lookup(j, _SRC_TILE)))],
        out_specs=pl.BlockSpec((None, D_MODEL, IN_TN), lambda l, j: (l, 0, j)),
        out_shape=jax.ShapeDtypeStruct((depth, D_MODEL, P_WIDTH), BF16),
        compiler_params=pltpu.CompilerParams(
            dimension_semantics=("parallel", "parallel"), vmem_limit_bytes=VMEM_LIMIT),
    )(w)


@functools.lru_cache(maxsize=None)
def _rotary_tables(seq):
    f32 = np.float32
    half = ROT_DIM // 2
    inv = (1.0 / (f32(ROPE_THETA) ** (np.arange(half, dtype=f32) / f32(half)))).astype(f32)
    rest = A_HEAD_DIM - ROT_DIM
    ones, zeros = np.ones((seq, rest), f32), np.zeros((seq, rest), f32)
    cos_tabs, sin_tabs = [], []
    for dil in DILATIONS:
        span = ATT_BLOCK * dil
        pos = np.arange(seq, dtype=f32).reshape(seq // span, ATT_BLOCK, dil).transpose(0, 2, 1).reshape(seq, 1)
        ang = pos * inv[None, :]
        cos, sin = np.cos(ang), np.sin(ang)
        cos_tabs.append(np.concatenate([cos, cos, ones, cos, cos, ones], axis=1))
        sin_tabs.append(np.concatenate([-sin, -sin, zeros, sin, sin, zeros], axis=1))
    cos_tabs = [t * f32(A_Q_SCALE) for t in cos_tabs] + cos_tabs
    sin_tabs = [t * f32(A_Q_SCALE) for t in sin_tabs] + sin_tabs
    half_r = R_KEY_DIM // 2
    inv_r = (1.0 / (f32(R_ROT_BASE) ** (np.arange(half_r, dtype=f32) / f32(half_r)))).astype(f32)
    ang_r = np.arange(seq, dtype=f32)[:, None] * inv_r[None, :]
    cos_tabs += [np.concatenate([np.cos(ang_r), np.cos(ang_r)], axis=1), np.ones((seq, LANES), f32)]
    sin_tabs += [np.concatenate([-np.sin(ang_r), np.sin(ang_r)], axis=1), np.zeros((seq, LANES), f32)]
    return np.stack(cos_tabs).astype(f32), np.stack(sin_tabs).astype(f32)


def _in_proj(x2d, gain, w_bf16, layer, tables, seq):
    m = x2d.shape[0]
    tiles_per_seq = seq // IN_TM
    tab_spec = pl.BlockSpec((None, IN_TM, LANES),
                            lambda i, j: (_step_lookup(j, TILE_TABLE), i % tiles_per_seq, 0))
    x_specs = [pl.BlockSpec((IN_TM, LANES), lambda i, j, c=c: (i, c)) for c in range(X_CHUNKS)]
    return pl.pallas_call(
        _in_proj_body,
        grid=(m // IN_TM, P_WIDTH // IN_TN),
        in_specs=x_specs + [
            pl.BlockSpec((1, D_MODEL), lambda i, j: (0, 0)),
            pl.BlockSpec((None, D_MODEL, IN_TN), lambda i, j: (layer, 0, j)),
            tab_spec, tab_spec,
        ],
        out_specs=pl.BlockSpec((IN_TM, IN_TN), lambda i, j: (i, j)),
        out_shape=jax.ShapeDtypeStruct((m, P_WIDTH), BF16),
        scratch_shapes=[pltpu.VMEM((N_GROUPS, IN_TM, D_MODEL), BF16),
                        pltpu.VMEM((IN_TM, LANES), F32),
                        pltpu.VMEM((X_CHUNKS, IN_TM, LANES), F32)],
        compiler_params=pltpu.CompilerParams(
            dimension_semantics=("parallel", "arbitrary"), vmem_limit_bytes=VMEM_LIMIT),
    )(*([x2d] * X_CHUNKS), gain, w_bf16, *tables)


A_HEADS_PER_STEP = LANES // A_HEAD_DIM
A_QUAD = 8


def _attn_body(*refs):
    q_in = refs[0:3]
    kc_in = refs[3:6]
    kp_in = refs[6:9]
    vc_in = refs[9:12]
    vp_in = refs[12:15]
    o_ref = refs[15]
    ro, rl, bias, s_buf = refs[16:20]
    so_buf, sl_buf, sy_buf = refs[20:23]
    jt = pl.program_id(1)

    qi = lax.broadcasted_iota(jnp.int32, (ATT_BLOCK, 2 * ATT_BLOCK), 0)
    kj = lax.broadcasted_iota(jnp.int32, (ATT_BLOCK, 2 * ATT_BLOCK), 1)
    dist = qi + ATT_BLOCK - kj
    band = (dist >= 0) & (dist <= ATT_BLOCK)
    bias[0] = jnp.where(band & (kj >= ATT_BLOCK), 0.0, NEG_INF).astype(BF16)
    bias[1] = jnp.where(band, 0.0, NEG_INF).astype(BF16)
    lane = lax.broadcasted_iota(jnp.int32, (ATT_BLOCK, LANES), 1)

    half = ROT_DIM // 2
    first_head = (lane < half) | ((lane >= ROT_DIM) & (lane < A_HEAD_DIM + half))
    head_lanes = (first_head, jnp.logical_not(first_head))
    first = lane < A_HEAD_DIM
    v_lane = lax.broadcasted_iota(jnp.int32, (2 * ATT_BLOCK, LANES), 1)
    v_lanes = (v_lane < A_HEAD_DIM, v_lane >= A_HEAD_DIM)

    def rows_of(u):
        return pl.ds(u * ATT_BLOCK, ATT_BLOCK)

    def previous(cur, prev, d, u):
        return prev[rows_of(u), :] if u < d else cur[rows_of(u - d), :]

    def stage_scores(slot, g, u):
        d = DILATIONS[g]
        b = bias[jnp.minimum(jt, 1)] if u < d else bias[1]
        q = q_in[g][rows_of(u), :]
        k2 = jnp.concatenate([previous(kc_in[g], kp_in[g], d, u), kc_in[g][rows_of(u), :]], axis=0)
        for h in range(A_HEADS_PER_STEP):
            qm = jnp.where(head_lanes[h], q, jnp.zeros_like(q))
            s = lax.dot_general(qm, k2, (((1,), (1,)), ((), ())), preferred_element_type=F32)
            s_buf[slot, h] = s.astype(BF16) + b

    def stage_finish(slot, g, u):
        d = DILATIONS[g]
        v2 = jnp.concatenate([previous(vc_in[g], vp_in[g], d, u), vc_in[g][rows_of(u), :]], axis=0)
        pvs, ms = [], []
        for h in range(A_HEADS_PER_STEP):
            s = s_buf[slot, h]
            m = jnp.max(s, axis=1, keepdims=True)
            pvs.append(jnp.dot(jnp.exp2(s - m), jnp.where(v_lanes[h], v2, jnp.ones_like(v2)),
                               preferred_element_type=F32))
            ms.append(m.astype(F32))
        out = jnp.where(first, pvs[0], pvs[1])
        den = pltpu.roll(jnp.where(first, pvs[1], pvs[0]), A_HEAD_DIM, 1)
        ro[g, rows_of(u), :] = out / den
        rl[g, rows_of(u), :] = jnp.where(first, ms[0], ms[1]) + jnp.log(den) * LOG2_E

    quads = [(g, qd) for g in range(N_GROUPS) for qd in range(A_UNITS // A_QUAD)]
    always = pl.program_id(0) >= 0
    stages = (stage_scores, stage_finish)
    for trip in range(len(quads) + len(stages) - 1):

        @pl.when(always)
        def _trip(trip=trip):
            for age in range(len(stages) - 1, -1, -1):
                n = trip - age
                if 0 <= n < len(quads):
                    g, qd = quads[n]
                    for i in range(A_QUAD):
                        stages[age]((n % 2) * A_QUAD + i, g, qd * A_QUAD + i)

    d2, d3 = DILATIONS[1], DILATIONS[2]
    for u in range(A_UNITS):
        for src, dst in _residue_pieces(d2, DILATIONS[0], u):
            so_buf[dst, :] = ro[0, src, :]
            sl_buf[dst, :] = rl[0, src, :]
    for r3 in range(A_UNITS):
        blk = pl.ds(r3 * ATT_BLOCK, ATT_BLOCK)
        pieces = list(_residue_pieces(d3, d2, r3))
        o1 = jnp.concatenate([so_buf[src, :] for src, _ in pieces], axis=0)
        l1 = jnp.concatenate([sl_buf[src, :] for src, _ in pieces], axis=0)
        o2 = jnp.concatenate([ro[1, src, :] for src, _ in pieces], axis=0)
        l2 = jnp.concatenate([rl[1, src, :] for src, _ in pieces], axis=0)
        o3, l3 = ro[2, blk, :], rl[2, blk, :]
        mx = jnp.maximum(jnp.maximum(l1, l2), l3)
        e1, e2, e3 = jnp.exp2(l1 - mx), jnp.exp2(l2 - mx), jnp.exp2(l3 - mx)
        y = (e1 * o1 + e2 * o2 + e3 * o3) / (e1 + e2 + e3)
        for k, (src, _) in enumerate(pieces):
            sy_buf[src, :] = y[k * ATT_BLOCK // len(pieces):(k + 1) * ATT_BLOCK // len(pieces), :]
    for u in range(A_UNITS):
        for src, dst in _residue_pieces(d2, DILATIONS[0], u):
            o_ref[src, :] = sy_buf[dst, :]


def _attention_a(proj, batch, seq):
    p = proj.reshape(batch, seq, P_WIDTH)

    def cur_spec(g, part):
        col = (P_QKV[g] + part * A_GROUP_W) // LANES
        return pl.BlockSpec((None, A_SPAN, LANES), lambda b, j, hp: (b, j, col + hp))

    def prev_spec(g, part):
        col = (P_QKV[g] + part * A_GROUP_W) // LANES
        span = ATT_BLOCK * DILATIONS[g]
        per = A_SPAN // span
        return pl.BlockSpec((None, span, LANES),
                            lambda b, j, hp: (b, jnp.maximum(j * per - 1, 0), col + hp))

    groups = range(N_GROUPS)
    in_specs = ([cur_spec(g, 0) for g in groups] + [cur_spec(g, 1) for g in groups]
                + [prev_spec(g, 1) for g in groups] + [cur_spec(g, 2) for g in groups]
                + [prev_spec(g, 2) for g in groups])
    return pl.pallas_call(
        _attn_body,
        grid=(batch, seq // A_SPAN, A_GROUP_W // LANES),
        in_specs=in_specs,
        out_specs=pl.BlockSpec((None, A_SPAN, LANES), lambda b, j, hp: (b, j, hp)),
        out_shape=jax.ShapeDtypeStruct((batch, seq, A_OUT), F32),
        scratch_shapes=[pltpu.VMEM((N_GROUPS, A_SPAN, LANES), F32),
                        pltpu.VMEM((N_GROUPS, A_SPAN, LANES), F32),
                        pltpu.VMEM((2, ATT_BLOCK, 2 * ATT_BLOCK), BF16),
                        pltpu.VMEM((2 * A_QUAD, A_HEADS_PER_STEP, ATT_BLOCK, 2 * ATT_BLOCK), BF16)]
                       + [pltpu.VMEM((A_SPAN, LANES), F32)] * 3,
        compiler_params=pltpu.CompilerParams(
            dimension_semantics=("parallel", "parallel", "parallel"), vmem_limit_bytes=VMEM_LIMIT),
    )(*([p] * 15))


RET_TB = 4096
_LOG_GAMMA = [float(v) for v in np.log1p(-(2.0 ** (-5.0 - np.arange(R_HEADS)))).astype(np.float32)]


def _ret_body(q_ref, k_ref, v_ref, g_ref, o_ref, st_ref):
    h = pl.program_id(1)
    t = pl.program_id(2)

    @pl.when(t == 0)
    def _reset():
        st_ref[...] = jnp.zeros_like(st_ref)

    lg = jnp.float32(_LOG_GAMMA[R_HEADS - 1])
    for hh in range(R_HEADS - 2, -1, -1):
        lg = jnp.where(h == hh, jnp.float32(_LOG_GAMMA[hh]), lg)
    row = lax.broadcasted_iota(jnp.int32, (R_CHUNK, R_CHUNK), 0).astype(F32)
    colv = lax.broadcasted_iota(jnp.int32, (R_CHUNK, R_CHUNK), 1).astype(F32)
    diff = row - colv
    decay = jnp.where(diff >= 0, jnp.exp(diff * lg), 0.0)
    zeta = jnp.exp((R_CHUNK - 1 - row) * lg)
    xi = jnp.exp((row + 1.0) * lg)
    chunk_decay = jnp.exp(jnp.full((R_KEY_DIM, R_VAL_DIM), float(R_CHUNK), F32) * lg)
    kscale = R_KEY_DIM ** -0.5

    for c in range(RET_TB // R_CHUNK):
        sl = slice(c * R_CHUNK, (c + 1) * R_CHUNK)
        q = q_ref[sl, :].astype(F32)
        k = k_ref[sl, :].astype(F32) * kscale
        v = v_ref[sl, :]
        s = lax.dot_general(q.astype(BF16), k.astype(BF16), (((1,), (1,)), ((), ())),
                            preferred_element_type=F32) * decay
        inner = jnp.dot(s.astype(BF16), v, preferred_element_type=F32)
        state = st_ref[...]
        cross = jnp.dot((q * xi).astype(BF16), state.astype(BF16), preferred_element_type=F32)
        kz_t = (k * zeta).T.astype(BF16)
        st_ref[...] = state * chunk_decay + jnp.dot(kz_t, v, preferred_element_type=F32)
        out = inner + cross
        ms = jnp.mean(out * out, axis=-1, keepdims=True)
        gate = g_ref[sl, :].astype(F32)
        o_ref[sl, :] = (gate * jax.nn.sigmoid(gate) * (out * lax.rsqrt(ms + NORM_EPS))).astype(BF16)


def _retention(proj, batch, seq):
    p = proj.reshape(batch, seq, P_WIDTH)
    return pl.pallas_call(
        _ret_body,
        grid=(batch, R_HEADS, seq // RET_TB),
        in_specs=[
            pl.BlockSpec((None, RET_TB, R_KEY_DIM), lambda b, h, t: (b, t, P_QR // R_KEY_DIM + h)),
            pl.BlockSpec((None, RET_TB, R_KEY_DIM), lambda b, h, t: (b, t, P_KR // R_KEY_DIM + h)),
            pl.BlockSpec((None, RET_TB, R_VAL_DIM), lambda b, h, t: (b, t, P_VR // R_VAL_DIM + h)),
            pl.BlockSpec((None, RET_TB, R_VAL_DIM), lambda b, h, t: (b, t, P_GR // R_VAL_DIM + h)),
        ],
        out_specs=pl.BlockSpec((None, RET_TB, R_VAL_DIM), lambda b, h, t: (b, t, h)),
        out_shape=jax.ShapeDtypeStruct((batch, seq, R_V), BF16),
        scratch_shapes=[pltpu.VMEM((R_KEY_DIM, R_VAL_DIM), F32)],
        compiler_params=pltpu.CompilerParams(
            dimension_semantics=("parallel", "parallel", "arbitrary"), vmem_limit_bytes=VMEM_LIMIT),
    )(p, p, p, p)


MERGE_TM = 1024


def _merge_body(x_ref, ya_ref, rb_ref, ga_ref, gb_ref, wa_ref, wb_ref, wo_ref, o_ref):
    ya = jnp.dot(ya_ref[...].astype(BF16), wa_ref[...], preferred_element_type=F32)
    yb = jnp.dot(rb_ref[...], wb_ref[...], preferred_element_type=F32)
    ga = ga_ref[...].astype(F32)
    gb = gb_ref[...].astype(F32)
    merged = jax.nn.sigmoid(ga) * ya + jax.nn.sigmoid(gb) * yb
    o_ref[...] = x_ref[...] + jnp.dot(merged.astype(BF16), wo_ref[...], preferred_element_type=F32)


def _merge(x2d, ya2d, rb2d, proj, wa, wb, wo, layer):
    m = x2d.shape[0]

    def full(shape):
        return pl.BlockSpec((None,) + shape, lambda i: (layer, 0, 0))

    return pl.pallas_call(
        _merge_body,
        grid=(m // MERGE_TM,),
        in_specs=[
            pl.BlockSpec((MERGE_TM, D_MODEL), lambda i: (i, 0)),
            pl.BlockSpec((MERGE_TM, A_OUT), lambda i: (i, 0)),
            pl.BlockSpec((MERGE_TM, R_V), lambda i: (i, 0)),
            pl.BlockSpec((MERGE_TM, D_MODEL), lambda i: (i, P_GA // D_MODEL)),
            pl.BlockSpec((MERGE_TM, D_MODEL), lambda i: (i, P_GB // D_MODEL)),
            full((A_OUT, D_MODEL)), full((R_V, D_MODEL)), full((D_MODEL, D_MODEL)),
        ],
        out_specs=pl.BlockSpec((MERGE_TM, D_MODEL), lambda i: (i, 0)),
        out_shape=jax.ShapeDtypeStruct((m, D_MODEL), F32),
        compiler_params=pltpu.CompilerParams(
            dimension_semantics=("parallel",), vmem_limit_bytes=VMEM_LIMIT),
    )(x2d, ya2d, rb2d, proj, proj, wa, wb, wo)


FFN_TM = 1024
FFN_TF = 2048


def _ffn_body(x_ref, g_ref, wu_ref, wd_ref, fg_ref, o_ref, h_ref, *, final):
    c = pl.program_id(1)

    @pl.when(c == 0)
    def _start():
        x = x_ref[...]
        h_ref[...] = _rms(x, g_ref[...]).astype(BF16)
        o_ref[...] = x

    up = jnp.dot(h_ref[...], wu_ref[...], preferred_element_type=F32)
    up = jnp.square(jnp.maximum(up, 0.0)).astype(BF16)
    o_ref[...] += jnp.dot(up, wd_ref[...], preferred_element_type=F32)

    if final:
        @pl.when(c == pl.num_programs(1) - 1)
        def _final_norm():
            o_ref[...] = _rms(o_ref[...], fg_ref[...])


def _ffn(x2d, gain, wu, wd, layer, final_gain, final):
    m = x2d.shape[0]
    return pl.pallas_call(
        functools.partial(_ffn_body, final=final),
        grid=(m // FFN_TM, D_FF // FFN_TF),
        in_specs=[
            pl.BlockSpec((FFN_TM, D_MODEL), lambda i, c: (i, 0)),
            pl.BlockSpec((1, D_MODEL), lambda i, c: (0, 0)),
            pl.BlockSpec((None, D_MODEL, FFN_TF), lambda i, c: (layer, 0, c)),
            pl.BlockSpec((None, FFN_TF, D_MODEL), lambda i, c: (layer, c, 0)),
            pl.BlockSpec((1, D_MODEL), lambda i, c: (0, 0)),
        ],
        out_specs=pl.BlockSpec((FFN_TM, D_MODEL), lambda i, c: (i, 0)),
        out_shape=jax.ShapeDtypeStruct((m, D_MODEL), F32),
        scratch_shapes=[pltpu.VMEM((FFN_TM, D_MODEL), BF16)],
        compiler_params=pltpu.CompilerParams(
            dimension_semantics=("parallel", "arbitrary"), vmem_limit_bytes=VMEM_LIMIT),
    )(x2d, gain, wu, wd, final_gain)


def kernel(x, mix_norm, w_in, w_a, w_b, w_o, ffn_norm, w_up, w_down, final_norm):
    batch, seq, _ = x.shape
    depth = w_in.shape[0]
    assert seq % A_SPAN == 0 and seq % RET_TB == 0
    tables = _rotary_tables(seq)
    final_gain = final_norm.reshape(1, D_MODEL)
    x2d = x.reshape(batch * seq, D_MODEL)
    w_in, w_a, w_b, w_o, w_up, w_down = (_arrange_w_in(w_in),) + tuple(
        w.astype(BF16) for w in (w_a, w_b, w_o, w_up, w_down))
    for layer in range(depth):
        proj = _in_proj(x2d, mix_norm[layer].reshape(1, D_MODEL), w_in, layer, tables, seq)
        ya = _attention_a(proj, batch, seq).reshape(batch * seq, A_OUT)
        rb = _retention(proj, batch, seq).reshape(batch * seq, R_V)
        x2d = _merge(x2d, ya, rb, proj, w_a, w_b, w_o, layer)
        x2d = _ffn(x2d, ffn_norm[layer].reshape(1, D_MODEL), w_up, w_down, layer, final_gain,
                   final=(layer == depth - 1))
    return x2d.reshape(batch, seq, D_MODEL)
```

```python
import functools

import numpy as np
import jax
import jax.numpy as jnp
from jax import lax
from jax.experimental import pallas as pl
from jax.experimental.pallas import tpu as pltpu

F32 = jnp.float32
BF16 = jnp.bfloat16

D_MODEL = 1024
N_GROUPS = 3
DILATIONS = (1, 4, 16)
A_HEADS = 8
A_HEAD_DIM = 64
A_GROUP_W = A_HEADS * A_HEAD_DIM
A_WIDTH = N_GROUPS * A_GROUP_W
A_OUT = A_GROUP_W
ROT_DIM = A_HEAD_DIM // 4
ROPE_THETA = 500000.0
ATT_BLOCK = 128
NEG_INF = -1e30
R_HEADS = 4
R_KEY_DIM = 128
R_VAL_DIM = 256
R_QK = R_HEADS * R_KEY_DIM
R_V = R_HEADS * R_VAL_DIM
R_CHUNK = 128
R_ROT_BASE = 10000.0
D_FF = 4 * D_MODEL
NORM_EPS = 1e-6

OFF_QA = 0
OFF_KA = A_WIDTH
OFF_VA = 2 * A_WIDTH
OFF_QR = 3 * A_WIDTH
OFF_KR = OFF_QR + R_QK
OFF_VR = OFF_KR + R_QK
OFF_GR = OFF_VR + R_V
OFF_GA = OFF_GR + R_V
OFF_GB = OFF_GA + D_MODEL

LANES = 128
VMEM_LIMIT = 56 * 1024 * 1024

A_SPAN = ATT_BLOCK * max(DILATIONS)
A_UNITS = A_SPAN // ATT_BLOCK

IN_TM = A_SPAN
IN_TN = 512
IN_CHUNKS = (512, 512, 512, 256, 256)
X_CHUNKS = D_MODEL // LANES

P_GA = 0
P_GB = P_GA + D_MODEL
P_VR = P_GB + D_MODEL
P_GR = P_VR + R_V
P_QR = P_GR + R_V
P_KR = P_QR + R_QK
P_QKV = tuple(P_KR + R_QK + g * 3 * A_GROUP_W for g in range(N_GROUPS))
P_WIDTH = P_QKV[-1] + 3 * A_GROUP_W

_TAB_R = 2 * N_GROUPS
_TAB_NONE = 2 * N_GROUPS + 1
TILE_TABLE = ((_TAB_NONE,) * (P_QR // IN_TN) + (_TAB_R,) * (2 * R_QK // IN_TN)
              + tuple(t for g in range(N_GROUPS) for t in (g, N_GROUPS + g, _TAB_NONE)))
LOG2_E = 1.4426950408889634
A_Q_SCALE = A_HEAD_DIM ** -0.5 * LOG2_E
TILE_ORDER = (0,) * (P_QKV[0] // IN_TN) + tuple(g for g in range(N_GROUPS) for _ in range(3))
N_TILES = P_WIDTH // IN_TN


def _rms(x, gain):
    ms = jnp.mean(x * x, axis=-1, keepdims=True)
    return x * lax.rsqrt(ms + NORM_EPS) * gain


def _step_lookup(j, values):
    out = values[-1]
    for t in range(len(values) - 2, -1, -1):
        if values[t] != values[t + 1]:
            out = jnp.where(j <= t, values[t], out)
    return out


def _residue_pieces(dil, src_dil, u):
    ratio = dil // src_dil
    piece = ATT_BLOCK // ratio
    ss, r = divmod(u, dil)
    for k in range(ratio):
        src_block = (ss * ratio + k) * src_dil + r % src_dil
        yield (pl.ds(src_block * ATT_BLOCK + r // src_dil, piece, stride=ratio),
               pl.ds(u * ATT_BLOCK + k * piece, piece))


def _in_proj_body(*refs):
    x_refs = refs[:X_CHUNKS]
    g_ref, w_ref, cos_ref, sin_ref, o_ref, h_ref, inv_ref, stage_ref = refs[X_CHUNKS:]
    j = pl.program_id(1)

    @pl.when(j == 0)
    def _norm():
        gains = [g_ref[:, c * LANES:(c + 1) * LANES] for c in range(X_CHUNKS)]
        for u in range(A_UNITS):
            rows = pl.ds(u * ATT_BLOCK, ATT_BLOCK)
            xc = [x_refs[c][rows, :] for c in range(X_CHUNKS)]
            sq = xc[0] * xc[0]
            for c in range(1, X_CHUNKS):
                sq = sq + xc[c] * xc[c]
            inv = lax.rsqrt(jnp.sum(sq, axis=-1, keepdims=True) * (1.0 / D_MODEL) + NORM_EPS)
            inv = jnp.broadcast_to(inv, (ATT_BLOCK, LANES))
            inv_ref[rows, :] = inv
            for c in range(X_CHUNKS):
                h_ref[0, rows, c * LANES:(c + 1) * LANES] = (xc[c] * inv * gains[c]).astype(BF16)
        for g in range(1, N_GROUPS):
            for u in range(A_UNITS):
                for src, dst in _residue_pieces(DILATIONS[g], DILATIONS[g - 1], u):
                    if g == 1:
                        inv = inv_ref[src, :]
                    for c in range(X_CHUNKS):
                        lanes = slice(c * LANES, (c + 1) * LANES)
                        if g == 1:
                            y = x_refs[c][src, :] * inv * gains[c]
                            if g + 1 < N_GROUPS:
                                stage_ref[c, dst, :] = y
                        else:
                            y = stage_ref[c, src, :]
                        h_ref[g, dst, lanes] = y.astype(BF16)

    def project(rotary, order):
        starts = np.cumsum((0,) + IN_CHUNKS)
        for start, size in zip(starts, IN_CHUNKS):
            rows = pl.ds(int(start), size)
            acc = jnp.dot(h_ref[order, rows, :], w_ref[...], preferred_element_type=F32)
            if not rotary:
                o_ref[rows, :] = acc.astype(BF16)
                continue
            for c in range(IN_TN // LANES):
                t = acc[:, c * LANES:(c + 1) * LANES]
                o_ref[rows, c * LANES:(c + 1) * LANES] = (
                    t * cos_ref[rows, :] + pltpu.roll(t, LANES // 2, 1) * sin_ref[rows, :]).astype(BF16)

    order = _step_lookup(j, TILE_ORDER)
    has_rotary = _step_lookup(j, tuple(int(t != _TAB_NONE) for t in TILE_TABLE)) == 1
    pl.when(has_rotary)(functools.partial(project, True, order))
    pl.when(jnp.logical_not(has_rotary))(functools.partial(project, False, order))


_SRC_TILE = tuple(off // IN_TN + t for off, n in ((OFF_GA, 2), (OFF_GB, 2), (OFF_VR, 2), (OFF_GR, 2), (OFF_QR, 1),
                                                   (OFF_KR, 1)) for t in range(n)) + tuple(
    (off + g * A_GROUP_W) // IN_TN for g in range(N_GROUPS) for off in (OFF_QA, OFF_KA, OFF_VA))
_PAIR_TILE = tuple(int(t < 2 * N_GROUPS) for t in TILE_TABLE)


def _arrange_body(w_ref, o_ref):
    j = pl.program_id(1)
    pair = _step_lookup(j, _PAIR_TILE) == 1

    @pl.when(jnp.logical_not(pair))
    def _plain():
        o_ref[...] = w_ref[...].astype(BF16)

    @pl.when(pair)
    def _pair():
        half = ROT_DIM // 2
        lane = lax.broadcasted_iota(jnp.int32, (D_MODEL, LANES), 1)
        to_b_x1 = (lane >= half) & (lane < ROT_DIM)
        to_a_x2 = (lane >= A_HEAD_DIM) & (lane < A_HEAD_DIM + half)
        for c in range(IN_TN // LANES):
            t = w_ref[:, c * LANES:(c + 1) * LANES]
            moved = jnp.where(to_b_x1, pltpu.roll(t, LANES - A_HEAD_DIM + half, 1),
                              pltpu.roll(t, A_HEAD_DIM - half, 1))
            o_ref[:, c * LANES:(c + 1) * LANES] = jnp.where(to_b_x1 | to_a_x2, moved, t).astype(BF16)


def _arrange_w_in(w):
    depth = w.shape[0]
    return pl.pallas_call(
        _arrange_body,
        grid=(depth, N_TILES),
        in_specs=[pl.BlockSpec((None, D_MODEL, IN_TN), lambda l, j: (l, 0, _step_lookup(j, _SRC_TILE)))],
        out_specs=pl.BlockSpec((None, D_MODEL, IN_TN), lambda l, j: (l, 0, j)),
        out_shape=jax.ShapeDtypeStruct((depth, D_MODEL, P_WIDTH), BF16),
        compiler_params=pltpu.CompilerParams(
            dimension_semantics=("parallel", "parallel"), vmem_limit_bytes=VMEM_LIMIT),
    )(w)


@functools.lru_cache(maxsize=None)
def _rotary_tables(seq):
    f32 = np.float32
    half = ROT_DIM // 2
    inv = (1.0 / (f32(ROPE_THETA) ** (np.arange(half, dtype=f32) / f32(half)))).astype(f32)
    rest = A_HEAD_DIM - ROT_DIM
    ones, zeros = np.ones((seq, rest), f32), np.zeros((seq, rest), f32)
    cos_tabs, sin_tabs = [], []
    for dil in DILATIONS:
        span = ATT_BLOCK * dil
        pos = np.arange(seq, dtype=f32).reshape(seq // span, ATT_BLOCK, dil).transpose(0, 2, 1).reshape(seq, 1)
        ang = pos * inv[None, :]
        cos, sin = np.cos(ang), np.sin(ang)
        cos_tabs.append(np.concatenate([cos, cos, ones, cos, cos, ones], axis=1))
        sin_tabs.append(np.concatenate([-sin, -sin, zeros, sin, sin, zeros], axis=1))
    cos_tabs = [t * f32(A_Q_SCALE) for t in cos_tabs] + cos_tabs
    sin_tabs = [t * f32(A_Q_SCALE) for t in sin_tabs] + sin_tabs
    half_r = R_KEY_DIM // 2
    inv_r = (1.0 / (f32(R_ROT_BASE) ** (np.arange(half_r, dtype=f32) / f32(half_r)))).astype(f32)
    ang_r = np.arange(seq, dtype=f32)[:, None] * inv_r[None, :]
    cos_tabs += [np.concatenate([np.cos(ang_r), np.cos(ang_r)], axis=1), np.ones((seq, LANES), f32)]
    sin_tabs += [np.concatenate([-np.sin(ang_r), np.sin(ang_r)], axis=1), np.zeros((seq, LANES), f32)]
    return np.stack(cos_tabs).astype(f32), np.stack(sin_tabs).astype(f32)


def _in_proj(x2d, gain, w_bf16, layer, tables, seq):
    m = x2d.shape[0]
    tiles_per_seq = seq // IN_TM
    tab_spec = pl.BlockSpec((None, IN_TM, LANES),
                            lambda i, j: (_step_lookup(j, TILE_TABLE), i % tiles_per_seq, 0))
    x_specs = [pl.BlockSpec((IN_TM, LANES), lambda i, j, c=c: (i, c)) for c in range(X_CHUNKS)]
    return pl.pallas_call(
        _in_proj_body,
        grid=(m // IN_TM, P_WIDTH // IN_TN),
        in_specs=x_specs + [
            pl.BlockSpec((1, D_MODEL), lambda i, j: (0, 0)),
            pl.BlockSpec((None, D_MODEL, IN_TN), lambda i, j: (layer, 0, j)),
            tab_spec, tab_spec,
        ],
        out_specs=pl.BlockSpec((IN_TM, IN_TN), lambda i, j: (i, j)),
        out_shape=jax.ShapeDtypeStruct((m, P_WIDTH), BF16),
        scratch_shapes=[pltpu.VMEM((N_GROUPS, IN_TM, D_MODEL), BF16),
                        pltpu.VMEM((IN_TM, LANES), F32),
                        pltpu.VMEM((X_CHUNKS, IN_TM, LANES), F32)],
        compiler_params=pltpu.CompilerParams(
            dimension_semantics=("parallel", "arbitrary"), vmem_limit_bytes=VMEM_LIMIT),
    )(*([x2d] * X_CHUNKS), gain, w_bf16, *tables)


A_HEADS_PER_STEP = LANES // A_HEAD_DIM
A_QUAD = 8


def _attn_body(*refs):
    q_in = refs[0:3]
    kc_in = refs[3:6]
    kp_in = refs[6:9]
    vc_in = refs[9:12]
    vp_in = refs[12:15]
    o_ref = refs[15]
    ro, rl, bias, s_buf, p_buf, m_buf = refs[16:22]
    so_buf, sl_buf, sy_buf = refs[22:25]
    jt = pl.program_id(1)

    qi = lax.broadcasted_iota(jnp.int32, (ATT_BLOCK, 2 * ATT_BLOCK), 0)
    kj = lax.broadcasted_iota(jnp.int32, (ATT_BLOCK, 2 * ATT_BLOCK), 1)
    dist = qi + ATT_BLOCK - kj
    band = (dist >= 0) & (dist <= ATT_BLOCK)
    bias[0] = jnp.where(band & (kj >= ATT_BLOCK), 0.0, NEG_INF).astype(BF16)
    bias[1] = jnp.where(band, 0.0, NEG_INF).astype(BF16)
    lane = lax.broadcasted_iota(jnp.int32, (ATT_BLOCK, LANES), 1)

    half = ROT_DIM // 2
    first_head = (lane < half) | ((lane >= ROT_DIM) & (lane < A_HEAD_DIM + half))
    head_lanes = (first_head, jnp.logical_not(first_head))
    first = lane < A_HEAD_DIM
    v_lane = lax.broadcasted_iota(jnp.int32, (2 * ATT_BLOCK, LANES), 1)
    v_lanes = (v_lane < A_HEAD_DIM, v_lane >= A_HEAD_DIM)

    def rows_of(u):
        return pl.ds(u * ATT_BLOCK, ATT_BLOCK)

    def previous(cur, prev, d, u):
        return prev[rows_of(u), :] if u < d else cur[rows_of(u - d), :]

    def stage_scores(slot, g, u):
        d = DILATIONS[g]
        b = bias[jnp.minimum(jt, 1)] if u < d else bias[1]
        q = q_in[g][rows_of(u), :]
        k2 = jnp.concatenate([previous(kc_in[g], kp_in[g], d, u), kc_in[g][rows_of(u), :]], axis=0)
        for h in range(A_HEADS_PER_STEP):
            qm = jnp.where(head_lanes[h], q, jnp.zeros_like(q))
            s = lax.dot_general(qm, k2, (((1,), (1,)), ((), ())), preferred_element_type=F32)
            s_buf[slot, h] = s.astype(BF16) + b

    def stage_softmax(slot, g, u):
        ms = []
        for h in range(A_HEADS_PER_STEP):
            s = s_buf[slot, h]
            m = jnp.max(s, axis=1, keepdims=True)
            p_buf[slot, h] = jnp.exp2(s - m)
            ms.append(m.astype(F32))
        m_buf[slot] = jnp.where(first, ms[0], ms[1])

    def stage_out(slot, g, u):
        d = DILATIONS[g]
        v2 = jnp.concatenate([previous(vc_in[g], vp_in[g], d, u), vc_in[g][rows_of(u), :]], axis=0)
        pvs = [jnp.dot(p_buf[slot, h], jnp.where(v_lanes[h], v2, jnp.ones_like(v2)), preferred_element_type=F32)
               for h in range(A_HEADS_PER_STEP)]
        out = jnp.where(first, pvs[0], pvs[1])
        den = pltpu.roll(jnp.where(first, pvs[1], pvs[0]), A_HEAD_DIM, 1)
        ro[g, rows_of(u), :] = out / den
        rl[g, rows_of(u), :] = m_buf[slot] + jnp.log(den) * LOG2_E

    quads = [(g, qd) for g in range(N_GROUPS) for qd in range(A_UNITS // A_QUAD)]
    always = pl.program_id(0) >= 0
    stages = (stage_scores, stage_softmax, stage_out)
    for trip in range(len(quads) + len(stages) - 1):

        @pl.when(always)
        def _trip(trip=trip):
            for age in range(len(stages) - 1, -1, -1):
                n = trip - age
                if 0 <= n < len(quads):
                    g, qd = quads[n]
                    for i in range(A_QUAD):
                        stages[age]((n % 2) * A_QUAD + i, g, qd * A_QUAD + i)

    d2, d3 = DILATIONS[1], DILATIONS[2]
    for u in range(A_UNITS):
        for src, dst in _residue_pieces(d2, DILATIONS[0], u):
            so_buf[dst, :] = ro[0, src, :]
            sl_buf[dst, :] = rl[0, src, :]
    for r3 in range(A_UNITS):
        blk = pl.ds(r3 * ATT_BLOCK, ATT_BLOCK)
        pieces = list(_residue_pieces(d3, d2, r3))
        o1 = jnp.concatenate([so_buf[src, :] for src, _ in pieces], axis=0)
        l1 = jnp.concatenate([sl_buf[src, :] for src, _ in pieces], axis=0)
        o2 = jnp.concatenate([ro[1, src, :] for src, _ in pieces], axis=0)
        l2 = jnp.concatenate([rl[1, src, :] for src, _ in pieces], axis=0)
        o3, l3 = ro[2, blk, :], rl[2, blk, :]
        mx = jnp.maximum(jnp.maximum(l1, l2), l3)
        e1, e2, e3 = jnp.exp2(l1 - mx), jnp.exp2(l2 - mx), jnp.exp2(l3 - mx)
        y = (e1 * o1 + e2 * o2 + e3 * o3) / (e1 + e2 + e3)
        for k, (src, _) in enumerate(pieces):
            sy_buf[src, :] = y[k * ATT_BLOCK // len(pieces):(k + 1) * ATT_BLOCK // len(pieces), :]
    for u in range(A_UNITS):
        for src, dst in _residue_pieces(d2, DILATIONS[0], u):
            o_ref[src, :] = sy_buf[dst, :]


def _attention_a(proj, batch, seq):
    p = proj.reshape(batch, seq, P_WIDTH)

    def cur_spec(g, part):
        col = (P_QKV[g] + part * A_GROUP_W) // LANES
        return pl.BlockSpec((None, A_SPAN, LANES), lambda b, j, hp: (b, j, col + hp))

    def prev_spec(g, part):
        col = (P_QKV[g] + part * A_GROUP_W) // LANES
        span = ATT_BLOCK * DILATIONS[g]
        per = A_SPAN // span
        return pl.BlockSpec((None, span, LANES),
                            lambda b, j, hp: (b, jnp.maximum(j * per - 1, 0), col + hp))

    groups = range(N_GROUPS)
    in_specs = ([cur_spec(g, 0) for g in groups] + [cur_spec(g, 1) for g in groups]
                + [prev_spec(g, 1) for g in groups] + [cur_spec(g, 2) for g in groups]
                + [prev_spec(g, 2) for g in groups])
    return pl.pallas_call(
        _attn_body,
        grid=(batch, seq // A_SPAN, A_GROUP_W // LANES),
        in_specs=in_specs,
        out_specs=pl.BlockSpec((None, A_SPAN, LANES), lambda b, j, hp: (b, j, hp)),
        out_shape=jax.ShapeDtypeStruct((batch, seq, A_OUT), F32),
        scratch_shapes=[pltpu.VMEM((N_GROUPS, A_SPAN, LANES), F32),
                        pltpu.VMEM((N_GROUPS, A_SPAN, LANES), F32),
                        pltpu.VMEM((2, ATT_BLOCK, 2 * ATT_BLOCK), BF16),
                        pltpu.VMEM((2 * A_QUAD, A_HEADS_PER_STEP, ATT_BLOCK, 2 * ATT_BLOCK), BF16),
                        pltpu.VMEM((2 * A_QUAD, A_HEADS_PER_STEP, ATT_BLOCK, 2 * ATT_BLOCK), BF16),
                        pltpu.VMEM((2 * A_QUAD, ATT_BLOCK, LANES), F32)]
                       + [pltpu.VMEM((A_SPAN, LANES), F32)] * 3,
        compiler_params=pltpu.CompilerParams(
            dimension_semantics=("parallel", "parallel", "parallel"), vmem_limit_bytes=VMEM_LIMIT),
    )(*([p] * 15))


RET_TB = 4096
_LOG_GAMMA = [float(v) for v in np.log1p(-(2.0 ** (-5.0 - np.arange(R_HEADS)))).astype(np.float32)]


def _ret_body(q_ref, k_ref, v_ref, g_ref, o_ref, st_ref):
    h = pl.program_id(1)
    t = pl.program_id(2)

    @pl.when(t == 0)
    def _reset():
        st_ref[...] = jnp.zeros_like(st_ref)

    lg = jnp.float32(_LOG_GAMMA[R_HEADS - 1])
    for hh in range(R_HEADS - 2, -1, -1):
        lg = jnp.where(h == hh, jnp.float32(_LOG_GAMMA[hh]), lg)
    row = lax.broadcasted_iota(jnp.int32, (R_CHUNK, R_CHUNK), 0).astype(F32)
    colv = lax.broadcasted_iota(jnp.int32, (R_CHUNK, R_CHUNK), 1).astype(F32)
    diff = row - colv
    decay = jnp.where(diff >= 0, jnp.exp(diff * lg), 0.0)
    zeta = jnp.exp((R_CHUNK - 1 - row) * lg)
    xi = jnp.exp((row + 1.0) * lg)
    chunk_decay = jnp.exp(jnp.full((R_KEY_DIM, R_VAL_DIM), float(R_CHUNK), F32) * lg)
    kscale = R_KEY_DIM ** -0.5

    for c in range(RET_TB // R_CHUNK):
        sl = slice(c * R_CHUNK, (c + 1) * R_CHUNK)
        q = q_ref[sl, :].astype(F32)
        k = k_ref[sl, :].astype(F32) * kscale
        v = v_ref[sl, :]
        s = lax.dot_general(q.astype(BF16), k.astype(BF16), (((1,), (1,)), ((), ())),
                            preferred_element_type=F32) * decay
        inner = jnp.dot(s.astype(BF16), v, preferred_element_type=F32)
        state = st_ref[...]
        cross = jnp.dot((q * xi).astype(BF16), state.astype(BF16), preferred_element_type=F32)
        kz_t = (k * zeta).T.astype(BF16)
        st_ref[...] = state * chunk_decay + jnp.dot(kz_t, v, preferred_element_type=F32)
        out = inner + cross
        ms = jnp.mean(out * out, axis=-1, keepdims=True)
        gate = g_ref[sl, :].astype(F32)
        o_ref[sl, :] = (gate * jax.nn.sigmoid(gate) * (out * lax.rsqrt(ms + NORM_EPS))).astype(BF16)


def _retention(proj, batch, seq):
    p = proj.reshape(batch, seq, P_WIDTH)
    return pl.pallas_call(
        _ret_body,
        grid=(batch, R_HEADS, seq // RET_TB),
        in_specs=[
            pl.BlockSpec((None, RET_TB, R_KEY_DIM), lambda b, h, t: (b, t, P_QR // R_KEY_DIM + h)),
            pl.BlockSpec((None, RET_TB, R_KEY_DIM), lambda b, h, t: (b, t, P_KR // R_KEY_DIM + h)),
            pl.BlockSpec((None, RET_TB, R_VAL_DIM), lambda b, h, t: (b, t, P_VR // R_VAL_DIM + h)),
            pl.BlockSpec((None, RET_TB, R_VAL_DIM), lambda b, h, t: (b, t, P_GR // R_VAL_DIM + h)),
        ],
        out_specs=pl.BlockSpec((None, RET_TB, R_VAL_DIM), lambda b, h, t: (b, t, h)),
        out_shape=jax.ShapeDtypeStruct((batch, seq, R_V), BF16),
        scratch_shapes=[pltpu.VMEM((R_KEY_DIM, R_VAL_DIM), F32)],
        compiler_params=pltpu.CompilerParams(
            dimension_semantics=("parallel", "parallel", "arbitrary"), vmem_limit_bytes=VMEM_LIMIT),
    )(p, p, p, p)


MERGE_TM = 1024


def _merge_body(x_ref, ya_ref, rb_ref, ga_ref, gb_ref, wa_ref, wb_ref, wo_ref, o_ref):
    ya = jnp.dot(ya_ref[...].astype(BF16), wa_ref[...], preferred_element_type=F32)
    yb = jnp.dot(rb_ref[...], wb_ref[...], preferred_element_type=F32)
    ga = ga_ref[...].astype(F32)
    gb = gb_ref[...].astype(F32)
    merged = jax.nn.sigmoid(ga) * ya + jax.nn.sigmoid(gb) * yb
    o_ref[...] = x_ref[...] + jnp.dot(merged.astype(BF16), wo_ref[...], preferred_element_type=F32)


def _merge(x2d, ya2d, rb2d, proj, wa, wb, wo, layer):
    m = x2d.shape[0]

    def full(shape):
        return pl.BlockSpec((None,) + shape, lambda i: (layer, 0, 0))

    return pl.pallas_call(
        _merge_body,
        grid=(m // MERGE_TM,),
        in_specs=[
            pl.BlockSpec((MERGE_TM, D_MODEL), lambda i: (i, 0)),
            pl.BlockSpec((MERGE_TM, A_OUT), lambda i: (i, 0)),
            pl.BlockSpec((MERGE_TM, R_V), lambda i: (i, 0)),
            pl.BlockSpec((MERGE_TM, D_MODEL), lambda i: (i, P_GA // D_MODEL)),
            pl.BlockSpec((MERGE_TM, D_MODEL), lambda i: (i, P_GB // D_MODEL)),
            full((A_OUT, D_MODEL)), full((R_V, D_MODEL)), full((D_MODEL, D_MODEL)),
        ],
        out_specs=pl.BlockSpec((MERGE_TM, D_MODEL), lambda i: (i, 0)),
        out_shape=jax.ShapeDtypeStruct((m, D_MODEL), F32),
        compiler_params=pltpu.CompilerParams(
            dimension_semantics=("parallel",), vmem_limit_bytes=VMEM_LIMIT),
    )(x2d, ya2d, rb2d, proj, proj, wa, wb, wo)


FFN_TM = 1024
FFN_TF = 2048


def _ffn_body(x_ref, g_ref, wu_ref, wd_ref, fg_ref, o_ref, h_ref, *, final):
    x = x_ref[...]
    h_ref[...] = _rms(x, g_ref[...]).astype(BF16)
    acc = x
    for c in range(D_FF // FFN_TF):
        up = jnp.dot(h_ref[...], wu_ref[:, c * FFN_TF:(c + 1) * FFN_TF], preferred_element_type=F32)
        up = jnp.square(jnp.maximum(up, 0.0)).astype(BF16)
        acc = acc + jnp.dot(up, wd_ref[c * FFN_TF:(c + 1) * FFN_TF, :], preferred_element_type=F32)
    o_ref[...] = _rms(acc, fg_ref[...]) if final else acc


def _ffn(x2d, gain, wu, wd, layer, final_gain, final):
    m = x2d.shape[0]
    return pl.pallas_call(
        functools.partial(_ffn_body, final=final),
        grid=(m // FFN_TM,),
        in_specs=[
            pl.BlockSpec((FFN_TM, D_MODEL), lambda i: (i, 0)),
            pl.BlockSpec((1, D_MODEL), lambda i: (0, 0)),
            pl.BlockSpec((None, D_MODEL, D_FF), lambda i: (layer, 0, 0), pipeline_mode=pl.Buffered(1)),
            pl.BlockSpec((None, D_FF, D_MODEL), lambda i: (layer, 0, 0), pipeline_mode=pl.Buffered(1)),
            pl.BlockSpec((1, D_MODEL), lambda i: (0, 0)),
        ],
        out_specs=pl.BlockSpec((FFN_TM, D_MODEL), lambda i: (i, 0)),
        out_shape=jax.ShapeDtypeStruct((m, D_MODEL), F32),
        scratch_shapes=[pltpu.VMEM((FFN_TM, D_MODEL), BF16)],
        compiler_params=pltpu.CompilerParams(
            dimension_semantics=("parallel",), vmem_limit_bytes=VMEM_LIMIT),
    )(x2d, gain, wu, wd, final_gain)


def kernel(x, mix_norm, w_in, w_a, w_b, w_o, ffn_norm, w_up, w_down, final_norm):
    batch, seq, _ = x.shape
    depth = w_in.shape[0]
    assert seq % A_SPAN == 0 and seq % RET_TB == 0
    tables = _rotary_tables(seq)
    final_gain = final_norm.reshape(1, D_MODEL)
    x2d = x.reshape(batch * seq, D_MODEL)
    w_in, w_a, w_b, w_o, w_up, w_down = (_arrange_w_in(w_in),) + tuple(
        w.astype(BF16) for w in (w_a, w_b, w_o, w_up, w_down))
    for layer in range(depth):
        proj = _in_proj(x2d, mix_norm[layer].reshape(1, D_MODEL), w_in, layer, tables, seq)
        ya = _attention_a(proj, batch, seq).reshape(batch * seq, A_OUT)
        rb = _retention(proj, batch, seq).reshape(batch * seq, R_V)
        x2d = _merge(x2d, ya, rb, proj, w_a, w_b, w_o, layer)
        x2d = _ffn(x2d, ffn_norm[layer].reshape(1, D_MODEL), w_up, w_down, layer, final_gain,
                   final=(layer == depth - 1))
    return x2d.reshape(batch, seq, D_MODEL)
```

```python
import functools

import numpy as np
import jax
import jax.numpy as jnp
from jax import lax
from jax.experimental import pallas as pl
from jax.experimental.pallas import tpu as pltpu

F32 = jnp.float32
BF16 = jnp.bfloat16

D_MODEL = 1024
N_GROUPS = 3
DILATIONS = (1, 4, 16)
A_HEADS = 8
A_HEAD_DIM = 64
A_GROUP_W = A_HEADS * A_HEAD_DIM
A_WIDTH = N_GROUPS * A_GROUP_W
A_OUT = A_GROUP_W
ROT_DIM = A_HEAD_DIM // 4
ROPE_THETA = 500000.0
ATT_BLOCK = 128
NEG_INF = -1e30
R_HEADS = 4
R_KEY_DIM = 128
R_VAL_DIM = 256
R_QK = R_HEADS * R_KEY_DIM
R_V = R_HEADS * R_VAL_DIM
R_CHUNK = 128
R_ROT_BASE = 10000.0
D_FF = 4 * D_MODEL
NORM_EPS = 1e-6

OFF_QA = 0
OFF_KA = A_WIDTH
OFF_VA = 2 * A_WIDTH
OFF_QR = 3 * A_WIDTH
OFF_KR = OFF_QR + R_QK
OFF_VR = OFF_KR + R_QK
OFF_GR = OFF_VR + R_V
OFF_GA = OFF_GR + R_V
OFF_GB = OFF_GA + D_MODEL

LANES = 128
VMEM_LIMIT = 56 * 1024 * 1024

A_SPAN = ATT_BLOCK * max(DILATIONS)
A_UNITS = A_SPAN // ATT_BLOCK

IN_TM = A_SPAN
IN_TN = 512
IN_CHUNKS = (512, 512, 512, 256, 256)
X_CHUNKS = D_MODEL // LANES

P_GA = 0
P_GB = P_GA + D_MODEL
P_VR = P_GB + D_MODEL
P_GR = P_VR + R_V
P_QR = P_GR + R_V
P_KR = P_QR + R_QK
P_QKV = tuple(P_KR + R_QK + g * 3 * A_GROUP_W for g in range(N_GROUPS))
P_WIDTH = P_QKV[-1] + 3 * A_GROUP_W

_TAB_R = 2 * N_GROUPS
_TAB_NONE = 2 * N_GROUPS + 1
TILE_TABLE = ((_TAB_NONE,) * (P_QR // IN_TN) + (_TAB_R,) * (2 * R_QK // IN_TN)
              + tuple(t for g in range(N_GROUPS) for t in (g, N_GROUPS + g, _TAB_NONE)))
LOG2_E = 1.4426950408889634
A_Q_SCALE = A_HEAD_DIM ** -0.5 * LOG2_E
TILE_ORDER = (0,) * (P_QKV[0] // IN_TN) + tuple(g for g in range(N_GROUPS) for _ in range(3))
N_TILES = P_WIDTH // IN_TN


def _rms(x, gain):
    ms = jnp.mean(x * x, axis=-1, keepdims=True)
    return x * lax.rsqrt(ms + NORM_EPS) * gain


def _step_lookup(j, values):
    out = values[-1]
    for t in range(len(values) - 2, -1, -1):
        if values[t] != values[t + 1]:
            out = jnp.where(j <= t, values[t], out)
    return out


def _residue_pieces(dil, src_dil, u):
    ratio = dil // src_dil
    piece = ATT_BLOCK // ratio
    ss, r = divmod(u, dil)
    for k in range(ratio):
        src_block = (ss * ratio + k) * src_dil + r % src_dil
        yield (pl.ds(src_block * ATT_BLOCK + r // src_dil, piece, stride=ratio),
               pl.ds(u * ATT_BLOCK + k * piece, piece))


def _in_proj_body(*refs):
    x_refs = refs[:X_CHUNKS]
    g_ref, w_ref, cos_ref, sin_ref, o_ref, h_ref, inv_ref, stage_ref = refs[X_CHUNKS:]
    j = pl.program_id(1)

    @pl.when(j == 0)
    def _norm():
        gains = [g_ref[:, c * LANES:(c + 1) * LANES] for c in range(X_CHUNKS)]
        for u in range(A_UNITS):
            rows = pl.ds(u * ATT_BLOCK, ATT_BLOCK)
            xc = [x_refs[c][rows, :] for c in range(X_CHUNKS)]
            sq = xc[0] * xc[0]
            for c in range(1, X_CHUNKS):
                sq = sq + xc[c] * xc[c]
            inv = lax.rsqrt(jnp.sum(sq, axis=-1, keepdims=True) * (1.0 / D_MODEL) + NORM_EPS)
            inv = jnp.broadcast_to(inv, (ATT_BLOCK, LANES))
            inv_ref[rows, :] = inv
            for c in range(X_CHUNKS):
                h_ref[0, rows, c * LANES:(c + 1) * LANES] = (xc[c] * inv * gains[c]).astype(BF16)
        for g in range(1, N_GROUPS):
            for u in range(A_UNITS):
                for src, dst in _residue_pieces(DILATIONS[g], DILATIONS[g - 1], u):
                    if g == 1:
                        inv = inv_ref[src, :]
                    for c in range(X_CHUNKS):
                        lanes = slice(c * LANES, (c + 1) * LANES)
                        if g == 1:
                            y = x_refs[c][src, :] * inv * gains[c]
                            if g + 1 < N_GROUPS:
                                stage_ref[c, dst, :] = y
                        else:
                            y = stage_ref[c, src, :]
                        h_ref[g, dst, lanes] = y.astype(BF16)

    def project(rotary, order):
        starts = np.cumsum((0,) + IN_CHUNKS)
        for start, size in zip(starts, IN_CHUNKS):
            rows = pl.ds(int(start), size)
            acc = jnp.dot(h_ref[order, rows, :], w_ref[...], preferred_element_type=F32)
            if not rotary:
                o_ref[rows, :] = acc.astype(BF16)
                continue
            for c in range(IN_TN // LANES):
                t = acc[:, c * LANES:(c + 1) * LANES]
                o_ref[rows, c * LANES:(c + 1) * LANES] = (
                    t * cos_ref[rows, :] + pltpu.roll(t, LANES // 2, 1) * sin_ref[rows, :]).astype(BF16)

    order = _step_lookup(j, TILE_ORDER)
    has_rotary = _step_lookup(j, tuple(int(t != _TAB_NONE) for t in TILE_TABLE)) == 1
    pl.when(has_rotary)(functools.partial(project, True, order))
    pl.when(jnp.logical_not(has_rotary))(functools.partial(project, False, order))


_SRC_TILE = tuple(off // IN_TN + t for off, n in ((OFF_GA, 2), (OFF_GB, 2), (OFF_VR, 2), (OFF_GR, 2), (OFF_QR, 1),
                                                   (OFF_KR, 1)) for t in range(n)) + tuple(
    (off + g * A_GROUP_W) // IN_TN for g in range(N_GROUPS) for off in (OFF_QA, OFF_KA, OFF_VA))
_PAIR_TILE = tuple(int(t < 2 * N_GROUPS) for t in TILE_TABLE)


def _arrange_body(w_ref, o_ref):
    j = pl.program_id(1)
    pair = _step_lookup(j, _PAIR_TILE) == 1

    @pl.when(jnp.logical_not(pair))
    def _plain():
        o_ref[...] = w_ref[...].astype(BF16)

    @pl.when(pair)
    def _pair():
        half = ROT_DIM // 2
        lane = lax.broadcasted_iota(jnp.int32, (D_MODEL, LANES), 1)
        to_b_x1 = (lane >= half) & (lane < ROT_DIM)
        to_a_x2 = (lane >= A_HEAD_DIM) & (lane < A_HEAD_DIM + half)
        for c in range(IN_TN // LANES):
            t = w_ref[:, c * LANES:(c + 1) * LANES]
            moved = jnp.where(to_b_x1, pltpu.roll(t, LANES - A_HEAD_DIM + half, 1),
                              pltpu.roll(t, A_HEAD_DIM - half, 1))
            o_ref[:, c * LANES:(c + 1) * LANES] = jnp.where(to_b_x1 | to_a_x2, moved, t).astype(BF16)


def _arrange_w_in(w):
    depth = w.shape[0]
    return pl.pallas_call(
        _arrange_body,
        grid=(depth, N_TILES),
        in_specs=[pl.BlockSpec((None, D_MODEL, IN_TN), lambda l, j: (l, 0, _step_lookup(j, _SRC_TILE)))],
        out_specs=pl.BlockSpec((None, D_MODEL, IN_TN), lambda l, j: (l, 0, j)),
        out_shape=jax.ShapeDtypeStruct((depth, D_MODEL, P_WIDTH), BF16),
        compiler_params=pltpu.CompilerParams(
            dimension_semantics=("parallel", "parallel"), vmem_limit_bytes=VMEM_LIMIT),
    )(w)


@functools.lru_cache(maxsize=None)
def _rotary_tables(seq):
    f32 = np.float32
    half = ROT_DIM // 2
    inv = (1.0 / (f32(ROPE_THETA) ** (np.arange(half, dtype=f32) / f32(half)))).astype(f32)
    rest = A_HEAD_DIM - ROT_DIM
    ones, zeros = np.ones((seq, rest), f32), np.zeros((seq, rest), f32)
    cos_tabs, sin_tabs = [], []
    for dil in DILATIONS:
        span = ATT_BLOCK * dil
        pos = np.arange(seq, dtype=f32).reshape(seq // span, ATT_BLOCK, dil).transpose(0, 2, 1).reshape(seq, 1)
        ang = pos * inv[None, :]
        cos, sin = np.cos(ang), np.sin(ang)
        cos_tabs.append(np.concatenate([cos, cos, ones, cos, cos, ones], axis=1))
        sin_tabs.append(np.concatenate([-sin, -sin, zeros, sin, sin, zeros], axis=1))
    cos_tabs = [t * f32(A_Q_SCALE) for t in cos_tabs] + cos_tabs
    sin_tabs = [t * f32(A_Q_SCALE) for t in sin_tabs] + sin_tabs
    half_r = R_KEY_DIM // 2
    inv_r = (1.0 / (f32(R_ROT_BASE) ** (np.arange(half_r, dtype=f32) / f32(half_r)))).astype(f32)
    ang_r = np.arange(seq, dtype=f32)[:, None] * inv_r[None, :]
    cos_tabs += [np.concatenate([np.cos(ang_r), np.cos(ang_r)], axis=1), np.ones((seq, LANES), f32)]
    sin_tabs += [np.concatenate([-np.sin(ang_r), np.sin(ang_r)], axis=1), np.zeros((seq, LANES), f32)]
    return np.stack(cos_tabs).astype(f32), np.stack(sin_tabs).astype(f32)


def _in_proj(x2d, gain, w_bf16, layer, tables, seq):
    m = x2d.shape[0]
    tiles_per_seq = seq // IN_TM
    tab_spec = pl.BlockSpec((None, IN_TM, LANES),
                            lambda i, j: (_step_lookup(j, TILE_TABLE), i % tiles_per_seq, 0))
    x_specs = [pl.BlockSpec((IN_TM, LANES), lambda i, j, c=c: (i, c)) for c in range(X_CHUNKS)]
    return pl.pallas_call(
        _in_proj_body,
        grid=(m // IN_TM, P_WIDTH // IN_TN),
        in_specs=x_specs + [
            pl.BlockSpec((1, D_MODEL), lambda i, j: (0, 0)),
            pl.BlockSpec((None, D_MODEL, IN_TN), lambda i, j: (layer, 0, j)),
            tab_spec, tab_spec,
        ],
        out_specs=pl.BlockSpec((IN_TM, IN_TN), lambda i, j: (i, j)),
        out_shape=jax.ShapeDtypeStruct((m, P_WIDTH), BF16),
        scratch_shapes=[pltpu.VMEM((N_GROUPS, IN_TM, D_MODEL), BF16),
                        pltpu.VMEM((IN_TM, LANES), F32),
                        pltpu.VMEM((X_CHUNKS, IN_TM, LANES), F32)],
        compiler_params=pltpu.CompilerParams(
            dimension_semantics=("parallel", "arbitrary"), vmem_limit_bytes=VMEM_LIMIT),
    )(*([x2d] * X_CHUNKS), gain, w_bf16, *tables)


A_HEADS_PER_STEP = LANES // A_HEAD_DIM
A_QUAD = 8


def _attn_body(*refs):
    q_in = refs[0:3]
    kc_in = refs[3:6]
    kp_in = refs[6:9]
    vc_in = refs[9:12]
    vp_in = refs[12:15]
    o_ref = refs[15]
    ro, rl, bias, s_buf, p_buf, m_buf = refs[16:22]
    so_buf, sl_buf, sy_buf = refs[22:25]
    jt = pl.program_id(1)

    qi = lax.broadcasted_iota(jnp.int32, (ATT_BLOCK, 2 * ATT_BLOCK), 0)
    kj = lax.broadcasted_iota(jnp.int32, (ATT_BLOCK, 2 * ATT_BLOCK), 1)
    dist = qi + ATT_BLOCK - kj
    band = (dist >= 0) & (dist <= ATT_BLOCK)
    bias[0] = jnp.where(band & (kj >= ATT_BLOCK), 0.0, NEG_INF).astype(BF16)
    bias[1] = jnp.where(band, 0.0, NEG_INF).astype(BF16)
    lane = lax.broadcasted_iota(jnp.int32, (ATT_BLOCK, LANES), 1)

    half = ROT_DIM // 2
    first_head = (lane < half) | ((lane >= ROT_DIM) & (lane < A_HEAD_DIM + half))
    head_lanes = (first_head, jnp.logical_not(first_head))
    first = lane < A_HEAD_DIM
    v_lane = lax.broadcasted_iota(jnp.int32, (2 * ATT_BLOCK, LANES), 1)
    v_lanes = (v_lane < A_HEAD_DIM, v_lane >= A_HEAD_DIM)

    def rows_of(u):
        return pl.ds(u * ATT_BLOCK, ATT_BLOCK)

    def previous(cur, prev, d, u):
        return prev[rows_of(u), :] if u < d else cur[rows_of(u - d), :]

    def stage_scores(slot, g, u):
        d = DILATIONS[g]
        b = bias[jnp.minimum(jt, 1)] if u < d else bias[1]
        q = q_in[g][rows_of(u), :]
        k2 = jnp.concatenate([previous(kc_in[g], kp_in[g], d, u), kc_in[g][rows_of(u), :]], axis=0)
        for h in range(A_HEADS_PER_STEP):
            qm = jnp.where(head_lanes[h], q, jnp.zeros_like(q))
            s = lax.dot_general(qm, k2, (((1,), (1,)), ((), ())), preferred_element_type=F32)
            s_buf[slot, h] = s.astype(BF16) + b

    def stage_softmax(slot, g, u):
        ms = []
        for h in range(A_HEADS_PER_STEP):
            s = s_buf[slot, h]
            m = jnp.max(s, axis=1, keepdims=True)
            p_buf[slot, h] = jnp.exp2(s - m)
            ms.append(m.astype(F32))
        m_buf[slot] = jnp.where(first, ms[0], ms[1])

    def stage_out(slot, g, u):
        d = DILATIONS[g]
        v2 = jnp.concatenate([previous(vc_in[g], vp_in[g], d, u), vc_in[g][rows_of(u), :]], axis=0)
        pvs = [jnp.dot(p_buf[slot, h], jnp.where(v_lanes[h], v2, jnp.ones_like(v2)), preferred_element_type=F32)
               for h in range(A_HEADS_PER_STEP)]
        out = jnp.where(first, pvs[0], pvs[1])
        den = pltpu.roll(jnp.where(first, pvs[1], pvs[0]), A_HEAD_DIM, 1)
        ro[g, rows_of(u), :] = out / den
        rl[g, rows_of(u), :] = m_buf[slot] + jnp.log(den) * LOG2_E

    quads = [(g, qd) for g in range(N_GROUPS) for qd in range(A_UNITS // A_QUAD)]
    always = pl.program_id(0) >= 0
    stages = (stage_scores, stage_softmax, stage_out)
    for trip in range(len(quads) + len(stages) - 1):

        @pl.when(always)
        def _trip(trip=trip):
            for age in range(len(stages) - 1, -1, -1):
                n = trip - age
                if 0 <= n < len(quads):
                    g, qd = quads[n]
                    for i in range(A_QUAD):
                        stages[age]((n % 2) * A_QUAD + i, g, qd * A_QUAD + i)

    d2, d3 = DILATIONS[1], DILATIONS[2]
    for u in range(A_UNITS):
        for src, dst in _residue_pieces(d2, DILATIONS[0], u):
            so_buf[dst, :] = ro[0, src, :]
            sl_buf[dst, :] = rl[0, src, :]
    for r3 in range(A_UNITS):
        blk = pl.ds(r3 * ATT_BLOCK, ATT_BLOCK)
        pieces = list(_residue_pieces(d3, d2, r3))
        o1 = jnp.concatenate([so_buf[src, :] for src, _ in pieces], axis=0)
        l1 = jnp.concatenate([sl_buf[src, :] for src, _ in pieces], axis=0)
        o2 = jnp.concatenate([ro[1, src, :] for src, _ in pieces], axis=0)
        l2 = jnp.concatenate([rl[1, src, :] for src, _ in pieces], axis=0)
        o3, l3 = ro[2, blk, :], rl[2, blk, :]
        mx = jnp.maximum(jnp.maximum(l1, l2), l3)
        e1, e2, e3 = jnp.exp2(l1 - mx), jnp.exp2(l2 - mx), jnp.exp2(l3 - mx)
        y = (e1 * o1 + e2 * o2 + e3 * o3) / (e1 + e2 + e3)
        for k, (src, _) in enumerate(pieces):
            sy_buf[src, :] = y[k * ATT_BLOCK // len(pieces):(k + 1) * ATT_BLOCK // len(pieces), :]
    for u in range(A_UNITS):
        for src, dst in _residue_pieces(d2, DILATIONS[0], u):
            o_ref[src, :] = sy_buf[dst, :]


def _attention_a(proj, batch, seq):
    p = proj.reshape(batch, seq, P_WIDTH)

    def cur_spec(g, part):
        col = (P_QKV[g] + part * A_GROUP_W) // LANES
        return pl.BlockSpec((None, A_SPAN, LANES), lambda b, j, hp: (b, j, col + hp))

    def prev_spec(g, part):
        col = (P_QKV[g] + part * A_GROUP_W) // LANES
        span = ATT_BLOCK * DILATIONS[g]
        per = A_SPAN // span
        return pl.BlockSpec((None, span, LANES),
                            lambda b, j, hp: (b, jnp.maximum(j * per - 1, 0), col + hp))

    groups = range(N_GROUPS)
    in_specs = ([cur_spec(g, 0) for g in groups] + [cur_spec(g, 1) for g in groups]
                + [prev_spec(g, 1) for g in groups] + [cur_spec(g, 2) for g in groups]
                + [prev_spec(g, 2) for g in groups])
    return pl.pallas_call(
        _attn_body,
        grid=(batch, seq // A_SPAN, A_GROUP_W // LANES),
        in_specs=in_specs,
        out_specs=pl.BlockSpec((None, A_SPAN, LANES), lambda b, j, hp: (b, j, hp)),
        out_shape=jax.ShapeDtypeStruct((batch, seq, A_OUT), F32),
        scratch_shapes=[pltpu.VMEM((N_GROUPS, A_SPAN, LANES), F32),
                        pltpu.VMEM((N_GROUPS, A_SPAN, LANES), F32),
                        pltpu.VMEM((2, ATT_BLOCK, 2 * ATT_BLOCK), BF16),
                        pltpu.VMEM((2 * A_QUAD, A_HEADS_PER_STEP, ATT_BLOCK, 2 * ATT_BLOCK), BF16),
                        pltpu.VMEM((2 * A_QUAD, A_HEADS_PER_STEP, ATT_BLOCK, 2 * ATT_BLOCK), BF16),
                        pltpu.VMEM((2 * A_QUAD, ATT_BLOCK, LANES), F32)]
                       + [pltpu.VMEM((A_SPAN, LANES), F32)] * 3,
        compiler_params=pltpu.CompilerParams(
            dimension_semantics=("parallel", "parallel", "parallel"), vmem_limit_bytes=VMEM_LIMIT),
    )(*([p] * 15))


RET_TB = 4096
_LOG_GAMMA = [float(v) for v in np.log1p(-(2.0 ** (-5.0 - np.arange(R_HEADS)))).astype(np.float32)]


def _ret_body(q_ref, k_ref, v_ref, g_ref, o_ref, st_ref):
    h = pl.program_id(1)
    t = pl.program_id(2)

    @pl.when(t == 0)
    def _reset():
        st_ref[...] = jnp.zeros_like(st_ref)

    lg = jnp.float32(_LOG_GAMMA[R_HEADS - 1])
    for hh in range(R_HEADS - 2, -1, -1):
        lg = jnp.where(h == hh, jnp.float32(_LOG_GAMMA[hh]), lg)
    row = lax.broadcasted_iota(jnp.int32, (R_CHUNK, R_CHUNK), 0).astype(F32)
    colv = lax.broadcasted_iota(jnp.int32, (R_CHUNK, R_CHUNK), 1).astype(F32)
    diff = row - colv
    decay = jnp.where(diff >= 0, jnp.exp(diff * lg), 0.0)
    zeta = jnp.exp((R_CHUNK - 1 - row) * lg)
    xi = jnp.exp((row + 1.0) * lg)
    chunk_decay = jnp.exp(jnp.full((R_KEY_DIM, R_VAL_DIM), float(R_CHUNK), F32) * lg)
    kscale = R_KEY_DIM ** -0.5

    for c in range(RET_TB // R_CHUNK):
        sl = slice(c * R_CHUNK, (c + 1) * R_CHUNK)
        q = q_ref[sl, :].astype(F32)
        k = k_ref[sl, :].astype(F32) * kscale
        v = v_ref[sl, :]
        s = lax.dot_general(q.astype(BF16), k.astype(BF16), (((1,), (1,)), ((), ())),
                            preferred_element_type=F32) * decay
        inner = jnp.dot(s.astype(BF16), v, preferred_element_type=F32)
        state = st_ref[...]
        cross = jnp.dot((q * xi).astype(BF16), state.astype(BF16), preferred_element_type=F32)
        kz_t = (k * zeta).T.astype(BF16)
        st_ref[...] = state * chunk_decay + jnp.dot(kz_t, v, preferred_element_type=F32)
        out = inner + cross
        ms = jnp.mean(out * out, axis=-1, keepdims=True)
        gate = g_ref[sl, :].astype(F32)
        o_ref[sl, :] = (gate * jax.nn.sigmoid(gate) * (out * lax.rsqrt(ms + NORM_EPS))).astype(BF16)


def _retention(proj, batch, seq):
    p = proj.reshape(batch, seq, P_WIDTH)
    return pl.pallas_call(
        _ret_body,
        grid=(batch, R_HEADS, seq // RET_TB),
        in_specs=[
            pl.BlockSpec((None, RET_TB, R_KEY_DIM), lambda b, h, t: (b, t, P_QR // R_KEY_DIM + h)),
            pl.BlockSpec((None, RET_TB, R_KEY_DIM), lambda b, h, t: (b, t, P_KR // R_KEY_DIM + h)),
            pl.BlockSpec((None, RET_TB, R_VAL_DIM), lambda b, h, t: (b, t, P_VR // R_VAL_DIM + h)),
            pl.BlockSpec((None, RET_TB, R_VAL_DIM), lambda b, h, t: (b, t, P_GR // R_VAL_DIM + h)),
        ],
        out_specs=pl.BlockSpec((None, RET_TB, R_VAL_DIM), lambda b, h, t: (b, t, h)),
        out_shape=jax.ShapeDtypeStruct((batch, seq, R_V), BF16),
        scratch_shapes=[pltpu.VMEM((R_KEY_DIM, R_VAL_DIM), F32)],
        compiler_params=pltpu.CompilerParams(
            dimension_semantics=("parallel", "parallel", "arbitrary"), vmem_limit_bytes=VMEM_LIMIT),
    )(p, p, p, p)


FFN_TM = 512
FFN_TF = 2048


def _ffn_body(x_ref, ya_ref, rb_ref, ga_ref, gb_ref, wa_ref, wb_ref, wo_ref, g_ref, wu_ref, wd_ref, fg_ref,
              o_ref, h_ref, *, final):
    ya = jnp.dot(ya_ref[...].astype(BF16), wa_ref[...], preferred_element_type=F32)
    yb = jnp.dot(rb_ref[...], wb_ref[...], preferred_element_type=F32)
    ga = ga_ref[...].astype(F32)
    gb = gb_ref[...].astype(F32)
    merged = jax.nn.sigmoid(ga) * ya + jax.nn.sigmoid(gb) * yb
    x = x_ref[...] + jnp.dot(merged.astype(BF16), wo_ref[...], preferred_element_type=F32)
    h_ref[...] = _rms(x, g_ref[...]).astype(BF16)
    acc = x
    for c in range(D_FF // FFN_TF):
        up = jnp.dot(h_ref[...], wu_ref[:, c * FFN_TF:(c + 1) * FFN_TF], preferred_element_type=F32)
        up = jnp.square(jnp.maximum(up, 0.0)).astype(BF16)
        acc = acc + jnp.dot(up, wd_ref[c * FFN_TF:(c + 1) * FFN_TF, :], preferred_element_type=F32)
    o_ref[...] = _rms(acc, fg_ref[...]) if final else acc


def _merge_ffn(x2d, ya2d, rb2d, proj, wa, wb, wo, gain, wu, wd, layer, final_gain, final):
    m = x2d.shape[0]

    def held(shape):
        return pl.BlockSpec((None,) + shape, lambda i: (layer, 0, 0), pipeline_mode=pl.Buffered(1))

    return pl.pallas_call(
        functools.partial(_ffn_body, final=final),
        grid=(m // FFN_TM,),
        in_specs=[
            pl.BlockSpec((FFN_TM, D_MODEL), lambda i: (i, 0)),
            pl.BlockSpec((FFN_TM, A_OUT), lambda i: (i, 0)),
            pl.BlockSpec((FFN_TM, R_V), lambda i: (i, 0)),
            pl.BlockSpec((FFN_TM, D_MODEL), lambda i: (i, P_GA // D_MODEL)),
            pl.BlockSpec((FFN_TM, D_MODEL), lambda i: (i, P_GB // D_MODEL)),
            held((A_OUT, D_MODEL)), held((R_V, D_MODEL)), held((D_MODEL, D_MODEL)),
            pl.BlockSpec((1, D_MODEL), lambda i: (0, 0)),
            held((D_MODEL, D_FF)), held((D_FF, D_MODEL)),
            pl.BlockSpec((1, D_MODEL), lambda i: (0, 0)),
        ],
        out_specs=pl.BlockSpec((FFN_TM, D_MODEL), lambda i: (i, 0)),
        out_shape=jax.ShapeDtypeStruct((m, D_MODEL), F32),
        scratch_shapes=[pltpu.VMEM((FFN_TM, D_MODEL), BF16)],
        compiler_params=pltpu.CompilerParams(
            dimension_semantics=("parallel",), vmem_limit_bytes=VMEM_LIMIT),
    )(x2d, ya2d, rb2d, proj, proj, wa, wb, wo, gain, wu, wd, final_gain)


def kernel(x, mix_norm, w_in, w_a, w_b, w_o, ffn_norm, w_up, w_down, final_norm):
    batch, seq, _ = x.shape
    depth = w_in.shape[0]
    assert seq % A_SPAN == 0 and seq % RET_TB == 0
    tables = _rotary_tables(seq)
    final_gain = final_norm.reshape(1, D_MODEL)
    x2d = x.reshape(batch * seq, D_MODEL)
    w_in, w_a, w_b, w_o, w_up, w_down = (_arrange_w_in(w_in),) + tuple(
        w.astype(BF16) for w in (w_a, w_b, w_o, w_up, w_down))
    for layer in range(depth):
        proj = _in_proj(x2d, mix_norm[layer].reshape(1, D_MODEL), w_in, layer, tables, seq)
        ya = _attention_a(proj, batch, seq).reshape(batch * seq, A_OUT)
        rb = _retention(proj, batch, seq).reshape(batch * seq, R_V)
        x2d = _merge_ffn(x2d, ya, rb, proj, w_a, w_b, w_o, ffn_norm[layer].reshape(1, D_MODEL), w_up, w_down,
                         layer, final_gain, final=(layer == depth - 1))
    return x2d.reshape(batch, seq, D_MODEL)
```
